```python
import jax, jax.numpy as jnp
from jax import lax
import numpy as np

D_MODEL = 1024
BATCH = 32
SEQ = 256
DEPTH = 2
DEC_BATCH = 8
DEC_SEQ = 2048
PAST_LEN = 256

GRID_W = 64
N_MIXERS = 4
MIX_W = D_MODEL
GROUP_W = MIX_W // N_MIXERS
HEAD_DIM = 64
GLA_H = GROUP_W // HEAD_DIM
GLA_LOWRANK = 16
GLA_GATE_NORM = 16.0
GLA_CHUNK = 16
DN_H = GROUP_W // HEAD_DIM
DN_CHUNK = 64
CONV_K = 3
S5_GROUP = 16
S5_G = GROUP_W // S5_GROUP
S5_P = 64
HG_H = GROUP_W // HEAD_DIM
PEER_HEADS = 8
PEER_NKEYS = 128
PEER_N = PEER_NKEYS * PEER_NKEYS
PEER_DQ = 256
PEER_TOPK = 16
PEER_BLOCK = 128
EPS = 1e-6
IN_SIZES = (
    GROUP_W, GROUP_W, GROUP_W, GROUP_W, 2 * GLA_LOWRANK,
    GROUP_W, GROUP_W, GROUP_W, GROUP_W, 2 * DN_H, 2 * DN_H,
    GROUP_W,
    GROUP_W, GROUP_W, GROUP_W, 2 * GROUP_W,
)
IN_COLS = sum(IN_SIZES)

kernel_name = 'hybrid_flow_backbone_step'


def rmsnorm(x, g):
    xf = x.astype(jnp.float32)
    xf = xf * lax.rsqrt(jnp.mean(xf * xf, axis=-1, keepdims=True) + EPS)
    return xf * g.astype(jnp.float32)


def l2norm(t):
    return t * lax.rsqrt(jnp.sum(t * t, axis=-1, keepdims=True) + EPS)


def to_heads(t):
    return t.reshape(t.shape[:-1] + (t.shape[-1] // HEAD_DIM, HEAD_DIM))


def flip(t):
    return jnp.flip(t, axis=1)


def split_cols(t, sizes):
    idx = [int(s) for s in np.cumsum(sizes)[:-1]]
    return jnp.split(t, idx, axis=-1)


def to_column_major(t):
    b, n, d = t.shape
    rows = n // GRID_W
    return t.reshape(b, rows, GRID_W, d).transpose(0, 2, 1, 3).reshape(b, n, d)


def to_row_major(t):
    b, n, d = t.shape
    rows = n // GRID_W
    return t.reshape(b, GRID_W, rows, d).transpose(0, 2, 1, 3).reshape(b, n, d)


def centred_dwconv(x, w):
    t = x.shape[1]
    p = CONV_K // 2
    xp = jnp.pad(x, ((0, 0), (p, p), (0, 0)))
    return sum(xp[:, j:j + t] * w[j] for j in range(CONV_K))


def chunk_gla(q, k, v, log_g, h0):
    b_, t, h, dk = q.shape
    dv = v.shape[-1]
    c = GLA_CHUNK
    n = t // c
    q = q.reshape(b_, n, c, h, dk)
    k = k.reshape(b_, n, c, h, dk)
    v = v.reshape(b_, n, c, h, dv)
    b = jnp.cumsum(log_g.reshape(b_, n, c, h, dk), axis=2)
    mask = jnp.tril(jnp.ones((c, c), bool))
    diff = b[:, :, :, None] - b[:, :, None, :]
    decay = jnp.where(mask[:, :, None, None], jnp.exp(jnp.minimum(diff, 0.0)), 0.0)
    attn = jnp.einsum('bnihd,bnjhd,bnijhd->bnhij', q, k, decay)
    o_intra = jnp.einsum('bnhij,bnjhv->bnihv', attn, v)
    b_last = b[:, :, -1]
    k_tail = k * jnp.exp(b_last[:, :, None] - b)
    upd = jnp.einsum('bnchd,bnchv->bnhdv', k_tail, v)

    def step(s, xs):
        dec, u = xs
        return dec[..., None] * s + u, s

    h_final, h_prev = lax.scan(step, h0, (jnp.moveaxis(jnp.exp(b_last), 1, 0), jnp.moveaxis(upd, 1, 0)))
    h_prev = jnp.moveaxis(h_prev, 0, 1)
    o_inter = jnp.einsum('bnchd,bnhdv->bnchv', q * jnp.exp(b), h_prev)
    return (o_intra + o_inter).reshape(b_, t, h, dv), h_final


def chunk_delta(q, k, v, beta, log_g, s0):
    b_, t, h, dk = q.shape
    dv = v.shape[-1]
    c = DN_CHUNK
    n = t // c
    rs = lambda a: a.reshape(b_, n, c, h, a.shape[-1]).transpose(0, 1, 3, 2, 4)
    q, k, v = rs(q), rs(k), rs(v)
    beta = beta.reshape(b_, n, c, h).transpose(0, 1, 3, 2)
    b = jnp.cumsum(log_g.reshape(b_, n, c, h).transpose(0, 1, 3, 2), axis=-1)
    incl = jnp.tril(jnp.ones((c, c), bool))
    strict = jnp.tril(jnp.ones((c, c), bool), -1)
    diff = b[..., :, None] - b[..., None, :]
    lmat = jnp.where(incl, jnp.exp(jnp.minimum(diff, 0.0)), 0.0)
    kk = jnp.einsum('bnhid,bnhjd->bnhij', k, k)
    m = jnp.where(strict, beta[..., :, None] * kk * lmat, 0.0)
    a_mat = m + jnp.eye(c, dtype=m.dtype)
    rhs = jnp.concatenate([v * beta[..., None], k * (beta * jnp.exp(b))[..., None]], axis=-1)
    sol = lax.linalg.triangular_solve(a_mat, rhs, left_side=True, lower=True, unit_diagonal=True)
    u, w = sol[..., :dv], sol[..., dv:]
    qk = jnp.einsum('bnhid,bnhjd->bnhij', q, k) * lmat
    q_dec = q * jnp.exp(b)[..., None]
    k_tail = k * jnp.exp(b[..., -1:] - b)[..., None]
    dec_last = jnp.exp(b[..., -1])

    def step(s, xs):
        u_n, w_n, qk_n, qd_n, kt_n, dl_n = xs
        v_new = u_n - jnp.einsum('bhcd,bhdv->bhcv', w_n, s)
        o_n = jnp.einsum('bhcd,bhdv->bhcv', qd_n, s) + jnp.einsum('bhij,bhjv->bhiv', qk_n, v_new)
        s = dl_n[..., None, None] * s + jnp.einsum('bhcd,bhcv->bhdv', kt_n, v_new)
        return s, o_n

    xs = tuple(jnp.moveaxis(a, 1, 0) for a in (u, w, qk, q_dec, k_tail, dec_last))
    s_final, o = lax.scan(step, s0, xs)
    o = jnp.moveaxis(o, 0, 1)
    return o.transpose(0, 1, 3, 2, 4).reshape(b_, t, h, dv), s_final


def gla_mixer(q, k, v, g, gk_low, gk_up, gk_bias, onorm_g, h0):
    b_, t, _ = q.shape
    h0 = h0.astype(jnp.float32)
    q = to_heads(q) * HEAD_DIM ** -0.5
    k, v = to_heads(k), to_heads(v)
    gk = jnp.einsum('btdr,drc->btdc', gk_low.reshape(b_, t, 2, GLA_LOWRANK), gk_up) + gk_bias
    log_a = to_heads(jax.nn.log_sigmoid(gk) / GLA_GATE_NORM)
    o_f, s_f = chunk_gla(q, k, v, log_a[:, :, 0], h0[:, 0])
    o_b, s_b = chunk_gla(flip(q), flip(k), flip(v), flip(log_a[:, :, 1]), h0[:, 1])
    o = rmsnorm(o_f + flip(o_b), onorm_g) * jax.nn.silu(to_heads(g))
    return o.reshape(b_, t, GROUP_W), jnp.stack([s_f, s_b], axis=1)


def delta_mixer(q, k, v, g, beta_l, a_l, conv_w, a_log, dt_bias, onorm_g, s0):
    b_, t, _ = q.shape
    s0 = s0.astype(jnp.float32)
    qkv = jax.nn.silu(centred_dwconv(jnp.concatenate([q, k, v], axis=-1), conv_w))
    q, k, v = jnp.split(qkv, 3, axis=-1)
    q = l2norm(to_heads(q)) * HEAD_DIM ** -0.5
    k = l2norm(to_heads(k))
    v = to_heads(v)
    beta = jax.nn.sigmoid(beta_l).reshape(b_, t, 2, DN_H)
    log_g = -jnp.exp(a_log) * jax.nn.softplus(a_l.reshape(b_, t, 2, DN_H) + dt_bias)
    o_f, s_f = chunk_delta(q, k, v, beta[:, :, 0], log_g[:, :, 0], s0[:, 0])
    o_b, s_b = chunk_delta(flip(q), flip(k), flip(v), flip(beta[:, :, 1]), flip(log_g[:, :, 1]), s0[:, 1])
    o = rmsnorm(o_f + flip(o_b), onorm_g) * jax.nn.silu(to_heads(g))
    return o.reshape(b_, t, GROUP_W), jnp.stack([s_f, s_b], axis=1)


def cmul(ar, ai, br, bi):
    return ar * br - ai * bi, ar * bi + ai * br


def s5_direction(u_g, lam_re, lam_im, log_dt, b_re, b_im, c_re, c_im, h0_re, h0_im):
    dt = jnp.exp(log_dt)[:, None]
    mag = jnp.exp(lam_re * dt)
    lbar_re, lbar_im = mag * jnp.cos(lam_im * dt), mag * jnp.sin(lam_im * dt)
    den = lam_re * lam_re + lam_im * lam_im
    num_re, num_im = lbar_re - 1.0, lbar_im
    coef_re = (num_re * lam_re + num_im * lam_im) / den
    coef_im = (num_im * lam_re - num_re * lam_im) / den
    bb_re, bb_im = cmul(coef_re[..., None], coef_im[..., None], b_re, b_im)
    bu_re = jnp.einsum('btgc,gpc->btgp', u_g, bb_re)
    bu_im = jnp.einsum('btgc,gpc->btgp', u_g, bb_im)
    a_re = jnp.broadcast_to(lbar_re, bu_re.shape)
    a_im = jnp.broadcast_to(lbar_im, bu_re.shape)

    def combine(e1, e2):
        a1r, a1i, b1r, b1i = e1
        a2r, a2i, b2r, b2i = e2
        ar, ai = cmul(a2r, a2i, a1r, a1i)
        br, bi = cmul(a2r, a2i, b1r, b1i)
        return ar, ai, br + b2r, bi + b2i

    acum_re, acum_im, x_re, x_im = lax.associative_scan(combine, (a_re, a_im, bu_re, bu_im), axis=1)
    hr, hi = cmul(acum_re, acum_im, h0_re[:, None], h0_im[:, None])
    x_re, x_im = x_re + hr, x_im + hi
    y = jnp.einsum('btgp,gcp->btgc', x_re, c_re) - jnp.einsum('btgp,gcp->btgc', x_im, c_im)
    return y, x_re[:, -1], x_im[:, -1]


def s5_mixer(u, lam_re, lam_im, log_dt, b_re, b_im, c_re, c_im, d_skip, glu_w, glu_b, h0_re, h0_im):
    b_, t, _ = u.shape
    f32 = jnp.float32
    lam_re, lam_im, log_dt = lam_re.astype(f32), lam_im.astype(f32), log_dt.astype(f32)
    b_re, b_im, c_re, c_im = b_re.astype(f32), b_im.astype(f32), c_re.astype(f32), c_im.astype(f32)
    h0_re, h0_im = h0_re.astype(f32), h0_im.astype(f32)
    ug = u.reshape(b_, t, S5_G, S5_GROUP)
    y_f, sr_f, si_f = s5_direction(ug, lam_re[0], lam_im[0], log_dt[0], b_re, b_im, c_re, c_im, h0_re[:, 0], h0_im[:, 0])
    y_b, sr_b, si_b = s5_direction(flip(ug), lam_re[1], lam_im[1], log_dt[1], b_re, b_im, c_re, c_im, h0_re[:, 1], h0_im[:, 1])
    y = (y_f + flip(y_b)).reshape(b_, t, GROUP_W) + d_skip * u
    y = jax.nn.gelu(y)
    y = y * jax.nn.sigmoid(y @ glu_w + glu_b)
    return y, jnp.stack([sr_f, sr_b], axis=1), jnp.stack([si_f, si_b], axis=1)


def hgrn2_mixer(q, i, g, f_l, lb, onorm_g, h0):
    b_, t, _ = q.shape
    h0 = h0.astype(jnp.float32)
    q = to_heads(jax.nn.silu(q)) * HEAD_DIM ** -0.5
    v = to_heads(i)
    z = f_l.reshape(b_, t, 2, GROUP_W)
    log_f = to_heads(jnp.logaddexp(jnp.log(lb), jnp.log1p(-lb) + jax.nn.log_sigmoid(z)))
    k = to_heads((1.0 - lb) * jax.nn.sigmoid(-z))
    o_f, s_f = chunk_gla(q, k[:, :, 0], v, log_f[:, :, 0], h0[:, 0])
    o_b, s_b = chunk_gla(flip(q), flip(k[:, :, 1]), flip(v), flip(log_f[:, :, 1]), h0[:, 1])
    o = rmsnorm(o_f + flip(o_b), onorm_g) * jax.nn.silu(to_heads(g))
    return o.reshape(b_, t, GROUP_W), jnp.stack([s_f, s_b], axis=1)


def peer_ffn(h, wq, keys, u_tab, v_tab):
    b_, t, d = h.shape
    hb_all = h.reshape(-1, PEER_BLOCK, d)

    def block(hb):
        qb = (hb @ wq).astype(jnp.float32).reshape(PEER_BLOCK, PEER_HEADS, 2, PEER_DQ // 2)
        s = jnp.einsum('nhpd,hpkd->nhpk', qb, keys.astype(jnp.float32))
        v1, i1 = lax.top_k(s[:, :, 0], PEER_TOPK)
        v2, i2 = lax.top_k(s[:, :, 1], PEER_TOPK)
        cand = (v1[..., :, None] + v2[..., None, :]).reshape(PEER_BLOCK, PEER_HEADS, PEER_TOPK * PEER_TOPK)
        sc, ci = lax.top_k(cand, PEER_TOPK)
        e = (jnp.take_along_axis(i1, ci // PEER_TOPK, axis=-1) * PEER_NKEYS
             + jnp.take_along_axis(i2, ci % PEER_TOPK, axis=-1))
        gate = jax.nn.softmax(sc, axis=-1)
        act = jax.nn.gelu(jnp.einsum('nhkd,nd->nhk', u_tab[e], hb))
        return jnp.einsum('nhk,nhkd->nd', gate * act, v_tab[e])

    return lax.map(block, hb_all).reshape(b_, t, d)


def mixing_sublayer(h, states, p, lb):
    proj = h @ p['w_in']
    (ga_q, ga_k, ga_v, ga_g, ga_low, db_q, db_k, db_v, db_g, db_beta, db_a,
     s5_u, hg_q, hg_i, hg_g, hg_f) = split_cols(proj, IN_SIZES)
    st_gla, st_dn, st_re, st_im, st_hg = states
    y_a, n_gla = gla_mixer(ga_q, ga_k, ga_v, ga_g, ga_low, p['gla_gk_up'], p['gla_gk_bias'], p['gla_onorm_g'], st_gla)
    y_b, n_dn = delta_mixer(db_q, db_k, db_v, db_g, db_beta, db_a, p['dn_conv_w'], p['dn_A_log'],
                            p['dn_dt_bias'], p['dn_onorm_g'], st_dn)
    y_c, n_re, n_im = s5_mixer(s5_u, p['s5_lambda_re'], p['s5_lambda_im'], p['s5_log_dt'], p['s5_B_re'],
                               p['s5_B_im'], p['s5_C_re'], p['s5_C_im'], p['s5_D'], p['s5_glu_w'],
                               p['s5_glu_b'], st_re, st_im)
    y_d, n_hg = hgrn2_mixer(hg_q, hg_i, hg_g, hg_f, lb, p['hgrn_onorm_g'], st_hg)
    y = jnp.concatenate([y_a, y_b, y_c, y_d], axis=-1) @ p['w_out']
    return y, (n_gla, n_dn, n_re, n_im, n_hg)


def trunk_layer(x, mod, states, p, lb, col_major):
    sh1, sc1, g1, sh2, sc2, g2 = jnp.split(mod[:, None, :], 6, axis=-1)
    h = rmsnorm(x, p['norm1_g']) * (1.0 + sc1) + sh1
    if col_major:
        h = to_column_major(h)
    y, new_states = mixing_sublayer(h, states, p, lb)
    if col_major:
        y = to_row_major(y)
    x = x + (g1 * y).astype(x.dtype)
    h = rmsnorm(x, p['norm2_g']) * (1.0 + sc2) + sh2
    x = x + (g2 * peer_ffn(h, p['peer_wq'], p['peer_keys'], p['peer_u'], p['peer_v'])).astype(x.dtype)
    return x, new_states


def setup_inputs(seed: int = 0) -> dict:
    key = jax.random.key(seed)
    ks = jax.random.split(key, 64)
    cnt = [0]

    def nk():
        cnt[0] += 1
        return ks[cnt[0] - 1]

    f32 = jnp.float32
    nrm = lambda shape, s: jax.random.normal(nk(), shape, f32) * s
    uni = lambda shape, lo, hi: jax.random.uniform(nk(), shape, f32, lo, hi)
    dt = jnp.exp(uni((DEPTH, 2, DN_H), float(np.log(1e-3)), float(np.log(1e-1))))
    return {
        'x_prompt': nrm((BATCH, SEQ, D_MODEL), 1.0),
        'x_sample': nrm((DEC_BATCH, DEC_SEQ, D_MODEL), 1.0),
        'state_gla': nrm((DEC_BATCH, DEPTH, 2, GLA_H, HEAD_DIM, HEAD_DIM), 0.5),
        'state_delta': nrm((DEC_BATCH, DEPTH, 2, DN_H, HEAD_DIM, HEAD_DIM), 0.2),
        'state_s5_re': nrm((DEC_BATCH, DEPTH, 2, S5_G, S5_P), 0.1),
        'state_s5_im': nrm((DEC_BATCH, DEPTH, 2, S5_G, S5_P), 0.1),
        'state_hgrn': nrm((DEC_BATCH, DEPTH, 2, HG_H, HEAD_DIM, HEAD_DIM), 0.5),
        'c': nrm((DEC_BATCH, D_MODEL), 1.0),
        'c_ctx': nrm((D_MODEL,), 1.0),
        'mod_w': nrm((DEPTH, D_MODEL, 6 * D_MODEL), 0.5 * D_MODEL ** -0.5),
        'mod_b': nrm((DEPTH, 6 * D_MODEL), 0.02),
        'norm1_g': 1.0 + nrm((DEPTH, D_MODEL), 0.02),
        'norm2_g': 1.0 + nrm((DEPTH, D_MODEL), 0.02),
        'w_in': nrm((DEPTH, D_MODEL, IN_COLS), D_MODEL ** -0.5),
        'w_out': nrm((DEPTH, MIX_W, D_MODEL), MIX_W ** -0.5),
        'gla_gk_up': nrm((DEPTH, 2, GLA_LOWRANK, GROUP_W), GLA_LOWRANK ** -0.5),
        'gla_gk_bias': nrm((DEPTH, 2, GROUP_W), 0.1),
        'gla_onorm_g': 1.0 + nrm((DEPTH, HEAD_DIM), 0.02),
        'dn_conv_w': nrm((DEPTH, CONV_K, 3 * GROUP_W), CONV_K ** -0.5),
        'dn_A_log': jnp.log(uni((DEPTH, 2, DN_H), 1.0, 16.0)),
        'dn_dt_bias': dt + jnp.log(-jnp.expm1(-dt)),
        'dn_onorm_g': 1.0 + nrm((DEPTH, HEAD_DIM), 0.02),
        's5_lambda_re': -0.5 + nrm((DEPTH, 2, S5_G, S5_P), 0.01),
        's5_lambda_im': np.pi * jnp.arange(S5_P, dtype=f32) + nrm((DEPTH, 2, S5_G, S5_P), 0.01),
        's5_log_dt': uni((DEPTH, 2, S5_G), float(np.log(1e-3)), float(np.log(1e-1))),
        's5_B_re': nrm((DEPTH, S5_G, S5_P, S5_GROUP), (2 * S5_GROUP) ** -0.5),
        's5_B_im': nrm((DEPTH, S5_G, S5_P, S5_GROUP), (2 * S5_GROUP) ** -0.5),
        's5_C_re': nrm((DEPTH, S5_G, S5_GROUP, S5_P), (2 * S5_P) ** -0.5),
        's5_C_im': nrm((DEPTH, S5_G, S5_GROUP, S5_P), (2 * S5_P) ** -0.5),
        's5_D': nrm((DEPTH, GROUP_W), 0.5),
        's5_glu_w': nrm((DEPTH, GROUP_W, GROUP_W), GROUP_W ** -0.5),
        's5_glu_b': nrm((DEPTH, GROUP_W), 0.02),
        'hgrn_lb_logits': nrm((DEPTH, GROUP_W), 0.5),
        'hgrn_onorm_g': 1.0 + nrm((DEPTH, HEAD_DIM), 0.02),
        'peer_wq': nrm((DEPTH, D_MODEL, PEER_HEADS * PEER_DQ), D_MODEL ** -0.5),
        'peer_keys': nrm((DEPTH, PEER_HEADS, 2, PEER_NKEYS, PEER_DQ // 2), (PEER_DQ // 2) ** -0.5),
        'peer_u': nrm((DEPTH, PEER_N, D_MODEL), D_MODEL ** -0.5),
        'peer_v': nrm((DEPTH, PEER_N, D_MODEL), 0.5),
        'final_norm_g': 1.0 + nrm((D_MODEL,), 0.02),
    }


def reference(x_prompt, x_sample, state_gla, state_delta, state_s5_re, state_s5_im, state_hgrn, c, c_ctx,
              mod_w, mod_b, norm1_g, norm2_g, w_in, w_out, gla_gk_up, gla_gk_bias, gla_onorm_g,
              dn_conv_w, dn_A_log, dn_dt_bias, dn_onorm_g, s5_lambda_re, s5_lambda_im, s5_log_dt,
              s5_B_re, s5_B_im, s5_C_re, s5_C_im, s5_D, s5_glu_w, s5_glu_b, hgrn_lb_logits,
              hgrn_onorm_g, peer_wq, peer_keys, peer_u, peer_v, final_norm_g):
    f32 = jnp.float32
    lb_cum = jnp.cumsum(jax.nn.softmax(hgrn_lb_logits.astype(f32), axis=0), axis=0)
    lower_bounds = lb_cum - lb_cum[0:1]
    bp = x_prompt.shape[0]
    zero_states = (jnp.zeros((bp, 2, GLA_H, HEAD_DIM, HEAD_DIM), f32),
                   jnp.zeros((bp, 2, DN_H, HEAD_DIM, HEAD_DIM), f32),
                   jnp.zeros((bp, 2, S5_G, S5_P), f32),
                   jnp.zeros((bp, 2, S5_G, S5_P), f32),
                   jnp.zeros((bp, 2, HG_H, HEAD_DIM, HEAD_DIM), f32))
    xp, xs = x_prompt, x_sample
    ctx_states = []
    for l in range(DEPTH):
        p = {'w_in': w_in[l], 'w_out': w_out[l], 'norm1_g': norm1_g[l], 'norm2_g': norm2_g[l],
             'gla_gk_up': gla_gk_up[l], 'gla_gk_bias': gla_gk_bias[l], 'gla_onorm_g': gla_onorm_g[l],
             'dn_conv_w': dn_conv_w[l], 'dn_A_log': dn_A_log[l], 'dn_dt_bias': dn_dt_bias[l],
             'dn_onorm_g': dn_onorm_g[l], 's5_lambda_re': s5_lambda_re[l], 's5_lambda_im': s5_lambda_im[l],
             's5_log_dt': s5_log_dt[l], 's5_B_re': s5_B_re[l], 's5_B_im': s5_B_im[l], 's5_C_re': s5_C_re[l],
             's5_C_im': s5_C_im[l], 's5_D': s5_D[l], 's5_glu_w': s5_glu_w[l], 's5_glu_b': s5_glu_b[l],
             'hgrn_onorm_g': hgrn_onorm_g[l], 'peer_wq': peer_wq[l], 'peer_keys': peer_keys[l],
             'peer_u': peer_u[l], 'peer_v': peer_v[l]}
        mod_ctx = jax.nn.silu(c_ctx)[None, :] @ mod_w[l] + mod_b[l]
        mod_lat = jax.nn.silu(c) @ mod_w[l] + mod_b[l]
        xp, st_ctx = trunk_layer(xp, mod_ctx, zero_states, p, lower_bounds[l], False)
        ctx_states.append(st_ctx)
        cached = (state_gla[:, l], state_delta[:, l], state_s5_re[:, l], state_s5_im[:, l], state_hgrn[:, l])
        xs, _ = trunk_layer(xs, mod_lat, cached, p, lower_bounds[l], l % 2 == 1)
    y_prompt = rmsnorm(xp, final_norm_g).astype(x_prompt.dtype)
    y_sample = rmsnorm(xs, final_norm_g).astype(x_sample.dtype)
    new_state_gla = jnp.stack([s[0] for s in ctx_states], axis=1)
    new_state_delta = jnp.stack([s[1] for s in ctx_states], axis=1)
    new_state_s5_re = jnp.stack([s[2] for s in ctx_states], axis=1)
    new_state_s5_im = jnp.stack([s[3] for s in ctx_states], axis=1)
    new_state_hgrn = jnp.stack([s[4] for s in ctx_states], axis=1)
    return (y_prompt, y_sample, new_state_gla, new_state_delta, new_state_s5_re, new_state_s5_im, new_state_hgrn)
```

```python
import functools

import numpy as np
import jax
import jax.numpy as jnp
from jax import lax
from jax.experimental import pallas as pl
from jax.experimental.pallas import tpu as pltpu

F32 = jnp.float32
BF16 = jnp.bfloat16

D_MODEL = 1024
DEPTH = 2
GRID_W = 64
GROUP_W = 256
HEAD_DIM = 64
N_HEADS = GROUP_W // HEAD_DIM
GLA_LOWRANK = 16
GLA_GATE_NORM = 16.0
DN_CHUNK = 64
S5_GROUP = 16
S5_G = GROUP_W // S5_GROUP
S5_P = 64
S5_STATE = S5_G * S5_P
PEER_HEADS = 8
PEER_NKEYS = 128
PEER_DQ = 256
PEER_TOPK = 16
PEER_SEL = PEER_HEADS * PEER_TOPK
EPS = 1e-6

LANES = 128
SUBLANES = 8
VMEM_LIMIT = 56 * 1024 * 1024

GLA_CHUNK = 64
S5_CHUNK = 256
TOKEN_BLOCK = 256
PEER_EXPERT_BLOCK = 2048
GLA_W = 4 * GROUP_W + LANES
DN_W = 4 * GROUP_W + LANES
HG_W = 5 * GROUP_W


def _dot(a, b):
    return jnp.dot(a, b, preferred_element_type=F32)


def _dot_nt(a, b):
    return lax.dot_general(a, b, (((1,), (1,)), ((), ())), preferred_element_type=F32)


def _dot_tn(a, b):
    return lax.dot_general(a, b, (((0,), (0,)), ((), ())), preferred_element_type=F32)


def _split2(x):
    hi = x.astype(BF16)
    lo = (x - hi.astype(F32)).astype(BF16)
    return hi, lo


def _split3(x):
    hi = x.astype(BF16)
    r = x - hi.astype(F32)
    mid = r.astype(BF16)
    lo = (r - mid.astype(F32)).astype(BF16)
    return hi, mid, lo


def _sel_dot(m01, x):
    hi, mid, lo = _split3(x)
    return _dot(m01, hi) + _dot(m01, mid) + _dot(m01, lo)


def _dot_sel(x, m01):
    hi, mid, lo = _split3(x)
    return _dot(hi, m01) + _dot(mid, m01) + _dot(lo, m01)


def _dot_hl(a, b_hi, b_lo):
    a_hi, a_lo = _split2(a)
    return _dot(a_hi, b_hi) + _dot(a_hi, b_lo) + _dot(a_lo, b_hi)


def _head_sum(x, ones_bd):
    hi, lo = _split2(x)
    return _dot(hi, ones_bd) + _dot(lo, ones_bd)


def _log_sigmoid(x):
    return jnp.minimum(x, 0.0) - jnp.log(1.0 + jnp.exp(-jnp.abs(x)))


def _softplus(x):
    return jnp.maximum(x, 0.0) + jnp.log(1.0 + jnp.exp(-jnp.abs(x)))


def _sigmoid(x):
    return 1.0 / (1.0 + jnp.exp(-x))


def _silu(x):
    return x * _sigmoid(x)


def _gelu_tanh(x):
    return 0.5 * x * (1.0 + jnp.tanh(0.7978845608028654 * (x + 0.044715 * x * x * x)))


def _rms_modulate(x, gain, shift, scale):
    ms = jnp.mean(x * x, axis=-1, keepdims=True)
    return x * lax.rsqrt(ms + EPS) * gain * (1.0 + scale) + shift


def _full_spec(shape):
    zeros = (0,) * len(shape)
    return pl.BlockSpec(shape, lambda *_: zeros)


def _params(sem):
    return pltpu.CompilerParams(dimension_semantics=sem, vmem_limit_bytes=VMEM_LIMIT)


def _scan_consts(c, reverse):
    levels = int(np.log2(c))
    idx = np.arange(c)
    i, t = idx[:, None], idx[None, :]
    sel, masks = [], []
    for lv in range(levels):
        s = 1 << lv
        right = (idx // s) % 2 == 1
        same_blk = (i // s) == (t // s)
        sel.append(np.where(right[:, None], same_blk & (t <= i), same_blk & (t > i)))
        same_parent = (i // (2 * s)) == (t // (2 * s))
        masks.append(same_parent & right[:, None] & ~right[None, :])
    sel.append(t <= i)
    sel.append(t > i)
    masks.append(i == t)
    if reverse:
        sel = [m[::-1, ::-1] for m in sel]
        masks = [m[::-1, ::-1] for m in masks]
    sel.append(np.ones((SUBLANES, c), bool))
    sel = np.concatenate(sel, axis=0).astype(np.float32)
    masks = np.stack(masks).astype(np.float32)
    return jnp.asarray(sel, BF16), jnp.asarray(masks, F32)


def _dn_consts(c, reverse):
    idx = np.arange(c)
    i, t = idx[:, None], idx[None, :]
    incl, tail = t <= i, t > i
    if reverse:
        incl, tail = incl[::-1, ::-1], tail[::-1, ::-1]
    sel = np.concatenate([incl, tail, np.ones((SUBLANES, c), bool)], axis=0).astype(np.float32)
    strict = incl & (i != t)
    masks = np.stack([incl, strict]).astype(np.float32)
    return jnp.asarray(sel, BF16), jnp.asarray(masks, F32)


def _ones_blockdiag():
    h = np.arange(GROUP_W) // HEAD_DIM
    return jnp.asarray((h[:, None] == h[None, :]).astype(np.float32), BF16)


def _mod_kernel(c_ref, w_ref, b_ref, o_ref):
    a_hi, a_lo = _split2(_silu(c_ref[...]))
    w_hi, w_lo = _split2(w_ref[0])
    o_ref[0] = _dot(a_hi, w_hi) + _dot(a_hi, w_lo) + _dot(a_lo, w_hi) + b_ref[0]


def _modulation(cc, mod_w, mod_b):
    rows = cc.shape[0]
    nb = 6
    return pl.pallas_call(
        _mod_kernel,
        grid=(DEPTH, nb),
        in_specs=[pl.BlockSpec((rows, D_MODEL), lambda l, j: (0, 0)),
                  pl.BlockSpec((1, D_MODEL, D_MODEL), lambda l, j: (l, 0, j)),
                  pl.BlockSpec((1, 1, D_MODEL), lambda l, j: (l, 0, j))],
        out_specs=pl.BlockSpec((1, rows, D_MODEL), lambda l, j: (l, 0, j)),
        out_shape=jax.ShapeDtypeStruct((DEPTH, rows, 6 * D_MODEL), F32),
        compiler_params=_params(("arbitrary", "arbitrary")),
        name="modulation",
    )(cc, mod_w, mod_b.reshape(DEPTH, 1, 6 * D_MODEL))


def _inproj_kernel(x_ref, mod_ref, g_ref, wa_ref, wb_ref, wc_ref, wd_ref, oa_ref, ob_ref, oc_ref, od_ref):
    mod = mod_ref[0]
    h = _rms_modulate(x_ref[0], g_ref[...], mod[:, 0:D_MODEL], mod[:, D_MODEL:2 * D_MODEL]).astype(BF16)
    oa_ref[0] = _dot(h, wa_ref[...])
    ob_ref[0] = _dot(h, wb_ref[...])
    oc_ref[0] = _dot(h, wc_ref[...])
    od_ref[0] = _dot(h, wd_ref[...])


def _in_projection(x, mod, gain, weights):
    b, t, _ = x.shape
    tb = min(TOKEN_BLOCK, t)
    widths = [w.shape[1] for w in weights]
    return pl.pallas_call(
        _inproj_kernel,
        grid=(b, t // tb),
        in_specs=[pl.BlockSpec((1, tb, D_MODEL), lambda i, j: (i, j, 0)),
                  pl.BlockSpec((1, 1, 6 * D_MODEL), lambda i, j: (i, 0, 0)),
                  _full_spec((1, D_MODEL))] + [_full_spec(w.shape) for w in weights],
        out_specs=[pl.BlockSpec((1, tb, n), lambda i, j: (i, j, 0)) for n in widths],
        out_shape=[jax.ShapeDtypeStruct((b, t, n), F32) for n in widths],
        compiler_params=_params(("arbitrary", "arbitrary")),
        name="in_projection",
    )(x, mod, gain, *weights)


def _decay_scan_chunk(q, k, v, g, sel_ref, mask_ref, state_ref, d, c):
    levels = int(np.log2(c))
    ex = jnp.exp(_sel_dot(sel_ref[...], g))
    attn = [None] * N_HEADS
    for lv in range(levels + 1):
        if lv < levels:
            e = ex[lv * c:(lv + 1) * c]
            qt, kt = (q * e).astype(BF16), (k * e).astype(BF16)
        else:
            qt, kt = q.astype(BF16), k.astype(BF16)
        m = mask_ref[lv]
        for h in range(N_HEADS):
            hs = slice(h * HEAD_DIM, (h + 1) * HEAD_DIM)
            p = _dot_nt(qt[:, hs], kt[:, hs]) * m
            attn[h] = p if attn[h] is None else attn[h] + p
    qd = (q * ex[levels * c:(levels + 1) * c]).astype(BF16)
    ktail = (k * ex[(levels + 1) * c:(levels + 2) * c]).astype(BF16)
    dlast = ex[(levels + 2) * c:(levels + 2) * c + 1]
    vb = v.astype(BF16)
    outs = []
    for h in range(N_HEADS):
        hs = slice(h * HEAD_DIM, (h + 1) * HEAD_DIM)
        st = state_ref[d, h]
        o = _dot(attn[h].astype(BF16), vb[:, hs]) + _dot_nt(qd[:, hs], st.astype(BF16))
        state_ref[d, h] = st * dlast[:, hs] + _dot_tn(vb[:, hs], ktail[:, hs])
        outs.append(o)
    return jnp.concatenate(outs, axis=1)


def _head_norm_gate(o, gate, onorm, ones_bd):
    ms = _head_sum(o * o, ones_bd) * (1.0 / HEAD_DIM)
    return o * lax.rsqrt(ms + EPS) * onorm * _silu(gate)


def _decay_mixer_kernel(mode, t, c, p_ref, self_ref, selb_ref, maskf_ref, maskb_ref, ones_ref, onorm_ref,
                        pa_ref, pb_ref, pc_ref, h0_ref, y_ref, sout_ref, of_ref, ob_ref, state_ref):
    n_chunks = t // c
    state_ref[...] = h0_ref[0]

    def inputs(rows, d):
        if mode == "gla":
            q = p_ref[0, rows, 0:GROUP_W] * (HEAD_DIM ** -0.5)
            k = p_ref[0, rows, GROUP_W:2 * GROUP_W]
            v = p_ref[0, rows, 2 * GROUP_W:3 * GROUP_W]
            low = p_ref[0, rows, 4 * GROUP_W:4 * GROUP_W + LANES]
            gk = _dot_hl(low, pa_ref[d], pb_ref[d]) + pc_ref[d]
            g = _log_sigmoid(gk) * (1.0 / GLA_GATE_NORM)
        else:
            q = _silu(p_ref[0, rows, 0:GROUP_W]) * (HEAD_DIM ** -0.5)
            v = p_ref[0, rows, GROUP_W:2 * GROUP_W]
            z = p_ref[0, rows, (3 + d) * GROUP_W:(4 + d) * GROUP_W]
            a = pa_ref[...]
            b = pb_ref[...] + _log_sigmoid(z)
            m = jnp.maximum(a, b)
            g = m + jnp.log(jnp.exp(a - m) + jnp.exp(b - m))
            k = pc_ref[...] * _sigmoid(-z)
        return q, k, v, g

    def body(n, carry):
        rf = pl.ds(pl.multiple_of(n * c, c), c)
        q, k, v, g = inputs(rf, 0)
        of_ref[rf, :] = _decay_scan_chunk(q, k, v, g, self_ref, maskf_ref, state_ref, 0, c)
        rb = pl.ds(pl.multiple_of((n_chunks - 1 - n) * c, c), c)
        q, k, v, g = inputs(rb, 1)
        ob_ref[rb, :] = _decay_scan_chunk(q, k, v, g, selb_ref, maskb_ref, state_ref, 1, c)
        return carry

    lax.fori_loop(0, n_chunks, body, 0)

    rbk = min(t, 256)
    gate_col = 3 * GROUP_W if mode == "gla" else 2 * GROUP_W

    def fin(i, carry):
        rows = pl.ds(pl.multiple_of(i * rbk, rbk), rbk)
        o = of_ref[rows, :] + ob_ref[rows, :]
        gate = p_ref[0, rows, gate_col:gate_col + GROUP_W]
        y_ref[0, rows, :] = _head_norm_gate(o, gate, onorm_ref[...], ones_ref[...]).astype(BF16)
        return carry

    lax.fori_loop(0, t // rbk, fin, 0)
    sout_ref[0] = state_ref[...]


def _decay_mixer(mode, p, pa, pb, pc, onorm, h0t):
    b, t, w = p.shape
    c = min(GLA_CHUNK, t)
    sel_f, mask_f = _scan_consts(c, False)
    sel_b, mask_b = _scan_consts(c, True)
    ones_bd = _ones_blockdiag()
    consts = [sel_f, sel_b, mask_f, mask_b, ones_bd, onorm, pa, pb, pc]
    state_shape = (2, N_HEADS, HEAD_DIM, HEAD_DIM)
    return pl.pallas_call(
        functools.partial(_decay_mixer_kernel, mode, t, c),
        grid=(b,),
        in_specs=[pl.BlockSpec((1, t, w), lambda i: (i, 0, 0), pipeline_mode=pl.Buffered(1))]
                 + [_full_spec(a.shape) for a in consts]
                 + [pl.BlockSpec((1,) + state_shape, lambda i: (i, 0, 0, 0, 0))],
        out_specs=[pl.BlockSpec((1, t, GROUP_W), lambda i: (i, 0, 0)),
                   pl.BlockSpec((1,) + state_shape, lambda i: (i, 0, 0, 0, 0))],
        out_shape=[jax.ShapeDtypeStruct((b, t, GROUP_W), BF16),
                   jax.ShapeDtypeStruct((b,) + state_shape, F32)],
        scratch_shapes=[pltpu.VMEM((t, GROUP_W), F32), pltpu.VMEM((t, GROUP_W), F32),
                        pltpu.VMEM(state_shape, F32)],
        compiler_params=_params(("arbitrary",)),
        name=mode + "_mixer",
    )(p, *consts, h0t)


def _unit_lower_solve(m_blocks, x_blocks, reverse):
    nb = len(x_blocks)
    c = nb * SUBLANES
    x_blocks = list(x_blocks)
    for step in range(c):
        j = c - 1 - step if reverse else step
        jb, jr = divmod(j, SUBLANES)
        xj = x_blocks[jb][jr:jr + 1, :]
        todo = range(0, jb + 1) if reverse else range(jb, nb)
        for r in todo:
            x_blocks[r] = x_blocks[r] - m_blocks[r][:, j:j + 1] * xj
    return x_blocks


def _delta_chunk(q, k, v, ba, sel_ref, mask_ref, expand_ref, nega_ref, dtb_ref, state_ref, d, c, reverse):
    logits = _dot_sel(ba, expand_ref[d])
    beta = _sigmoid(logits[:, 0:GROUP_W])
    g = nega_ref[d] * _softplus(logits[:, GROUP_W:2 * GROUP_W] + dtb_ref[d])
    cs = _sel_dot(sel_ref[...], g)
    b_incl = cs[0:c]
    eb = jnp.exp(b_incl)
    etail = jnp.exp(cs[c:2 * c])
    dlast = jnp.exp(cs[2 * c:2 * c + 1])
    incl, strict = mask_ref[0], mask_ref[1]
    kb = k * beta
    rhs_u, rhs_w = v * beta, kb * eb
    qd, ktail = (q * eb).astype(BF16), (k * etail).astype(BF16)
    qb16, kb16 = q.astype(BF16), k.astype(BF16)
    outs = []
    for h in range(N_HEADS):
        hs = slice(h * HEAD_DIM, (h + 1) * HEAD_DIM)
        k_hi, k_lo = _split2(k[:, hs])
        kk = _dot_nt(k_hi, k_hi) + _dot_nt(k_hi, k_lo) + _dot_nt(k_lo, k_hi)
        bh = b_incl[:, hs]
        lmat = jnp.exp(jnp.minimum(bh - bh.T, 0.0)) * incl
        m = beta[:, hs] * kk * lmat * strict
        rhs = jnp.concatenate([rhs_u[:, hs], rhs_w[:, hs]], axis=1)
        nb = c // SUBLANES
        xb = _unit_lower_solve([m[r * SUBLANES:(r + 1) * SUBLANES] for r in range(nb)],
                               [rhs[r * SUBLANES:(r + 1) * SUBLANES] for r in range(nb)], reverse)
        x = jnp.concatenate(xb, axis=0)
        u, w = x[:, 0:HEAD_DIM], x[:, HEAD_DIM:2 * HEAD_DIM]
        st = state_ref[d, h]
        st16 = st.astype(BF16)
        v_new = u - _dot(w.astype(BF16), st16)
        v16 = v_new.astype(BF16)
        qk = (_dot_nt(qb16[:, hs], kb16[:, hs]) * lmat).astype(BF16)
        outs.append(_dot(qd[:, hs], st16) + _dot(qk, v16))
        state_ref[d, h] = st * dlast[:, hs] + _dot_tn(ktail[:, hs], v16)
    return jnp.concatenate(outs, axis=1)


def _delta_mixer_kernel(t, c, p_ref, self_ref, selb_ref, maskf_ref, maskb_ref, ones_ref, onorm_ref,
                        conv_ref, expand_ref, nega_ref, dtb_ref, h0_ref, y_ref, sout_ref,
                        qkv_ref, of_ref, ob_ref, state_ref):
    n_chunks = t // c
    qkv_w = 3 * GROUP_W
    state_ref[...] = h0_ref[0]
    row_id = lax.broadcasted_iota(jnp.int32, (c, qkv_w), 0)
    w_prev, w_mid, w_next = conv_ref[0:1, :], conv_ref[1:2, :], conv_ref[2:3, :]

    def conv_body(n, carry):
        r0 = pl.multiple_of(n * c, c)
        x = p_ref[0, pl.ds(r0, c), 0:qkv_w]
        before = p_ref[0, pl.ds(pl.multiple_of(jnp.maximum(r0 - SUBLANES, 0), SUBLANES), SUBLANES), 0:qkv_w]
        after = p_ref[0, pl.ds(pl.multiple_of(jnp.minimum(r0 + c, t - SUBLANES), SUBLANES), SUBLANES), 0:qkv_w]
        before_row = jnp.where(n > 0, before[SUBLANES - 1:SUBLANES, :], 0.0)
        after_row = jnp.where(n < n_chunks - 1, after[0:1, :], 0.0)
        x_prev = jnp.where(row_id == 0, before_row, pltpu.roll(x, 1, axis=0))
        x_next = jnp.where(row_id == c - 1, after_row, pltpu.roll(x, c - 1, axis=0))
        y = _silu(x_prev * w_prev + x * w_mid + x_next * w_next)
        qk = y[:, 0:2 * GROUP_W]
        ones2 = ones_ref[...]
        ss = jnp.concatenate([_head_sum(qk[:, 0:GROUP_W] * qk[:, 0:GROUP_W], ones2),
                              _head_sum(qk[:, GROUP_W:] * qk[:, GROUP_W:], ones2)], axis=1)
        qk = qk * lax.rsqrt(ss + EPS)
        qkv_ref[pl.ds(r0, c), 0:GROUP_W] = qk[:, 0:GROUP_W] * (HEAD_DIM ** -0.5)
        qkv_ref[pl.ds(r0, c), GROUP_W:2 * GROUP_W] = qk[:, GROUP_W:]
        qkv_ref[pl.ds(r0, c), 2 * GROUP_W:qkv_w] = y[:, 2 * GROUP_W:]
        return carry

    lax.fori_loop(0, n_chunks, conv_body, 0)

    def chunk(rows, d, sel, mask, reverse):
        q = qkv_ref[rows, 0:GROUP_W]
        k = qkv_ref[rows, GROUP_W:2 * GROUP_W]
        v = qkv_ref[rows, 2 * GROUP_W:qkv_w]
        ba = p_ref[0, rows, 4 * GROUP_W:4 * GROUP_W + LANES]
        return _delta_chunk(q, k, v, ba, sel, mask, expand_ref, nega_ref, dtb_ref, state_ref, d, c, reverse)

    def body(n, carry):
        rf = pl.ds(pl.multiple_of(n * c, c), c)
        of_ref[rf, :] = chunk(rf, 0, self_ref, maskf_ref, False)
        rb = pl.ds(pl.multiple_of((n_chunks - 1 - n) * c, c), c)
        ob_ref[rb, :] = chunk(rb, 1, selb_ref, maskb_ref, True)
        return carry

    lax.fori_loop(0, n_chunks, body, 0)

    rbk = min(t, 256)

    def fin(i, carry):
        rows = pl.ds(pl.multiple_of(i * rbk, rbk), rbk)
        o = of_ref[rows, :] + ob_ref[rows, :]
        gate = p_ref[0, rows, 3 * GROUP_W:4 * GROUP_W]
        y_ref[0, rows, :] = _head_norm_gate(o, gate, onorm_ref[...], ones_ref[...]).astype(BF16)
        return carry

    lax.fori_loop(0, t // rbk, fin, 0)
    sout_ref[0] = state_ref[...]


def _delta_mixer(p, conv_w, expand, nega, dtb, onorm, s0):
    b, t, w = p.shape
    c = DN_CHUNK
    sel_f, mask_f = _dn_consts(c, False)
    sel_b, mask_b = _dn_consts(c, True)
    consts = [sel_f, sel_b, mask_f, mask_b, _ones_blockdiag(), onorm, conv_w, expand, nega, dtb]
    state_shape = (2, N_HEADS, HEAD_DIM, HEAD_DIM)
    return pl.pallas_call(
        functools.partial(_delta_mixer_kernel, t, c),
        grid=(b,),
        in_specs=[pl.BlockSpec((1, t, w), lambda i: (i, 0, 0), pipeline_mode=pl.Buffered(1))]
                 + [_full_spec(a.shape) for a in consts]
                 + [pl.BlockSpec((1,) + state_shape, lambda i: (i, 0, 0, 0, 0))],
        out_specs=[pl.BlockSpec((1, t, GROUP_W), lambda i: (i, 0, 0)),
                   pl.BlockSpec((1,) + state_shape, lambda i: (i, 0, 0, 0, 0))],
        out_shape=[jax.ShapeDtypeStruct((b, t, GROUP_W), BF16),
                   jax.ShapeDtypeStruct((b,) + state_shape, F32)],
        scratch_shapes=[pltpu.VMEM((t, 3 * GROUP_W), F32), pltpu.VMEM((t, GROUP_W), F32),
                        pltpu.VMEM((t, GROUP_W), F32), pltpu.VMEM(state_shape, F32)],
        compiler_params=_params(("arbitrary",)),
        name="delta_mixer",
    )(p, *consts, s0)


def _s5_kernel(t, tc, u_ref, bh_ref, bl_ref, cm_ref, lbar_ref, dskip_ref, gluw_ref, glub_ref, h0_ref,
               y_ref, sout_ref, x_ref, yf_ref, yb_ref, carry_ref):
    n_chunks = t // tc
    n_lane_blocks = S5_STATE // LANES
    steps = int(np.log2(tc))
    carry_ref[...] = h0_ref[0]
    row_id = lax.broadcasted_iota(jnp.int32, (tc, LANES), 0)

    def scan_dir(rows, d, y_out_ref):
        u_hi, u_lo = _split2(u_ref[0, rows, :])
        for j in range(n_lane_blocks):
            x_ref[j] = (_dot(u_hi, bh_ref[d, j]) + _dot(u_hi, bl_ref[d, j]) + _dot(u_lo, bh_ref[d, j]))
        y_out_ref[rows, :] = jnp.zeros((tc, GROUP_W), F32)

        def lane_block(j, carry):
            bu = x_ref[j]
            xr, xi = bu[:, 0:LANES], bu[:, LANES:2 * LANES]
            lb = lbar_ref[d, j]
            ar, ai = lb[0:1, :], lb[1:2, :]
            cr, ci = carry_ref[d, 0, j], carry_ref[d, 1, j]
            first = row_id == (tc - 1 if d else 0)
            xr = xr + jnp.where(first, ar * cr - ai * ci, 0.0)
            xi = xi + jnp.where(first, ar * ci + ai * cr, 0.0)
            for s in range(steps):
                sh = 1 << s
                if d == 0:
                    keep = row_id >= sh
                    sr = jnp.where(keep, pltpu.roll(xr, sh, axis=0), 0.0)
                    si = jnp.where(keep, pltpu.roll(xi, sh, axis=0), 0.0)
                else:
                    keep = row_id < tc - sh
                    sr = jnp.where(keep, pltpu.roll(xr, tc - sh, axis=0), 0.0)
                    si = jnp.where(keep, pltpu.roll(xi, tc - sh, axis=0), 0.0)
                xr, xi = xr + ar * sr - ai * si, xi + ar * si + ai * sr
                ar, ai = ar * ar - ai * ai, 2.0 * ar * ai
            last = 0 if d else tc - 1
            carry_ref[d, 0, j] = xr[last:last + 1, :]
            carry_ref[d, 1, j] = xi[last:last + 1, :]
            xc = jnp.concatenate([xr, xi], axis=1).astype(BF16)
            y_out_ref[rows, :] += _dot(xc, cm_ref[j])
            return carry

        lax.fori_loop(0, n_lane_blocks, lane_block, 0)

    def body(n, carry):
        scan_dir(pl.ds(pl.multiple_of(n * tc, tc), tc), 0, yf_ref)
        scan_dir(pl.ds(pl.multiple_of((n_chunks - 1 - n) * tc, tc), tc), 1, yb_ref)
        return carry

    lax.fori_loop(0, n_chunks, body, 0)

    def fin(i, carry):
        rows = pl.ds(pl.multiple_of(i * tc, tc), tc)
        u = u_ref[0, rows, :]
        y = _gelu_tanh(yf_ref[rows, :] + yb_ref[rows, :] + dskip_ref[...] * u)
        z = _dot(y.astype(BF16), gluw_ref[...]) + glub_ref[...]
        y_ref[0, rows, :] = (y * _sigmoid(z)).astype(BF16)
        return carry

    lax.fori_loop(0, n_chunks, fin, 0)
    sout_ref[0] = carry_ref[...]


def _s5_mixer(u, b_hi, b_lo, cmat, lbar, dskip, glu_w, glu_b, h0):
    b, t, _ = u.shape
    tc = min(S5_CHUNK, t)
    nlb = S5_STATE // LANES
    consts = [b_hi, b_lo, cmat, lbar, dskip, glu_w, glu_b]
    st_shape = (2, 2, nlb, 1, LANES)
    return pl.pallas_call(
        functools.partial(_s5_kernel, t, tc),
        grid=(b,),
        in_specs=[pl.BlockSpec((1, t, GROUP_W), lambda i: (i, 0, 0))]
                 + [_full_spec(a.shape) for a in consts]
                 + [pl.BlockSpec((1,) + st_shape, lambda i: (i, 0, 0, 0, 0, 0))],
        out_specs=[pl.BlockSpec((1, t, GROUP_W), lambda i: (i, 0, 0)),
                   pl.BlockSpec((1,) + st_shape, lambda i: (i, 0, 0, 0, 0, 0))],
        out_shape=[jax.ShapeDtypeStruct((b, t, GROUP_W), BF16),
                   jax.ShapeDtypeStruct((b,) + st_shape, F32)],
        scratch_shapes=[pltpu.VMEM((nlb, tc, 2 * LANES), F32), pltpu.VMEM((t, GROUP_W), F32),
                        pltpu.VMEM((t, GROUP_W), F32), pltpu.VMEM(st_shape, F32)],
        compiler_params=_params(("arbitrary",)),
        name="s5_mixer",
    )(u, *consts, h0)


def _outproj_kernel(x_ref, mod_ref, ya_ref, yb_ref, yc_ref, yd_ref, w_ref, o_ref):
    y = (_dot(ya_ref[0], w_ref[0]) + _dot(yb_ref[0], w_ref[1])
         + _dot(yc_ref[0], w_ref[2]) + _dot(yd_ref[0], w_ref[3]))
    o_ref[0] = x_ref[0] + mod_ref[0][:, 2 * D_MODEL:3 * D_MODEL] * y


def _out_projection(x, mod, ys, w_out):
    b, t, _ = x.shape
    tb = min(TOKEN_BLOCK, t)
    tok = lambda n: pl.BlockSpec((1, tb, n), lambda i, j: (i, j, 0))
    return pl.pallas_call(
        _outproj_kernel,
        grid=(b, t // tb),
        in_specs=[tok(D_MODEL), pl.BlockSpec((1, 1, 6 * D_MODEL), lambda i, j: (i, 0, 0))]
                 + [tok(GROUP_W)] * 4 + [_full_spec(w_out.shape)],
        out_specs=tok(D_MODEL),
        out_shape=jax.ShapeDtypeStruct(x.shape, F32),
        compiler_params=_params(("arbitrary", "arbitrary")),
        name="out_projection",
    )(x, mod, *ys, w_out)


def _topk_rows(s, k):
    r, n = s.shape
    rows = lax.broadcasted_iota(jnp.int32, (r, n), 0).astype(F32)
    slot = lax.broadcasted_iota(jnp.int32, (k, n), 0)
    vals = jnp.zeros((k, n), F32)
    ids = jnp.zeros((k, n), F32)
    for it in range(k):
        m = jnp.max(s, axis=0, keepdims=True)
        idx = jnp.min(jnp.where(s == m, rows, float(r)), axis=0, keepdims=True)
        vals = jnp.where(slot == it, m, vals)
        ids = jnp.where(slot == it, idx, ids)
        s = jnp.where(rows == idx, -jnp.inf, s)
    return vals, ids


def _take_rows(table, idx, k):
    out = jnp.zeros(idx.shape, F32)
    for r in range(k):
        out = jnp.where(idx == float(r), table[r:r + 1, :], out)
    return out


def _peer_route_kernel(x_ref, mod_ref, g_ref, wqh_ref, wql_ref, kh_ref, kl_ref,
                       h_ref, a_ref, c_ref, gate_ref):
    mod = mod_ref[0]
    h = _rms_modulate(x_ref[0], g_ref[...], mod[:, 3 * D_MODEL:4 * D_MODEL], mod[:, 4 * D_MODEL:5 * D_MODEL])
    h_hi, h_lo = _split2(h)
    h_ref[0] = h_hi
    wq_hi = wqh_ref[...]
    q_t = _dot_nt(wq_hi, h_hi) + _dot_nt(wq_hi, h_lo) + _dot_nt(wql_ref[...], h_hi)
    half = PEER_DQ // 2
    a_rows, c_rows, g_rows = [], [], []
    for head in range(PEER_HEADS):
        vals, ids = [], []
        for p in range(2):
            hp = head * 2 + p
            q_hi, q_lo = _split2(q_t[hp * half:(hp + 1) * half, :])
            s = _dot(kh_ref[hp], q_hi) + _dot(kh_ref[hp], q_lo) + _dot(kl_ref[hp], q_hi)
            v, i = _topk_rows(s, PEER_TOPK)
            vals.append(v)
            ids.append(i)
        cand = jnp.concatenate([vals[0][r:r + 1, :] + vals[1] for r in range(PEER_TOPK)], axis=0)
        sc, ci = _topk_rows(cand, PEER_TOPK)
        r1 = jnp.floor(ci * (1.0 / PEER_TOPK))
        r2 = ci - r1 * PEER_TOPK
        a_rows.append(_take_rows(ids[0], r1, PEER_TOPK))
        c_rows.append(_take_rows(ids[1], r2, PEER_TOPK))
        e = jnp.exp(sc - sc[0:1, :])
        g_rows.append(e / jnp.sum(e, axis=0, keepdims=True))
    a_ref[0] = jnp.concatenate(a_rows, axis=0).T.astype(jnp.int32)
    c_ref[0] = jnp.concatenate(c_rows, axis=0).T.astype(jnp.int32)
    gate_ref[0] = jnp.concatenate(g_rows, axis=0).T


def _peer_route(x, mod, gain, wq_hi, wq_lo, keys_hi, keys_lo):
    b, t, _ = x.shape
    tb = min(TOKEN_BLOCK, t)
    tok = lambda n: pl.BlockSpec((1, tb, n), lambda i, j: (i, j, 0))
    consts = [gain, wq_hi, wq_lo, keys_hi, keys_lo]
    return pl.pallas_call(
        _peer_route_kernel,
        grid=(b, t // tb),
        in_specs=[tok(D_MODEL), pl.BlockSpec((1, 1, 6 * D_MODEL), lambda i, j: (i, 0, 0))]
                 + [_full_spec(a.shape) for a in consts],
        out_specs=[tok(D_MODEL), tok(PEER_SEL), tok(PEER_SEL), tok(PEER_SEL)],
        out_shape=[jax.ShapeDtypeStruct((b, t, D_MODEL), BF16),
                   jax.ShapeDtypeStruct((b, t, PEER_SEL), jnp.int32),
                   jax.ShapeDtypeStruct((b, t, PEER_SEL), jnp.int32),
                   jax.ShapeDtypeStruct((b, t, PEER_SEL), F32)],
        compiler_params=_params(("arbitrary", "arbitrary")),
        name="peer_route",
    )(x, mod, *consts)


def _peer_expert_kernel(tb, n_slabs, x_ref, mod_ref, h_ref, a_ref, c_ref, gate_ref, ut_ref, v_ref,
                        o_ref, gmat_ref, acc_ref):
    j = pl.program_id(2)
    nk = PEER_NKEYS

    @pl.when(j == 0)
    def _():
        acc_ref[...] = jnp.zeros_like(acc_ref)
        key_id = lax.broadcasted_iota(jnp.int32, (nk, PEER_SEL), 0)

        def build(n, carry):
            a_row = a_ref[0, pl.ds(n, 1), :]
            c_row = c_ref[0, pl.ds(n, 1), :]
            g_row = gate_ref[0, pl.ds(n, 1), :]
            row_sel = jnp.where(key_id == a_row, 1.0, 0.0).astype(BF16)
            col_sel = jnp.where(key_id == c_row, g_row, 0.0).astype(BF16)
            gmat_ref[pl.ds(pl.multiple_of(n * nk, nk), nk), :] = _dot_nt(row_sel, col_sel)
            return carry

        lax.fori_loop(0, tb, build, 0)

    h = h_ref[0]

    def slab_pair(i, carry):
        a0 = (j * n_slabs + 2 * i)
        cols = pl.ds(pl.multiple_of(i * 2 * nk, 2 * nk), 2 * nk)
        s = _dot(h, ut_ref[:, cols])
        g = jnp.concatenate([gmat_ref[pl.ds(a0, tb, stride=nk), :],
                             gmat_ref[pl.ds(a0 + 1, tb, stride=nk), :]], axis=1)
        w = (g * _gelu_tanh(s)).astype(BF16)
        acc_ref[...] += _dot(w, v_ref[cols, :])
        return carry

    lax.fori_loop(0, n_slabs // 2, slab_pair, 0)

    @pl.when(j == pl.num_programs(2) - 1)
    def _():
        o_ref[0] = x_ref[0] + mod_ref[0][:, 5 * D_MODEL:6 * D_MODEL] * acc_ref[...]


def _peer_experts(x, mod, h, a, c, gate, u_t, v_tab):
    b, t, _ = x.shape
    tb = min(TOKEN_BLOCK, t)
    n_exp = u_t.shape[1]
    eb = PEER_EXPERT_BLOCK
    n_slabs = eb // PEER_NKEYS
    tok = lambda n: pl.BlockSpec((1, tb, n), lambda i, k, j: (i, k, 0))
    return pl.pallas_call(
        functools.partial(_peer_expert_kernel, tb, n_slabs),
        grid=(b, t // tb, n_exp // eb),
        in_specs=[tok(D_MODEL), pl.BlockSpec((1, 1, 6 * D_MODEL), lambda i, k, j: (i, 0, 0)),
                  tok(D_MODEL), tok(PEER_SEL), tok(PEER_SEL), tok(PEER_SEL),
                  pl.BlockSpec((D_MODEL, eb), lambda i, k, j: (0, j)),
                  pl.BlockSpec((eb, D_MODEL), lambda i, k, j: (j, 0))],
        out_specs=tok(D_MODEL),
        out_shape=jax.ShapeDtypeStruct(x.shape, F32),
        scratch_shapes=[pltpu.VMEM((tb * PEER_NKEYS, PEER_NKEYS), F32), pltpu.VMEM((tb, D_MODEL), F32)],
        compiler_params=_params(("arbitrary", "arbitrary", "arbitrary")),
        name="peer_experts",
    )(x, mod, h, a, c, gate, u_t, v_tab)


def _final_norm_kernel(x_ref, g_ref, o_ref):
    x = x_ref[0]
    ms = jnp.mean(x * x, axis=-1, keepdims=True)
    o_ref[0] = x * lax.rsqrt(ms + EPS) * g_ref[...]


def _final_norm(x, gain):
    b, t, _ = x.shape
    tb = min(TOKEN_BLOCK, t)
    tok = pl.BlockSpec((1, tb, D_MODEL), lambda i, j: (i, j, 0))
    return pl.pallas_call(
        _final_norm_kernel,
        grid=(b, t // tb),
        in_specs=[tok, _full_spec((1, D_MODEL))],
        out_specs=tok,
        out_shape=jax.ShapeDtypeStruct(x.shape, F32),
        compiler_params=_params(("arbitrary", "arbitrary")),
        name="final_norm",
    )(x, gain)


def _hi_lo(w):
    hi = w.astype(BF16)
    return hi, (w - hi.astype(F32)).astype(BF16)


def _pad_cols(w, n):
    return jnp.pad(w, ((0, 0), (0, n - w.shape[1])))


def _layer_weights(l, w_in, w_out, gla_gk_up, gla_gk_bias, gla_onorm_g, dn_conv_w, dn_A_log, dn_dt_bias,
                   dn_onorm_g, s5_lambda_re, s5_lambda_im, s5_log_dt, s5_B_re, s5_B_im, s5_C_re, s5_C_im,
                   s5_D, s5_glu_w, s5_glu_b, lower_bound, hgrn_onorm_g, peer_wq, peer_keys, peer_u, peer_v):
    gw = GROUP_W
    sizes = (gw, gw, gw, gw, 2 * GLA_LOWRANK, gw, gw, gw, gw, 2 * N_HEADS, 2 * N_HEADS, gw, gw, gw, gw, 2 * gw)
    offs = np.concatenate([[0], np.cumsum(sizes)])
    col = lambda a, b: w_in[l][:, int(offs[a]):int(offs[b])]
    wl = {}
    wl["w_gla"] = _pad_cols(col(0, 5), GLA_W).astype(BF16)
    wl["w_dn"] = _pad_cols(col(5, 11), DN_W).astype(BF16)
    wl["w_s5"] = col(11, 12).astype(BF16)
    wl["w_hg"] = col(12, 16).astype(BF16)
    wl["w_out"] = w_out[l].reshape(4, gw, D_MODEL).astype(BF16)
    tile_head = lambda g: jnp.tile(g, N_HEADS).reshape(1, gw)
    up = jnp.zeros((2, LANES, gw), F32)
    for d in range(2):
        up = up.at[d, d * GLA_LOWRANK:(d + 1) * GLA_LOWRANK, :].set(gla_gk_up[l, d])
    wl["gla_up_hi"], wl["gla_up_lo"] = _hi_lo(up)
    wl["gla_bias"] = gla_gk_bias[l].reshape(2, 1, gw)
    wl["gla_onorm"] = tile_head(gla_onorm_g[l])
    expand = np.zeros((2, LANES, 2 * gw), np.float32)
    for d in range(2):
        for h in range(N_HEADS):
            expand[d, d * N_HEADS + h, h * HEAD_DIM:(h + 1) * HEAD_DIM] = 1.0
            expand[d, 2 * N_HEADS + d * N_HEADS + h, gw + h * HEAD_DIM:gw + (h + 1) * HEAD_DIM] = 1.0
    wl["dn_expand"] = jnp.asarray(expand, BF16)
    wl["dn_conv"] = jnp.pad(dn_conv_w[l], ((0, SUBLANES - 3), (0, 0)))
    wl["dn_nega"] = jnp.repeat(-jnp.exp(dn_A_log[l]), HEAD_DIM, axis=-1).reshape(2, 1, gw)
    wl["dn_dtb"] = jnp.repeat(dn_dt_bias[l], HEAD_DIM, axis=-1).reshape(2, 1, gw)
    wl["dn_onorm"] = tile_head(dn_onorm_g[l])
    nlb = S5_STATE // LANES
    lam_re, lam_im = s5_lambda_re[l].astype(F32), s5_lambda_im[l].astype(F32)
    dt = jnp.exp(s5_log_dt[l].astype(F32))[:, :, None]
    mag = jnp.exp(lam_re * dt)
    lbar_re, lbar_im = mag * jnp.cos(lam_im * dt), mag * jnp.sin(lam_im * dt)
    den = lam_re * lam_re + lam_im * lam_im
    num_re, num_im = lbar_re - 1.0, lbar_im
    coef_re = (num_re * lam_re + num_im * lam_im) / den
    coef_im = (num_im * lam_re - num_re * lam_im) / den
    b_re, b_im = s5_B_re[l].astype(F32), s5_B_im[l].astype(F32)
    bb_re = coef_re[..., None] * b_re - coef_im[..., None] * b_im
    bb_im = coef_re[..., None] * b_im + coef_im[..., None] * b_re
    eye_g = jnp.eye(S5_G, dtype=F32)
    bd_in = lambda m: jnp.einsum('dgpc,gk->dgckp', m, eye_g).reshape(2, gw, S5_STATE)
    blk = lambda m: m.reshape(2, gw, nlb, LANES).transpose(0, 2, 1, 3)
    b_cat = jnp.concatenate([blk(bd_in(bb_re)), blk(bd_in(bb_im))], axis=-1)
    wl["s5_b_hi"], wl["s5_b_lo"] = _hi_lo(b_cat)
    bd_out = lambda m: jnp.einsum('gcp,gk->gpkc', m.astype(F32), eye_g).reshape(S5_STATE, gw)
    c_re, c_im = bd_out(s5_C_re[l]), bd_out(s5_C_im[l])
    wl["s5_c"] = jnp.concatenate([c_re.reshape(nlb, LANES, gw), -c_im.reshape(nlb, LANES, gw)],
                                 axis=1).astype(BF16)
    wl["s5_lbar"] = jnp.stack([lbar_re.reshape(2, nlb, LANES), lbar_im.reshape(2, nlb, LANES)], axis=2)
    wl["s5_d"] = s5_D[l].reshape(1, gw)
    wl["s5_glu_w"] = s5_glu_w[l].astype(BF16)
    wl["s5_glu_b"] = s5_glu_b[l].reshape(1, gw)
    wl["hg_loglb"] = jnp.log(lower_bound).reshape(1, gw)
    wl["hg_log1m"] = jnp.log1p(-lower_bound).reshape(1, gw)
    wl["hg_1m"] = (1.0 - lower_bound).reshape(1, gw)
    wl["hg_onorm"] = tile_head(hgrn_onorm_g[l])
    wl["wq_hi"], wl["wq_lo"] = _hi_lo(peer_wq[l].T)
    wl["keys_hi"], wl["keys_lo"] = _hi_lo(peer_keys[l].reshape(PEER_HEADS * 2, PEER_NKEYS, PEER_DQ // 2))
    wl["u_t"] = peer_u[l].astype(BF16).T
    wl["v_tab"] = peer_v[l].astype(BF16)
    return wl


def _to_column_major(x):
    b, n, d = x.shape
    return x.reshape(b, n // GRID_W, GRID_W, d).transpose(0, 2, 1, 3).reshape(b, n, d)


def _to_row_major(x):
    b, n, d = x.shape
    return x.reshape(b, GRID_W, n // GRID_W, d).transpose(0, 2, 1, 3).reshape(b, n, d)


def _trunk_layer(x, mod, wl, norm1_g, norm2_g, states):
    st_gla, st_dn, st_s5, st_hg = states
    p_gla, p_dn, p_s5, p_hg = _in_projection(x, mod, norm1_g, [wl["w_gla"], wl["w_dn"], wl["w_s5"], wl["w_hg"]])
    y_a, n_gla = _decay_mixer("gla", p_gla, wl["gla_up_hi"], wl["gla_up_lo"], wl["gla_bias"], wl["gla_onorm"], st_gla)
    y_b, n_dn = _delta_mixer(p_dn, wl["dn_conv"], wl["dn_expand"], wl["dn_nega"], wl["dn_dtb"], wl["dn_onorm"], st_dn)
    y_c, n_s5 = _s5_mixer(p_s5, wl["s5_b_hi"], wl["s5_b_lo"], wl["s5_c"], wl["s5_lbar"], wl["s5_d"],
                          wl["s5_glu_w"], wl["s5_glu_b"], st_s5)
    y_d, n_hg = _decay_mixer("hgrn", p_hg, wl["hg_loglb"], wl["hg_log1m"], wl["hg_1m"], wl["hg_onorm"], st_hg)
    x = _out_projection(x, mod, [y_a, y_b, y_c, y_d], wl["w_out"])
    h, a, c, gate = _peer_route(x, mod, norm2_g, wl["wq_hi"], wl["wq_lo"], wl["keys_hi"], wl["keys_lo"])
    x = _peer_experts(x, mod, h, a, c, gate, wl["u_t"], wl["v_tab"])
    return x, (n_gla, n_dn, n_s5, n_hg)


def kernel(x_prompt, x_sample, state_gla, state_delta, state_s5_re, state_s5_im, state_hgrn, c, c_ctx,
           mod_w, mod_b, norm1_g, norm2_g, w_in, w_out, gla_gk_up, gla_gk_bias, gla_onorm_g,
           dn_conv_w, dn_A_log, dn_dt_bias, dn_onorm_g, s5_lambda_re, s5_lambda_im, s5_log_dt,
           s5_B_re, s5_B_im, s5_C_re, s5_C_im, s5_D, s5_glu_w, s5_glu_b, hgrn_lb_logits,
           hgrn_onorm_g, peer_wq, peer_keys, peer_u, peer_v, final_norm_g):
    bp, bs = x_prompt.shape[0], x_sample.shape[0]
    nlb = S5_STATE // LANES
    lb_cum = jnp.cumsum(jax.nn.softmax(hgrn_lb_logits.astype(F32), axis=0), axis=0)
    lower_bounds = lb_cum - lb_cum[0:1]

    rows = 2 * SUBLANES
    cc = jnp.concatenate([c, c_ctx[None, :], jnp.zeros((rows - bs - 1, D_MODEL), F32)], axis=0)
    mod_all = _modulation(cc, mod_w, mod_b)

    zeros_hh = jnp.zeros((bp, 2, N_HEADS, HEAD_DIM, HEAD_DIM), F32)
    zeros_s5 = jnp.zeros((bp, 2, 2, nlb, 1, LANES), F32)
    swap = lambda s: jnp.swapaxes(s, -1, -2)

    xp, xs = x_prompt, x_sample
    ctx_states = []
    for l in range(DEPTH):
        wl = _layer_weights(l, w_in, w_out, gla_gk_up, gla_gk_bias, gla_onorm_g, dn_conv_w, dn_A_log,
                            dn_dt_bias, dn_onorm_g, s5_lambda_re, s5_lambda_im, s5_log_dt, s5_B_re, s5_B_im,
                            s5_C_re, s5_C_im, s5_D, s5_glu_w, s5_glu_b, lower_bounds[l], hgrn_onorm_g,
                            peer_wq, peer_keys, peer_u, peer_v)
        n1, n2 = norm1_g[l].reshape(1, D_MODEL), norm2_g[l].reshape(1, D_MODEL)
        mod_ctx = jnp.broadcast_to(mod_all[l, bs].reshape(1, 1, 6 * D_MODEL), (bp, 1, 6 * D_MODEL))
        mod_lat = mod_all[l, 0:bs].reshape(bs, 1, 6 * D_MODEL)
        xp, st_ctx = _trunk_layer(xp, mod_ctx, wl, n1, n2, (zeros_hh, zeros_hh, zeros_s5, zeros_hh))
        ctx_states.append(st_ctx)
        cached_s5 = jnp.stack([state_s5_re[:, l], state_s5_im[:, l]], axis=2).reshape(bs, 2, 2, nlb, 1, LANES)
        cached = (swap(state_gla[:, l]), state_delta[:, l], cached_s5, swap(state_hgrn[:, l]))
        if l % 2 == 1:
            xs = _to_column_major(xs)
        xs, _ = _trunk_layer(xs, mod_lat, wl, n1, n2, cached)
        if l % 2 == 1:
            xs = _to_row_major(xs)

    gain = final_norm_g.reshape(1, D_MODEL)
    y_prompt = _final_norm(xp, gain)
    y_sample = _final_norm(xs, gain)
    new_gla = jnp.stack([swap(s[0]) for s in ctx_states], axis=1)
    new_dn = jnp.stack([s[1] for s in ctx_states], axis=1)
    s5 = jnp.stack([s[2].reshape(bp, 2, 2, S5_G, S5_P) for s in ctx_states], axis=1)
    new_hg = jnp.stack([swap(s[3]) for s in ctx_states], axis=1)
    return (y_prompt, y_sample, new_gla, new_dn, s5[:, :, :, 0], s5[:, :, :, 1], new_hg)
```

```python
import functools

import numpy as np
import jax
import jax.numpy as jnp
from jax import lax
from jax.experimental import pallas as pl
from jax.experimental.pallas import tpu as pltpu

F32 = jnp.float32
BF16 = jnp.bfloat16

D_MODEL = 1024
DEPTH = 2
GRID_W = 64
GROUP_W = 256
HEAD_DIM = 64
N_HEADS = GROUP_W // HEAD_DIM
GLA_LOWRANK = 16
GLA_GATE_NORM = 16.0
DN_CHUNK = 64
S5_GROUP = 16
S5_G = GROUP_W // S5_GROUP
S5_P = 64
S5_STATE = S5_G * S5_P
PEER_HEADS = 8
PEER_NKEYS = 128
PEER_DQ = 256
PEER_TOPK = 16
PEER_SEL = PEER_HEADS * PEER_TOPK
EPS = 1e-6

LANES = 128
SUBLANES = 8
VMEM_LIMIT = 56 * 1024 * 1024

GLA_CHUNK = 64
S5_CHUNK = 256
TOKEN_BLOCK = 256
PEER_EXPERT_BLOCK = 2048
GLA_W = 4 * GROUP_W + LANES
DN_W = 4 * GROUP_W + LANES
HG_W = 5 * GROUP_W


def _dot(a, b):
    return jnp.dot(a, b, preferred_element_type=F32)


def _dot_nt(a, b):
    return lax.dot_general(a, b, (((1,), (1,)), ((), ())), preferred_element_type=F32)


def _dot_tn(a, b):
    return lax.dot_general(a, b, (((0,), (0,)), ((), ())), preferred_element_type=F32)


def _split2(x):
    hi = x.astype(BF16)
    lo = (x - hi.astype(F32)).astype(BF16)
    return hi, lo


def _split3(x):
    hi = x.astype(BF16)
    r = x - hi.astype(F32)
    mid = r.astype(BF16)
    lo = (r - mid.astype(F32)).astype(BF16)
    return hi, mid, lo


def _sel_dot(m01x3, x):
    return _dot(m01x3, jnp.concatenate(_split3(x), axis=0))


def _dot_sel(x, m01):
    hi, mid, lo = _split3(x)
    return _dot(hi, m01) + _dot(mid, m01) + _dot(lo, m01)


def _dot_hl(a, b_hi, b_lo):
    a_hi, a_lo = _split2(a)
    return _dot(a_hi, b_hi) + _dot(a_hi, b_lo) + _dot(a_lo, b_hi)


def _head_sum(x, ones_bd):
    hi, lo = _split2(x)
    return _dot(hi, ones_bd) + _dot(lo, ones_bd)


def _head_blockdiag(x16, ones_bd):
    return jnp.tile(x16, (N_HEADS, 1)) * ones_bd


def _per_head_mm(a, x, ones_bd):
    a_hi, a_lo = _split2(a)
    x_hi, x_lo = _split2(x)
    bd_hi = _head_blockdiag(x_hi, ones_bd)
    return _dot(a_hi, bd_hi) + _dot(a_hi, _head_blockdiag(x_lo, ones_bd)) + _dot(a_lo, bd_hi)


def _head_diag_blocks(full):
    lane_head = lax.broadcasted_iota(jnp.int32, (HEAD_DIM, GROUP_W), 1) // HEAD_DIM
    out = None
    for h in range(N_HEADS):
        blk = jnp.where(lane_head == h, full[h * HEAD_DIM:(h + 1) * HEAD_DIM, :], 0.0)
        out = blk if out is None else out + blk
    return out


def _log_sigmoid(x):
    return jnp.minimum(x, 0.0) - jnp.log(1.0 + jnp.exp(-jnp.abs(x)))


def _softplus(x):
    return jnp.maximum(x, 0.0) + jnp.log(1.0 + jnp.exp(-jnp.abs(x)))


def _sigmoid(x):
    return 1.0 / (1.0 + jnp.exp(-x))


def _silu(x):
    return x * _sigmoid(x)


def _gelu_tanh(x):
    return 0.5 * x * (1.0 + jnp.tanh(0.7978845608028654 * (x + 0.044715 * x * x * x)))


def _rms_modulate(x, gain, shift, scale):
    ms = jnp.mean(x * x, axis=-1, keepdims=True)
    return x * lax.rsqrt(ms + EPS) * gain * (1.0 + scale) + shift


def _full_spec(shape):
    zeros = (0,) * len(shape)
    return pl.BlockSpec(shape, lambda *_: zeros)


def _params(sem):
    return pltpu.CompilerParams(dimension_semantics=sem, vmem_limit_bytes=VMEM_LIMIT)


def _scan_consts(c, reverse):
    levels = int(np.log2(c))
    idx = np.arange(c)
    i, t = idx[:, None], idx[None, :]
    sel, masks = [], []
    for lv in range(levels):
        s = 1 << lv
        right = (idx // s) % 2 == 1
        same_blk = (i // s) == (t // s)
        sel.append(np.where(right[:, None], same_blk & (t <= i), same_blk & (t > i)))
        same_parent = (i // (2 * s)) == (t // (2 * s))
        masks.append(same_parent & right[:, None] & ~right[None, :])
    sel.append(t <= i)
    sel.append(t > i)
    masks.append(i == t)
    if reverse:
        sel = [m[::-1, ::-1] for m in sel]
        masks = [m[::-1, ::-1] for m in masks]
    sel.append(np.ones((SUBLANES, c), bool))
    sel = np.tile(np.concatenate(sel, axis=0).astype(np.float32), (1, 3))
    masks = np.tile(np.stack(masks).astype(np.float32), (1, 1, N_HEADS))
    return jnp.asarray(sel, BF16), jnp.asarray(masks, F32)


def _dn_consts(c, reverse):
    levels = int(np.log2(c))
    idx = np.arange(c)
    i, t = idx[:, None], idx[None, :]
    incl, tail = t <= i, t > i
    masks = []
    for lv in range(levels):
        s = 1 << lv
        right = (idx // s) % 2 == 1
        masks.append(((i // (2 * s)) == (t // (2 * s))) & right[:, None] & ~right[None, :])
    masks += [incl, incl & (i != t), i == t]
    if reverse:
        incl, tail = incl[::-1, ::-1], tail[::-1, ::-1]
        masks = [m[::-1, ::-1] for m in masks]
    sel = np.tile(np.concatenate([incl, tail, np.ones((SUBLANES, c), bool)], axis=0).astype(np.float32), (1, 3))
    masks = np.tile(np.stack(masks).astype(np.float32), (1, 1, N_HEADS))
    return jnp.asarray(sel, BF16), jnp.asarray(masks, F32)


def _ones_blockdiag():
    h = np.arange(GROUP_W) // HEAD_DIM
    return jnp.asarray((h[:, None] == h[None, :]).astype(np.float32), BF16)


def _mod_kernel(c_ref, w_ref, b_ref, o_ref):
    a_hi, a_lo = _split2(_silu(c_ref[...]))
    w_hi, w_lo = _split2(w_ref[0])
    o_ref[0] = _dot(a_hi, w_hi) + _dot(a_hi, w_lo) + _dot(a_lo, w_hi) + b_ref[0]


def _modulation(cc, mod_w, mod_b):
    rows = cc.shape[0]
    nb = 6
    return pl.pallas_call(
        _mod_kernel,
        grid=(DEPTH, nb),
        in_specs=[pl.BlockSpec((rows, D_MODEL), lambda l, j: (0, 0)),
                  pl.BlockSpec((1, D_MODEL, D_MODEL), lambda l, j: (l, 0, j)),
                  pl.BlockSpec((1, 1, D_MODEL), lambda l, j: (l, 0, j))],
        out_specs=pl.BlockSpec((1, rows, D_MODEL), lambda l, j: (l, 0, j)),
        out_shape=jax.ShapeDtypeStruct((DEPTH, rows, 6 * D_MODEL), F32),
        compiler_params=_params(("arbitrary", "arbitrary")),
        name="modulation",
    )(cc, mod_w, mod_b.reshape(DEPTH, 1, 6 * D_MODEL))


def _inproj_kernel(x_ref, mod_ref, g_ref, wa_ref, wb_ref, wc_ref, wd_ref, oa_ref, ob_ref, oc_ref, od_ref):
    mod = mod_ref[0]
    h = _rms_modulate(x_ref[0], g_ref[...], mod[:, 0:D_MODEL], mod[:, D_MODEL:2 * D_MODEL]).astype(BF16)
    oa_ref[0] = _dot(h, wa_ref[...])
    ob_ref[0] = _dot(h, wb_ref[...])
    oc_ref[0] = _dot(h, wc_ref[...])
    od_ref[0] = _dot(h, wd_ref[...])


def _in_projection(x, mod, gain, weights):
    b, t, _ = x.shape
    tb = min(TOKEN_BLOCK, t)
    widths = [w.shape[1] for w in weights]
    return pl.pallas_call(
        _inproj_kernel,
        grid=(b, t // tb),
        in_specs=[pl.BlockSpec((1, tb, D_MODEL), lambda i, j: (i, j, 0)),
                  pl.BlockSpec((1, 1, 6 * D_MODEL), lambda i, j: (i, 0, 0)),
                  _full_spec((1, D_MODEL))] + [_full_spec(w.shape) for w in weights],
        out_specs=[pl.BlockSpec((1, tb, n), lambda i, j: (i, j, 0)) for n in widths],
        out_shape=[jax.ShapeDtypeStruct((b, t, n), F32) for n in widths],
        compiler_params=_params(("arbitrary", "arbitrary")),
        name="in_projection",
    )(x, mod, gain, *weights)


def _decay_scan_prepare(probs, ones_bd, c):
    levels = int(np.log2(c))
    exs = [jnp.exp(_sel_dot(pr["sel"][...], pr["g"])) for pr in probs]
    attns = [None] * len(probs)
    for lv in range(levels + 1):
        for i, (pr, ex) in enumerate(zip(probs, exs)):
            q, k = pr["q"], pr["k"]
            if lv < levels:
                e = ex[lv * c:(lv + 1) * c]
                qt, kt = (q * e).astype(BF16), (k * e).astype(BF16)
            else:
                qt, kt = q.astype(BF16), k.astype(BF16)
            p = _dot_nt(qt, _head_blockdiag(kt, ones_bd)) * pr["mask"][lv]
            attns[i] = p if attns[i] is None else attns[i] + p
    res = []
    for pr, ex, attn in zip(probs, exs, attns):
        o = _dot(attn.astype(BF16), _head_blockdiag(pr["v"].astype(BF16), ones_bd))
        qd = (pr["q"] * ex[levels * c:(levels + 1) * c]).astype(BF16)
        ktail = (pr["k"] * ex[(levels + 1) * c:(levels + 2) * c]).astype(BF16)
        res.append((o, qd, ktail, ex[(levels + 2) * c:(levels + 2) * c + SUBLANES]))
    return res


def _decay_scan_state_steps(items, ones_bd, state_ref):
    sts = [state_ref[it[0]] for it in items]
    outs = [it[1] + _dot_nt(it[2], _head_blockdiag(st.astype(BF16), ones_bd)) for it, st in zip(items, sts)]
    for it, st in zip(items, sts):
        state_ref[it[0]] = st * it[5] + _head_diag_blocks(_dot_tn(it[4], it[3]))
    return outs


def _head_norm_gate(o, gate, onorm, ones_bd):
    ms = _head_sum(o * o, ones_bd) * (1.0 / HEAD_DIM)
    return o * lax.rsqrt(ms + EPS) * onorm * _silu(gate)


def _decay_mixer_kernel(mode, t, c, p_ref, self_ref, selb_ref, maskf_ref, maskb_ref, ones_ref, onorm_ref,
                        pa_ref, pb_ref, pc_ref, h0_ref, y_ref, sout_ref,
                        qd_ref, kt_ref, v16_ref, dl_ref, of_ref, ob_ref, state_ref):
    n_chunks = t // c
    state_ref[...] = h0_ref[0]
    sels, masks, o_refs = (self_ref, selb_ref), (maskf_ref, maskb_ref), (of_ref, ob_ref)

    def inputs(rows, d):
        if mode == "gla":
            q = p_ref[0, rows, 0:GROUP_W] * (HEAD_DIM ** -0.5)
            k = p_ref[0, rows, GROUP_W:2 * GROUP_W]
            v = p_ref[0, rows, 2 * GROUP_W:3 * GROUP_W]
            low = p_ref[0, rows, 4 * GROUP_W:4 * GROUP_W + LANES]
            gk = _dot_hl(low, pa_ref[d], pb_ref[d]) + pc_ref[d]
            g = _log_sigmoid(gk) * (1.0 / GLA_GATE_NORM)
        else:
            q = _silu(p_ref[0, rows, 0:GROUP_W]) * (HEAD_DIM ** -0.5)
            v = p_ref[0, rows, GROUP_W:2 * GROUP_W]
            z = p_ref[0, rows, (3 + d) * GROUP_W:(4 + d) * GROUP_W]
            a = pa_ref[...]
            b = pb_ref[...] + _log_sigmoid(z)
            m = jnp.maximum(a, b)
            g = m + jnp.log(jnp.exp(a - m) + jnp.exp(b - m))
            k = pc_ref[...] * _sigmoid(-z)
        return q, k, v, g

    def prepare(n, carry):
        rows = pl.ds(pl.multiple_of(n * c, c), c)
        probs = []
        for d in range(2):
            q, k, v, g = inputs(rows, d)
            probs.append(dict(q=q, k=k, v=v, g=g, sel=sels[d], mask=masks[d]))
        v16_ref[rows, :] = probs[0]["v"].astype(BF16)
        for d, (o, qd, ktail, dlast) in enumerate(_decay_scan_prepare(probs, ones_ref[...], c)):
            o_refs[d][rows, :] = o
            qd_ref[d, rows, :] = qd
            kt_ref[d, rows, :] = ktail
            dl_ref[d, pl.ds(pl.multiple_of(n * SUBLANES, SUBLANES), SUBLANES), :] = dlast
        return carry

    lax.fori_loop(0, n_chunks, prepare, 0)

    def body(i, carry):
        items, rows_of = [], []
        for d, n in ((0, i), (1, n_chunks - 1 - i)):
            rows = pl.ds(pl.multiple_of(n * c, c), c)
            dlast = dl_ref[d, pl.ds(pl.multiple_of(n * SUBLANES, SUBLANES), SUBLANES), :][0:1, :]
            items.append((d, o_refs[d][rows, :], qd_ref[d, rows, :], kt_ref[d, rows, :], v16_ref[rows, :], dlast))
            rows_of.append(rows)
        o_f, o_b = _decay_scan_state_steps(items, ones_ref[...], state_ref)
        of_ref[rows_of[0], :] = o_f
        ob_ref[rows_of[1], :] = o_b
        return carry

    lax.fori_loop(0, n_chunks, body, 0)

    rbk = min(t, 256)
    gate_col = 3 * GROUP_W if mode == "gla" else 2 * GROUP_W

    def fin(i, carry):
        rows = pl.ds(pl.multiple_of(i * rbk, rbk), rbk)
        o = of_ref[rows, :] + ob_ref[rows, :]
        gate = p_ref[0, rows, gate_col:gate_col + GROUP_W]
        y_ref[0, rows, :] = _head_norm_gate(o, gate, onorm_ref[...], ones_ref[...]).astype(BF16)
        return carry

    lax.fori_loop(0, t // rbk, fin, 0)
    sout_ref[0] = state_ref[...]


def _decay_mixer(mode, p, pa, pb, pc, onorm, h0t):
    b, t, w = p.shape
    c = GLA_CHUNK
    sel_f, mask_f = _scan_consts(c, False)
    sel_b, mask_b = _scan_consts(c, True)
    ones_bd = _ones_blockdiag()
    consts = [sel_f, sel_b, mask_f, mask_b, ones_bd, onorm, pa, pb, pc]
    state_shape = (2, HEAD_DIM, GROUP_W)
    return pl.pallas_call(
        functools.partial(_decay_mixer_kernel, mode, t, c),
        grid=(b,),
        in_specs=[pl.BlockSpec((1, t, w), lambda i: (i, 0, 0), pipeline_mode=pl.Buffered(1))]
                 + [_full_spec(a.shape) for a in consts]
                 + [pl.BlockSpec((1,) + state_shape, lambda i: (i, 0, 0, 0))],
        out_specs=[pl.BlockSpec((1, t, GROUP_W), lambda i: (i, 0, 0)),
                   pl.BlockSpec((1,) + state_shape, lambda i: (i, 0, 0, 0))],
        out_shape=[jax.ShapeDtypeStruct((b, t, GROUP_W), BF16),
                   jax.ShapeDtypeStruct((b,) + state_shape, F32)],
        scratch_shapes=[pltpu.VMEM((2, t, GROUP_W), BF16), pltpu.VMEM((2, t, GROUP_W), BF16),
                        pltpu.VMEM((t, GROUP_W), BF16), pltpu.VMEM((2, (t // c) * SUBLANES, GROUP_W), F32),
                        pltpu.VMEM((t, GROUP_W), F32), pltpu.VMEM((t, GROUP_W), F32),
                        pltpu.VMEM(state_shape, F32)],
        compiler_params=_params(("arbitrary",)),
        name=mode + "_mixer",
    )(p, *consts, h0t)


def _delta_prepare(probs, ones_bd, expand_ref, nega_ref, dtb_ref, c):
    levels = int(np.log2(c))
    work = []
    for pr in probs:
        d, mask = pr["d"], pr["mask"]
        logits = _dot_sel(pr["ba"], expand_ref[d])
        beta = _sigmoid(logits[:, 0:GROUP_W])
        g = nega_ref[d] * _softplus(logits[:, GROUP_W:2 * GROUP_W] + dtb_ref[d])
        cs = _sel_dot(pr["sel"][...], g)
        b_incl = cs[0:c]
        b_t = jnp.concatenate([b_incl[:, h * HEAD_DIM:(h + 1) * HEAD_DIM].T for h in range(N_HEADS)], axis=1)
        lmat = jnp.exp(jnp.minimum(b_incl - b_t, 0.0)) * mask[levels]
        m = beta * pr["kk"] * lmat * mask[levels + 1]
        work.append(dict(beta=beta, eb=jnp.exp(b_incl), etail=jnp.exp(cs[c:2 * c]),
                         dlast=jnp.exp(cs[2 * c:2 * c + SUBLANES]), lmat=lmat, m=m,
                         t=mask[levels + 2] - m * mask[0]))
    for lv in range(1, levels):
        ps = [_per_head_mm(wk["m"] * pr["mask"][lv], wk["t"], ones_bd) for pr, wk in zip(probs, work)]
        for wk, p in zip(work, ps):
            wk["t"] = wk["t"] - _per_head_mm(wk["t"], p, ones_bd)
    us = [_per_head_mm(wk["t"], pr["v"] * wk["beta"], ones_bd) for pr, wk in zip(probs, work)]
    ws = [_per_head_mm(wk["t"], pr["k"] * wk["beta"] * wk["eb"], ones_bd) for pr, wk in zip(probs, work)]
    return [(u, w.astype(BF16), (pr["qk_raw"] * wk["lmat"]).astype(BF16), (pr["q"] * wk["eb"]).astype(BF16),
             (pr["k"] * wk["etail"]).astype(BF16), wk["dlast"]) for pr, wk, u, w in zip(probs, work, us, ws)]


def _delta_state_steps(items, ones_bd, state_ref):
    sts = [state_ref[it[0]] for it in items]
    bds = [_head_blockdiag(st.astype(BF16), ones_bd) for st in sts]
    v16s = [(it[1] - _dot(it[2], bd)).astype(BF16) for it, bd in zip(items, bds)]
    outs = [_dot(it[4], bd) + _dot(it[3], _head_blockdiag(v16, ones_bd)) for it, bd, v16 in zip(items, bds, v16s)]
    for it, st, v16 in zip(items, sts, v16s):
        state_ref[it[0]] = st * it[6] + _head_diag_blocks(_dot_tn(it[5], v16))
    return outs


def _delta_mixer_kernel(t, c, p_ref, self_ref, selb_ref, maskf_ref, maskb_ref, ones_ref, onorm_ref,
                        conv_ref, expand_ref, nega_ref, dtb_ref, h0_ref, y_ref, sout_ref,
                        u_ref, w_ref, qk_ref, qd_ref, kt_ref, dl_ref, of_ref, ob_ref, state_ref):
    n_chunks = t // c
    qkv_w = 3 * GROUP_W
    state_ref[...] = h0_ref[0]
    row_id = lax.broadcasted_iota(jnp.int32, (c, qkv_w), 0)
    w_prev, w_mid, w_next = conv_ref[0:1, :], conv_ref[1:2, :], conv_ref[2:3, :]
    sels, masks = (self_ref, selb_ref), (maskf_ref, maskb_ref)

    def conv_qkv(n):
        r0 = pl.multiple_of(n * c, c)
        x = p_ref[0, pl.ds(r0, c), 0:qkv_w]
        before = p_ref[0, pl.ds(pl.multiple_of(jnp.maximum(r0 - SUBLANES, 0), SUBLANES), SUBLANES), 0:qkv_w]
        after = p_ref[0, pl.ds(pl.multiple_of(jnp.minimum(r0 + c, t - SUBLANES), SUBLANES), SUBLANES), 0:qkv_w]
        before_row = jnp.where(n > 0, before[SUBLANES - 1:SUBLANES, :], 0.0)
        after_row = jnp.where(n < n_chunks - 1, after[0:1, :], 0.0)
        x_prev = jnp.where(row_id == 0, before_row, pltpu.roll(x, 1, axis=0))
        x_next = jnp.where(row_id == c - 1, after_row, pltpu.roll(x, c - 1, axis=0))
        y = _silu(x_prev * w_prev + x * w_mid + x_next * w_next)
        qk = y[:, 0:2 * GROUP_W]
        ones2 = ones_ref[...]
        ss = jnp.concatenate([_head_sum(qk[:, 0:GROUP_W] * qk[:, 0:GROUP_W], ones2),
                              _head_sum(qk[:, GROUP_W:] * qk[:, GROUP_W:], ones2)], axis=1)
        qk = qk * lax.rsqrt(ss + EPS)
        q, k, v = qk[:, 0:GROUP_W] * (HEAD_DIM ** -0.5), qk[:, GROUP_W:], y[:, 2 * GROUP_W:]
        k_hi, k_lo = _split2(k)
        bd_k = _head_blockdiag(k_hi, ones2)
        kk = _dot_nt(k_hi, bd_k) + _dot_nt(k_hi, _head_blockdiag(k_lo, ones2)) + _dot_nt(k_lo, bd_k)
        qk_raw = _dot_nt(q.astype(BF16), bd_k)
        ba = p_ref[0, pl.ds(r0, c), 4 * GROUP_W:4 * GROUP_W + LANES]
        return dict(q=q, k=k, v=v, kk=kk, qk_raw=qk_raw, ba=ba)

    per_iter = 2 if n_chunks % 2 == 0 else 1

    def prepare(i, carry):
        chunks = [i * per_iter + j for j in range(per_iter)]
        probs = []
        for n in chunks:
            shared = conv_qkv(n)
            probs += [dict(shared, sel=sels[d], mask=masks[d], d=d) for d in range(2)]
        res = _delta_prepare(probs, ones_ref[...], expand_ref, nega_ref, dtb_ref, c)
        for idx, (u, w16, qk16, qd, ktail, dlast) in enumerate(res):
            n, d = chunks[idx // 2], idx % 2
            rows = pl.ds(pl.multiple_of(n * c, c), c)
            u_ref[d, rows, :] = u
            w_ref[d, rows, :] = w16
            qk_ref[d, rows, :] = qk16
            qd_ref[d, rows, :] = qd
            kt_ref[d, rows, :] = ktail
            dl_ref[d, pl.ds(pl.multiple_of(n * SUBLANES, SUBLANES), SUBLANES), :] = dlast
        return carry

    lax.fori_loop(0, n_chunks // per_iter, prepare, 0)

    def body(i, carry):
        items, rows_of = [], []
        for d, n in ((0, i), (1, n_chunks - 1 - i)):
            rows = pl.ds(pl.multiple_of(n * c, c), c)
            dlast = dl_ref[d, pl.ds(pl.multiple_of(n * SUBLANES, SUBLANES), SUBLANES), :][0:1, :]
            items.append((d, u_ref[d, rows, :], w_ref[d, rows, :], qk_ref[d, rows, :], qd_ref[d, rows, :],
                          kt_ref[d, rows, :], dlast))
            rows_of.append(rows)
        o_f, o_b = _delta_state_steps(items, ones_ref[...], state_ref)
        of_ref[rows_of[0], :] = o_f
        ob_ref[rows_of[1], :] = o_b
        return carry

    lax.fori_loop(0, n_chunks, body, 0)

    rbk = min(t, 256)

    def fin(i, carry):
        rows = pl.ds(pl.multiple_of(i * rbk, rbk), rbk)
        o = of_ref[rows, :] + ob_ref[rows, :]
        gate = p_ref[0, rows, 3 * GROUP_W:4 * GROUP_W]
        y_ref[0, rows, :] = _head_norm_gate(o, gate, onorm_ref[...], ones_ref[...]).astype(BF16)
        return carry

    lax.fori_loop(0, t // rbk, fin, 0)
    sout_ref[0] = state_ref[...]


def _delta_mixer(p, conv_w, expand, nega, dtb, onorm, s0):
    b, t, w = p.shape
    c = DN_CHUNK
    sel_f, mask_f = _dn_consts(c, False)
    sel_b, mask_b = _dn_consts(c, True)
    consts = [sel_f, sel_b, mask_f, mask_b, _ones_blockdiag(), onorm, conv_w, expand, nega, dtb]
    state_shape = (2, HEAD_DIM, GROUP_W)
    return pl.pallas_call(
        functools.partial(_delta_mixer_kernel, t, c),
        grid=(b,),
        in_specs=[pl.BlockSpec((1, t, w), lambda i: (i, 0, 0), pipeline_mode=pl.Buffered(1))]
                 + [_full_spec(a.shape) for a in consts]
                 + [pl.BlockSpec((1,) + state_shape, lambda i: (i, 0, 0, 0))],
        out_specs=[pl.BlockSpec((1, t, GROUP_W), lambda i: (i, 0, 0)),
                   pl.BlockSpec((1,) + state_shape, lambda i: (i, 0, 0, 0))],
        out_shape=[jax.ShapeDtypeStruct((b, t, GROUP_W), BF16),
                   jax.ShapeDtypeStruct((b,) + state_shape, F32)],
        scratch_shapes=[pltpu.VMEM((2, t, GROUP_W), F32)]
                       + [pltpu.VMEM((2, t, GROUP_W), BF16)] * 4
                       + [pltpu.VMEM((2, (t // c) * SUBLANES, GROUP_W), F32),
                          pltpu.VMEM((t, GROUP_W), F32), pltpu.VMEM((t, GROUP_W), F32),
                          pltpu.VMEM(state_shape, F32)],
        compiler_params=_params(("arbitrary",)),
        name="delta_mixer",
    )(p, *consts, s0)


def _s5_kernel(t, tc, u_ref, bh_ref, bl_ref, cm_ref, lbar_ref, dskip_ref, gluw_ref, glub_ref, h0_ref,
               y_ref, sout_ref, x_ref, yf_ref, yb_ref, carry_ref):
    n_chunks = t // tc
    n_lane_blocks = S5_STATE // LANES
    steps = int(np.log2(tc))
    carry_ref[...] = h0_ref[0]
    row_id = lax.broadcasted_iota(jnp.int32, (tc, LANES), 0)

    def scan_dir(rows, d, y_out_ref):
        u_hi, u_lo = _split2(u_ref[0, rows, :])
        for j in range(n_lane_blocks):
            x_ref[j] = (_dot(u_hi, bh_ref[d, j]) + _dot(u_hi, bl_ref[d, j]) + _dot(u_lo, bh_ref[d, j]))
        y_out_ref[rows, :] = jnp.zeros((tc, GROUP_W), F32)

        def lane_block(j, carry):
            bu = x_ref[j]
            xr, xi = bu[:, 0:LANES], bu[:, LANES:2 * LANES]
            lb = lbar_ref[d, j]
            ar, ai = lb[0:1, :], lb[1:2, :]
            cr, ci = carry_ref[d, 0, j], carry_ref[d, 1, j]
            first = row_id == (tc - 1 if d else 0)
            xr = xr + jnp.where(first, ar * cr - ai * ci, 0.0)
            xi = xi + jnp.where(first, ar * ci + ai * cr, 0.0)
            for s in range(steps):
                sh = 1 << s
                if d == 0:
                    keep = row_id >= sh
                    sr = jnp.where(keep, pltpu.roll(xr, sh, axis=0), 0.0)
                    si = jnp.where(keep, pltpu.roll(xi, sh, axis=0), 0.0)
                else:
                    keep = row_id < tc - sh
                    sr = jnp.where(keep, pltpu.roll(xr, tc - sh, axis=0), 0.0)
                    si = jnp.where(keep, pltpu.roll(xi, tc - sh, axis=0), 0.0)
                xr, xi = xr + ar * sr - ai * si, xi + ar * si + ai * sr
                ar, ai = ar * ar - ai * ai, 2.0 * ar * ai
            last = 0 if d else tc - 1
            carry_ref[d, 0, j] = xr[last:last + 1, :]
            carry_ref[d, 1, j] = xi[last:last + 1, :]
            xc = jnp.concatenate([xr, xi], axis=1).astype(BF16)
            y_out_ref[rows, :] += _dot(xc, cm_ref[j])
            return carry

        lax.fori_loop(0, n_lane_blocks, lane_block, 0)

    def body(n, carry):
        scan_dir(pl.ds(pl.multiple_of(n * tc, tc), tc), 0, yf_ref)
        scan_dir(pl.ds(pl.multiple_of((n_chunks - 1 - n) * tc, tc), tc), 1, yb_ref)
        return carry

    lax.fori_loop(0, n_chunks, body, 0)

    def fin(i, carry):
        rows = pl.ds(pl.multiple_of(i * tc, tc), tc)
        u = u_ref[0, rows, :]
        y = _gelu_tanh(yf_ref[rows, :] + yb_ref[rows, :] + dskip_ref[...] * u)
        z = _dot(y.astype(BF16), gluw_ref[...]) + glub_ref[...]
        y_ref[0, rows, :] = (y * _sigmoid(z)).astype(BF16)
        return carry

    lax.fori_loop(0, n_chunks, fin, 0)
    sout_ref[0] = carry_ref[...]


def _s5_mixer(u, b_hi, b_lo, cmat, lbar, dskip, glu_w, glu_b, h0):
    b, t, _ = u.shape
    tc = min(S5_CHUNK, t)
    nlb = S5_STATE // LANES
    consts = [b_hi, b_lo, cmat, lbar, dskip, glu_w, glu_b]
    st_shape = (2, 2, nlb, 1, LANES)
    return pl.pallas_call(
        functools.partial(_s5_kernel, t, tc),
        grid=(b,),
        in_specs=[pl.BlockSpec((1, t, GROUP_W), lambda i: (i, 0, 0))]
                 + [_full_spec(a.shape) for a in consts]
                 + [pl.BlockSpec((1,) + st_shape, lambda i: (i, 0, 0, 0, 0, 0))],
        out_specs=[pl.BlockSpec((1, t, GROUP_W), lambda i: (i, 0, 0)),
                   pl.BlockSpec((1,) + st_shape, lambda i: (i, 0, 0, 0, 0, 0))],
        out_shape=[jax.ShapeDtypeStruct((b, t, GROUP_W), BF16),
                   jax.ShapeDtypeStruct((b,) + st_shape, F32)],
        scratch_shapes=[pltpu.VMEM((nlb, tc, 2 * LANES), F32), pltpu.VMEM((t, GROUP_W), F32),
                        pltpu.VMEM((t, GROUP_W), F32), pltpu.VMEM(st_shape, F32)],
        compiler_params=_params(("arbitrary",)),
        name="s5_mixer",
    )(u, *consts, h0)


def _outproj_kernel(x_ref, mod_ref, ya_ref, yb_ref, yc_ref, yd_ref, w_ref, o_ref):
    y = (_dot(ya_ref[0], w_ref[0]) + _dot(yb_ref[0], w_ref[1])
         + _dot(yc_ref[0], w_ref[2]) + _dot(yd_ref[0], w_ref[3]))
    o_ref[0] = x_ref[0] + mod_ref[0][:, 2 * D_MODEL:3 * D_MODEL] * y


def _out_projection(x, mod, ys, w_out):
    b, t, _ = x.shape
    tb = min(TOKEN_BLOCK, t)
    tok = lambda n: pl.BlockSpec((1, tb, n), lambda i, j: (i, j, 0))
    return pl.pallas_call(
        _outproj_kernel,
        grid=(b, t // tb),
        in_specs=[tok(D_MODEL), pl.BlockSpec((1, 1, 6 * D_MODEL), lambda i, j: (i, 0, 0))]
                 + [tok(GROUP_W)] * 4 + [_full_spec(w_out.shape)],
        out_specs=tok(D_MODEL),
        out_shape=jax.ShapeDtypeStruct(x.shape, F32),
        compiler_params=_params(("arbitrary", "arbitrary")),
        name="out_projection",
    )(x, mod, *ys, w_out)


def _topk_rows(scores, k, row_ids):
    n = scores[0].shape[1]
    slot = lax.broadcasted_iota(jnp.int32, (k, n), 0)
    scores = list(scores)
    vals = [jnp.zeros((k, n), F32) for _ in scores]
    ids = [jnp.zeros((k, n), F32) for _ in scores]
    for it in range(k):
        ms = [jnp.max(s, axis=0, keepdims=True) for s in scores]
        idxs = [jnp.min(jnp.where(s == m, row_ids, 1e9), axis=0, keepdims=True) for s, m in zip(scores, ms)]
        vals = [jnp.where(slot == it, m, v) for m, v in zip(ms, vals)]
        ids = [jnp.where(slot == it, idx, i) for idx, i in zip(idxs, ids)]
        scores = [jnp.where(row_ids == idx, -jnp.inf, s) for s, idx in zip(scores, idxs)]
    return list(zip(vals, ids))


def _staircase():
    pairs = [(i, j) for i in range(PEER_TOPK) for j in range(PEER_TOPK) if (i + 1) * (j + 1) <= PEER_TOPK]
    rows = -(-len(pairs) // SUBLANES) * SUBLANES
    sel = np.zeros((2, rows, PEER_TOPK), np.float32)
    aux = np.zeros((2, rows, LANES), np.float32)
    aux[0, len(pairs):] = -np.inf
    aux[1, len(pairs):] = 1e9
    for r, (i, j) in enumerate(pairs):
        sel[0, r, i] = 1.0
        sel[1, r, j] = 1.0
        aux[1, r] = i * PEER_TOPK + j
    return jnp.asarray(np.tile(sel, (1, 1, 3)), BF16), jnp.asarray(aux, F32)


def _take_rows(table, idx, k):
    out = jnp.zeros(idx.shape, F32)
    for r in range(k):
        out = jnp.where(idx == float(r), table[r:r + 1, :], out)
    return out


def _peer_route_kernel(tb, x_ref, mod_ref, g_ref, wqh_ref, wql_ref, kh_ref, kl_ref, stsel_ref, staux_ref,
                       h_ref, a_ref, c_ref, gate_ref, qt_ref, at_ref, ct_ref, gt_ref):
    mod = mod_ref[0]
    h = _rms_modulate(x_ref[0], g_ref[...], mod[:, 3 * D_MODEL:4 * D_MODEL], mod[:, 4 * D_MODEL:5 * D_MODEL])
    h_hi, h_lo = _split2(h)
    h_ref[0] = h_hi
    wq_hi = wqh_ref[...]
    qt_ref[...] = _dot_nt(wq_hi, h_hi) + _dot_nt(wq_hi, h_lo) + _dot_nt(wql_ref[...], h_hi)
    half = PEER_DQ // 2
    key_ids = lax.broadcasted_iota(jnp.int32, (PEER_NKEYS, LANES), 0).astype(F32)

    def head_body(head, carry):
        out_rows = pl.ds(pl.multiple_of(head * PEER_TOPK, PEER_TOPK), PEER_TOPK)
        tiles = [slice(lt * LANES, (lt + 1) * LANES) for lt in range(tb // LANES)]
        scores = []
        for lanes in tiles:
            for p in range(2):
                hp = head * 2 + p
                q_hi, q_lo = _split2(qt_ref[pl.ds(pl.multiple_of(hp * half, half), half), lanes])
                scores.append(_dot(kh_ref[hp], q_hi) + _dot(kh_ref[hp], q_lo) + _dot(kl_ref[hp], q_hi))
        top = _topk_rows(scores, PEER_TOPK, key_ids)
        cands = [_sel_dot(stsel_ref[0], top[2 * i][0]) + _sel_dot(stsel_ref[1], top[2 * i + 1][0]) + staux_ref[0]
                 for i in range(len(tiles))]
        best = _topk_rows(cands, PEER_TOPK, staux_ref[1])
        for i, lanes in enumerate(tiles):
            sc, flat = best[i]
            r1 = jnp.floor(flat * (1.0 / PEER_TOPK))
            r2 = flat - r1 * PEER_TOPK
            e = jnp.exp(sc - sc[0:1, :])
            at_ref[out_rows, lanes] = _take_rows(top[2 * i][1], r1, PEER_TOPK)
            ct_ref[out_rows, lanes] = _take_rows(top[2 * i + 1][1], r2, PEER_TOPK)
            gt_ref[out_rows, lanes] = e / jnp.sum(e, axis=0, keepdims=True)
        return carry

    lax.fori_loop(0, PEER_HEADS, head_body, 0)
    a_ref[0] = at_ref[...].T.astype(jnp.int32)
    c_ref[0] = ct_ref[...].T.astype(jnp.int32)
    gate_ref[0] = gt_ref[...].T


def _peer_route(x, mod, gain, wq_hi, wq_lo, keys_hi, keys_lo):
    b, t, _ = x.shape
    tb = min(TOKEN_BLOCK, t)
    tok = lambda n: pl.BlockSpec((1, tb, n), lambda i, j: (i, j, 0))
    consts = [gain, wq_hi, wq_lo, keys_hi, keys_lo, *_staircase()]
    return pl.pallas_call(
        functools.partial(_peer_route_kernel, tb),
        grid=(b, t // tb),
        scratch_shapes=[pltpu.VMEM((PEER_HEADS * PEER_DQ, tb), F32)] + [pltpu.VMEM((PEER_SEL, tb), F32)] * 3,
        in_specs=[tok(D_MODEL), pl.BlockSpec((1, 1, 6 * D_MODEL), lambda i, j: (i, 0, 0))]
                 + [_full_spec(a.shape) for a in consts],
        out_specs=[tok(D_MODEL), tok(PEER_SEL), tok(PEER_SEL), tok(PEER_SEL)],
        out_shape=[jax.ShapeDtypeStruct((b, t, D_MODEL), BF16),
                   jax.ShapeDtypeStruct((b, t, PEER_SEL), jnp.int32),
                   jax.ShapeDtypeStruct((b, t, PEER_SEL), jnp.int32),
                   jax.ShapeDtypeStruct((b, t, PEER_SEL), F32)],
        compiler_params=_params(("arbitrary", "arbitrary")),
        name="peer_route",
    )(x, mod, *consts)


def _peer_expert_kernel(tb, n_slabs, x_ref, mod_ref, h_ref, a_ref, c_ref, gate_ref, ut_ref, v_ref,
                        o_ref, gmat_ref, acc_ref):
    j = pl.program_id(2)
    nk = PEER_NKEYS

    @pl.when(j == 0)
    def _():
        acc_ref[...] = jnp.zeros_like(acc_ref)
        key_id = lax.broadcasted_iota(jnp.int32, (nk, PEER_SEL), 0)

        def build(nb, carry):
            n0 = pl.multiple_of(nb * SUBLANES, SUBLANES)
            a_rows = a_ref[0, pl.ds(n0, SUBLANES), :]
            c_rows = c_ref[0, pl.ds(n0, SUBLANES), :]
            g_rows = gate_ref[0, pl.ds(n0, SUBLANES), :]
            mats = []
            for r in range(SUBLANES):
                row_sel = jnp.where(key_id == a_rows[r:r + 1], 1.0, 0.0).astype(BF16)
                col_sel = jnp.where(key_id == c_rows[r:r + 1], g_rows[r:r + 1], 0.0).astype(BF16)
                mats.append(_dot_nt(row_sel, col_sel))
            gmat_ref[:, pl.ds(n0, SUBLANES), :] = pltpu.einshape("nac->anc", jnp.stack(mats, axis=0))
            return carry

        lax.fori_loop(0, tb // SUBLANES, build, 0)

    h = h_ref[0]
    n_pairs = n_slabs // 2
    cols = [slice(i * 2 * nk, (i + 1) * 2 * nk) for i in range(n_pairs)]
    lookahead = 2
    s = [_dot(h, ut_ref[:, cols[i]]) for i in range(min(lookahead, n_pairs))]
    acc = None
    for i in range(n_pairs):
        if i + lookahead < n_pairs:
            s.append(_dot(h, ut_ref[:, cols[i + lookahead]]))
        a0 = j * n_slabs + 2 * i
        g = jnp.concatenate([gmat_ref[a0], gmat_ref[a0 + 1]], axis=1)
        w = (g * _gelu_tanh(s[i])).astype(BF16)
        p = _dot(w, v_ref[cols[i], :])
        acc = p if acc is None else acc + p
    acc_ref[...] += acc

    @pl.when(j == pl.num_programs(2) - 1)
    def _():
        o_ref[0] = x_ref[0] + mod_ref[0][:, 5 * D_MODEL:6 * D_MODEL] * acc_ref[...]


def _peer_experts(x, mod, h, a, c, gate, u_t, v_tab):
    b, t, _ = x.shape
    tb = min(TOKEN_BLOCK, t)
    n_exp = u_t.shape[1]
    eb = PEER_EXPERT_BLOCK
    n_slabs = eb // PEER_NKEYS
    tok = lambda n: pl.BlockSpec((1, tb, n), lambda i, k, j: (i, k, 0))
    return pl.pallas_call(
        functools.partial(_peer_expert_kernel, tb, n_slabs),
        grid=(b, t // tb, n_exp // eb),
        in_specs=[tok(D_MODEL), pl.BlockSpec((1, 1, 6 * D_MODEL), lambda i, k, j: (i, 0, 0)),
                  tok(D_MODEL), tok(PEER_SEL), tok(PEER_SEL), tok(PEER_SEL),
                  pl.BlockSpec((D_MODEL, eb), lambda i, k, j: (0, j)),
                  pl.BlockSpec((eb, D_MODEL), lambda i, k, j: (j, 0))],
        out_specs=tok(D_MODEL),
        out_shape=jax.ShapeDtypeStruct(x.shape, F32),
        scratch_shapes=[pltpu.VMEM((PEER_NKEYS, tb, PEER_NKEYS), F32), pltpu.VMEM((tb, D_MODEL), F32)],
        compiler_params=_params(("arbitrary", "arbitrary", "arbitrary")),
        name="peer_experts",
    )(x, mod, h, a, c, gate, u_t, v_tab)


def _final_norm_kernel(x_ref, g_ref, o_ref):
    x = x_ref[0]
    ms = jnp.mean(x * x, axis=-1, keepdims=True)
    o_ref[0] = x * lax.rsqrt(ms + EPS) * g_ref[...]


def _final_norm(x, gain):
    b, t, _ = x.shape
    tb = min(TOKEN_BLOCK, t)
    tok = pl.BlockSpec((1, tb, D_MODEL), lambda i, j: (i, j, 0))
    return pl.pallas_call(
        _final_norm_kernel,
        grid=(b, t // tb),
        in_specs=[tok, _full_spec((1, D_MODEL))],
        out_specs=tok,
        out_shape=jax.ShapeDtypeStruct(x.shape, F32),
        compiler_params=_params(("arbitrary", "arbitrary")),
        name="final_norm",
    )(x, gain)


def _hi_lo(w):
    hi = w.astype(BF16)
    return hi, (w - hi.astype(F32)).astype(BF16)


def _pad_cols(w, n):
    return jnp.pad(w, ((0, 0), (0, n - w.shape[1])))


def _layer_weights(l, w_in, w_out, gla_gk_up, gla_gk_bias, gla_onorm_g, dn_conv_w, dn_A_log, dn_dt_bias,
                   dn_onorm_g, s5_lambda_re, s5_lambda_im, s5_log_dt, s5_B_re, s5_B_im, s5_C_re, s5_C_im,
                   s5_D, s5_glu_w, s5_glu_b, lower_bound, hgrn_onorm_g, peer_wq, peer_keys, peer_u, peer_v):
    gw = GROUP_W
    sizes = (gw, gw, gw, gw, 2 * GLA_LOWRANK, gw, gw, gw, gw, 2 * N_HEADS, 2 * N_HEADS, gw, gw, gw, gw, 2 * gw)
    offs = np.concatenate([[0], np.cumsum(sizes)])
    col = lambda a, b: w_in[l][:, int(offs[a]):int(offs[b])]
    wl = {}
    wl["w_gla"] = _pad_cols(col(0, 5), GLA_W).astype(BF16)
    wl["w_dn"] = _pad_cols(col(5, 11), DN_W).astype(BF16)
    wl["w_s5"] = col(11, 12).astype(BF16)
    wl["w_hg"] = col(12, 16).astype(BF16)
    wl["w_out"] = w_out[l].reshape(4, gw, D_MODEL).astype(BF16)
    tile_head = lambda g: jnp.tile(g, N_HEADS).reshape(1, gw)
    up = jnp.zeros((2, LANES, gw), F32)
    for d in range(2):
        up = up.at[d, d * GLA_LOWRANK:(d + 1) * GLA_LOWRANK, :].set(gla_gk_up[l, d])
    wl["gla_up_hi"], wl["gla_up_lo"] = _hi_lo(up)
    wl["gla_bias"] = gla_gk_bias[l].reshape(2, 1, gw)
    wl["gla_onorm"] = tile_head(gla_onorm_g[l])
    expand = np.zeros((2, LANES, 2 * gw), np.float32)
    for d in range(2):
        for h in range(N_HEADS):
            expand[d, d * N_HEADS + h, h * HEAD_DIM:(h + 1) * HEAD_DIM] = 1.0
            expand[d, 2 * N_HEADS + d * N_HEADS + h, gw + h * HEAD_DIM:gw + (h + 1) * HEAD_DIM] = 1.0
    wl["dn_expand"] = jnp.asarray(expand, BF16)
    wl["dn_conv"] = jnp.pad(dn_conv_w[l], ((0, SUBLANES - 3), (0, 0)))
    wl["dn_nega"] = jnp.repeat(-jnp.exp(dn_A_log[l]), HEAD_DIM, axis=-1).reshape(2, 1, gw)
    wl["dn_dtb"] = jnp.repeat(dn_dt_bias[l], HEAD_DIM, axis=-1).reshape(2, 1, gw)
    wl["dn_onorm"] = tile_head(dn_onorm_g[l])
    nlb = S5_STATE // LANES
    lam_re, lam_im = s5_lambda_re[l].astype(F32), s5_lambda_im[l].astype(F32)
    dt = jnp.exp(s5_log_dt[l].astype(F32))[:, :, None]
    mag = jnp.exp(lam_re * dt)
    lbar_re, lbar_im = mag * jnp.cos(lam_im * dt), mag * jnp.sin(lam_im * dt)
    den = lam_re * lam_re + lam_im * lam_im
    num_re, num_im = lbar_re - 1.0, lbar_im
    coef_re = (num_re * lam_re + num_im * lam_im) / den
    coef_im = (num_im * lam_re - num_re * lam_im) / den
    b_re, b_im = s5_B_re[l].astype(F32), s5_B_im[l].astype(F32)
    bb_re = coef_re[..., None] * b_re - coef_im[..., None] * b_im
    bb_im = coef_re[..., None] * b_im + coef_im[..., None] * b_re
    eye_g = jnp.eye(S5_G, dtype=F32)
    bd_in = lambda m: jnp.einsum('dgpc,gk->dgckp', m, eye_g).reshape(2, gw, S5_STATE)
    blk = lambda m: m.reshape(2, gw, nlb, LANES).transpose(0, 2, 1, 3)
    b_cat = jnp.concatenate([blk(bd_in(bb_re)), blk(bd_in(bb_im))], axis=-1)
    wl["s5_b_hi"], wl["s5_b_lo"] = _hi_lo(b_cat)
    bd_out = lambda m: jnp.einsum('gcp,gk->gpkc', m.astype(F32), eye_g).reshape(S5_STATE, gw)
    c_re, c_im = bd_out(s5_C_re[l]), bd_out(s5_C_im[l])
    wl["s5_c"] = jnp.concatenate([c_re.reshape(nlb, LANES, gw), -c_im.reshape(nlb, LANES, gw)],
                                 axis=1).astype(BF16)
    wl["s5_lbar"] = jnp.stack([lbar_re.reshape(2, nlb, LANES), lbar_im.reshape(2, nlb, LANES)], axis=2)
    wl["s5_d"] = s5_D[l].reshape(1, gw)
    wl["s5_glu_w"] = s5_glu_w[l].astype(BF16)
    wl["s5_glu_b"] = s5_glu_b[l].reshape(1, gw)
    wl["hg_loglb"] = jnp.log(lower_bound).reshape(1, gw)
    wl["hg_log1m"] = jnp.log1p(-lower_bound).reshape(1, gw)
    wl["hg_1m"] = (1.0 - lower_bound).reshape(1, gw)
    wl["hg_onorm"] = tile_head(hgrn_onorm_g[l])
    wl["wq_hi"], wl["wq_lo"] = _hi_lo(peer_wq[l].T)
    wl["keys_hi"], wl["keys_lo"] = _hi_lo(peer_keys[l].reshape(PEER_HEADS * 2, PEER_NKEYS, PEER_DQ // 2))
    wl["u_t"] = peer_u[l].astype(BF16).T
    wl["v_tab"] = peer_v[l].astype(BF16)
    return wl


def _heads_to_lanes(s, transposed):
    perm = (0, 1, 4, 2, 3) if transposed else (0, 1, 3, 2, 4)
    return s.astype(F32).transpose(perm).reshape(s.shape[0], 2, HEAD_DIM, GROUP_W)


def _lanes_to_heads(s, transposed):
    s = s.reshape(s.shape[0], 2, HEAD_DIM, N_HEADS, HEAD_DIM)
    return s.transpose((0, 1, 3, 4, 2) if transposed else (0, 1, 3, 2, 4))


def _to_column_major(x):
    b, n, d = x.shape
    return x.reshape(b, n // GRID_W, GRID_W, d).transpose(0, 2, 1, 3).reshape(b, n, d)


def _to_row_major(x):
    b, n, d = x.shape
    return x.reshape(b, GRID_W, n // GRID_W, d).transpose(0, 2, 1, 3).reshape(b, n, d)


def _trunk_layer(x, mod, wl, norm1_g, norm2_g, states):
    st_gla, st_dn, st_s5, st_hg = states
    p_gla, p_dn, p_s5, p_hg = _in_projection(x, mod, norm1_g, [wl["w_gla"], wl["w_dn"], wl["w_s5"], wl["w_hg"]])
    y_a, n_gla = _decay_mixer("gla", p_gla, wl["gla_up_hi"], wl["gla_up_lo"], wl["gla_bias"], wl["gla_onorm"], st_gla)
    y_b, n_dn = _delta_mixer(p_dn, wl["dn_conv"], wl["dn_expand"], wl["dn_nega"], wl["dn_dtb"], wl["dn_onorm"], st_dn)
    y_c, n_s5 = _s5_mixer(p_s5, wl["s5_b_hi"], wl["s5_b_lo"], wl["s5_c"], wl["s5_lbar"], wl["s5_d"],
                          wl["s5_glu_w"], wl["s5_glu_b"], st_s5)
    y_d, n_hg = _decay_mixer("hgrn", p_hg, wl["hg_loglb"], wl["hg_log1m"], wl["hg_1m"], wl["hg_onorm"], st_hg)
    x = _out_projection(x, mod, [y_a, y_b, y_c, y_d], wl["w_out"])
    h, a, c, gate = _peer_route(x, mod, norm2_g, wl["wq_hi"], wl["wq_lo"], wl["keys_hi"], wl["keys_lo"])
    x = _peer_experts(x, mod, h, a, c, gate, wl["u_t"], wl["v_tab"])
    return x, (n_gla, n_dn, n_s5, n_hg)


def kernel(x_prompt, x_sample, state_gla, state_delta, state_s5_re, state_s5_im, state_hgrn, c, c_ctx,
           mod_w, mod_b, norm1_g, norm2_g, w_in, w_out, gla_gk_up, gla_gk_bias, gla_onorm_g,
           dn_conv_w, dn_A_log, dn_dt_bias, dn_onorm_g, s5_lambda_re, s5_lambda_im, s5_log_dt,
           s5_B_re, s5_B_im, s5_C_re, s5_C_im, s5_D, s5_glu_w, s5_glu_b, hgrn_lb_logits,
           hgrn_onorm_g, peer_wq, peer_keys, peer_u, peer_v, final_norm_g):
    bp, bs = x_prompt.shape[0], x_sample.shape[0]
    nlb = S5_STATE // LANES
    lb_cum = jnp.cumsum(jax.nn.softmax(hgrn_lb_logits.astype(F32), axis=0), axis=0)
    lower_bounds = lb_cum - lb_cum[0:1]

    rows = 2 * SUBLANES
    cc = jnp.concatenate([c, c_ctx[None, :], jnp.zeros((rows - bs - 1, D_MODEL), F32)], axis=0)
    mod_all = _modulation(cc, mod_w, mod_b)

    zeros_hh = jnp.zeros((bp, 2, HEAD_DIM, GROUP_W), F32)
    zeros_s5 = jnp.zeros((bp, 2, 2, nlb, 1, LANES), F32)

    xp, xs = x_prompt, x_sample
    ctx_states = []
    for l in range(DEPTH):
        wl = _layer_weights(l, w_in, w_out, gla_gk_up, gla_gk_bias, gla_onorm_g, dn_conv_w, dn_A_log,
                            dn_dt_bias, dn_onorm_g, s5_lambda_re, s5_lambda_im, s5_log_dt, s5_B_re, s5_B_im,
                            s5_C_re, s5_C_im, s5_D, s5_glu_w, s5_glu_b, lower_bounds[l], hgrn_onorm_g,
                            peer_wq, peer_keys, peer_u, peer_v)
        n1, n2 = norm1_g[l].reshape(1, D_MODEL), norm2_g[l].reshape(1, D_MODEL)
        mod_ctx = jnp.broadcast_to(mod_all[l, bs].reshape(1, 1, 6 * D_MODEL), (bp, 1, 6 * D_MODEL))
        mod_lat = mod_all[l, 0:bs].reshape(bs, 1, 6 * D_MODEL)
        xp, st_ctx = _trunk_layer(xp, mod_ctx, wl, n1, n2, (zeros_hh, zeros_hh, zeros_s5, zeros_hh))
        ctx_states.append(st_ctx)
        cached_s5 = jnp.stack([state_s5_re[:, l], state_s5_im[:, l]], axis=2).reshape(bs, 2, 2, nlb, 1, LANES)
        cached = (_heads_to_lanes(state_gla[:, l], True), _heads_to_lanes(state_delta[:, l], False), cached_s5,
                  _heads_to_lanes(state_hgrn[:, l], True))
        if l % 2 == 1:
            xs = _to_column_major(xs)
        xs, _ = _trunk_layer(xs, mod_lat, wl, n1, n2, cached)
        if l % 2 == 1:
            xs = _to_row_major(xs)

    gain = final_norm_g.reshape(1, D_MODEL)
    y_prompt = _final_norm(xp, gain)
    y_sample = _final_norm(xs, gain)
    new_gla = jnp.stack([_lanes_to_heads(s[0], True) for s in ctx_states], axis=1)
    new_dn = jnp.stack([_lanes_to_heads(s[1], False) for s in ctx_states], axis=1)
    s5 = jnp.stack([s[2].reshape(bp, 2, 2, S5_G, S5_P) for s in ctx_states], axis=1)
    new_hg = jnp.stack([_lanes_to_heads(s[3], True) for s in ctx_states], axis=1)
    return (y_prompt, y_sample, new_gla, new_dn, s5[:, :, :, 0], s5[:, :, :, 1], new_hg)
```

```python
import functools

import numpy as np
import jax
import jax.numpy as jnp
from jax import lax
from jax.experimental import pallas as pl
from jax.experimental.pallas import tpu as pltpu

F32 = jnp.float32
BF16 = jnp.bfloat16

D_MODEL = 1024
DEPTH = 2
GRID_W = 64
GROUP_W = 256
HEAD_DIM = 64
N_HEADS = GROUP_W // HEAD_DIM
GLA_LOWRANK = 16
GLA_GATE_NORM = 16.0
DN_CHUNK = 64
S5_GROUP = 16
S5_G = GROUP_W // S5_GROUP
S5_P = 64
S5_STATE = S5_G * S5_P
PEER_HEADS = 8
PEER_NKEYS = 128
PEER_DQ = 256
PEER_TOPK = 16
PEER_SEL = PEER_HEADS * PEER_TOPK
EPS = 1e-6

LANES = 128
SUBLANES = 8
VMEM_LIMIT = 56 * 1024 * 1024

GLA_CHUNK = 64
S5_CHUNK = 256
TOKEN_BLOCK = 256
PEER_EXPERT_BLOCK = 2048
GLA_W = 4 * GROUP_W + LANES
DN_W = 4 * GROUP_W + LANES
HG_W = 5 * GROUP_W


def _dot(a, b):
    return jnp.dot(a, b, preferred_element_type=F32)


def _dot_nt(a, b):
    return lax.dot_general(a, b, (((1,), (1,)), ((), ())), preferred_element_type=F32)


def _dot_tn(a, b):
    return lax.dot_general(a, b, (((0,), (0,)), ((), ())), preferred_element_type=F32)


def _split2(x):
    hi = x.astype(BF16)
    lo = (x - hi.astype(F32)).astype(BF16)
    return hi, lo


def _split3(x):
    hi = x.astype(BF16)
    r = x - hi.astype(F32)
    mid = r.astype(BF16)
    lo = (r - mid.astype(F32)).astype(BF16)
    return hi, mid, lo


def _sel_dot(m01x3, x):
    return _dot(m01x3, jnp.concatenate(_split3(x), axis=0))


def _dot_sel(x, m01):
    hi, mid, lo = _split3(x)
    return _dot(hi, m01) + _dot(mid, m01) + _dot(lo, m01)


def _dot_hl(a, b_hi, b_lo):
    a_hi, a_lo = _split2(a)
    return _dot(a_hi, b_hi) + _dot(a_hi, b_lo) + _dot(a_lo, b_hi)


def _head_sum(x, ones_bd):
    hi, lo = _split2(x)
    return _dot(hi, ones_bd) + _dot(lo, ones_bd)


def _head_blockdiag(x16, ones_bd):
    return jnp.tile(x16, (N_HEADS, 1)) * ones_bd


def _per_head_mm(a, x, ones_bd):
    a_hi, a_lo = _split2(a)
    x_hi, x_lo = _split2(x)
    bd_hi = _head_blockdiag(x_hi, ones_bd)
    return _dot(a_hi, bd_hi) + _dot(a_hi, _head_blockdiag(x_lo, ones_bd)) + _dot(a_lo, bd_hi)


def _head_diag_blocks(full):
    lane_head = lax.broadcasted_iota(jnp.int32, (HEAD_DIM, GROUP_W), 1) // HEAD_DIM
    out = None
    for h in range(N_HEADS):
        blk = jnp.where(lane_head == h, full[h * HEAD_DIM:(h + 1) * HEAD_DIM, :], 0.0)
        out = blk if out is None else out + blk
    return out


def _log_sigmoid(x):
    return jnp.minimum(x, 0.0) - jnp.log(1.0 + jnp.exp(-jnp.abs(x)))


def _softplus(x):
    return jnp.maximum(x, 0.0) + jnp.log(1.0 + jnp.exp(-jnp.abs(x)))


def _sigmoid(x):
    return 1.0 / (1.0 + jnp.exp(-x))


def _silu(x):
    return x * _sigmoid(x)


def _gelu_tanh(x):
    return 0.5 * x * (1.0 + jnp.tanh(0.7978845608028654 * (x + 0.044715 * x * x * x)))


def _rms_modulate(x, gain, shift, scale):
    ms = jnp.mean(x * x, axis=-1, keepdims=True)
    return x * lax.rsqrt(ms + EPS) * gain * (1.0 + scale) + shift


def _full_spec(shape):
    zeros = (0,) * len(shape)
    return pl.BlockSpec(shape, lambda *_: zeros)


def _params(sem):
    return pltpu.CompilerParams(dimension_semantics=sem, vmem_limit_bytes=VMEM_LIMIT)


def _scan_consts(c, reverse):
    levels = int(np.log2(c))
    idx = np.arange(c)
    i, t = idx[:, None], idx[None, :]
    sel, masks = [], []
    for lv in range(levels):
        s = 1 << lv
        right = (idx // s) % 2 == 1
        same_blk = (i // s) == (t // s)
        sel.append(np.where(right[:, None], same_blk & (t <= i), same_blk & (t > i)))
        same_parent = (i // (2 * s)) == (t // (2 * s))
        masks.append(same_parent & right[:, None] & ~right[None, :])
    sel.append(t <= i)
    sel.append(t > i)
    masks.append(i == t)
    if reverse:
        sel = [m[::-1, ::-1] for m in sel]
        masks = [m[::-1, ::-1] for m in masks]
    sel.append(np.ones((SUBLANES, c), bool))
    sel = np.tile(np.concatenate(sel, axis=0).astype(np.float32), (1, 3))
    masks = np.tile(np.stack(masks).astype(np.float32), (1, 1, N_HEADS))
    return jnp.asarray(sel, BF16), jnp.asarray(masks, F32)


def _dn_consts(c, reverse):
    levels = int(np.log2(c))
    idx = np.arange(c)
    i, t = idx[:, None], idx[None, :]
    incl, tail = t <= i, t > i
    masks = []
    for lv in range(levels):
        s = 1 << lv
        right = (idx // s) % 2 == 1
        masks.append(((i // (2 * s)) == (t // (2 * s))) & right[:, None] & ~right[None, :])
    masks += [incl, incl & (i != t), i == t]
    if reverse:
        incl, tail = incl[::-1, ::-1], tail[::-1, ::-1]
        masks = [m[::-1, ::-1] for m in masks]
    sel = np.tile(np.concatenate([incl, tail, np.ones((SUBLANES, c), bool)], axis=0).astype(np.float32), (1, 3))
    masks = np.tile(np.stack(masks).astype(np.float32), (1, 1, N_HEADS))
    return jnp.asarray(sel, BF16), jnp.asarray(masks, F32)


def _ones_blockdiag():
    h = np.arange(GROUP_W) // HEAD_DIM
    return jnp.asarray((h[:, None] == h[None, :]).astype(np.float32), BF16)


def _mod_kernel(c_ref, w_ref, b_ref, o_ref):
    a_hi, a_lo = _split2(_silu(c_ref[...]))
    w_hi, w_lo = _split2(w_ref[0])
    o_ref[0] = _dot(a_hi, w_hi) + _dot(a_hi, w_lo) + _dot(a_lo, w_hi) + b_ref[0]


def _modulation(cc, mod_w, mod_b):
    rows = cc.shape[0]
    nb = 6
    return pl.pallas_call(
        _mod_kernel,
        grid=(DEPTH, nb),
        in_specs=[pl.BlockSpec((rows, D_MODEL), lambda l, j: (0, 0)),
                  pl.BlockSpec((1, D_MODEL, D_MODEL), lambda l, j: (l, 0, j)),
                  pl.BlockSpec((1, 1, D_MODEL), lambda l, j: (l, 0, j))],
        out_specs=pl.BlockSpec((1, rows, D_MODEL), lambda l, j: (l, 0, j)),
        out_shape=jax.ShapeDtypeStruct((DEPTH, rows, 6 * D_MODEL), F32),
        compiler_params=_params(("arbitrary", "arbitrary")),
        name="modulation",
    )(cc, mod_w, mod_b.reshape(DEPTH, 1, 6 * D_MODEL))


def _inproj_kernel(x_ref, mod_ref, g_ref, wa_ref, wb_ref, wc_ref, wd_ref, oa_ref, ob_ref, oc_ref, od_ref):
    mod = mod_ref[0]
    h = _rms_modulate(x_ref[0], g_ref[...], mod[:, 0:D_MODEL], mod[:, D_MODEL:2 * D_MODEL]).astype(BF16)
    oa_ref[0] = _dot(h, wa_ref[...])
    ob_ref[0] = _dot(h, wb_ref[...])
    oc_ref[0] = _dot(h, wc_ref[...])
    od_ref[0] = _dot(h, wd_ref[...])


def _in_projection(x, mod, gain, weights):
    b, t, _ = x.shape
    tb = min(TOKEN_BLOCK, t)
    widths = [w.shape[1] for w in weights]
    return pl.pallas_call(
        _inproj_kernel,
        grid=(b, t // tb),
        in_specs=[pl.BlockSpec((1, tb, D_MODEL), lambda i, j: (i, j, 0)),
                  pl.BlockSpec((1, 1, 6 * D_MODEL), lambda i, j: (i, 0, 0)),
                  _full_spec((1, D_MODEL))] + [_full_spec(w.shape) for w in weights],
        out_specs=[pl.BlockSpec((1, tb, n), lambda i, j: (i, j, 0)) for n in widths],
        out_shape=[jax.ShapeDtypeStruct((b, t, n), F32) for n in widths],
        compiler_params=_params(("arbitrary", "arbitrary")),
        name="in_projection",
    )(x, mod, gain, *weights)


def _decay_scan_prepare(probs, ones_bd, c):
    levels = int(np.log2(c))
    exs = [jnp.exp(_sel_dot(pr["sel"][...], pr["g"])) for pr in probs]
    attns = [None] * len(probs)
    for lv in range(levels + 1):
        for i, (pr, ex) in enumerate(zip(probs, exs)):
            q, k = pr["q"], pr["k"]
            if lv < levels:
                e = ex[lv * c:(lv + 1) * c]
                qt, kt = (q * e).astype(BF16), (k * e).astype(BF16)
            else:
                qt, kt = q.astype(BF16), k.astype(BF16)
            p = _dot_nt(qt, _head_blockdiag(kt, ones_bd)) * pr["mask"][lv]
            attns[i] = p if attns[i] is None else attns[i] + p
    res = []
    for pr, ex, attn in zip(probs, exs, attns):
        o = _dot(attn.astype(BF16), _head_blockdiag(pr["v"].astype(BF16), ones_bd))
        qd = (pr["q"] * ex[levels * c:(levels + 1) * c]).astype(BF16)
        ktail = (pr["k"] * ex[(levels + 1) * c:(levels + 2) * c]).astype(BF16)
        res.append((o, qd, ktail, ex[(levels + 2) * c:(levels + 2) * c + SUBLANES]))
    return res


def _decay_scan_state_steps(items, ones_bd, state_ref):
    sts = [state_ref[it[0]] for it in items]
    outs = [it[1] + _dot_nt(it[2], _head_blockdiag(st.astype(BF16), ones_bd)) for it, st in zip(items, sts)]
    for it, st in zip(items, sts):
        state_ref[it[0]] = st * it[5] + _head_diag_blocks(_dot_tn(it[4], it[3]))
    return outs


def _head_norm_gate(o, gate, onorm, ones_bd):
    ms = _head_sum(o * o, ones_bd) * (1.0 / HEAD_DIM)
    return o * lax.rsqrt(ms + EPS) * onorm * _silu(gate)


def _decay_mixer_kernel(mode, t, c, p_ref, self_ref, selb_ref, maskf_ref, maskb_ref, ones_ref, onorm_ref,
                        pa_ref, pb_ref, pc_ref, h0_ref, y_ref, sout_ref,
                        qd_ref, kt_ref, v16_ref, dl_ref, of_ref, ob_ref, state_ref):
    n_chunks = t // c
    state_ref[...] = h0_ref[0]
    sels, masks, o_refs = (self_ref, selb_ref), (maskf_ref, maskb_ref), (of_ref, ob_ref)

    def inputs(rows, d):
        if mode == "gla":
            q = p_ref[0, rows, 0:GROUP_W] * (HEAD_DIM ** -0.5)
            k = p_ref[0, rows, GROUP_W:2 * GROUP_W]
            v = p_ref[0, rows, 2 * GROUP_W:3 * GROUP_W]
            low = p_ref[0, rows, 4 * GROUP_W:4 * GROUP_W + LANES]
            gk = _dot_hl(low, pa_ref[d], pb_ref[d]) + pc_ref[d]
            g = _log_sigmoid(gk) * (1.0 / GLA_GATE_NORM)
        else:
            q = _silu(p_ref[0, rows, 0:GROUP_W]) * (HEAD_DIM ** -0.5)
            v = p_ref[0, rows, GROUP_W:2 * GROUP_W]
            z = p_ref[0, rows, (3 + d) * GROUP_W:(4 + d) * GROUP_W]
            a = pa_ref[...]
            b = pb_ref[...] + _log_sigmoid(z)
            m = jnp.maximum(a, b)
            g = m + jnp.log(jnp.exp(a - m) + jnp.exp(b - m))
            k = pc_ref[...] * _sigmoid(-z)
        return q, k, v, g

    per_iter = 2 if n_chunks % 2 == 0 else 1

    def prepare(i, carry):
        chunks = [i * per_iter + j for j in range(per_iter)]
        probs = []
        for n in chunks:
            rows = pl.ds(pl.multiple_of(n * c, c), c)
            for d in range(2):
                q, k, v, g = inputs(rows, d)
                probs.append(dict(q=q, k=k, v=v, g=g, sel=sels[d], mask=masks[d]))
            v16_ref[rows, :] = probs[-1]["v"].astype(BF16)
        for idx, (o, qd, ktail, dlast) in enumerate(_decay_scan_prepare(probs, ones_ref[...], c)):
            n, d = chunks[idx // 2], idx % 2
            rows = pl.ds(pl.multiple_of(n * c, c), c)
            o_refs[d][rows, :] = o
            qd_ref[d, rows, :] = qd
            kt_ref[d, rows, :] = ktail
            dl_ref[d, pl.ds(pl.multiple_of(n * SUBLANES, SUBLANES), SUBLANES), :] = dlast
        return carry

    lax.fori_loop(0, n_chunks // per_iter, prepare, 0)

    def body(i, carry):
        items, rows_of = [], []
        for d, n in ((0, i), (1, n_chunks - 1 - i)):
            rows = pl.ds(pl.multiple_of(n * c, c), c)
            dlast = dl_ref[d, pl.ds(pl.multiple_of(n * SUBLANES, SUBLANES), SUBLANES), :][0:1, :]
            items.append((d, o_refs[d][rows, :], qd_ref[d, rows, :], kt_ref[d, rows, :], v16_ref[rows, :], dlast))
            rows_of.append(rows)
        o_f, o_b = _decay_scan_state_steps(items, ones_ref[...], state_ref)
        of_ref[rows_of[0], :] = o_f
        ob_ref[rows_of[1], :] = o_b
        return carry

    lax.fori_loop(0, n_chunks, body, 0)

    rbk = min(t, 256)
    gate_col = 3 * GROUP_W if mode == "gla" else 2 * GROUP_W

    def fin(i, carry):
        rows = pl.ds(pl.multiple_of(i * rbk, rbk), rbk)
        o = of_ref[rows, :] + ob_ref[rows, :]
        gate = p_ref[0, rows, gate_col:gate_col + GROUP_W]
        y_ref[0, rows, :] = _head_norm_gate(o, gate, onorm_ref[...], ones_ref[...]).astype(BF16)
        return carry

    lax.fori_loop(0, t // rbk, fin, 0)
    sout_ref[0] = state_ref[...]


def _decay_mixer(mode, p, pa, pb, pc, onorm, h0t):
    b, t, w = p.shape
    c = GLA_CHUNK
    sel_f, mask_f = _scan_consts(c, False)
    sel_b, mask_b = _scan_consts(c, True)
    ones_bd = _ones_blockdiag()
    consts = [sel_f, sel_b, mask_f, mask_b, ones_bd, onorm, pa, pb, pc]
    state_shape = (2, HEAD_DIM, GROUP_W)
    return pl.pallas_call(
        functools.partial(_decay_mixer_kernel, mode, t, c),
        grid=(b,),
        in_specs=[pl.BlockSpec((1, t, w), lambda i: (i, 0, 0), pipeline_mode=pl.Buffered(1))]
                 + [_full_spec(a.shape) for a in consts]
                 + [pl.BlockSpec((1,) + state_shape, lambda i: (i, 0, 0, 0))],
        out_specs=[pl.BlockSpec((1, t, GROUP_W), lambda i: (i, 0, 0)),
                   pl.BlockSpec((1,) + state_shape, lambda i: (i, 0, 0, 0))],
        out_shape=[jax.ShapeDtypeStruct((b, t, GROUP_W), BF16),
                   jax.ShapeDtypeStruct((b,) + state_shape, F32)],
        scratch_shapes=[pltpu.VMEM((2, t, GROUP_W), BF16), pltpu.VMEM((2, t, GROUP_W), BF16),
                        pltpu.VMEM((t, GROUP_W), BF16), pltpu.VMEM((2, (t // c) * SUBLANES, GROUP_W), F32),
                        pltpu.VMEM((t, GROUP_W), F32), pltpu.VMEM((t, GROUP_W), F32),
                        pltpu.VMEM(state_shape, F32)],
        compiler_params=_params(("arbitrary",)),
        name=mode + "_mixer",
    )(p, *consts, h0t)


def _delta_prepare(probs, ones_bd, expand_ref, nega_ref, dtb_ref, c):
    levels = int(np.log2(c))
    work = []
    for pr in probs:
        d, mask = pr["d"], pr["mask"]
        logits = _dot_sel(pr["ba"], expand_ref[d])
        beta = _sigmoid(logits[:, 0:GROUP_W])
        g = nega_ref[d] * _softplus(logits[:, GROUP_W:2 * GROUP_W] + dtb_ref[d])
        cs = _sel_dot(pr["sel"][...], g)
        b_incl = cs[0:c]
        b_t = jnp.concatenate([b_incl[:, h * HEAD_DIM:(h + 1) * HEAD_DIM].T for h in range(N_HEADS)], axis=1)
        lmat = jnp.exp(jnp.minimum(b_incl - b_t, 0.0)) * mask[levels]
        m = beta * pr["kk"] * lmat * mask[levels + 1]
        work.append(dict(beta=beta, eb=jnp.exp(b_incl), etail=jnp.exp(cs[c:2 * c]),
                         dlast=jnp.exp(cs[2 * c:2 * c + SUBLANES]), lmat=lmat, m=m,
                         t=mask[levels + 2] - m * mask[0]))
    for lv in range(1, levels):
        ps = [_per_head_mm(wk["m"] * pr["mask"][lv], wk["t"], ones_bd) for pr, wk in zip(probs, work)]
        for wk, p in zip(work, ps):
            wk["t"] = wk["t"] - _per_head_mm(wk["t"], p, ones_bd)
    us = [_per_head_mm(wk["t"], pr["v"] * wk["beta"], ones_bd) for pr, wk in zip(probs, work)]
    ws = [_per_head_mm(wk["t"], pr["k"] * wk["beta"] * wk["eb"], ones_bd) for pr, wk in zip(probs, work)]
    return [(u, w.astype(BF16), (pr["qk_raw"] * wk["lmat"]).astype(BF16), (pr["q"] * wk["eb"]).astype(BF16),
             (pr["k"] * wk["etail"]).astype(BF16), wk["dlast"]) for pr, wk, u, w in zip(probs, work, us, ws)]


def _delta_state_steps(items, ones_bd, state_ref):
    sts = [state_ref[it[0]] for it in items]
    bds = [_head_blockdiag(st.astype(BF16), ones_bd) for st in sts]
    v16s = [(it[1] - _dot(it[2], bd)).astype(BF16) for it, bd in zip(items, bds)]
    outs = [_dot(it[4], bd) + _dot(it[3], _head_blockdiag(v16, ones_bd)) for it, bd, v16 in zip(items, bds, v16s)]
    for it, st, v16 in zip(items, sts, v16s):
        state_ref[it[0]] = st * it[6] + _head_diag_blocks(_dot_tn(it[5], v16))
    return outs


def _delta_mixer_kernel(t, c, p_ref, self_ref, selb_ref, maskf_ref, maskb_ref, ones_ref, onorm_ref,
                        conv_ref, expand_ref, nega_ref, dtb_ref, h0_ref, y_ref, sout_ref,
                        u_ref, w_ref, qk_ref, qd_ref, kt_ref, dl_ref, of_ref, ob_ref, state_ref):
    n_chunks = t // c
    qkv_w = 3 * GROUP_W
    state_ref[...] = h0_ref[0]
    row_id = lax.broadcasted_iota(jnp.int32, (c, qkv_w), 0)
    w_prev, w_mid, w_next = conv_ref[0:1, :], conv_ref[1:2, :], conv_ref[2:3, :]
    sels, masks = (self_ref, selb_ref), (maskf_ref, maskb_ref)

    def conv_qkv(n):
        r0 = pl.multiple_of(n * c, c)
        x = p_ref[0, pl.ds(r0, c), 0:qkv_w]
        before = p_ref[0, pl.ds(pl.multiple_of(jnp.maximum(r0 - SUBLANES, 0), SUBLANES), SUBLANES), 0:qkv_w]
        after = p_ref[0, pl.ds(pl.multiple_of(jnp.minimum(r0 + c, t - SUBLANES), SUBLANES), SUBLANES), 0:qkv_w]
        before_row = jnp.where(n > 0, before[SUBLANES - 1:SUBLANES, :], 0.0)
        after_row = jnp.where(n < n_chunks - 1, after[0:1, :], 0.0)
        x_prev = jnp.where(row_id == 0, before_row, pltpu.roll(x, 1, axis=0))
        x_next = jnp.where(row_id == c - 1, after_row, pltpu.roll(x, c - 1, axis=0))
        y = _silu(x_prev * w_prev + x * w_mid + x_next * w_next)
        qk = y[:, 0:2 * GROUP_W]
        ones2 = ones_ref[...]
        ss = jnp.concatenate([_head_sum(qk[:, 0:GROUP_W] * qk[:, 0:GROUP_W], ones2),
                              _head_sum(qk[:, GROUP_W:] * qk[:, GROUP_W:], ones2)], axis=1)
        qk = qk * lax.rsqrt(ss + EPS)
        q, k, v = qk[:, 0:GROUP_W] * (HEAD_DIM ** -0.5), qk[:, GROUP_W:], y[:, 2 * GROUP_W:]
        k_hi, k_lo = _split2(k)
        bd_k = _head_blockdiag(k_hi, ones2)
        kk = _dot_nt(k_hi, bd_k) + _dot_nt(k_hi, _head_blockdiag(k_lo, ones2)) + _dot_nt(k_lo, bd_k)
        qk_raw = _dot_nt(q.astype(BF16), bd_k)
        ba = p_ref[0, pl.ds(r0, c), 4 * GROUP_W:4 * GROUP_W + LANES]
        return dict(q=q, k=k, v=v, kk=kk, qk_raw=qk_raw, ba=ba)

    per_iter = 2 if n_chunks % 2 == 0 else 1

    def prepare(i, carry):
        chunks = [i * per_iter + j for j in range(per_iter)]
        probs = []
        for n in chunks:
            shared = conv_qkv(n)
            probs += [dict(shared, sel=sels[d], mask=masks[d], d=d) for d in range(2)]
        res = _delta_prepare(probs, ones_ref[...], expand_ref, nega_ref, dtb_ref, c)
        for idx, (u, w16, qk16, qd, ktail, dlast) in enumerate(res):
            n, d = chunks[idx // 2], idx % 2
            rows = pl.ds(pl.multiple_of(n * c, c), c)
            u_ref[d, rows, :] = u
            w_ref[d, rows, :] = w16
            qk_ref[d, rows, :] = qk16
            qd_ref[d, rows, :] = qd
            kt_ref[d, rows, :] = ktail
            dl_ref[d, pl.ds(pl.multiple_of(n * SUBLANES, SUBLANES), SUBLANES), :] = dlast
        return carry

    lax.fori_loop(0, n_chunks // per_iter, prepare, 0)

    def body(i, carry):
        items, rows_of = [], []
        for d, n in ((0, i), (1, n_chunks - 1 - i)):
            rows = pl.ds(pl.multiple_of(n * c, c), c)
            dlast = dl_ref[d, pl.ds(pl.multiple_of(n * SUBLANES, SUBLANES), SUBLANES), :][0:1, :]
            items.append((d, u_ref[d, rows, :], w_ref[d, rows, :], qk_ref[d, rows, :], qd_ref[d, rows, :],
                          kt_ref[d, rows, :], dlast))
            rows_of.append(rows)
        o_f, o_b = _delta_state_steps(items, ones_ref[...], state_ref)
        of_ref[rows_of[0], :] = o_f
        ob_ref[rows_of[1], :] = o_b
        return carry

    lax.fori_loop(0, n_chunks, body, 0)

    rbk = min(t, 256)

    def fin(i, carry):
        rows = pl.ds(pl.multiple_of(i * rbk, rbk), rbk)
        o = of_ref[rows, :] + ob_ref[rows, :]
        gate = p_ref[0, rows, 3 * GROUP_W:4 * GROUP_W]
        y_ref[0, rows, :] = _head_norm_gate(o, gate, onorm_ref[...], ones_ref[...]).astype(BF16)
        return carry

    lax.fori_loop(0, t // rbk, fin, 0)
    sout_ref[0] = state_ref[...]


def _delta_mixer(p, conv_w, expand, nega, dtb, onorm, s0):
    b, t, w = p.shape
    c = DN_CHUNK
    sel_f, mask_f = _dn_consts(c, False)
    sel_b, mask_b = _dn_consts(c, True)
    consts = [sel_f, sel_b, mask_f, mask_b, _ones_blockdiag(), onorm, conv_w, expand, nega, dtb]
    state_shape = (2, HEAD_DIM, GROUP_W)
    return pl.pallas_call(
        functools.partial(_delta_mixer_kernel, t, c),
        grid=(b,),
        in_specs=[pl.BlockSpec((1, t, w), lambda i: (i, 0, 0), pipeline_mode=pl.Buffered(1))]
                 + [_full_spec(a.shape) for a in consts]
                 + [pl.BlockSpec((1,) + state_shape, lambda i: (i, 0, 0, 0))],
        out_specs=[pl.BlockSpec((1, t, GROUP_W), lambda i: (i, 0, 0)),
                   pl.BlockSpec((1,) + state_shape, lambda i: (i, 0, 0, 0))],
        out_shape=[jax.ShapeDtypeStruct((b, t, GROUP_W), BF16),
                   jax.ShapeDtypeStruct((b,) + state_shape, F32)],
        scratch_shapes=[pltpu.VMEM((2, t, GROUP_W), F32)]
                       + [pltpu.VMEM((2, t, GROUP_W), BF16)] * 4
                       + [pltpu.VMEM((2, (t // c) * SUBLANES, GROUP_W), F32),
                          pltpu.VMEM((t, GROUP_W), F32), pltpu.VMEM((t, GROUP_W), F32),
                          pltpu.VMEM(state_shape, F32)],
        compiler_params=_params(("arbitrary",)),
        name="delta_mixer",
    )(p, *consts, s0)


def _s5_kernel(t, tc, u_ref, bh_ref, bl_ref, cm_ref, lbar_ref, dskip_ref, gluw_ref, glub_ref, h0_ref,
               y_ref, sout_ref, x_ref, yf_ref, yb_ref, carry_ref):
    n_chunks = t // tc
    n_lane_blocks = S5_STATE // LANES
    carry_ref[...] = h0_ref[0]
    sub8 = lax.broadcasted_iota(jnp.int32, (SUBLANES, LANES), 0)

    def scan_dirs(rows_of, y_refs):
        for d in range(2):
            u_hi, u_lo = _split2(u_ref[0, rows_of[d], :])
            for j in range(n_lane_blocks):
                x_ref[d, j] = (_dot(u_hi, bh_ref[d, j]) + _dot(u_hi, bl_ref[d, j]) + _dot(u_lo, bh_ref[d, j]))
            y_refs[d][rows_of[d], :] = jnp.zeros((tc, GROUP_W), F32)

        def scan_block(d, j):
            bu = x_ref[d, j]
            xr, xi = bu[:, 0:LANES], bu[:, LANES:2 * LANES]
            lb = lbar_ref[d, j]
            pw = [(lb[0:1, :], lb[1:2, :])]
            for _ in range(SUBLANES - 1):
                pr, pi = pw[-1]
                pw.append((pr * pw[0][0] - pi * pw[0][1], pr * pw[0][1] + pi * pw[0][0]))
            n_groups = tc // SUBLANES
            blocks_r = [xr[v * SUBLANES:(v + 1) * SUBLANES] for v in range(n_groups)]
            blocks_i = [xi[v * SUBLANES:(v + 1) * SUBLANES] for v in range(n_groups)]
            for s in range(int(np.log2(SUBLANES))):
                sh = 1 << s
                keep = sub8 < SUBLANES - sh if d else sub8 >= sh
                m_re, m_im = jnp.where(keep, pw[sh - 1][0], 0.0), jnp.where(keep, pw[sh - 1][1], 0.0)
                rot = SUBLANES - sh if d else sh
                for v in range(n_groups):
                    sr, si = pltpu.roll(blocks_r[v], rot, axis=0), pltpu.roll(blocks_i[v], rot, axis=0)
                    blocks_r[v] = blocks_r[v] + m_re * sr - m_im * si
                    blocks_i[v] = blocks_i[v] + m_re * si + m_im * sr
            p_re = jnp.zeros((SUBLANES, LANES), F32)
            p_im = jnp.zeros((SUBLANES, LANES), F32)
            for r in range(SUBLANES):
                pr, pi = pw[SUBLANES - 1 - r] if d else pw[r]
                p_re = jnp.where(sub8 == r, pr, p_re)
                p_im = jnp.where(sub8 == r, pi, p_im)
            cr, ci = carry_ref[d, 0, j], carry_ref[d, 1, j]
            last = 0 if d else SUBLANES - 1
            for step in range(n_groups):
                v = n_groups - 1 - step if d else step
                blocks_r[v] = blocks_r[v] + (p_re * cr - p_im * ci)
                blocks_i[v] = blocks_i[v] + (p_re * ci + p_im * cr)
                cr, ci = blocks_r[v][last:last + 1, :], blocks_i[v][last:last + 1, :]
            carry_ref[d, 0, j] = cr
            carry_ref[d, 1, j] = ci
            return jnp.concatenate([jnp.concatenate(blocks_r, axis=0), jnp.concatenate(blocks_i, axis=0)],
                                   axis=1).astype(BF16)

        def lane_block(j, carry):
            states = [scan_block(d, j) for d in range(2)]
            for d in range(2):
                y_refs[d][rows_of[d], :] += _dot(states[d], cm_ref[j])
            return carry

        lax.fori_loop(0, n_lane_blocks, lane_block, 0)

    def body(n, carry):
        scan_dirs((pl.ds(pl.multiple_of(n * tc, tc), tc), pl.ds(pl.multiple_of((n_chunks - 1 - n) * tc, tc), tc)),
                  (yf_ref, yb_ref))
        return carry

    lax.fori_loop(0, n_chunks, body, 0)

    def fin(i, carry):
        rows = pl.ds(pl.multiple_of(i * tc, tc), tc)
        u = u_ref[0, rows, :]
        y = _gelu_tanh(yf_ref[rows, :] + yb_ref[rows, :] + dskip_ref[...] * u)
        z = _dot(y.astype(BF16), gluw_ref[...]) + glub_ref[...]
        y_ref[0, rows, :] = (y * _sigmoid(z)).astype(BF16)
        return carry

    lax.fori_loop(0, n_chunks, fin, 0)
    sout_ref[0] = carry_ref[...]


def _s5_mixer(u, b_hi, b_lo, cmat, lbar, dskip, glu_w, glu_b, h0):
    b, t, _ = u.shape
    tc = min(S5_CHUNK, t)
    nlb = S5_STATE // LANES
    consts = [b_hi, b_lo, cmat, lbar, dskip, glu_w, glu_b]
    st_shape = (2, 2, nlb, 1, LANES)
    return pl.pallas_call(
        functools.partial(_s5_kernel, t, tc),
        grid=(b,),
        in_specs=[pl.BlockSpec((1, t, GROUP_W), lambda i: (i, 0, 0))]
                 + [_full_spec(a.shape) for a in consts]
                 + [pl.BlockSpec((1,) + st_shape, lambda i: (i, 0, 0, 0, 0, 0))],
        out_specs=[pl.BlockSpec((1, t, GROUP_W), lambda i: (i, 0, 0)),
                   pl.BlockSpec((1,) + st_shape, lambda i: (i, 0, 0, 0, 0, 0))],
        out_shape=[jax.ShapeDtypeStruct((b, t, GROUP_W), BF16),
                   jax.ShapeDtypeStruct((b,) + st_shape, F32)],
        scratch_shapes=[pltpu.VMEM((2, nlb, tc, 2 * LANES), F32), pltpu.VMEM((t, GROUP_W), F32),
                        pltpu.VMEM((t, GROUP_W), F32), pltpu.VMEM(st_shape, F32)],
        compiler_params=_params(("arbitrary",)),
        name="s5_mixer",
    )(u, *consts, h0)


def _outproj_kernel(x_ref, mod_ref, ya_ref, yb_ref, yc_ref, yd_ref, w_ref, o_ref):
    y = (_dot(ya_ref[0], w_ref[0]) + _dot(yb_ref[0], w_ref[1])
         + _dot(yc_ref[0], w_ref[2]) + _dot(yd_ref[0], w_ref[3]))
    o_ref[0] = x_ref[0] + mod_ref[0][:, 2 * D_MODEL:3 * D_MODEL] * y


def _out_projection(x, mod, ys, w_out):
    b, t, _ = x.shape
    tb = min(TOKEN_BLOCK, t)
    tok = lambda n: pl.BlockSpec((1, tb, n), lambda i, j: (i, j, 0))
    return pl.pallas_call(
        _outproj_kernel,
        grid=(b, t // tb),
        in_specs=[tok(D_MODEL), pl.BlockSpec((1, 1, 6 * D_MODEL), lambda i, j: (i, 0, 0))]
                 + [tok(GROUP_W)] * 4 + [_full_spec(w_out.shape)],
        out_specs=tok(D_MODEL),
        out_shape=jax.ShapeDtypeStruct(x.shape, F32),
        compiler_params=_params(("arbitrary", "arbitrary")),
        name="out_projection",
    )(x, mod, *ys, w_out)


def _topk_rows(scores, k, row_ids):
    n = scores[0].shape[1]
    slot = lax.broadcasted_iota(jnp.int32, (k, n), 0)
    scores = list(scores)
    vals = [jnp.zeros((k, n), F32) for _ in scores]
    ids = [jnp.zeros((k, n), F32) for _ in scores]
    for it in range(k):
        ms = [jnp.max(s, axis=0, keepdims=True) for s in scores]
        idxs = [jnp.min(jnp.where(s == m, row_ids, 1e9), axis=0, keepdims=True) for s, m in zip(scores, ms)]
        vals = [jnp.where(slot == it, m, v) for m, v in zip(ms, vals)]
        ids = [jnp.where(slot == it, idx, i) for idx, i in zip(idxs, ids)]
        scores = [jnp.where(row_ids == idx, -jnp.inf, s) for s, idx in zip(scores, idxs)]
    return list(zip(vals, ids))


def _staircase():
    pairs = [(i, j) for i in range(PEER_TOPK) for j in range(PEER_TOPK) if (i + 1) * (j + 1) <= PEER_TOPK]
    rows = -(-len(pairs) // SUBLANES) * SUBLANES
    sel = np.zeros((2, rows, PEER_TOPK), np.float32)
    aux = np.zeros((2, rows, LANES), np.float32)
    aux[0, len(pairs):] = -np.inf
    aux[1, len(pairs):] = 1e9
    for r, (i, j) in enumerate(pairs):
        sel[0, r, i] = 1.0
        sel[1, r, j] = 1.0
        aux[1, r] = i * PEER_TOPK + j
    return jnp.asarray(np.tile(sel, (1, 1, 3)), BF16), jnp.asarray(aux, F32)


def _take_rows(table, idx, k):
    out = jnp.zeros(idx.shape, F32)
    for r in range(k):
        out = jnp.where(idx == float(r), table[r:r + 1, :], out)
    return out


def _peer_route_kernel(tb, x_ref, mod_ref, g_ref, wqh_ref, wql_ref, kh_ref, kl_ref, stsel_ref, staux_ref,
                       h_ref, a_ref, c_ref, gate_ref, qt_ref, at_ref, ct_ref, gt_ref):
    mod = mod_ref[0]
    h = _rms_modulate(x_ref[0], g_ref[...], mod[:, 3 * D_MODEL:4 * D_MODEL], mod[:, 4 * D_MODEL:5 * D_MODEL])
    h_hi, h_lo = _split2(h)
    h_ref[0] = h_hi
    wq_hi = wqh_ref[...]
    qt_ref[...] = _dot_nt(wq_hi, h_hi) + _dot_nt(wq_hi, h_lo) + _dot_nt(wql_ref[...], h_hi)
    half = PEER_DQ // 2
    key_ids = lax.broadcasted_iota(jnp.int32, (PEER_NKEYS, LANES), 0).astype(F32)

    def head_body(head, carry):
        out_rows = pl.ds(pl.multiple_of(head * PEER_TOPK, PEER_TOPK), PEER_TOPK)
        tiles = [slice(lt * LANES, (lt + 1) * LANES) for lt in range(tb // LANES)]
        scores = []
        for lanes in tiles:
            for p in range(2):
                hp = head * 2 + p
                q_hi, q_lo = _split2(qt_ref[pl.ds(pl.multiple_of(hp * half, half), half), lanes])
                scores.append(_dot(kh_ref[hp], q_hi) + _dot(kh_ref[hp], q_lo) + _dot(kl_ref[hp], q_hi))
        top = _topk_rows(scores, PEER_TOPK, key_ids)
        cands = [_sel_dot(stsel_ref[0], top[2 * i][0]) + _sel_dot(stsel_ref[1], top[2 * i + 1][0]) + staux_ref[0]
                 for i in range(len(tiles))]
        best = _topk_rows(cands, PEER_TOPK, staux_ref[1])
        for i, lanes in enumerate(tiles):
            sc, flat = best[i]
            r1 = jnp.floor(flat * (1.0 / PEER_TOPK))
            r2 = flat - r1 * PEER_TOPK
            e = jnp.exp(sc - sc[0:1, :])
            at_ref[out_rows, lanes] = _take_rows(top[2 * i][1], r1, PEER_TOPK)
            ct_ref[out_rows, lanes] = _take_rows(top[2 * i + 1][1], r2, PEER_TOPK)
            gt_ref[out_rows, lanes] = e / jnp.sum(e, axis=0, keepdims=True)
        return carry

    lax.fori_loop(0, PEER_HEADS, head_body, 0)
    a_ref[0] = at_ref[...].T.astype(jnp.int32)
    c_ref[0] = ct_ref[...].T.astype(jnp.int32)
    gate_ref[0] = gt_ref[...].T


def _peer_route(x, mod, gain, wq_hi, wq_lo, keys_hi, keys_lo):
    b, t, _ = x.shape
    tb = min(TOKEN_BLOCK, t)
    tok = lambda n: pl.BlockSpec((1, tb, n), lambda i, j: (i, j, 0))
    consts = [gain, wq_hi, wq_lo, keys_hi, keys_lo, *_staircase()]
    return pl.pallas_call(
        functools.partial(_peer_route_kernel, tb),
        grid=(b, t // tb),
        scratch_shapes=[pltpu.VMEM((PEER_HEADS * PEER_DQ, tb), F32)] + [pltpu.VMEM((PEER_SEL, tb), F32)] * 3,
        in_specs=[tok(D_MODEL), pl.BlockSpec((1, 1, 6 * D_MODEL), lambda i, j: (i, 0, 0))]
                 + [_full_spec(a.shape) for a in consts],
        out_specs=[tok(D_MODEL), tok(PEER_SEL), tok(PEER_SEL), tok(PEER_SEL)],
        out_shape=[jax.ShapeDtypeStruct((b, t, D_MODEL), BF16),
                   jax.ShapeDtypeStruct((b, t, PEER_SEL), jnp.int32),
                   jax.ShapeDtypeStruct((b, t, PEER_SEL), jnp.int32),
                   jax.ShapeDtypeStruct((b, t, PEER_SEL), F32)],
        compiler_params=_params(("arbitrary", "arbitrary")),
        name="peer_route",
    )(x, mod, *consts)


def _peer_expert_kernel(tb, n_slabs, x_ref, mod_ref, h_ref, a_ref, c_ref, gate_ref, ut_ref, v_ref,
                        o_ref, gmat_ref, acc_ref):
    j = pl.program_id(2)
    nk = PEER_NKEYS

    @pl.when(j == 0)
    def _():
        acc_ref[...] = jnp.zeros_like(acc_ref)
        key_id = lax.broadcasted_iota(jnp.int32, (nk, PEER_SEL), 0)

        groups = 4

        def build(nb, carry):
            n0 = pl.multiple_of(nb * groups * SUBLANES, groups * SUBLANES)
            a_rows = a_ref[0, pl.ds(n0, groups * SUBLANES), :]
            c_rows = c_ref[0, pl.ds(n0, groups * SUBLANES), :]
            g_rows = gate_ref[0, pl.ds(n0, groups * SUBLANES), :]
            mats = []
            for r in range(groups * SUBLANES):
                row_sel = jnp.where(key_id == a_rows[r:r + 1], 1.0, 0.0).astype(BF16)
                col_sel = jnp.where(key_id == c_rows[r:r + 1], g_rows[r:r + 1], 0.0).astype(BF16)
                mats.append(_dot_nt(row_sel, col_sel))
            for gi in range(groups):
                stacked = jnp.stack(mats[gi * SUBLANES:(gi + 1) * SUBLANES], axis=0)
                gmat_ref[:, pl.ds(n0 + gi * SUBLANES, SUBLANES), :] = pltpu.einshape("nac->anc", stacked)
            return carry

        lax.fori_loop(0, tb // (groups * SUBLANES), build, 0)

    h = h_ref[0]
    n_pairs = n_slabs // 2
    cols = [slice(i * 2 * nk, (i + 1) * 2 * nk) for i in range(n_pairs)]
    lookahead = 2
    s = [_dot(h, ut_ref[:, cols[i]]) for i in range(min(lookahead, n_pairs))]
    acc = None
    for i in range(n_pairs):
        if i + lookahead < n_pairs:
            s.append(_dot(h, ut_ref[:, cols[i + lookahead]]))
        a0 = j * n_slabs + 2 * i
        g = jnp.concatenate([gmat_ref[a0], gmat_ref[a0 + 1]], axis=1)
        w = (g * _gelu_tanh(s[i])).astype(BF16)
        p = _dot(w, v_ref[cols[i], :])
        acc = p if acc is None else acc + p
    acc_ref[...] += acc

    @pl.when(j == pl.num_programs(2) - 1)
    def _():
        o_ref[0] = x_ref[0] + mod_ref[0][:, 5 * D_MODEL:6 * D_MODEL] * acc_ref[...]


def _peer_experts(x, mod, h, a, c, gate, u_t, v_tab):
    b, t, _ = x.shape
    tb = min(TOKEN_BLOCK, t)
    n_exp = u_t.shape[1]
    eb = PEER_EXPERT_BLOCK
    n_slabs = eb // PEER_NKEYS
    tok = lambda n: pl.BlockSpec((1, tb, n), lambda i, k, j: (i, k, 0))
    return pl.pallas_call(
        functools.partial(_peer_expert_kernel, tb, n_slabs),
        grid=(b, t // tb, n_exp // eb),
        in_specs=[tok(D_MODEL), pl.BlockSpec((1, 1, 6 * D_MODEL), lambda i, k, j: (i, 0, 0)),
                  tok(D_MODEL), tok(PEER_SEL), tok(PEER_SEL), tok(PEER_SEL),
                  pl.BlockSpec((D_MODEL, eb), lambda i, k, j: (0, j)),
                  pl.BlockSpec((eb, D_MODEL), lambda i, k, j: (j, 0))],
        out_specs=tok(D_MODEL),
        out_shape=jax.ShapeDtypeStruct(x.shape, F32),
        scratch_shapes=[pltpu.VMEM((PEER_NKEYS, tb, PEER_NKEYS), F32), pltpu.VMEM((tb, D_MODEL), F32)],
        compiler_params=_params(("arbitrary", "arbitrary", "arbitrary")),
        name="peer_experts",
    )(x, mod, h, a, c, gate, u_t, v_tab)


def _final_norm_kernel(x_ref, g_ref, o_ref):
    x = x_ref[0]
    ms = jnp.mean(x * x, axis=-1, keepdims=True)
    o_ref[0] = x * lax.rsqrt(ms + EPS) * g_ref[...]


def _final_norm(x, gain):
    b, t, _ = x.shape
    tb = min(TOKEN_BLOCK, t)
    tok = pl.BlockSpec((1, tb, D_MODEL), lambda i, j: (i, j, 0))
    return pl.pallas_call(
        _final_norm_kernel,
        grid=(b, t // tb),
        in_specs=[tok, _full_spec((1, D_MODEL))],
        out_specs=tok,
        out_shape=jax.ShapeDtypeStruct(x.shape, F32),
        compiler_params=_params(("arbitrary", "arbitrary")),
        name="final_norm",
    )(x, gain)


def _hi_lo(w):
    hi = w.astype(BF16)
    return hi, (w - hi.astype(F32)).astype(BF16)


def _pad_cols(w, n):
    return jnp.pad(w, ((0, 0), (0, n - w.shape[1])))


def _layer_weights(l, w_in, w_out, gla_gk_up, gla_gk_bias, gla_onorm_g, dn_conv_w, dn_A_log, dn_dt_bias,
                   dn_onorm_g, s5_lambda_re, s5_lambda_im, s5_log_dt, s5_B_re, s5_B_im, s5_C_re, s5_C_im,
                   s5_D, s5_glu_w, s5_glu_b, lower_bound, hgrn_onorm_g, peer_wq, peer_keys, peer_u, peer_v):
    gw = GROUP_W
    sizes = (gw, gw, gw, gw, 2 * GLA_LOWRANK, gw, gw, gw, gw, 2 * N_HEADS, 2 * N_HEADS, gw, gw, gw, gw, 2 * gw)
    offs = np.concatenate([[0], np.cumsum(sizes)])
    col = lambda a, b: w_in[l][:, int(offs[a]):int(offs[b])]
    wl = {}
    wl["w_gla"] = _pad_cols(col(0, 5), GLA_W).astype(BF16)
    wl["w_dn"] = _pad_cols(col(5, 11), DN_W).astype(BF16)
    wl["w_s5"] = col(11, 12).astype(BF16)
    wl["w_hg"] = col(12, 16).astype(BF16)
    wl["w_out"] = w_out[l].reshape(4, gw, D_MODEL).astype(BF16)
    tile_head = lambda g: jnp.tile(g, N_HEADS).reshape(1, gw)
    up = jnp.zeros((2, LANES, gw), F32)
    for d in range(2):
        up = up.at[d, d * GLA_LOWRANK:(d + 1) * GLA_LOWRANK, :].set(gla_gk_up[l, d])
    wl["gla_up_hi"], wl["gla_up_lo"] = _hi_lo(up)
    wl["gla_bias"] = gla_gk_bias[l].reshape(2, 1, gw)
    wl["gla_onorm"] = tile_head(gla_onorm_g[l])
    expand = np.zeros((2, LANES, 2 * gw), np.float32)
    for d in range(2):
        for h in range(N_HEADS):
            expand[d, d * N_HEADS + h, h * HEAD_DIM:(h + 1) * HEAD_DIM] = 1.0
            expand[d, 2 * N_HEADS + d * N_HEADS + h, gw + h * HEAD_DIM:gw + (h + 1) * HEAD_DIM] = 1.0
    wl["dn_expand"] = jnp.asarray(expand, BF16)
    wl["dn_conv"] = jnp.pad(dn_conv_w[l], ((0, SUBLANES - 3), (0, 0)))
    wl["dn_nega"] = jnp.repeat(-jnp.exp(dn_A_log[l]), HEAD_DIM, axis=-1).reshape(2, 1, gw)
    wl["dn_dtb"] = jnp.repeat(dn_dt_bias[l], HEAD_DIM, axis=-1).reshape(2, 1, gw)
    wl["dn_onorm"] = tile_head(dn_onorm_g[l])
    nlb = S5_STATE // LANES
    lam_re, lam_im = s5_lambda_re[l].astype(F32), s5_lambda_im[l].astype(F32)
    dt = jnp.exp(s5_log_dt[l].astype(F32))[:, :, None]
    mag = jnp.exp(lam_re * dt)
    lbar_re, lbar_im = mag * jnp.cos(lam_im * dt), mag * jnp.sin(lam_im * dt)
    den = lam_re * lam_re + lam_im * lam_im
    num_re, num_im = lbar_re - 1.0, lbar_im
    coef_re = (num_re * lam_re + num_im * lam_im) / den
    coef_im = (num_im * lam_re - num_re * lam_im) / den
    b_re, b_im = s5_B_re[l].astype(F32), s5_B_im[l].astype(F32)
    bb_re = coef_re[..., None] * b_re - coef_im[..., None] * b_im
    bb_im = coef_re[..., None] * b_im + coef_im[..., None] * b_re
    eye_g = jnp.eye(S5_G, dtype=F32)
    bd_in = lambda m: jnp.einsum('dgpc,gk->dgckp', m, eye_g).reshape(2, gw, S5_STATE)
    blk = lambda m: m.reshape(2, gw, nlb, LANES).transpose(0, 2, 1, 3)
    b_cat = jnp.concatenate([blk(bd_in(bb_re)), blk(bd_in(bb_im))], axis=-1)
    wl["s5_b_hi"], wl["s5_b_lo"] = _hi_lo(b_cat)
    bd_out = lambda m: jnp.einsum('gcp,gk->gpkc', m.astype(F32), eye_g).reshape(S5_STATE, gw)
    c_re, c_im = bd_out(s5_C_re[l]), bd_out(s5_C_im[l])
    wl["s5_c"] = jnp.concatenate([c_re.reshape(nlb, LANES, gw), -c_im.reshape(nlb, LANES, gw)],
                                 axis=1).astype(BF16)
    wl["s5_lbar"] = jnp.stack([lbar_re.reshape(2, nlb, LANES), lbar_im.reshape(2, nlb, LANES)], axis=2)
    wl["s5_d"] = s5_D[l].reshape(1, gw)
    wl["s5_glu_w"] = s5_glu_w[l].astype(BF16)
    wl["s5_glu_b"] = s5_glu_b[l].reshape(1, gw)
    wl["hg_loglb"] = jnp.log(lower_bound).reshape(1, gw)
    wl["hg_log1m"] = jnp.log1p(-lower_bound).reshape(1, gw)
    wl["hg_1m"] = (1.0 - lower_bound).reshape(1, gw)
    wl["hg_onorm"] = tile_head(hgrn_onorm_g[l])
    wl["wq_hi"], wl["wq_lo"] = _hi_lo(peer_wq[l].T)
    wl["keys_hi"], wl["keys_lo"] = _hi_lo(peer_keys[l].reshape(PEER_HEADS * 2, PEER_NKEYS, PEER_DQ // 2))
    wl["u_t"] = peer_u[l].astype(BF16).T
    wl["v_tab"] = peer_v[l].astype(BF16)
    return wl


def _heads_to_lanes(s, transposed):
    perm = (0, 1, 4, 2, 3) if transposed else (0, 1, 3, 2, 4)
    return s.astype(F32).transpose(perm).reshape(s.shape[0], 2, HEAD_DIM, GROUP_W)


def _lanes_to_heads(s, transposed):
    s = s.reshape(s.shape[0], 2, HEAD_DIM, N_HEADS, HEAD_DIM)
    return s.transpose((0, 1, 3, 4, 2) if transposed else (0, 1, 3, 2, 4))


def _to_column_major(x):
    b, n, d = x.shape
    return x.reshape(b, n // GRID_W, GRID_W, d).transpose(0, 2, 1, 3).reshape(b, n, d)


def _to_row_major(x):
    b, n, d = x.shape
    return x.reshape(b, GRID_W, n // GRID_W, d).transpose(0, 2, 1, 3).reshape(b, n, d)


def _trunk_layer(x, mod, wl, norm1_g, norm2_g, states):
    st_gla, st_dn, st_s5, st_hg = states
    p_gla, p_dn, p_s5, p_hg = _in_projection(x, mod, norm1_g, [wl["w_gla"], wl["w_dn"], wl["w_s5"], wl["w_hg"]])
    y_a, n_gla = _decay_mixer("gla", p_gla, wl["gla_up_hi"], wl["gla_up_lo"], wl["gla_bias"], wl["gla_onorm"], st_gla)
    y_b, n_dn = _delta_mixer(p_dn, wl["dn_conv"], wl["dn_expand"], wl["dn_nega"], wl["dn_dtb"], wl["dn_onorm"], st_dn)
    y_c, n_s5 = _s5_mixer(p_s5, wl["s5_b_hi"], wl["s5_b_lo"], wl["s5_c"], wl["s5_lbar"], wl["s5_d"],
                          wl["s5_glu_w"], wl["s5_glu_b"], st_s5)
    y_d, n_hg = _decay_mixer("hgrn", p_hg, wl["hg_loglb"], wl["hg_log1m"], wl["hg_1m"], wl["hg_onorm"], st_hg)
    x = _out_projection(x, mod, [y_a, y_b, y_c, y_d], wl["w_out"])
    h, a, c, gate = _peer_route(x, mod, norm2_g, wl["wq_hi"], wl["wq_lo"], wl["keys_hi"], wl["keys_lo"])
    x = _peer_experts(x, mod, h, a, c, gate, wl["u_t"], wl["v_tab"])
    return x, (n_gla, n_dn, n_s5, n_hg)


def kernel(x_prompt, x_sample, state_gla, state_delta, state_s5_re, state_s5_im, state_hgrn, c, c_ctx,
           mod_w, mod_b, norm1_g, norm2_g, w_in, w_out, gla_gk_up, gla_gk_bias, gla_onorm_g,
           dn_conv_w, dn_A_log, dn_dt_bias, dn_onorm_g, s5_lambda_re, s5_lambda_im, s5_log_dt,
           s5_B_re, s5_B_im, s5_C_re, s5_C_im, s5_D, s5_glu_w, s5_glu_b, hgrn_lb_logits,
           hgrn_onorm_g, peer_wq, peer_keys, peer_u, peer_v, final_norm_g):
    bp, bs = x_prompt.shape[0], x_sample.shape[0]
    nlb = S5_STATE // LANES
    lb_cum = jnp.cumsum(jax.nn.softmax(hgrn_lb_logits.astype(F32), axis=0), axis=0)
    lower_bounds = lb_cum - lb_cum[0:1]

    rows = 2 * SUBLANES
    cc = jnp.concatenate([c, c_ctx[None, :], jnp.zeros((rows - bs - 1, D_MODEL), F32)], axis=0)
    mod_all = _modulation(cc, mod_w, mod_b)

    zeros_hh = jnp.zeros((bp, 2, HEAD_DIM, GROUP_W), F32)
    zeros_s5 = jnp.zeros((bp, 2, 2, nlb, 1, LANES), F32)

    xp, xs = x_prompt, x_sample
    ctx_states = []
    for l in range(DEPTH):
        wl = _layer_weights(l, w_in, w_out, gla_gk_up, gla_gk_bias, gla_onorm_g, dn_conv_w, dn_A_log,
                            dn_dt_bias, dn_onorm_g, s5_lambda_re, s5_lambda_im, s5_log_dt, s5_B_re, s5_B_im,
                            s5_C_re, s5_C_im, s5_D, s5_glu_w, s5_glu_b, lower_bounds[l], hgrn_onorm_g,
                            peer_wq, peer_keys, peer_u, peer_v)
        n1, n2 = norm1_g[l].reshape(1, D_MODEL), norm2_g[l].reshape(1, D_MODEL)
        mod_ctx = jnp.broadcast_to(mod_all[l, bs].reshape(1, 1, 6 * D_MODEL), (bp, 1, 6 * D_MODEL))
        mod_lat = mod_all[l, 0:bs].reshape(bs, 1, 6 * D_MODEL)
        xp, st_ctx = _trunk_layer(xp, mod_ctx, wl, n1, n2, (zeros_hh, zeros_hh, zeros_s5, zeros_hh))
        ctx_states.append(st_ctx)
        cached_s5 = jnp.stack([state_s5_re[:, l], state_s5_im[:, l]], axis=2).reshape(bs, 2, 2, nlb, 1, LANES)
        cached = (_heads_to_lanes(state_gla[:, l], True), _heads_to_lanes(state_delta[:, l], False), cached_s5,
                  _heads_to_lanes(state_hgrn[:, l], True))
        if l % 2 == 1:
            xs = _to_column_major(xs)
        xs, _ = _trunk_layer(xs, mod_lat, wl, n1, n2, cached)
        if l % 2 == 1:
            xs = _to_row_major(xs)

    gain = final_norm_g.reshape(1, D_MODEL)
    y_prompt = _final_norm(xp, gain)
    y_sample = _final_norm(xs, gain)
    new_gla = jnp.stack([_lanes_to_heads(s[0], True) for s in ctx_states], axis=1)
    new_dn = jnp.stack([_lanes_to_heads(s[1], False) for s in ctx_states], axis=1)
    s5 = jnp.stack([s[2].reshape(bp, 2, 2, S5_G, S5_P) for s in ctx_states], axis=1)
    new_hg = jnp.stack([_lanes_to_heads(s[3], True) for s in ctx_states], axis=1)
    return (y_prompt, y_sample, new_gla, new_dn, s5[:, :, :, 0], s5[:, :, :, 1], new_hg)
```

```python
import functools

import numpy as np
import jax
import jax.numpy as jnp
from jax import lax
from jax.experimental import pallas as pl
from jax.experimental.pallas import tpu as pltpu

F32 = jnp.float32
BF16 = jnp.bfloat16

D_MODEL = 1024
DEPTH = 2
GRID_W = 64
GROUP_W = 256
HEAD_DIM = 64
N_HEADS = GROUP_W // HEAD_DIM
GLA_LOWRANK = 16
GLA_GATE_NORM = 16.0
DN_CHUNK = 64
S5_GROUP = 16
S5_G = GROUP_W // S5_GROUP
S5_P = 64
S5_STATE = S5_G * S5_P
PEER_HEADS = 8
PEER_NKEYS = 128
PEER_DQ = 256
PEER_TOPK = 16
PEER_SEL = PEER_HEADS * PEER_TOPK
EPS = 1e-6

LANES = 128
SUBLANES = 8
VMEM_LIMIT = 56 * 1024 * 1024

GLA_CHUNK = 64
S5_CHUNK = 256
TOKEN_BLOCK = 256
PEER_TOKEN_BLOCK = 512
PEER_EXPERT_BLOCK = 2048
GLA_W = 4 * GROUP_W + LANES
DN_W = 4 * GROUP_W + LANES
HG_W = 5 * GROUP_W


def _dot(a, b):
    return jnp.dot(a, b, preferred_element_type=F32)


def _dot_nt(a, b):
    return lax.dot_general(a, b, (((1,), (1,)), ((), ())), preferred_element_type=F32)


def _dot_tn(a, b):
    return lax.dot_general(a, b, (((0,), (0,)), ((), ())), preferred_element_type=F32)


def _split2(x):
    hi = x.astype(BF16)
    lo = (x - hi.astype(F32)).astype(BF16)
    return hi, lo


def _split3(x):
    hi = x.astype(BF16)
    r = x - hi.astype(F32)
    mid = r.astype(BF16)
    lo = (r - mid.astype(F32)).astype(BF16)
    return hi, mid, lo


def _sel_dot(m01x3, x):
    return _dot(m01x3, jnp.concatenate(_split3(x), axis=0))


def _dot_sel(x, m01):
    hi, mid, lo = _split3(x)
    return _dot(hi, m01) + _dot(mid, m01) + _dot(lo, m01)


def _dot_hl(a, b_hi, b_lo):
    a_hi, a_lo = _split2(a)
    return _dot(a_hi, b_hi) + _dot(a_hi, b_lo) + _dot(a_lo, b_hi)


def _head_sum(x, ones_bd):
    hi, lo = _split2(x)
    return _dot(hi, ones_bd) + _dot(lo, ones_bd)


def _head_blockdiag(x16, ones_bd):
    return jnp.tile(x16, (N_HEADS, 1)) * ones_bd


def _per_head_mm(a, x, ones_bd):
    a_hi, a_lo = _split2(a)
    x_hi, x_lo = _split2(x)
    bd_hi = _head_blockdiag(x_hi, ones_bd)
    return _dot(a_hi, bd_hi) + _dot(a_hi, _head_blockdiag(x_lo, ones_bd)) + _dot(a_lo, bd_hi)


def _per_head_mm1(a, x, ones_bd):
    return _dot(a.astype(BF16), _head_blockdiag(x.astype(BF16), ones_bd))


def _head_diag_blocks(full):
    lane_head = lax.broadcasted_iota(jnp.int32, (HEAD_DIM, GROUP_W), 1) // HEAD_DIM
    out = None
    for h in range(N_HEADS):
        blk = jnp.where(lane_head == h, full[h * HEAD_DIM:(h + 1) * HEAD_DIM, :], 0.0)
        out = blk if out is None else out + blk
    return out


def _log_sigmoid(x):
    return jnp.minimum(x, 0.0) - jnp.log(1.0 + jnp.exp(-jnp.abs(x)))


def _softplus(x):
    return jnp.maximum(x, 0.0) + jnp.log(1.0 + jnp.exp(-jnp.abs(x)))


def _sigmoid(x):
    return 1.0 / (1.0 + jnp.exp(-x))


def _silu(x):
    return x * _sigmoid(x)


def _gelu_tanh(x):
    return 0.5 * x * (1.0 + jnp.tanh(0.7978845608028654 * (x + 0.044715 * x * x * x)))


def _rms_modulate(x, gain, shift, scale):
    ms = jnp.mean(x * x, axis=-1, keepdims=True)
    return x * lax.rsqrt(ms + EPS) * gain * (1.0 + scale) + shift


def _full_spec(shape):
    zeros = (0,) * len(shape)
    return pl.BlockSpec(shape, lambda *_: zeros)


def _params(sem):
    return pltpu.CompilerParams(dimension_semantics=sem, vmem_limit_bytes=VMEM_LIMIT)


def _scan_consts(c, reverse):
    levels = int(np.log2(c))
    idx = np.arange(c)
    i, t = idx[:, None], idx[None, :]
    sel, masks = [], []
    for lv in range(levels):
        s = 1 << lv
        right = (idx // s) % 2 == 1
        same_blk = (i // s) == (t // s)
        sel.append(np.where(right[:, None], same_blk & (t <= i), same_blk & (t > i)))
        same_parent = (i // (2 * s)) == (t // (2 * s))
        masks.append(same_parent & right[:, None] & ~right[None, :])
    sel.append(t <= i)
    sel.append(t > i)
    masks.append(i == t)
    if reverse:
        sel = [m[::-1, ::-1] for m in sel]
        masks = [m[::-1, ::-1] for m in masks]
    sel.append(np.ones((SUBLANES, c), bool))
    sel = np.tile(np.concatenate(sel, axis=0).astype(np.float32), (1, 3))
    masks = np.tile(np.stack(masks).astype(np.float32), (1, 1, N_HEADS))
    return jnp.asarray(sel, BF16), jnp.asarray(masks, F32)


def _dn_consts(c, reverse):
    levels = int(np.log2(c))
    idx = np.arange(c)
    i, t = idx[:, None], idx[None, :]
    incl, tail = t <= i, t > i
    masks = []
    for lv in range(levels):
        s = 1 << lv
        right = (idx // s) % 2 == 1
        masks.append(((i // (2 * s)) == (t // (2 * s))) & right[:, None] & ~right[None, :])
    masks += [incl, incl & (i != t), i == t]
    if reverse:
        incl, tail = incl[::-1, ::-1], tail[::-1, ::-1]
        masks = [m[::-1, ::-1] for m in masks]
    sel = np.tile(np.concatenate([incl, tail, np.ones((SUBLANES, c), bool)], axis=0).astype(np.float32), (1, 3))
    masks = np.tile(np.stack(masks).astype(np.float32), (1, 1, N_HEADS))
    return jnp.asarray(sel, BF16), jnp.asarray(masks, F32)


def _ones_blockdiag():
    h = np.arange(GROUP_W) // HEAD_DIM
    return jnp.asarray((h[:, None] == h[None, :]).astype(np.float32), BF16)


def _mod_kernel(c_ref, w_ref, b_ref, o_ref):
    a_hi, a_lo = _split2(_silu(c_ref[...]))
    w_hi, w_lo = _split2(w_ref[0])
    o_ref[0] = _dot(a_hi, w_hi) + _dot(a_hi, w_lo) + _dot(a_lo, w_hi) + b_ref[0]


def _modulation(cc, mod_w, mod_b):
    rows = cc.shape[0]
    nb = 6
    return pl.pallas_call(
        _mod_kernel,
        grid=(DEPTH, nb),
        in_specs=[pl.BlockSpec((rows, D_MODEL), lambda l, j: (0, 0)),
                  pl.BlockSpec((1, D_MODEL, D_MODEL), lambda l, j: (l, 0, j)),
                  pl.BlockSpec((1, 1, D_MODEL), lambda l, j: (l, 0, j))],
        out_specs=pl.BlockSpec((1, rows, D_MODEL), lambda l, j: (l, 0, j)),
        out_shape=jax.ShapeDtypeStruct((DEPTH, rows, 6 * D_MODEL), F32),
        compiler_params=_params(("arbitrary", "arbitrary")),
        name="modulation",
    )(cc, mod_w, mod_b.reshape(DEPTH, 1, 6 * D_MODEL))


def _inproj_kernel(x_ref, mod_ref, g_ref, wa_ref, wb_ref, wc_ref, wd_ref, oa_ref, ob_ref, oc_ref, od_ref):
    mod = mod_ref[0]
    h = _rms_modulate(x_ref[0], g_ref[...], mod[:, 0:D_MODEL], mod[:, D_MODEL:2 * D_MODEL]).astype(BF16)
    oa_ref[0] = _dot(h, wa_ref[...])
    ob_ref[0] = _dot(h, wb_ref[...])
    oc_ref[0] = _dot(h, wc_ref[...])
    od_ref[0] = _dot(h, wd_ref[...])


def _in_projection(x, mod, gain, weights):
    b, t, _ = x.shape
    tb = min(TOKEN_BLOCK, t)
    widths = [w.shape[1] for w in weights]
    return pl.pallas_call(
        _inproj_kernel,
        grid=(b, t // tb),
        in_specs=[pl.BlockSpec((1, tb, D_MODEL), lambda i, j: (i, j, 0)),
                  pl.BlockSpec((1, 1, 6 * D_MODEL), lambda i, j: (i, 0, 0)),
                  _full_spec((1, D_MODEL))] + [_full_spec(w.shape) for w in weights],
        out_specs=[pl.BlockSpec((1, tb, n), lambda i, j: (i, j, 0)) for n in widths],
        out_shape=[jax.ShapeDtypeStruct((b, t, n), F32) for n in widths],
        compiler_params=_params(("arbitrary", "arbitrary")),
        name="in_projection",
    )(x, mod, gain, *weights)


def _decay_scan_prepare(probs, ones_bd, c):
    levels = int(np.log2(c))
    exs = [jnp.exp(_sel_dot(pr["sel"][...], pr["g"])) for pr in probs]
    attns = [None] * len(probs)
    for lv in range(levels + 1):
        for i, (pr, ex) in enumerate(zip(probs, exs)):
            q, k = pr["q"], pr["k"]
            if lv < levels:
                e = ex[lv * c:(lv + 1) * c]
                qt, kt = (q * e).astype(BF16), (k * e).astype(BF16)
            else:
                qt, kt = q.astype(BF16), k.astype(BF16)
            p = _dot_nt(qt, _head_blockdiag(kt, ones_bd)) * pr["mask"][lv]
            attns[i] = p if attns[i] is None else attns[i] + p
    res = []
    for pr, ex, attn in zip(probs, exs, attns):
        o = _dot(attn.astype(BF16), _head_blockdiag(pr["v"].astype(BF16), ones_bd))
        qd = (pr["q"] * ex[levels * c:(levels + 1) * c]).astype(BF16)
        ktail = (pr["k"] * ex[(levels + 1) * c:(levels + 2) * c]).astype(BF16)
        res.append((o, qd, ktail, ex[(levels + 2) * c:(levels + 2) * c + SUBLANES]))
    return res


def _decay_scan_state_steps(items, ones_bd, state_ref):
    sts = [state_ref[it[0]] for it in items]
    outs = [it[1] + _dot_nt(it[2], _head_blockdiag(st.astype(BF16), ones_bd)) for it, st in zip(items, sts)]
    for it, st in zip(items, sts):
        state_ref[it[0]] = st * it[5] + _head_diag_blocks(_dot_tn(it[4], it[3]))
    return outs


def _head_norm_gate(o, gate, onorm, ones_bd):
    ms = _head_sum(o * o, ones_bd) * (1.0 / HEAD_DIM)
    return o * lax.rsqrt(ms + EPS) * onorm * _silu(gate)


def _decay_mixer_kernel(mode, t, c, p_ref, self_ref, selb_ref, maskf_ref, maskb_ref, ones_ref, onorm_ref,
                        pa_ref, pb_ref, pc_ref, h0_ref, y_ref, sout_ref,
                        qd_ref, kt_ref, v16_ref, dl_ref, of_ref, ob_ref, state_ref):
    n_chunks = t // c
    state_ref[...] = h0_ref[0]
    sels, masks, o_refs = (self_ref, selb_ref), (maskf_ref, maskb_ref), (of_ref, ob_ref)

    def inputs(rows, d):
        if mode == "gla":
            q = p_ref[0, rows, 0:GROUP_W] * (HEAD_DIM ** -0.5)
            k = p_ref[0, rows, GROUP_W:2 * GROUP_W]
            v = p_ref[0, rows, 2 * GROUP_W:3 * GROUP_W]
            low = p_ref[0, rows, 4 * GROUP_W:4 * GROUP_W + LANES]
            gk = _dot_hl(low, pa_ref[d], pb_ref[d]) + pc_ref[d]
            g = _log_sigmoid(gk) * (1.0 / GLA_GATE_NORM)
        else:
            q = _silu(p_ref[0, rows, 0:GROUP_W]) * (HEAD_DIM ** -0.5)
            v = p_ref[0, rows, GROUP_W:2 * GROUP_W]
            z = p_ref[0, rows, (3 + d) * GROUP_W:(4 + d) * GROUP_W]
            a = pa_ref[...]
            b = pb_ref[...] + _log_sigmoid(z)
            m = jnp.maximum(a, b)
            g = m + jnp.log(jnp.exp(a - m) + jnp.exp(b - m))
            k = pc_ref[...] * _sigmoid(-z)
        return q, k, v, g

    per_iter = 2 if n_chunks % 2 == 0 else 1

    def prepare(i, carry):
        chunks = [i * per_iter + j for j in range(per_iter)]
        probs = []
        for n in chunks:
            rows = pl.ds(pl.multiple_of(n * c, c), c)
            for d in range(2):
                q, k, v, g = inputs(rows, d)
                probs.append(dict(q=q, k=k, v=v, g=g, sel=sels[d], mask=masks[d]))
            v16_ref[rows, :] = probs[-1]["v"].astype(BF16)
        for idx, (o, qd, ktail, dlast) in enumerate(_decay_scan_prepare(probs, ones_ref[...], c)):
            n, d = chunks[idx // 2], idx % 2
            rows = pl.ds(pl.multiple_of(n * c, c), c)
            o_refs[d][rows, :] = o
            qd_ref[d, rows, :] = qd
            kt_ref[d, rows, :] = ktail
            dl_ref[d, pl.ds(pl.multiple_of(n * SUBLANES, SUBLANES), SUBLANES), :] = dlast
        return carry

    lax.fori_loop(0, n_chunks // per_iter, prepare, 0)

    def body(i, carry):
        items, rows_of = [], []
        for d, n in ((0, i), (1, n_chunks - 1 - i)):
            rows = pl.ds(pl.multiple_of(n * c, c), c)
            dlast = dl_ref[d, pl.ds(pl.multiple_of(n * SUBLANES, SUBLANES), SUBLANES), :][0:1, :]
            items.append((d, o_refs[d][rows, :], qd_ref[d, rows, :], kt_ref[d, rows, :], v16_ref[rows, :], dlast))
            rows_of.append(rows)
        o_f, o_b = _decay_scan_state_steps(items, ones_ref[...], state_ref)
        of_ref[rows_of[0], :] = o_f
        ob_ref[rows_of[1], :] = o_b
        return carry

    lax.fori_loop(0, n_chunks, body, 0)

    rbk = min(t, 256)
    gate_col = 3 * GROUP_W if mode == "gla" else 2 * GROUP_W

    def fin(i, carry):
        rows = pl.ds(pl.multiple_of(i * rbk, rbk), rbk)
        o = of_ref[rows, :] + ob_ref[rows, :]
        gate = p_ref[0, rows, gate_col:gate_col + GROUP_W]
        y_ref[0, rows, :] = _head_norm_gate(o, gate, onorm_ref[...], ones_ref[...]).astype(BF16)
        return carry

    lax.fori_loop(0, t // rbk, fin, 0)
    sout_ref[0] = state_ref[...]


def _decay_mixer(mode, p, pa, pb, pc, onorm, h0t):
    b, t, w = p.shape
    c = GLA_CHUNK
    sel_f, mask_f = _scan_consts(c, False)
    sel_b, mask_b = _scan_consts(c, True)
    ones_bd = _ones_blockdiag()
    consts = [sel_f, sel_b, mask_f, mask_b, ones_bd, onorm, pa, pb, pc]
    state_shape = (2, HEAD_DIM, GROUP_W)
    return pl.pallas_call(
        functools.partial(_decay_mixer_kernel, mode, t, c),
        grid=(b,),
        in_specs=[pl.BlockSpec((1, t, w), lambda i: (i, 0, 0), pipeline_mode=pl.Buffered(1))]
                 + [_full_spec(a.shape) for a in consts]
                 + [pl.BlockSpec((1,) + state_shape, lambda i: (i, 0, 0, 0))],
        out_specs=[pl.BlockSpec((1, t, GROUP_W), lambda i: (i, 0, 0)),
                   pl.BlockSpec((1,) + state_shape, lambda i: (i, 0, 0, 0))],
        out_shape=[jax.ShapeDtypeStruct((b, t, GROUP_W), BF16),
                   jax.ShapeDtypeStruct((b,) + state_shape, F32)],
        scratch_shapes=[pltpu.VMEM((2, t, GROUP_W), BF16), pltpu.VMEM((2, t, GROUP_W), BF16),
                        pltpu.VMEM((t, GROUP_W), BF16), pltpu.VMEM((2, (t // c) * SUBLANES, GROUP_W), F32),
                        pltpu.VMEM((t, GROUP_W), F32), pltpu.VMEM((t, GROUP_W), F32),
                        pltpu.VMEM(state_shape, F32)],
        compiler_params=_params(("arbitrary",)),
        name=mode + "_mixer",
    )(p, *consts, h0t)


def _delta_prepare(probs, ones_bd, expand_ref, nega_ref, dtb_ref, c):
    levels = int(np.log2(c))
    work = []
    for pr in probs:
        d, mask = pr["d"], pr["mask"]
        logits = _dot_sel(pr["ba"], expand_ref[d])
        beta = _sigmoid(logits[:, 0:GROUP_W])
        g = nega_ref[d] * _softplus(logits[:, GROUP_W:2 * GROUP_W] + dtb_ref[d])
        cs = _sel_dot(pr["sel"][...], g)
        b_incl = cs[0:c]
        b_t = jnp.concatenate([b_incl[:, h * HEAD_DIM:(h + 1) * HEAD_DIM].T for h in range(N_HEADS)], axis=1)
        lmat = jnp.exp(jnp.minimum(b_incl - b_t, 0.0)) * mask[levels]
        m = beta * pr["kk"] * lmat * mask[levels + 1]
        work.append(dict(beta=beta, eb=jnp.exp(b_incl), etail=jnp.exp(cs[c:2 * c]),
                         dlast=jnp.exp(cs[2 * c:2 * c + SUBLANES]), lmat=lmat, m=m,
                         t=mask[levels + 2] - m * mask[0]))
    for lv in range(1, levels):
        mm = _per_head_mm1 if lv == levels - 1 else _per_head_mm
        ps = [mm(wk["m"] * pr["mask"][lv], wk["t"], ones_bd) for pr, wk in zip(probs, work)]
        for wk, p in zip(work, ps):
            wk["t"] = wk["t"] - mm(wk["t"], p, ones_bd)
    us = [_per_head_mm1(wk["t"], pr["v"] * wk["beta"], ones_bd) for pr, wk in zip(probs, work)]
    ws = [_per_head_mm1(wk["t"], pr["k"] * wk["beta"] * wk["eb"], ones_bd) for pr, wk in zip(probs, work)]
    return [(u, w.astype(BF16), (pr["qk_raw"] * wk["lmat"]).astype(BF16), (pr["q"] * wk["eb"]).astype(BF16),
             (pr["k"] * wk["etail"]).astype(BF16), wk["dlast"]) for pr, wk, u, w in zip(probs, work, us, ws)]


def _delta_state_steps(items, ones_bd, state_ref):
    sts = [state_ref[it[0]] for it in items]
    bds = [_head_blockdiag(st.astype(BF16), ones_bd) for st in sts]
    v16s = [(it[1] - _dot(it[2], bd)).astype(BF16) for it, bd in zip(items, bds)]
    outs = [_dot(it[4], bd) + _dot(it[3], _head_blockdiag(v16, ones_bd)) for it, bd, v16 in zip(items, bds, v16s)]
    for it, st, v16 in zip(items, sts, v16s):
        state_ref[it[0]] = st * it[6] + _head_diag_blocks(_dot_tn(it[5], v16))
    return outs


def _delta_mixer_kernel(t, c, p_ref, self_ref, selb_ref, maskf_ref, maskb_ref, ones_ref, onorm_ref,
                        conv_ref, expand_ref, nega_ref, dtb_ref, h0_ref, y_ref, sout_ref,
                        u_ref, w_ref, qk_ref, qd_ref, kt_ref, dl_ref, of_ref, ob_ref, state_ref):
    n_chunks = t // c
    qkv_w = 3 * GROUP_W
    state_ref[...] = h0_ref[0]
    row_id = lax.broadcasted_iota(jnp.int32, (c, qkv_w), 0)
    w_prev, w_mid, w_next = conv_ref[0:1, :], conv_ref[1:2, :], conv_ref[2:3, :]
    sels, masks = (self_ref, selb_ref), (maskf_ref, maskb_ref)

    def conv_qkv(n):
        r0 = pl.multiple_of(n * c, c)
        x = p_ref[0, pl.ds(r0, c), 0:qkv_w]
        before = p_ref[0, pl.ds(pl.multiple_of(jnp.maximum(r0 - SUBLANES, 0), SUBLANES), SUBLANES), 0:qkv_w]
        after = p_ref[0, pl.ds(pl.multiple_of(jnp.minimum(r0 + c, t - SUBLANES), SUBLANES), SUBLANES), 0:qkv_w]
        before_row = jnp.where(n > 0, before[SUBLANES - 1:SUBLANES, :], 0.0)
        after_row = jnp.where(n < n_chunks - 1, after[0:1, :], 0.0)
        x_prev = jnp.where(row_id == 0, before_row, pltpu.roll(x, 1, axis=0))
        x_next = jnp.where(row_id == c - 1, after_row, pltpu.roll(x, c - 1, axis=0))
        y = _silu(x_prev * w_prev + x * w_mid + x_next * w_next)
        qk = y[:, 0:2 * GROUP_W]
        ones2 = ones_ref[...]
        ss = jnp.concatenate([_head_sum(qk[:, 0:GROUP_W] * qk[:, 0:GROUP_W], ones2),
                              _head_sum(qk[:, GROUP_W:] * qk[:, GROUP_W:], ones2)], axis=1)
        qk = qk * lax.rsqrt(ss + EPS)
        q, k, v = qk[:, 0:GROUP_W] * (HEAD_DIM ** -0.5), qk[:, GROUP_W:], y[:, 2 * GROUP_W:]
        k_hi, k_lo = _split2(k)
        bd_k = _head_blockdiag(k_hi, ones2)
        kk = _dot_nt(k_hi, bd_k) + _dot_nt(k_hi, _head_blockdiag(k_lo, ones2)) + _dot_nt(k_lo, bd_k)
        qk_raw = _dot_nt(q.astype(BF16), bd_k)
        ba = p_ref[0, pl.ds(r0, c), 4 * GROUP_W:4 * GROUP_W + LANES]
        return dict(q=q, k=k, v=v, kk=kk, qk_raw=qk_raw, ba=ba)

    per_iter = 2 if n_chunks % 2 == 0 else 1

    def prepare(i, carry):
        chunks = [i * per_iter + j for j in range(per_iter)]
        probs = []
        for n in chunks:
            shared = conv_qkv(n)
            probs += [dict(shared, sel=sels[d], mask=masks[d], d=d) for d in range(2)]
        res = _delta_prepare(probs, ones_ref[...], expand_ref, nega_ref, dtb_ref, c)
        for idx, (u, w16, qk16, qd, ktail, dlast) in enumerate(res):
            n, d = chunks[idx // 2], idx % 2
            rows = pl.ds(pl.multiple_of(n * c, c), c)
            u_ref[d, rows, :] = u
            w_ref[d, rows, :] = w16
            qk_ref[d, rows, :] = qk16
            qd_ref[d, rows, :] = qd
            kt_ref[d, rows, :] = ktail
            dl_ref[d, pl.ds(pl.multiple_of(n * SUBLANES, SUBLANES), SUBLANES), :] = dlast
        return carry

    lax.fori_loop(0, n_chunks // per_iter, prepare, 0)

    def body(i, carry):
        items, rows_of = [], []
        for d, n in ((0, i), (1, n_chunks - 1 - i)):
            rows = pl.ds(pl.multiple_of(n * c, c), c)
            dlast = dl_ref[d, pl.ds(pl.multiple_of(n * SUBLANES, SUBLANES), SUBLANES), :][0:1, :]
            items.append((d, u_ref[d, rows, :], w_ref[d, rows, :], qk_ref[d, rows, :], qd_ref[d, rows, :],
                          kt_ref[d, rows, :], dlast))
            rows_of.append(rows)
        o_f, o_b = _delta_state_steps(items, ones_ref[...], state_ref)
        of_ref[rows_of[0], :] = o_f
        ob_ref[rows_of[1], :] = o_b
        return carry

    lax.fori_loop(0, n_chunks, body, 0)

    rbk = min(t, 256)

    def fin(i, carry):
        rows = pl.ds(pl.multiple_of(i * rbk, rbk), rbk)
        o = of_ref[rows, :] + ob_ref[rows, :]
        gate = p_ref[0, rows, 3 * GROUP_W:4 * GROUP_W]
        y_ref[0, rows, :] = _head_norm_gate(o, gate, onorm_ref[...], ones_ref[...]).astype(BF16)
        return carry

    lax.fori_loop(0, t // rbk, fin, 0)
    sout_ref[0] = state_ref[...]


def _delta_mixer(p, conv_w, expand, nega, dtb, onorm, s0):
    b, t, w = p.shape
    c = DN_CHUNK
    sel_f, mask_f = _dn_consts(c, False)
    sel_b, mask_b = _dn_consts(c, True)
    consts = [sel_f, sel_b, mask_f, mask_b, _ones_blockdiag(), onorm, conv_w, expand, nega, dtb]
    state_shape = (2, HEAD_DIM, GROUP_W)
    return pl.pallas_call(
        functools.partial(_delta_mixer_kernel, t, c),
        grid=(b,),
        in_specs=[pl.BlockSpec((1, t, w), lambda i: (i, 0, 0), pipeline_mode=pl.Buffered(1))]
                 + [_full_spec(a.shape) for a in consts]
                 + [pl.BlockSpec((1,) + state_shape, lambda i: (i, 0, 0, 0))],
        out_specs=[pl.BlockSpec((1, t, GROUP_W), lambda i: (i, 0, 0)),
                   pl.BlockSpec((1,) + state_shape, lambda i: (i, 0, 0, 0))],
        out_shape=[jax.ShapeDtypeStruct((b, t, GROUP_W), BF16),
                   jax.ShapeDtypeStruct((b,) + state_shape, F32)],
        scratch_shapes=[pltpu.VMEM((2, t, GROUP_W), F32)]
                       + [pltpu.VMEM((2, t, GROUP_W), BF16)] * 4
                       + [pltpu.VMEM((2, (t // c) * SUBLANES, GROUP_W), F32),
                          pltpu.VMEM((t, GROUP_W), F32), pltpu.VMEM((t, GROUP_W), F32),
                          pltpu.VMEM(state_shape, F32)],
        compiler_params=_params(("arbitrary",)),
        name="delta_mixer",
    )(p, *consts, s0)


def _s5_kernel(t, tc, u_ref, b_ref, cm_ref, lbar_ref, dskip_ref, gluw_ref, glub_ref, h0_ref,
               y_ref, sout_ref, x_ref, yf_ref, yb_ref, carry_ref):
    n_chunks = t // tc
    n_lane_blocks = S5_STATE // LANES
    carry_ref[...] = h0_ref[0]
    sub8 = lax.broadcasted_iota(jnp.int32, (SUBLANES, LANES), 0)

    def scan_dirs(rows_of, y_refs):
        for d in range(2):
            u16 = u_ref[0, rows_of[d], :].astype(BF16)
            for j in range(n_lane_blocks):
                x_ref[d, j] = _dot(u16, b_ref[d, j])
            y_refs[d][rows_of[d], :] = jnp.zeros((tc, GROUP_W), F32)

        def scan_block(d, j):
            bu = x_ref[d, j]
            xr, xi = bu[:, 0:LANES], bu[:, LANES:2 * LANES]
            lb = lbar_ref[d, j]
            pw = [(lb[0:1, :], lb[1:2, :])]
            for _ in range(SUBLANES - 1):
                pr, pi = pw[-1]
                pw.append((pr * pw[0][0] - pi * pw[0][1], pr * pw[0][1] + pi * pw[0][0]))
            n_groups = tc // SUBLANES
            blocks_r = [xr[v * SUBLANES:(v + 1) * SUBLANES] for v in range(n_groups)]
            blocks_i = [xi[v * SUBLANES:(v + 1) * SUBLANES] for v in range(n_groups)]
            for s in range(int(np.log2(SUBLANES))):
                sh = 1 << s
                keep = sub8 < SUBLANES - sh if d else sub8 >= sh
                m_re, m_im = jnp.where(keep, pw[sh - 1][0], 0.0), jnp.where(keep, pw[sh - 1][1], 0.0)
                rot = SUBLANES - sh if d else sh
                for v in range(n_groups):
                    sr, si = pltpu.roll(blocks_r[v], rot, axis=0), pltpu.roll(blocks_i[v], rot, axis=0)
                    blocks_r[v] = blocks_r[v] + m_re * sr - m_im * si
                    blocks_i[v] = blocks_i[v] + m_re * si + m_im * sr
            p_re = jnp.zeros((SUBLANES, LANES), F32)
            p_im = jnp.zeros((SUBLANES, LANES), F32)
            for r in range(SUBLANES):
                pr, pi = pw[SUBLANES - 1 - r] if d else pw[r]
                p_re = jnp.where(sub8 == r, pr, p_re)
                p_im = jnp.where(sub8 == r, pi, p_im)
            cr, ci = carry_ref[d, 0, j], carry_ref[d, 1, j]
            last = 0 if d else SUBLANES - 1
            for step in range(n_groups):
                v = n_groups - 1 - step if d else step
                blocks_r[v] = blocks_r[v] + (p_re * cr - p_im * ci)
                blocks_i[v] = blocks_i[v] + (p_re * ci + p_im * cr)
                cr, ci = blocks_r[v][last:last + 1, :], blocks_i[v][last:last + 1, :]
            carry_ref[d, 0, j] = cr
            carry_ref[d, 1, j] = ci
            return jnp.concatenate([jnp.concatenate(blocks_r, axis=0), jnp.concatenate(blocks_i, axis=0)],
                                   axis=1).astype(BF16)

        def lane_block(j, carry):
            states = [scan_block(d, j) for d in range(2)]
            for d in range(2):
                y_refs[d][rows_of[d], :] += _dot(states[d], cm_ref[j])
            return carry

        lax.fori_loop(0, n_lane_blocks, lane_block, 0)

    def body(n, carry):
        scan_dirs((pl.ds(pl.multiple_of(n * tc, tc), tc), pl.ds(pl.multiple_of((n_chunks - 1 - n) * tc, tc), tc)),
                  (yf_ref, yb_ref))
        return carry

    lax.fori_loop(0, n_chunks, body, 0)

    def fin(i, carry):
        rows = pl.ds(pl.multiple_of(i * tc, tc), tc)
        u = u_ref[0, rows, :]
        y = _gelu_tanh(yf_ref[rows, :] + yb_ref[rows, :] + dskip_ref[...] * u)
        z = _dot(y.astype(BF16), gluw_ref[...]) + glub_ref[...]
        y_ref[0, rows, :] = (y * _sigmoid(z)).astype(BF16)
        return carry

    lax.fori_loop(0, n_chunks, fin, 0)
    sout_ref[0] = carry_ref[...]


def _s5_mixer(u, b_in, cmat, lbar, dskip, glu_w, glu_b, h0):
    b, t, _ = u.shape
    tc = min(S5_CHUNK, t)
    nlb = S5_STATE // LANES
    consts = [b_in, cmat, lbar, dskip, glu_w, glu_b]
    st_shape = (2, 2, nlb, 1, LANES)
    return pl.pallas_call(
        functools.partial(_s5_kernel, t, tc),
        grid=(b,),
        in_specs=[pl.BlockSpec((1, t, GROUP_W), lambda i: (i, 0, 0))]
                 + [_full_spec(a.shape) for a in consts]
                 + [pl.BlockSpec((1,) + st_shape, lambda i: (i, 0, 0, 0, 0, 0))],
        out_specs=[pl.BlockSpec((1, t, GROUP_W), lambda i: (i, 0, 0)),
                   pl.BlockSpec((1,) + st_shape, lambda i: (i, 0, 0, 0, 0, 0))],
        out_shape=[jax.ShapeDtypeStruct((b, t, GROUP_W), BF16),
                   jax.ShapeDtypeStruct((b,) + st_shape, F32)],
        scratch_shapes=[pltpu.VMEM((2, nlb, tc, 2 * LANES), F32), pltpu.VMEM((t, GROUP_W), F32),
                        pltpu.VMEM((t, GROUP_W), F32), pltpu.VMEM(st_shape, F32)],
        compiler_params=_params(("arbitrary",)),
        name="s5_mixer",
    )(u, *consts, h0)


def _outproj_kernel(x_ref, mod_ref, ya_ref, yb_ref, yc_ref, yd_ref, w_ref, o_ref):
    y = (_dot(ya_ref[0], w_ref[0]) + _dot(yb_ref[0], w_ref[1])
         + _dot(yc_ref[0], w_ref[2]) + _dot(yd_ref[0], w_ref[3]))
    o_ref[0] = x_ref[0] + mod_ref[0][:, 2 * D_MODEL:3 * D_MODEL] * y


def _out_projection(x, mod, ys, w_out):
    b, t, _ = x.shape
    tb = min(TOKEN_BLOCK, t)
    tok = lambda n: pl.BlockSpec((1, tb, n), lambda i, j: (i, j, 0))
    return pl.pallas_call(
        _outproj_kernel,
        grid=(b, t // tb),
        in_specs=[tok(D_MODEL), pl.BlockSpec((1, 1, 6 * D_MODEL), lambda i, j: (i, 0, 0))]
                 + [tok(GROUP_W)] * 4 + [_full_spec(w_out.shape)],
        out_specs=tok(D_MODEL),
        out_shape=jax.ShapeDtypeStruct(x.shape, F32),
        compiler_params=_params(("arbitrary", "arbitrary")),
        name="out_projection",
    )(x, mod, *ys, w_out)


def _topk_rows(scores, k, row_ids):
    n = scores[0].shape[1]
    slot = lax.broadcasted_iota(jnp.int32, (k, n), 0)
    scores = list(scores)
    vals = [jnp.zeros((k, n), F32) for _ in scores]
    ids = [jnp.zeros((k, n), F32) for _ in scores]
    for it in range(k):
        ms = [jnp.max(s, axis=0, keepdims=True) for s in scores]
        idxs = [jnp.min(jnp.where(s == m, row_ids, 1e9), axis=0, keepdims=True) for s, m in zip(scores, ms)]
        vals = [jnp.where(slot == it, m, v) for m, v in zip(ms, vals)]
        ids = [jnp.where(slot == it, idx, i) for idx, i in zip(idxs, ids)]
        scores = [jnp.where(row_ids == idx, -jnp.inf, s) for s, idx in zip(scores, idxs)]
    return list(zip(vals, ids))


def _staircase():
    pairs = [(i, j) for i in range(PEER_TOPK) for j in range(PEER_TOPK) if (i + 1) * (j + 1) <= PEER_TOPK]
    rows = -(-len(pairs) // SUBLANES) * SUBLANES
    sel = np.zeros((2, rows, PEER_TOPK), np.float32)
    aux = np.zeros((2, rows, LANES), np.float32)
    aux[0, len(pairs):] = -np.inf
    aux[1, len(pairs):] = 1e9
    for r, (i, j) in enumerate(pairs):
        sel[0, r, i] = 1.0
        sel[1, r, j] = 1.0
        aux[1, r] = i * PEER_TOPK + j
    return jnp.asarray(np.tile(sel, (1, 1, 3)), BF16), jnp.asarray(aux, F32)


def _take_rows(table, idx, k):
    out = jnp.zeros(idx.shape, F32)
    for r in range(k):
        out = jnp.where(idx == float(r), table[r:r + 1, :], out)
    return out


def _peer_route_kernel(tb, x_ref, mod_ref, g_ref, wqh_ref, wql_ref, kh_ref, kl_ref, stsel_ref, staux_ref,
                       h_ref, a_ref, c_ref, gate_ref, qt_ref, at_ref, ct_ref, gt_ref):
    mod = mod_ref[0]
    h = _rms_modulate(x_ref[0], g_ref[...], mod[:, 3 * D_MODEL:4 * D_MODEL], mod[:, 4 * D_MODEL:5 * D_MODEL])
    h_hi, h_lo = _split2(h)
    h_ref[0] = h_hi
    wq_hi = wqh_ref[...]
    qt_ref[...] = _dot_nt(wq_hi, h_hi) + _dot_nt(wq_hi, h_lo) + _dot_nt(wql_ref[...], h_hi)
    half = PEER_DQ // 2
    key_ids = lax.broadcasted_iota(jnp.int32, (PEER_NKEYS, LANES), 0).astype(F32)

    heads_per_iter = 4

    def head_body(it, carry):
        heads = [it * heads_per_iter + i for i in range(heads_per_iter)]
        tiles = [slice(lt * LANES, (lt + 1) * LANES) for lt in range(tb // LANES)]
        probs = [(head, lanes) for head in heads for lanes in tiles]
        scores = []
        for head, lanes in probs:
            for p in range(2):
                hp = head * 2 + p
                q_hi, q_lo = _split2(qt_ref[pl.ds(pl.multiple_of(hp * half, half), half), lanes])
                scores.append(_dot(kh_ref[hp], q_hi) + _dot(kh_ref[hp], q_lo) + _dot(kl_ref[hp], q_hi))
        top = _topk_rows(scores, PEER_TOPK, key_ids)
        cands = [_sel_dot(stsel_ref[0], top[2 * i][0]) + _sel_dot(stsel_ref[1], top[2 * i + 1][0]) + staux_ref[0]
                 for i in range(len(probs))]
        best = _topk_rows(cands, PEER_TOPK, staux_ref[1])
        for i, (head, lanes) in enumerate(probs):
            out_rows = pl.ds(pl.multiple_of(head * PEER_TOPK, PEER_TOPK), PEER_TOPK)
            sc, flat = best[i]
            r1 = jnp.floor(flat * (1.0 / PEER_TOPK))
            r2 = flat - r1 * PEER_TOPK
            e = jnp.exp(sc - sc[0:1, :])
            at_ref[out_rows, lanes] = _take_rows(top[2 * i][1], r1, PEER_TOPK)
            ct_ref[out_rows, lanes] = _take_rows(top[2 * i + 1][1], r2, PEER_TOPK)
            gt_ref[out_rows, lanes] = e / jnp.sum(e, axis=0, keepdims=True)
        return carry

    lax.fori_loop(0, PEER_HEADS // heads_per_iter, head_body, 0)
    a_ref[0] = at_ref[...].T.astype(jnp.int32)
    c_ref[0] = ct_ref[...].T.astype(jnp.int32)
    gate_ref[0] = gt_ref[...].T


def _peer_route(x, mod, gain, wq_hi, wq_lo, keys_hi, keys_lo):
    b, t, _ = x.shape
    tb = min(TOKEN_BLOCK, t)
    tok = lambda n: pl.BlockSpec((1, tb, n), lambda i, j: (i, j, 0))
    consts = [gain, wq_hi, wq_lo, keys_hi, keys_lo, *_staircase()]
    return pl.pallas_call(
        functools.partial(_peer_route_kernel, tb),
        grid=(b, t // tb),
        scratch_shapes=[pltpu.VMEM((PEER_HEADS * PEER_DQ, tb), F32)] + [pltpu.VMEM((PEER_SEL, tb), F32)] * 3,
        in_specs=[tok(D_MODEL), pl.BlockSpec((1, 1, 6 * D_MODEL), lambda i, j: (i, 0, 0))]
                 + [_full_spec(a.shape) for a in consts],
        out_specs=[tok(D_MODEL), tok(PEER_SEL), tok(PEER_SEL), tok(PEER_SEL)],
        out_shape=[jax.ShapeDtypeStruct((b, t, D_MODEL), BF16),
                   jax.ShapeDtypeStruct((b, t, PEER_SEL), jnp.int32),
                   jax.ShapeDtypeStruct((b, t, PEER_SEL), jnp.int32),
                   jax.ShapeDtypeStruct((b, t, PEER_SEL), F32)],
        compiler_params=_params(("arbitrary", "arbitrary")),
        name="peer_route",
    )(x, mod, *consts)


def _peer_expert_kernel(tb, n_slabs, x_ref, mod_ref, h_ref, a_ref, c_ref, gate_ref, ut_ref, v_ref,
                        o_ref, gmat_ref, acc_ref):
    j = pl.program_id(2)
    nk = PEER_NKEYS

    @pl.when(j == 0)
    def _():
        acc_ref[...] = jnp.zeros_like(acc_ref)
        key_id = lax.broadcasted_iota(jnp.int32, (nk, PEER_SEL), 0)

        pack = 2 * SUBLANES
        groups = 2

        def build(nb, carry):
            n0 = pl.multiple_of(nb * groups * pack, groups * pack)
            a_rows = a_ref[0, pl.ds(n0, groups * pack), :]
            c_rows = c_ref[0, pl.ds(n0, groups * pack), :]
            g_rows = gate_ref[0, pl.ds(n0, groups * pack), :]
            mats = []
            for r in range(groups * pack):
                row_sel = jnp.where(key_id == a_rows[r:r + 1], 1.0, 0.0).astype(BF16)
                col_sel = jnp.where(key_id == c_rows[r:r + 1], g_rows[r:r + 1], 0.0).astype(BF16)
                mats.append(_dot_nt(row_sel, col_sel).astype(BF16))
            for gi in range(groups):
                stacked = jnp.stack(mats[gi * pack:(gi + 1) * pack], axis=0)
                gmat_ref[:, pl.ds(n0 + gi * pack, pack), :] = pltpu.einshape("nac->anc", stacked)
            return carry

        lax.fori_loop(0, tb // (groups * pack), build, 0)

    h = h_ref[0]
    n_pairs = n_slabs // 2
    cols = [slice(i * 2 * nk, (i + 1) * 2 * nk) for i in range(n_pairs)]
    lookahead = 2
    s = [_dot(h, ut_ref[:, cols[i]]) for i in range(min(lookahead, n_pairs))]
    acc = None
    for i in range(n_pairs):
        if i + lookahead < n_pairs:
            s.append(_dot(h, ut_ref[:, cols[i + lookahead]]))
        a0 = j * n_slabs + 2 * i
        g = jnp.concatenate([gmat_ref[a0], gmat_ref[a0 + 1]], axis=1)
        w = g * _gelu_tanh(s[i]).astype(BF16)
        p = _dot(w, v_ref[cols[i], :])
        acc = p if acc is None else acc + p
    acc_ref[...] += acc

    @pl.when(j == pl.num_programs(2) - 1)
    def _():
        o_ref[0] = x_ref[0] + mod_ref[0][:, 5 * D_MODEL:6 * D_MODEL] * acc_ref[...]


def _peer_experts(x, mod, h, a, c, gate, u_t, v_tab):
    b, t, _ = x.shape
    tb = min(PEER_TOKEN_BLOCK, t)
    n_exp = u_t.shape[1]
    eb = PEER_EXPERT_BLOCK
    n_slabs = eb // PEER_NKEYS
    tok = lambda n: pl.BlockSpec((1, tb, n), lambda i, k, j: (i, k, 0))
    return pl.pallas_call(
        functools.partial(_peer_expert_kernel, tb, n_slabs),
        grid=(b, t // tb, n_exp // eb),
        in_specs=[tok(D_MODEL), pl.BlockSpec((1, 1, 6 * D_MODEL), lambda i, k, j: (i, 0, 0)),
                  tok(D_MODEL), tok(PEER_SEL), tok(PEER_SEL), tok(PEER_SEL),
                  pl.BlockSpec((D_MODEL, eb), lambda i, k, j: (0, j)),
                  pl.BlockSpec((eb, D_MODEL), lambda i, k, j: (j, 0))],
        out_specs=tok(D_MODEL),
        out_shape=jax.ShapeDtypeStruct(x.shape, F32),
        scratch_shapes=[pltpu.VMEM((PEER_NKEYS, tb, PEER_NKEYS), BF16), pltpu.VMEM((tb, D_MODEL), F32)],
        compiler_params=_params(("arbitrary", "arbitrary", "arbitrary")),
        name="peer_experts",
    )(x, mod, h, a, c, gate, u_t, v_tab)


def _final_norm_kernel(x_ref, g_ref, o_ref):
    x = x_ref[0]
    ms = jnp.mean(x * x, axis=-1, keepdims=True)
    o_ref[0] = x * lax.rsqrt(ms + EPS) * g_ref[...]


def _final_norm(x, gain):
    b, t, _ = x.shape
    tb = min(TOKEN_BLOCK, t)
    tok = pl.BlockSpec((1, tb, D_MODEL), lambda i, j: (i, j, 0))
    return pl.pallas_call(
        _final_norm_kernel,
        grid=(b, t // tb),
        in_specs=[tok, _full_spec((1, D_MODEL))],
        out_specs=tok,
        out_shape=jax.ShapeDtypeStruct(x.shape, F32),
        compiler_params=_params(("arbitrary", "arbitrary")),
        name="final_norm",
    )(x, gain)


def _hi_lo(w):
    hi = w.astype(BF16)
    return hi, (w - hi.astype(F32)).astype(BF16)


def _pad_cols(w, n):
    return jnp.pad(w, ((0, 0), (0, n - w.shape[1])))


def _layer_weights(l, w_in, w_out, gla_gk_up, gla_gk_bias, gla_onorm_g, dn_conv_w, dn_A_log, dn_dt_bias,
                   dn_onorm_g, s5_lambda_re, s5_lambda_im, s5_log_dt, s5_B_re, s5_B_im, s5_C_re, s5_C_im,
                   s5_D, s5_glu_w, s5_glu_b, lower_bound, hgrn_onorm_g, peer_wq, peer_keys, peer_u, peer_v):
    gw = GROUP_W
    sizes = (gw, gw, gw, gw, 2 * GLA_LOWRANK, gw, gw, gw, gw, 2 * N_HEADS, 2 * N_HEADS, gw, gw, gw, gw, 2 * gw)
    offs = np.concatenate([[0], np.cumsum(sizes)])
    col = lambda a, b: w_in[l][:, int(offs[a]):int(offs[b])]
    wl = {}
    wl["w_gla"] = _pad_cols(col(0, 5), GLA_W).astype(BF16)
    wl["w_dn"] = _pad_cols(col(5, 11), DN_W).astype(BF16)
    wl["w_s5"] = col(11, 12).astype(BF16)
    wl["w_hg"] = col(12, 16).astype(BF16)
    wl["w_out"] = w_out[l].reshape(4, gw, D_MODEL).astype(BF16)
    tile_head = lambda g: jnp.tile(g, N_HEADS).reshape(1, gw)
    up = jnp.zeros((2, LANES, gw), F32)
    for d in range(2):
        up = up.at[d, d * GLA_LOWRANK:(d + 1) * GLA_LOWRANK, :].set(gla_gk_up[l, d])
    wl["gla_up_hi"], wl["gla_up_lo"] = _hi_lo(up)
    wl["gla_bias"] = gla_gk_bias[l].reshape(2, 1, gw)
    wl["gla_onorm"] = tile_head(gla_onorm_g[l])
    expand = np.zeros((2, LANES, 2 * gw), np.float32)
    for d in range(2):
        for h in range(N_HEADS):
            expand[d, d * N_HEADS + h, h * HEAD_DIM:(h + 1) * HEAD_DIM] = 1.0
            expand[d, 2 * N_HEADS + d * N_HEADS + h, gw + h * HEAD_DIM:gw + (h + 1) * HEAD_DIM] = 1.0
    wl["dn_expand"] = jnp.asarray(expand, BF16)
    wl["dn_conv"] = jnp.pad(dn_conv_w[l], ((0, SUBLANES - 3), (0, 0)))
    wl["dn_nega"] = jnp.repeat(-jnp.exp(dn_A_log[l]), HEAD_DIM, axis=-1).reshape(2, 1, gw)
    wl["dn_dtb"] = jnp.repeat(dn_dt_bias[l], HEAD_DIM, axis=-1).reshape(2, 1, gw)
    wl["dn_onorm"] = tile_head(dn_onorm_g[l])
    nlb = S5_STATE // LANES
    lam_re, lam_im = s5_lambda_re[l].astype(F32), s5_lambda_im[l].astype(F32)
    dt = jnp.exp(s5_log_dt[l].astype(F32))[:, :, None]
    mag = jnp.exp(lam_re * dt)
    lbar_re, lbar_im = mag * jnp.cos(lam_im * dt), mag * jnp.sin(lam_im * dt)
    den = lam_re * lam_re + lam_im * lam_im
    num_re, num_im = lbar_re - 1.0, lbar_im
    coef_re = (num_re * lam_re + num_im * lam_im) / den
    coef_im = (num_im * lam_re - num_re * lam_im) / den
    b_re, b_im = s5_B_re[l].astype(F32), s5_B_im[l].astype(F32)
    bb_re = coef_re[..., None] * b_re - coef_im[..., None] * b_im
    bb_im = coef_re[..., None] * b_im + coef_im[..., None] * b_re
    eye_g = jnp.eye(S5_G, dtype=F32)
    bd_in = lambda m: jnp.einsum('dgpc,gk->dgckp', m, eye_g).reshape(2, gw, S5_STATE)
    blk = lambda m: m.reshape(2, gw, nlb, LANES).transpose(0, 2, 1, 3)
    b_cat = jnp.concatenate([blk(bd_in(bb_re)), blk(bd_in(bb_im))], axis=-1)
    wl["s5_b"] = b_cat.astype(BF16)
    bd_out = lambda m: jnp.einsum('gcp,gk->gpkc', m.astype(F32), eye_g).reshape(S5_STATE, gw)
    c_re, c_im = bd_out(s5_C_re[l]), bd_out(s5_C_im[l])
    wl["s5_c"] = jnp.concatenate([c_re.reshape(nlb, LANES, gw), -c_im.reshape(nlb, LANES, gw)],
                                 axis=1).astype(BF16)
    wl["s5_lbar"] = jnp.stack([lbar_re.reshape(2, nlb, LANES), lbar_im.reshape(2, nlb, LANES)], axis=2)
    wl["s5_d"] = s5_D[l].reshape(1, gw)
    wl["s5_glu_w"] = s5_glu_w[l].astype(BF16)
    wl["s5_glu_b"] = s5_glu_b[l].reshape(1, gw)
    wl["hg_loglb"] = jnp.log(lower_bound).reshape(1, gw)
    wl["hg_log1m"] = jnp.log1p(-lower_bound).reshape(1, gw)
    wl["hg_1m"] = (1.0 - lower_bound).reshape(1, gw)
    wl["hg_onorm"] = tile_head(hgrn_onorm_g[l])
    wl["wq_hi"], wl["wq_lo"] = _hi_lo(peer_wq[l].T)
    wl["keys_hi"], wl["keys_lo"] = _hi_lo(peer_keys[l].reshape(PEER_HEADS * 2, PEER_NKEYS, PEER_DQ // 2))
    wl["u_t"] = peer_u[l].astype(BF16).T
    wl["v_tab"] = peer_v[l].astype(BF16)
    return wl


def _heads_to_lanes(s, transposed):
    perm = (0, 1, 4, 2, 3) if transposed else (0, 1, 3, 2, 4)
    return s.astype(F32).transpose(perm).reshape(s.shape[0], 2, HEAD_DIM, GROUP_W)


def _lanes_to_heads(s, transposed):
    s = s.reshape(s.shape[0], 2, HEAD_DIM, N_HEADS, HEAD_DIM)
    return s.transpose((0, 1, 3, 4, 2) if transposed else (0, 1, 3, 2, 4))


def _to_column_major(x):
    b, n, d = x.shape
    return x.reshape(b, n // GRID_W, GRID_W, d).transpose(0, 2, 1, 3).reshape(b, n, d)


def _to_row_major(x):
    b, n, d = x.shape
    return x.reshape(b, GRID_W, n // GRID_W, d).transpose(0, 2, 1, 3).reshape(b, n, d)


def _trunk_layer(x, mod, wl, norm1_g, norm2_g, states, shared_mod):
    st_gla, st_dn, st_s5, st_hg = states
    p_gla, p_dn, p_s5, p_hg = _in_projection(x, mod, norm1_g, [wl["w_gla"], wl["w_dn"], wl["w_s5"], wl["w_hg"]])
    y_a, n_gla = _decay_mixer("gla", p_gla, wl["gla_up_hi"], wl["gla_up_lo"], wl["gla_bias"], wl["gla_onorm"], st_gla)
    y_b, n_dn = _delta_mixer(p_dn, wl["dn_conv"], wl["dn_expand"], wl["dn_nega"], wl["dn_dtb"], wl["dn_onorm"], st_dn)
    y_c, n_s5 = _s5_mixer(p_s5, wl["s5_b"], wl["s5_c"], wl["s5_lbar"], wl["s5_d"],
                          wl["s5_glu_w"], wl["s5_glu_b"], st_s5)
    y_d, n_hg = _decay_mixer("hgrn", p_hg, wl["hg_loglb"], wl["hg_log1m"], wl["hg_1m"], wl["hg_onorm"], st_hg)
    x = _out_projection(x, mod, [y_a, y_b, y_c, y_d], wl["w_out"])
    h, a, c, gate = _peer_route(x, mod, norm2_g, wl["wq_hi"], wl["wq_lo"], wl["keys_hi"], wl["keys_lo"])
    b, t, _ = x.shape
    fold = max(1, min(PEER_TOKEN_BLOCK // t, b)) if shared_mod else 1
    fb = lambda z: z.reshape(b // fold, fold * t, z.shape[-1])
    x = _peer_experts(fb(x), mod[0:b // fold], fb(h), fb(a), fb(c), fb(gate), wl["u_t"], wl["v_tab"]).reshape(b, t, -1)
    return x, (n_gla, n_dn, n_s5, n_hg)


def kernel(x_prompt, x_sample, state_gla, state_delta, state_s5_re, state_s5_im, state_hgrn, c, c_ctx,
           mod_w, mod_b, norm1_g, norm2_g, w_in, w_out, gla_gk_up, gla_gk_bias, gla_onorm_g,
           dn_conv_w, dn_A_log, dn_dt_bias, dn_onorm_g, s5_lambda_re, s5_lambda_im, s5_log_dt,
           s5_B_re, s5_B_im, s5_C_re, s5_C_im, s5_D, s5_glu_w, s5_glu_b, hgrn_lb_logits,
           hgrn_onorm_g, peer_wq, peer_keys, peer_u, peer_v, final_norm_g):
    bp, bs = x_prompt.shape[0], x_sample.shape[0]
    nlb = S5_STATE // LANES
    lb_cum = jnp.cumsum(jax.nn.softmax(hgrn_lb_logits.astype(F32), axis=0), axis=0)
    lower_bounds = lb_cum - lb_cum[0:1]

    rows = 2 * SUBLANES
    cc = jnp.concatenate([c, c_ctx[None, :], jnp.zeros((rows - bs - 1, D_MODEL), F32)], axis=0)
    mod_all = _modulation(cc, mod_w, mod_b)

    zeros_hh = jnp.zeros((bp, 2, HEAD_DIM, GROUP_W), F32)
    zeros_s5 = jnp.zeros((bp, 2, 2, nlb, 1, LANES), F32)

    xp, xs = x_prompt, x_sample
    ctx_states = []
    for l in range(DEPTH):
        wl = _layer_weights(l, w_in, w_out, gla_gk_up, gla_gk_bias, gla_onorm_g, dn_conv_w, dn_A_log,
                            dn_dt_bias, dn_onorm_g, s5_lambda_re, s5_lambda_im, s5_log_dt, s5_B_re, s5_B_im,
                            s5_C_re, s5_C_im, s5_D, s5_glu_w, s5_glu_b, lower_bounds[l], hgrn_onorm_g,
                            peer_wq, peer_keys, peer_u, peer_v)
        n1, n2 = norm1_g[l].reshape(1, D_MODEL), norm2_g[l].reshape(1, D_MODEL)
        mod_ctx = jnp.broadcast_to(mod_all[l, bs].reshape(1, 1, 6 * D_MODEL), (bp, 1, 6 * D_MODEL))
        mod_lat = mod_all[l, 0:bs].reshape(bs, 1, 6 * D_MODEL)
        xp, st_ctx = _trunk_layer(xp, mod_ctx, wl, n1, n2, (zeros_hh, zeros_hh, zeros_s5, zeros_hh), True)
        ctx_states.append(st_ctx)
        cached_s5 = jnp.stack([state_s5_re[:, l], state_s5_im[:, l]], axis=2).reshape(bs, 2, 2, nlb, 1, LANES)
        cached = (_heads_to_lanes(state_gla[:, l], True), _heads_to_lanes(state_delta[:, l], False), cached_s5,
                  _heads_to_lanes(state_hgrn[:, l], True))
        if l % 2 == 1:
            xs = _to_column_major(xs)
        xs, _ = _trunk_layer(xs, mod_lat, wl, n1, n2, cached, False)
        if l % 2 == 1:
            xs = _to_row_major(xs)

    gain = final_norm_g.reshape(1, D_MODEL)
    y_prompt = _final_norm(xp, gain)
    y_sample = _final_norm(xs, gain)
    new_gla = jnp.stack([_lanes_to_heads(s[0], True) for s in ctx_states], axis=1)
    new_dn = jnp.stack([_lanes_to_heads(s[1], False) for s in ctx_states], axis=1)
    s5 = jnp.stack([s[2].reshape(bp, 2, 2, S5_G, S5_P) for s in ctx_states], axis=1)
    new_hg = jnp.stack([_lanes_to_heads(s[3], True) for s in ctx_states], axis=1)
    return (y_prompt, y_sample, new_gla, new_dn, s5[:, :, :, 0], s5[:, :, :, 1], new_hg)
```

```python
import functools

import numpy as np
import jax
import jax.numpy as jnp
from jax import lax
from jax.experimental import pallas as pl
from jax.experimental.pallas import tpu as pltpu

F32 = jnp.float32
BF16 = jnp.bfloat16

D_MODEL = 1024
DEPTH = 2
GRID_W = 64
GROUP_W = 256
HEAD_DIM = 64
N_HEADS = GROUP_W // HEAD_DIM
GLA_LOWRANK = 16
GLA_GATE_NORM = 16.0
DN_CHUNK = 64
S5_GROUP = 16
S5_G = GROUP_W // S5_GROUP
S5_P = 64
S5_STATE = S5_G * S5_P
PEER_HEADS = 8
PEER_NKEYS = 128
PEER_DQ = 256
PEER_TOPK = 16
PEER_SEL = PEER_HEADS * PEER_TOPK
EPS = 1e-6

LANES = 128
SUBLANES = 8
VMEM_LIMIT = 56 * 1024 * 1024

GLA_CHUNK = 64
S5_CHUNK = 256
TOKEN_BLOCK = 256
PEER_TOKEN_BLOCK = 512
PEER_EXPERT_BLOCK = 2048
GLA_W = 4 * GROUP_W + LANES
DN_W = 4 * GROUP_W + LANES
HG_W = 5 * GROUP_W


def _dot(a, b):
    return jnp.dot(a, b, preferred_element_type=F32)


def _dot_nt(a, b):
    return lax.dot_general(a, b, (((1,), (1,)), ((), ())), preferred_element_type=F32)


def _dot_tn(a, b):
    return lax.dot_general(a, b, (((0,), (0,)), ((), ())), preferred_element_type=F32)


def _split2(x):
    hi = x.astype(BF16)
    lo = (x - hi.astype(F32)).astype(BF16)
    return hi, lo


def _split3(x):
    hi = x.astype(BF16)
    r = x - hi.astype(F32)
    mid = r.astype(BF16)
    lo = (r - mid.astype(F32)).astype(BF16)
    return hi, mid, lo


def _sel_dot(m01x3, x):
    return _dot(m01x3, jnp.concatenate(_split3(x), axis=0))


def _dot_sel(x, m01):
    hi, mid, lo = _split3(x)
    return _dot(hi, m01) + _dot(mid, m01) + _dot(lo, m01)


def _dot_hl(a, b_hi, b_lo):
    a_hi, a_lo = _split2(a)
    return _dot(a_hi, b_hi) + _dot(a_hi, b_lo) + _dot(a_lo, b_hi)


def _head_sum(x, ones_bd):
    hi, lo = _split2(x)
    return _dot(hi, ones_bd) + _dot(lo, ones_bd)


def _head_blockdiag(x16, ones_bd):
    return jnp.tile(x16, (N_HEADS, 1)) * ones_bd


def _per_head_mm(a, x, ones_bd):
    a_hi, a_lo = _split2(a)
    x_hi, x_lo = _split2(x)
    bd_hi = _head_blockdiag(x_hi, ones_bd)
    return _dot(a_hi, bd_hi) + _dot(a_hi, _head_blockdiag(x_lo, ones_bd)) + _dot(a_lo, bd_hi)


def _per_head_mm1(a, x, ones_bd):
    return _dot(a.astype(BF16), _head_blockdiag(x.astype(BF16), ones_bd))


def _head_diag_blocks(full):
    lane_head = lax.broadcasted_iota(jnp.int32, (HEAD_DIM, GROUP_W), 1) // HEAD_DIM
    out = None
    for h in range(N_HEADS):
        blk = jnp.where(lane_head == h, full[h * HEAD_DIM:(h + 1) * HEAD_DIM, :], 0.0)
        out = blk if out is None else out + blk
    return out


def _log_sigmoid(x):
    return jnp.minimum(x, 0.0) - jnp.log(1.0 + jnp.exp(-jnp.abs(x)))


def _softplus(x):
    return jnp.maximum(x, 0.0) + jnp.log(1.0 + jnp.exp(-jnp.abs(x)))


def _sigmoid(x):
    return 1.0 / (1.0 + jnp.exp(-x))


def _silu(x):
    return x * _sigmoid(x)


def _gelu_tanh(x):
    return 0.5 * x * (1.0 + jnp.tanh(0.7978845608028654 * (x + 0.044715 * x * x * x)))


def _rms_modulate(x, gain, shift, scale):
    ms = jnp.mean(x * x, axis=-1, keepdims=True)
    return x * lax.rsqrt(ms + EPS) * gain * (1.0 + scale) + shift


def _full_spec(shape):
    zeros = (0,) * len(shape)
    return pl.BlockSpec(shape, lambda *_: zeros)


def _params(sem):
    return pltpu.CompilerParams(dimension_semantics=sem, vmem_limit_bytes=VMEM_LIMIT)


def _scan_consts(c, reverse):
    levels = int(np.log2(c))
    idx = np.arange(c)
    i, t = idx[:, None], idx[None, :]
    sel, masks = [], []
    for lv in range(levels):
        s = 1 << lv
        right = (idx // s) % 2 == 1
        same_blk = (i // s) == (t // s)
        sel.append(np.where(right[:, None], same_blk & (t <= i), same_blk & (t > i)))
        same_parent = (i // (2 * s)) == (t // (2 * s))
        masks.append(same_parent & right[:, None] & ~right[None, :])
    sel.append(t <= i)
    sel.append(t > i)
    masks.append(i == t)
    if reverse:
        sel = [m[::-1, ::-1] for m in sel]
        masks = [m[::-1, ::-1] for m in masks]
    sel.append(np.ones((SUBLANES, c), bool))
    sel = np.tile(np.concatenate(sel, axis=0).astype(np.float32), (1, 3))
    masks = np.tile(np.stack(masks).astype(np.float32), (1, 1, N_HEADS))
    return jnp.asarray(sel, BF16), jnp.asarray(masks, F32)


def _dn_consts(c, reverse):
    levels = int(np.log2(c))
    idx = np.arange(c)
    i, t = idx[:, None], idx[None, :]
    incl, tail = t <= i, t > i
    masks = []
    for lv in range(levels):
        s = 1 << lv
        right = (idx // s) % 2 == 1
        masks.append(((i // (2 * s)) == (t // (2 * s))) & right[:, None] & ~right[None, :])
    masks += [incl, incl & (i != t), i == t]
    if reverse:
        incl, tail = incl[::-1, ::-1], tail[::-1, ::-1]
        masks = [m[::-1, ::-1] for m in masks]
    sel = np.tile(np.concatenate([incl, tail, np.ones((SUBLANES, c), bool)], axis=0).astype(np.float32), (1, 3))
    masks = np.tile(np.stack(masks).astype(np.float32), (1, 1, N_HEADS))
    return jnp.asarray(sel, BF16), jnp.asarray(masks, F32)


def _ones_blockdiag():
    h = np.arange(GROUP_W) // HEAD_DIM
    return jnp.asarray((h[:, None] == h[None, :]).astype(np.float32), BF16)


def _mod_kernel(c_ref, w_ref, b_ref, o_ref):
    a_hi, a_lo = _split2(_silu(c_ref[...]))
    w_hi, w_lo = _split2(w_ref[0])
    o_ref[0] = _dot(a_hi, w_hi) + _dot(a_hi, w_lo) + _dot(a_lo, w_hi) + b_ref[0]


def _modulation(cc, mod_w, mod_b):
    rows = cc.shape[0]
    nb = 6
    return pl.pallas_call(
        _mod_kernel,
        grid=(DEPTH, nb),
        in_specs=[pl.BlockSpec((rows, D_MODEL), lambda l, j: (0, 0)),
                  pl.BlockSpec((1, D_MODEL, D_MODEL), lambda l, j: (l, 0, j)),
                  pl.BlockSpec((1, 1, D_MODEL), lambda l, j: (l, 0, j))],
        out_specs=pl.BlockSpec((1, rows, D_MODEL), lambda l, j: (l, 0, j)),
        out_shape=jax.ShapeDtypeStruct((DEPTH, rows, 6 * D_MODEL), F32),
        compiler_params=_params(("arbitrary", "arbitrary")),
        name="modulation",
    )(cc, mod_w, mod_b.reshape(DEPTH, 1, 6 * D_MODEL))


def _inproj_kernel(x_ref, mod_ref, g_ref, wa_ref, wb_ref, wc_ref, wd_ref, oa_ref, ob_ref, oc_ref, od_ref):
    mod = mod_ref[0]
    h = _rms_modulate(x_ref[0], g_ref[...], mod[:, 0:D_MODEL], mod[:, D_MODEL:2 * D_MODEL]).astype(BF16)
    oa_ref[0] = _dot(h, wa_ref[...])
    ob_ref[0] = _dot(h, wb_ref[...])
    oc_ref[0] = _dot(h, wc_ref[...])
    od_ref[0] = _dot(h, wd_ref[...])


def _in_projection(x, mod, gain, weights):
    b, t, _ = x.shape
    tb = min(TOKEN_BLOCK, t)
    widths = [w.shape[1] for w in weights]
    return pl.pallas_call(
        _inproj_kernel,
        grid=(b, t // tb),
        in_specs=[pl.BlockSpec((1, tb, D_MODEL), lambda i, j: (i, j, 0)),
                  pl.BlockSpec((1, 1, 6 * D_MODEL), lambda i, j: (i, 0, 0)),
                  _full_spec((1, D_MODEL))] + [_full_spec(w.shape) for w in weights],
        out_specs=[pl.BlockSpec((1, tb, n), lambda i, j: (i, j, 0)) for n in widths],
        out_shape=[jax.ShapeDtypeStruct((b, t, n), F32) for n in widths],
        compiler_params=_params(("arbitrary", "arbitrary")),
        name="in_projection",
    )(x, mod, gain, *weights)


def _decay_scan_prepare(probs, ones_bd, c):
    levels = int(np.log2(c))
    exs = [jnp.exp(_sel_dot(pr["sel"][...], pr["g"])) for pr in probs]
    attns = [None] * len(probs)
    for lv in range(levels + 1):
        for i, (pr, ex) in enumerate(zip(probs, exs)):
            q, k = pr["q"], pr["k"]
            if lv < levels:
                e = ex[lv * c:(lv + 1) * c]
                qt, kt = (q * e).astype(BF16), (k * e).astype(BF16)
            else:
                qt, kt = q.astype(BF16), k.astype(BF16)
            p = _dot_nt(qt, _head_blockdiag(kt, ones_bd)) * pr["mask"][lv]
            attns[i] = p if attns[i] is None else attns[i] + p
    res = []
    for pr, ex, attn in zip(probs, exs, attns):
        o = _dot(attn.astype(BF16), _head_blockdiag(pr["v"].astype(BF16), ones_bd))
        qd = (pr["q"] * ex[levels * c:(levels + 1) * c]).astype(BF16)
        ktail = (pr["k"] * ex[(levels + 1) * c:(levels + 2) * c]).astype(BF16)
        res.append((o, qd, ktail, ex[(levels + 2) * c:(levels + 2) * c + SUBLANES]))
    return res


def _decay_scan_state_steps(items, ones_bd, state_ref):
    sts = [state_ref[it[0]] for it in items]
    outs = [it[1] + _dot_nt(it[2], _head_blockdiag(st.astype(BF16), ones_bd)) for it, st in zip(items, sts)]
    for it, st in zip(items, sts):
        state_ref[it[0]] = st * it[5] + _head_diag_blocks(_dot_tn(it[4], it[3]))
    return outs


def _head_norm_gate(o, gate, onorm, ones_bd):
    ms = _head_sum(o * o, ones_bd) * (1.0 / HEAD_DIM)
    return o * lax.rsqrt(ms + EPS) * onorm * _silu(gate)


def _decay_mixer_kernel(mode, t, c, p_ref, self_ref, selb_ref, maskf_ref, maskb_ref, ones_ref, onorm_ref,
                        pa_ref, pb_ref, pc_ref, h0_ref, y_ref, sout_ref,
                        qd_ref, kt_ref, v16_ref, dl_ref, of_ref, ob_ref, state_ref):
    n_chunks = t // c
    state_ref[...] = h0_ref[0]
    sels, masks, o_refs = (self_ref, selb_ref), (maskf_ref, maskb_ref), (of_ref, ob_ref)

    def inputs(rows, d):
        if mode == "gla":
            q = p_ref[0, rows, 0:GROUP_W] * (HEAD_DIM ** -0.5)
            k = p_ref[0, rows, GROUP_W:2 * GROUP_W]
            v = p_ref[0, rows, 2 * GROUP_W:3 * GROUP_W]
            low = p_ref[0, rows, 4 * GROUP_W:4 * GROUP_W + LANES]
            gk = _dot_hl(low, pa_ref[d], pb_ref[d]) + pc_ref[d]
            g = _log_sigmoid(gk) * (1.0 / GLA_GATE_NORM)
        else:
            q = _silu(p_ref[0, rows, 0:GROUP_W]) * (HEAD_DIM ** -0.5)
            v = p_ref[0, rows, GROUP_W:2 * GROUP_W]
            z = p_ref[0, rows, (3 + d) * GROUP_W:(4 + d) * GROUP_W]
            a = pa_ref[...]
            b = pb_ref[...] + _log_sigmoid(z)
            m = jnp.maximum(a, b)
            g = m + jnp.log(jnp.exp(a - m) + jnp.exp(b - m))
            k = pc_ref[...] * _sigmoid(-z)
        return q, k, v, g

    per_iter = 4 if n_chunks % 4 == 0 else 1

    def prepare(i, carry):
        chunks = [i * per_iter + j for j in range(per_iter)]
        probs = []
        for n in chunks:
            rows = pl.ds(pl.multiple_of(n * c, c), c)
            for d in range(2):
                q, k, v, g = inputs(rows, d)
                probs.append(dict(q=q, k=k, v=v, g=g, sel=sels[d], mask=masks[d]))
            v16_ref[rows, :] = probs[-1]["v"].astype(BF16)
        for idx, (o, qd, ktail, dlast) in enumerate(_decay_scan_prepare(probs, ones_ref[...], c)):
            n, d = chunks[idx // 2], idx % 2
            rows = pl.ds(pl.multiple_of(n * c, c), c)
            o_refs[d][rows, :] = o
            qd_ref[d, rows, :] = qd
            kt_ref[d, rows, :] = ktail
            dl_ref[d, pl.ds(pl.multiple_of(n * SUBLANES, SUBLANES), SUBLANES), :] = dlast
        return carry

    lax.fori_loop(0, n_chunks // per_iter, prepare, 0)

    def body(i, carry):
        items, rows_of = [], []
        for d, n in ((0, i), (1, n_chunks - 1 - i)):
            rows = pl.ds(pl.multiple_of(n * c, c), c)
            dlast = dl_ref[d, pl.ds(pl.multiple_of(n * SUBLANES, SUBLANES), SUBLANES), :][0:1, :]
            items.append((d, o_refs[d][rows, :], qd_ref[d, rows, :], kt_ref[d, rows, :], v16_ref[rows, :], dlast))
            rows_of.append(rows)
        o_f, o_b = _decay_scan_state_steps(items, ones_ref[...], state_ref)
        of_ref[rows_of[0], :] = o_f
        ob_ref[rows_of[1], :] = o_b
        return carry

    lax.fori_loop(0, n_chunks, body, 0)

    rbk = min(t, 256)
    gate_col = 3 * GROUP_W if mode == "gla" else 2 * GROUP_W

    def fin(i, carry):
        rows = pl.ds(pl.multiple_of(i * rbk, rbk), rbk)
        o = of_ref[rows, :] + ob_ref[rows, :]
        gate = p_ref[0, rows, gate_col:gate_col + GROUP_W]
        y_ref[0, rows, :] = _head_norm_gate(o, gate, onorm_ref[...], ones_ref[...]).astype(BF16)
        return carry

    lax.fori_loop(0, t // rbk, fin, 0)
    sout_ref[0] = state_ref[...]


def _decay_mixer(mode, p, pa, pb, pc, onorm, h0t):
    b, t, w = p.shape
    c = GLA_CHUNK
    sel_f, mask_f = _scan_consts(c, False)
    sel_b, mask_b = _scan_consts(c, True)
    ones_bd = _ones_blockdiag()
    consts = [sel_f, sel_b, mask_f, mask_b, ones_bd, onorm, pa, pb, pc]
    state_shape = (2, HEAD_DIM, GROUP_W)
    return pl.pallas_call(
        functools.partial(_decay_mixer_kernel, mode, t, c),
        grid=(b,),
        in_specs=[pl.BlockSpec((1, t, w), lambda i: (i, 0, 0), pipeline_mode=pl.Buffered(1))]
                 + [_full_spec(a.shape) for a in consts]
                 + [pl.BlockSpec((1,) + state_shape, lambda i: (i, 0, 0, 0))],
        out_specs=[pl.BlockSpec((1, t, GROUP_W), lambda i: (i, 0, 0)),
                   pl.BlockSpec((1,) + state_shape, lambda i: (i, 0, 0, 0))],
        out_shape=[jax.ShapeDtypeStruct((b, t, GROUP_W), BF16),
                   jax.ShapeDtypeStruct((b,) + state_shape, F32)],
        scratch_shapes=[pltpu.VMEM((2, t, GROUP_W), BF16), pltpu.VMEM((2, t, GROUP_W), BF16),
                        pltpu.VMEM((t, GROUP_W), BF16), pltpu.VMEM((2, (t // c) * SUBLANES, GROUP_W), F32),
                        pltpu.VMEM((t, GROUP_W), F32), pltpu.VMEM((t, GROUP_W), F32),
                        pltpu.VMEM(state_shape, F32)],
        compiler_params=_params(("arbitrary",)),
        name=mode + "_mixer",
    )(p, *consts, h0t)


def _delta_prepare(probs, ones_bd, expand_ref, nega_ref, dtb_ref, c):
    levels = int(np.log2(c))
    work = []
    for pr in probs:
        d, mask = pr["d"], pr["mask"]
        logits = _dot_sel(pr["ba"], expand_ref[d])
        beta = _sigmoid(logits[:, 0:GROUP_W])
        g = nega_ref[d] * _softplus(logits[:, GROUP_W:2 * GROUP_W] + dtb_ref[d])
        cs = _sel_dot(pr["sel"][...], g)
        b_incl = cs[0:c]
        b_t = jnp.concatenate([b_incl[:, h * HEAD_DIM:(h + 1) * HEAD_DIM].T for h in range(N_HEADS)], axis=1)
        lmat = jnp.exp(jnp.minimum(b_incl - b_t, 0.0)) * mask[levels]
        m = beta * pr["kk"] * lmat * mask[levels + 1]
        work.append(dict(beta=beta, eb=jnp.exp(b_incl), etail=jnp.exp(cs[c:2 * c]),
                         dlast=jnp.exp(cs[2 * c:2 * c + SUBLANES]), lmat=lmat, m=m,
                         t=mask[levels + 2] - m * mask[0]))
    for lv in range(1, levels):
        mm = _per_head_mm1 if lv == levels - 1 else _per_head_mm
        ps = [mm(wk["m"] * pr["mask"][lv], wk["t"], ones_bd) for pr, wk in zip(probs, work)]
        for wk, p in zip(work, ps):
            wk["t"] = wk["t"] - mm(wk["t"], p, ones_bd)
    us = [_per_head_mm1(wk["t"], pr["v"] * wk["beta"], ones_bd) for pr, wk in zip(probs, work)]
    ws = [_per_head_mm1(wk["t"], pr["k"] * wk["beta"] * wk["eb"], ones_bd) for pr, wk in zip(probs, work)]
    return [(u, w.astype(BF16), (pr["qk_raw"] * wk["lmat"]).astype(BF16), (pr["q"] * wk["eb"]).astype(BF16),
             (pr["k"] * wk["etail"]).astype(BF16), wk["dlast"]) for pr, wk, u, w in zip(probs, work, us, ws)]


def _delta_state_steps(items, ones_bd, state_ref):
    sts = [state_ref[it[0]] for it in items]
    bds = [_head_blockdiag(st.astype(BF16), ones_bd) for st in sts]
    v16s = [(it[1] - _dot(it[2], bd)).astype(BF16) for it, bd in zip(items, bds)]
    outs = [_dot(it[4], bd) + _dot(it[3], _head_blockdiag(v16, ones_bd)) for it, bd, v16 in zip(items, bds, v16s)]
    for it, st, v16 in zip(items, sts, v16s):
        state_ref[it[0]] = st * it[6] + _head_diag_blocks(_dot_tn(it[5], v16))
    return outs


def _delta_mixer_kernel(t, c, p_ref, self_ref, selb_ref, maskf_ref, maskb_ref, ones_ref, onorm_ref,
                        conv_ref, expand_ref, nega_ref, dtb_ref, h0_ref, y_ref, sout_ref,
                        u_ref, w_ref, qk_ref, qd_ref, kt_ref, dl_ref, of_ref, ob_ref, state_ref):
    n_chunks = t // c
    qkv_w = 3 * GROUP_W
    state_ref[...] = h0_ref[0]
    row_id = lax.broadcasted_iota(jnp.int32, (c, qkv_w), 0)
    w_prev, w_mid, w_next = conv_ref[0:1, :], conv_ref[1:2, :], conv_ref[2:3, :]
    sels, masks = (self_ref, selb_ref), (maskf_ref, maskb_ref)

    def conv_qkv(n):
        r0 = pl.multiple_of(n * c, c)
        x = p_ref[0, pl.ds(r0, c), 0:qkv_w]
        before = p_ref[0, pl.ds(pl.multiple_of(jnp.maximum(r0 - SUBLANES, 0), SUBLANES), SUBLANES), 0:qkv_w]
        after = p_ref[0, pl.ds(pl.multiple_of(jnp.minimum(r0 + c, t - SUBLANES), SUBLANES), SUBLANES), 0:qkv_w]
        before_row = jnp.where(n > 0, before[SUBLANES - 1:SUBLANES, :], 0.0)
        after_row = jnp.where(n < n_chunks - 1, after[0:1, :], 0.0)
        x_prev = jnp.where(row_id == 0, before_row, pltpu.roll(x, 1, axis=0))
        x_next = jnp.where(row_id == c - 1, after_row, pltpu.roll(x, c - 1, axis=0))
        y = _silu(x_prev * w_prev + x * w_mid + x_next * w_next)
        qk = y[:, 0:2 * GROUP_W]
        ones2 = ones_ref[...]
        ss = jnp.concatenate([_head_sum(qk[:, 0:GROUP_W] * qk[:, 0:GROUP_W], ones2),
                              _head_sum(qk[:, GROUP_W:] * qk[:, GROUP_W:], ones2)], axis=1)
        qk = qk * lax.rsqrt(ss + EPS)
        q, k, v = qk[:, 0:GROUP_W] * (HEAD_DIM ** -0.5), qk[:, GROUP_W:], y[:, 2 * GROUP_W:]
        k_hi, k_lo = _split2(k)
        bd_k = _head_blockdiag(k_hi, ones2)
        kk = _dot_nt(k_hi, bd_k) + _dot_nt(k_hi, _head_blockdiag(k_lo, ones2)) + _dot_nt(k_lo, bd_k)
        qk_raw = _dot_nt(q.astype(BF16), bd_k)
        ba = p_ref[0, pl.ds(r0, c), 4 * GROUP_W:4 * GROUP_W + LANES]
        return dict(q=q, k=k, v=v, kk=kk, qk_raw=qk_raw, ba=ba)

    per_iter = 4 if n_chunks % 4 == 0 else 1

    def prepare(i, carry):
        chunks = [i * per_iter + j for j in range(per_iter)]
        probs = []
        for n in chunks:
            shared = conv_qkv(n)
            probs += [dict(shared, sel=sels[d], mask=masks[d], d=d) for d in range(2)]
        res = _delta_prepare(probs, ones_ref[...], expand_ref, nega_ref, dtb_ref, c)
        for idx, (u, w16, qk16, qd, ktail, dlast) in enumerate(res):
            n, d = chunks[idx // 2], idx % 2
            rows = pl.ds(pl.multiple_of(n * c, c), c)
            u_ref[d, rows, :] = u
            w_ref[d, rows, :] = w16
            qk_ref[d, rows, :] = qk16
            qd_ref[d, rows, :] = qd
            kt_ref[d, rows, :] = ktail
            dl_ref[d, pl.ds(pl.multiple_of(n * SUBLANES, SUBLANES), SUBLANES), :] = dlast
        return carry

    lax.fori_loop(0, n_chunks // per_iter, prepare, 0)

    def body(i, carry):
        items, rows_of = [], []
        for d, n in ((0, i), (1, n_chunks - 1 - i)):
            rows = pl.ds(pl.multiple_of(n * c, c), c)
            dlast = dl_ref[d, pl.ds(pl.multiple_of(n * SUBLANES, SUBLANES), SUBLANES), :][0:1, :]
            items.append((d, u_ref[d, rows, :], w_ref[d, rows, :], qk_ref[d, rows, :], qd_ref[d, rows, :],
                          kt_ref[d, rows, :], dlast))
            rows_of.append(rows)
        o_f, o_b = _delta_state_steps(items, ones_ref[...], state_ref)
        of_ref[rows_of[0], :] = o_f
        ob_ref[rows_of[1], :] = o_b
        return carry

    lax.fori_loop(0, n_chunks, body, 0)

    rbk = min(t, 256)

    def fin(i, carry):
        rows = pl.ds(pl.multiple_of(i * rbk, rbk), rbk)
        o = of_ref[rows, :] + ob_ref[rows, :]
        gate = p_ref[0, rows, 3 * GROUP_W:4 * GROUP_W]
        y_ref[0, rows, :] = _head_norm_gate(o, gate, onorm_ref[...], ones_ref[...]).astype(BF16)
        return carry

    lax.fori_loop(0, t // rbk, fin, 0)
    sout_ref[0] = state_ref[...]


def _delta_mixer(p, conv_w, expand, nega, dtb, onorm, s0):
    b, t, w = p.shape
    c = DN_CHUNK
    sel_f, mask_f = _dn_consts(c, False)
    sel_b, mask_b = _dn_consts(c, True)
    consts = [sel_f, sel_b, mask_f, mask_b, _ones_blockdiag(), onorm, conv_w, expand, nega, dtb]
    state_shape = (2, HEAD_DIM, GROUP_W)
    return pl.pallas_call(
        functools.partial(_delta_mixer_kernel, t, c),
        grid=(b,),
        in_specs=[pl.BlockSpec((1, t, w), lambda i: (i, 0, 0), pipeline_mode=pl.Buffered(1))]
                 + [_full_spec(a.shape) for a in consts]
                 + [pl.BlockSpec((1,) + state_shape, lambda i: (i, 0, 0, 0))],
        out_specs=[pl.BlockSpec((1, t, GROUP_W), lambda i: (i, 0, 0)),
                   pl.BlockSpec((1,) + state_shape, lambda i: (i, 0, 0, 0))],
        out_shape=[jax.ShapeDtypeStruct((b, t, GROUP_W), BF16),
                   jax.ShapeDtypeStruct((b,) + state_shape, F32)],
        scratch_shapes=[pltpu.VMEM((2, t, GROUP_W), F32)]
                       + [pltpu.VMEM((2, t, GROUP_W), BF16)] * 4
                       + [pltpu.VMEM((2, (t // c) * SUBLANES, GROUP_W), F32),
                          pltpu.VMEM((t, GROUP_W), F32), pltpu.VMEM((t, GROUP_W), F32),
                          pltpu.VMEM(state_shape, F32)],
        compiler_params=_params(("arbitrary",)),
        name="delta_mixer",
    )(p, *consts, s0)


def _s5_kernel(t, tc, u_ref, b_ref, cm_ref, lbar_ref, dskip_ref, gluw_ref, glub_ref, h0_ref,
               y_ref, sout_ref, x_ref, yf_ref, yb_ref, carry_ref):
    n_chunks = t // tc
    n_lane_blocks = S5_STATE // LANES
    carry_ref[...] = h0_ref[0]
    sub8 = lax.broadcasted_iota(jnp.int32, (SUBLANES, LANES), 0)

    def scan_dirs(rows_of, y_refs):
        for d in range(2):
            u16 = u_ref[0, rows_of[d], :].astype(BF16)
            for j in range(n_lane_blocks):
                x_ref[d, j] = _dot(u16, b_ref[d, j])
            y_refs[d][rows_of[d], :] = jnp.zeros((tc, GROUP_W), F32)

        def scan_block(d, j):
            bu = x_ref[d, j]
            xr, xi = bu[:, 0:LANES], bu[:, LANES:2 * LANES]
            lb = lbar_ref[d, j]
            pw = [(lb[0:1, :], lb[1:2, :])]
            for _ in range(SUBLANES - 1):
                pr, pi = pw[-1]
                pw.append((pr * pw[0][0] - pi * pw[0][1], pr * pw[0][1] + pi * pw[0][0]))
            n_groups = tc // SUBLANES
            blocks_r = [xr[v * SUBLANES:(v + 1) * SUBLANES] for v in range(n_groups)]
            blocks_i = [xi[v * SUBLANES:(v + 1) * SUBLANES] for v in range(n_groups)]
            for s in range(int(np.log2(SUBLANES))):
                sh = 1 << s
                keep = sub8 < SUBLANES - sh if d else sub8 >= sh
                m_re, m_im = jnp.where(keep, pw[sh - 1][0], 0.0), jnp.where(keep, pw[sh - 1][1], 0.0)
                rot = SUBLANES - sh if d else sh
                for v in range(n_groups):
                    sr, si = pltpu.roll(blocks_r[v], rot, axis=0), pltpu.roll(blocks_i[v], rot, axis=0)
                    blocks_r[v] = blocks_r[v] + m_re * sr - m_im * si
                    blocks_i[v] = blocks_i[v] + m_re * si + m_im * sr
            p_re = jnp.zeros((SUBLANES, LANES), F32)
            p_im = jnp.zeros((SUBLANES, LANES), F32)
            for r in range(SUBLANES):
                pr, pi = pw[SUBLANES - 1 - r] if d else pw[r]
                p_re = jnp.where(sub8 == r, pr, p_re)
                p_im = jnp.where(sub8 == r, pi, p_im)
            cr, ci = carry_ref[d, 0, j], carry_ref[d, 1, j]
            last = 0 if d else SUBLANES - 1
            for step in range(n_groups):
                v = n_groups - 1 - step if d else step
                blocks_r[v] = blocks_r[v] + (p_re * cr - p_im * ci)
                blocks_i[v] = blocks_i[v] + (p_re * ci + p_im * cr)
                cr, ci = blocks_r[v][last:last + 1, :], blocks_i[v][last:last + 1, :]
            carry_ref[d, 0, j] = cr
            carry_ref[d, 1, j] = ci
            return jnp.concatenate([jnp.concatenate(blocks_r, axis=0), jnp.concatenate(blocks_i, axis=0)],
                                   axis=1).astype(BF16)

        def lane_block(j, carry):
            states = [scan_block(d, j) for d in range(2)]
            for d in range(2):
                y_refs[d][rows_of[d], :] += _dot(states[d], cm_ref[j])
            return carry

        lax.fori_loop(0, n_lane_blocks, lane_block, 0)

    def body(n, carry):
        scan_dirs((pl.ds(pl.multiple_of(n * tc, tc), tc), pl.ds(pl.multiple_of((n_chunks - 1 - n) * tc, tc), tc)),
                  (yf_ref, yb_ref))
        return carry

    lax.fori_loop(0, n_chunks, body, 0)

    def fin(i, carry):
        rows = pl.ds(pl.multiple_of(i * tc, tc), tc)
        u = u_ref[0, rows, :]
        y = _gelu_tanh(yf_ref[rows, :] + yb_ref[rows, :] + dskip_ref[...] * u)
        z = _dot(y.astype(BF16), gluw_ref[...]) + glub_ref[...]
        y_ref[0, rows, :] = (y * _sigmoid(z)).astype(BF16)
        return carry

    lax.fori_loop(0, n_chunks, fin, 0)
    sout_ref[0] = carry_ref[...]


def _s5_mixer(u, b_in, cmat, lbar, dskip, glu_w, glu_b, h0):
    b, t, _ = u.shape
    tc = min(S5_CHUNK, t)
    nlb = S5_STATE // LANES
    consts = [b_in, cmat, lbar, dskip, glu_w, glu_b]
    st_shape = (2, 2, nlb, 1, LANES)
    return pl.pallas_call(
        functools.partial(_s5_kernel, t, tc),
        grid=(b,),
        in_specs=[pl.BlockSpec((1, t, GROUP_W), lambda i: (i, 0, 0))]
                 + [_full_spec(a.shape) for a in consts]
                 + [pl.BlockSpec((1,) + st_shape, lambda i: (i, 0, 0, 0, 0, 0))],
        out_specs=[pl.BlockSpec((1, t, GROUP_W), lambda i: (i, 0, 0)),
                   pl.BlockSpec((1,) + st_shape, lambda i: (i, 0, 0, 0, 0, 0))],
        out_shape=[jax.ShapeDtypeStruct((b, t, GROUP_W), BF16),
                   jax.ShapeDtypeStruct((b,) + st_shape, F32)],
        scratch_shapes=[pltpu.VMEM((2, nlb, tc, 2 * LANES), F32), pltpu.VMEM((t, GROUP_W), F32),
                        pltpu.VMEM((t, GROUP_W), F32), pltpu.VMEM(st_shape, F32)],
        compiler_params=_params(("arbitrary",)),
        name="s5_mixer",
    )(u, *consts, h0)


def _outproj_kernel(x_ref, mod_ref, ya_ref, yb_ref, yc_ref, yd_ref, w_ref, o_ref):
    y = (_dot(ya_ref[0], w_ref[0]) + _dot(yb_ref[0], w_ref[1])
         + _dot(yc_ref[0], w_ref[2]) + _dot(yd_ref[0], w_ref[3]))
    o_ref[0] = x_ref[0] + mod_ref[0][:, 2 * D_MODEL:3 * D_MODEL] * y


def _out_projection(x, mod, ys, w_out):
    b, t, _ = x.shape
    tb = min(TOKEN_BLOCK, t)
    tok = lambda n: pl.BlockSpec((1, tb, n), lambda i, j: (i, j, 0))
    return pl.pallas_call(
        _outproj_kernel,
        grid=(b, t // tb),
        in_specs=[tok(D_MODEL), pl.BlockSpec((1, 1, 6 * D_MODEL), lambda i, j: (i, 0, 0))]
                 + [tok(GROUP_W)] * 4 + [_full_spec(w_out.shape)],
        out_specs=tok(D_MODEL),
        out_shape=jax.ShapeDtypeStruct(x.shape, F32),
        compiler_params=_params(("arbitrary", "arbitrary")),
        name="out_projection",
    )(x, mod, *ys, w_out)


def _topk_rows(scores, k, row_ids):
    n = scores[0].shape[1]
    slot = lax.broadcasted_iota(jnp.int32, (k, n), 0)
    scores = list(scores)
    vals = [jnp.zeros((k, n), F32) for _ in scores]
    ids = [jnp.zeros((k, n), F32) for _ in scores]
    for it in range(k):
        ms = [jnp.max(s, axis=0, keepdims=True) for s in scores]
        idxs = [jnp.min(jnp.where(s == m, row_ids, 1e9), axis=0, keepdims=True) for s, m in zip(scores, ms)]
        vals = [jnp.where(slot == it, m, v) for m, v in zip(ms, vals)]
        ids = [jnp.where(slot == it, idx, i) for idx, i in zip(idxs, ids)]
        scores = [jnp.where(row_ids == idx, -jnp.inf, s) for s, idx in zip(scores, idxs)]
    return list(zip(vals, ids))


def _staircase():
    pairs = [(i, j) for i in range(PEER_TOPK) for j in range(PEER_TOPK) if (i + 1) * (j + 1) <= PEER_TOPK]
    rows = -(-len(pairs) // SUBLANES) * SUBLANES
    sel = np.zeros((2, rows, PEER_TOPK), np.float32)
    aux = np.zeros((2, rows, LANES), np.float32)
    aux[0, len(pairs):] = -np.inf
    aux[1, len(pairs):] = 1e9
    for r, (i, j) in enumerate(pairs):
        sel[0, r, i] = 1.0
        sel[1, r, j] = 1.0
        aux[1, r] = i * PEER_TOPK + j
    return jnp.asarray(np.tile(sel, (1, 1, 3)), BF16), jnp.asarray(aux, F32)


def _take_rows(table, idx, k):
    out = jnp.zeros(idx.shape, F32)
    for r in range(k):
        out = jnp.where(idx == float(r), table[r:r + 1, :], out)
    return out


def _peer_route_kernel(tb, x_ref, mod_ref, g_ref, wqh_ref, wql_ref, kh_ref, kl_ref, stsel_ref, staux_ref,
                       h_ref, a_ref, c_ref, gate_ref, qt_ref, at_ref, ct_ref, gt_ref):
    mod = mod_ref[0]
    h = _rms_modulate(x_ref[0], g_ref[...], mod[:, 3 * D_MODEL:4 * D_MODEL], mod[:, 4 * D_MODEL:5 * D_MODEL])
    h_hi, h_lo = _split2(h)
    h_ref[0] = h_hi
    wq_hi = wqh_ref[...]
    qt_ref[...] = _dot_nt(wq_hi, h_hi) + _dot_nt(wq_hi, h_lo) + _dot_nt(wql_ref[...], h_hi)
    half = PEER_DQ // 2
    key_ids = lax.broadcasted_iota(jnp.int32, (PEER_NKEYS, LANES), 0).astype(F32)

    heads_per_iter = 4

    def head_body(it, carry):
        heads = [it * heads_per_iter + i for i in range(heads_per_iter)]
        tiles = [slice(lt * LANES, (lt + 1) * LANES) for lt in range(tb // LANES)]
        probs = [(head, lanes) for head in heads for lanes in tiles]
        scores = []
        for head, lanes in probs:
            for p in range(2):
                hp = head * 2 + p
                q_hi, q_lo = _split2(qt_ref[pl.ds(pl.multiple_of(hp * half, half), half), lanes])
                scores.append(_dot(kh_ref[hp], q_hi) + _dot(kh_ref[hp], q_lo) + _dot(kl_ref[hp], q_hi))
        top = _topk_rows(scores, PEER_TOPK, key_ids)
        cands = [_sel_dot(stsel_ref[0], top[2 * i][0]) + _sel_dot(stsel_ref[1], top[2 * i + 1][0]) + staux_ref[0]
                 for i in range(len(probs))]
        best = _topk_rows(cands, PEER_TOPK, staux_ref[1])
        for i, (head, lanes) in enumerate(probs):
            out_rows = pl.ds(pl.multiple_of(head * PEER_TOPK, PEER_TOPK), PEER_TOPK)
            sc, flat = best[i]
            r1 = jnp.floor(flat * (1.0 / PEER_TOPK))
            r2 = flat - r1 * PEER_TOPK
            e = jnp.exp(sc - sc[0:1, :])
            at_ref[out_rows, lanes] = _take_rows(top[2 * i][1], r1, PEER_TOPK)
            ct_ref[out_rows, lanes] = _take_rows(top[2 * i + 1][1], r2, PEER_TOPK)
            gt_ref[out_rows, lanes] = e / jnp.sum(e, axis=0, keepdims=True)
        return carry

    lax.fori_loop(0, PEER_HEADS // heads_per_iter, head_body, 0)
    a_ref[0] = at_ref[...].T.astype(jnp.int32)
    c_ref[0] = ct_ref[...].T.astype(jnp.int32)
    gate_ref[0] = gt_ref[...].T


def _peer_route(x, mod, gain, wq_hi, wq_lo, keys_hi, keys_lo):
    b, t, _ = x.shape
    tb = min(TOKEN_BLOCK, t)
    tok = lambda n: pl.BlockSpec((1, tb, n), lambda i, j: (i, j, 0))
    consts = [gain, wq_hi, wq_lo, keys_hi, keys_lo, *_staircase()]
    return pl.pallas_call(
        functools.partial(_peer_route_kernel, tb),
        grid=(b, t // tb),
        scratch_shapes=[pltpu.VMEM((PEER_HEADS * PEER_DQ, tb), F32)] + [pltpu.VMEM((PEER_SEL, tb), F32)] * 3,
        in_specs=[tok(D_MODEL), pl.BlockSpec((1, 1, 6 * D_MODEL), lambda i, j: (i, 0, 0))]
                 + [_full_spec(a.shape) for a in consts],
        out_specs=[tok(D_MODEL), tok(PEER_SEL), tok(PEER_SEL), tok(PEER_SEL)],
        out_shape=[jax.ShapeDtypeStruct((b, t, D_MODEL), BF16),
                   jax.ShapeDtypeStruct((b, t, PEER_SEL), jnp.int32),
                   jax.ShapeDtypeStruct((b, t, PEER_SEL), jnp.int32),
                   jax.ShapeDtypeStruct((b, t, PEER_SEL), F32)],
        compiler_params=_params(("arbitrary", "arbitrary")),
        name="peer_route",
    )(x, mod, *consts)


def _peer_expert_kernel(tb, n_slabs, x_ref, mod_ref, h_ref, a_ref, c_ref, gate_ref, ut_ref, v_ref,
                        o_ref, gmat_ref, acc_ref):
    j = pl.program_id(2)
    nk = PEER_NKEYS

    @pl.when(j == 0)
    def _():
        acc_ref[...] = jnp.zeros_like(acc_ref)
        key_id = lax.broadcasted_iota(jnp.int32, (nk, PEER_SEL), 0)

        pack = 2 * SUBLANES
        groups = 2

        def build(nb, carry):
            n0 = pl.multiple_of(nb * groups * pack, groups * pack)
            a_rows = a_ref[0, pl.ds(n0, groups * pack), :]
            c_rows = c_ref[0, pl.ds(n0, groups * pack), :]
            g_rows = gate_ref[0, pl.ds(n0, groups * pack), :]
            mats = []
            for r in range(groups * pack):
                row_sel = jnp.where(key_id == a_rows[r:r + 1], 1.0, 0.0).astype(BF16)
                col_sel = jnp.where(key_id == c_rows[r:r + 1], g_rows[r:r + 1], 0.0).astype(BF16)
                mats.append(_dot_nt(row_sel, col_sel).astype(BF16))
            for gi in range(groups):
                stacked = jnp.stack(mats[gi * pack:(gi + 1) * pack], axis=0)
                gmat_ref[:, pl.ds(n0 + gi * pack, pack), :] = pltpu.einshape("nac->anc", stacked)
            return carry

        lax.fori_loop(0, tb // (groups * pack), build, 0)

    h = h_ref[0]
    n_pairs = n_slabs // 2
    cols = [slice(i * 2 * nk, (i + 1) * 2 * nk) for i in range(n_pairs)]
    lookahead = 2
    s = [_dot(h, ut_ref[:, cols[i]]) for i in range(min(lookahead, n_pairs))]
    acc = None
    for i in range(n_pairs):
        if i + lookahead < n_pairs:
            s.append(_dot(h, ut_ref[:, cols[i + lookahead]]))
        a0 = j * n_slabs + 2 * i
        g = jnp.concatenate([gmat_ref[a0], gmat_ref[a0 + 1]], axis=1)
        w = g * _gelu_tanh(s[i]).astype(BF16)
        p = _dot(w, v_ref[cols[i], :])
        acc = p if acc is None else acc + p
    acc_ref[...] += acc

    @pl.when(j == pl.num_programs(2) - 1)
    def _():
        o_ref[0] = x_ref[0] + mod_ref[0][:, 5 * D_MODEL:6 * D_MODEL] * acc_ref[...]


def _peer_experts(x, mod, h, a, c, gate, u_t, v_tab):
    b, t, _ = x.shape
    tb = min(PEER_TOKEN_BLOCK, t)
    n_exp = u_t.shape[1]
    eb = PEER_EXPERT_BLOCK
    n_slabs = eb // PEER_NKEYS
    tok = lambda n: pl.BlockSpec((1, tb, n), lambda i, k, j: (i, k, 0))
    return pl.pallas_call(
        functools.partial(_peer_expert_kernel, tb, n_slabs),
        grid=(b, t // tb, n_exp // eb),
        in_specs=[tok(D_MODEL), pl.BlockSpec((1, 1, 6 * D_MODEL), lambda i, k, j: (i, 0, 0)),
                  tok(D_MODEL), tok(PEER_SEL), tok(PEER_SEL), tok(PEER_SEL),
                  pl.BlockSpec((D_MODEL, eb), lambda i, k, j: (0, j)),
                  pl.BlockSpec((eb, D_MODEL), lambda i, k, j: (j, 0))],
        out_specs=tok(D_MODEL),
        out_shape=jax.ShapeDtypeStruct(x.shape, F32),
        scratch_shapes=[pltpu.VMEM((PEER_NKEYS, tb, PEER_NKEYS), BF16), pltpu.VMEM((tb, D_MODEL), F32)],
        compiler_params=_params(("arbitrary", "arbitrary", "arbitrary")),
        name="peer_experts",
    )(x, mod, h, a, c, gate, u_t, v_tab)


def _final_norm_kernel(x_ref, g_ref, o_ref):
    x = x_ref[0]
    ms = jnp.mean(x * x, axis=-1, keepdims=True)
    o_ref[0] = x * lax.rsqrt(ms + EPS) * g_ref[...]


def _final_norm(x, gain):
    b, t, _ = x.shape
    tb = min(TOKEN_BLOCK, t)
    tok = pl.BlockSpec((1, tb, D_MODEL), lambda i, j: (i, j, 0))
    return pl.pallas_call(
        _final_norm_kernel,
        grid=(b, t // tb),
        in_specs=[tok, _full_spec((1, D_MODEL))],
        out_specs=tok,
        out_shape=jax.ShapeDtypeStruct(x.shape, F32),
        compiler_params=_params(("arbitrary", "arbitrary")),
        name="final_norm",
    )(x, gain)


def _hi_lo(w):
    hi = w.astype(BF16)
    return hi, (w - hi.astype(F32)).astype(BF16)


def _pad_cols(w, n):
    return jnp.pad(w, ((0, 0), (0, n - w.shape[1])))


def _layer_weights(l, w_in, w_out, gla_gk_up, gla_gk_bias, gla_onorm_g, dn_conv_w, dn_A_log, dn_dt_bias,
                   dn_onorm_g, s5_lambda_re, s5_lambda_im, s5_log_dt, s5_B_re, s5_B_im, s5_C_re, s5_C_im,
                   s5_D, s5_glu_w, s5_glu_b, lower_bound, hgrn_onorm_g, peer_wq, peer_keys, peer_u, peer_v):
    gw = GROUP_W
    sizes = (gw, gw, gw, gw, 2 * GLA_LOWRANK, gw, gw, gw, gw, 2 * N_HEADS, 2 * N_HEADS, gw, gw, gw, gw, 2 * gw)
    offs = np.concatenate([[0], np.cumsum(sizes)])
    col = lambda a, b: w_in[l][:, int(offs[a]):int(offs[b])]
    wl = {}
    wl["w_gla"] = _pad_cols(col(0, 5), GLA_W).astype(BF16)
    wl["w_dn"] = _pad_cols(col(5, 11), DN_W).astype(BF16)
    wl["w_s5"] = col(11, 12).astype(BF16)
    wl["w_hg"] = col(12, 16).astype(BF16)
    wl["w_out"] = w_out[l].reshape(4, gw, D_MODEL).astype(BF16)
    tile_head = lambda g: jnp.tile(g, N_HEADS).reshape(1, gw)
    up = jnp.zeros((2, LANES, gw), F32)
    for d in range(2):
        up = up.at[d, d * GLA_LOWRANK:(d + 1) * GLA_LOWRANK, :].set(gla_gk_up[l, d])
    wl["gla_up_hi"], wl["gla_up_lo"] = _hi_lo(up)
    wl["gla_bias"] = gla_gk_bias[l].reshape(2, 1, gw)
    wl["gla_onorm"] = tile_head(gla_onorm_g[l])
    expand = np.zeros((2, LANES, 2 * gw), np.float32)
    for d in range(2):
        for h in range(N_HEADS):
            expand[d, d * N_HEADS + h, h * HEAD_DIM:(h + 1) * HEAD_DIM] = 1.0
            expand[d, 2 * N_HEADS + d * N_HEADS + h, gw + h * HEAD_DIM:gw + (h + 1) * HEAD_DIM] = 1.0
    wl["dn_expand"] = jnp.asarray(expand, BF16)
    wl["dn_conv"] = jnp.pad(dn_conv_w[l], ((0, SUBLANES - 3), (0, 0)))
    wl["dn_nega"] = jnp.repeat(-jnp.exp(dn_A_log[l]), HEAD_DIM, axis=-1).reshape(2, 1, gw)
    wl["dn_dtb"] = jnp.repeat(dn_dt_bias[l], HEAD_DIM, axis=-1).reshape(2, 1, gw)
    wl["dn_onorm"] = tile_head(dn_onorm_g[l])
    nlb = S5_STATE // LANES
    lam_re, lam_im = s5_lambda_re[l].astype(F32), s5_lambda_im[l].astype(F32)
    dt = jnp.exp(s5_log_dt[l].astype(F32))[:, :, None]
    mag = jnp.exp(lam_re * dt)
    lbar_re, lbar_im = mag * jnp.cos(lam_im * dt), mag * jnp.sin(lam_im * dt)
    den = lam_re * lam_re + lam_im * lam_im
    num_re, num_im = lbar_re - 1.0, lbar_im
    coef_re = (num_re * lam_re + num_im * lam_im) / den
    coef_im = (num_im * lam_re - num_re * lam_im) / den
    b_re, b_im = s5_B_re[l].astype(F32), s5_B_im[l].astype(F32)
    bb_re = coef_re[..., None] * b_re - coef_im[..., None] * b_im
    bb_im = coef_re[..., None] * b_im + coef_im[..., None] * b_re
    eye_g = jnp.eye(S5_G, dtype=F32)
    bd_in = lambda m: jnp.einsum('dgpc,gk->dgckp', m, eye_g).reshape(2, gw, S5_STATE)
    blk = lambda m: m.reshape(2, gw, nlb, LANES).transpose(0, 2, 1, 3)
    b_cat = jnp.concatenate([blk(bd_in(bb_re)), blk(bd_in(bb_im))], axis=-1)
    wl["s5_b"] = b_cat.astype(BF16)
    bd_out = lambda m: jnp.einsum('gcp,gk->gpkc', m.astype(F32), eye_g).reshape(S5_STATE, gw)
    c_re, c_im = bd_out(s5_C_re[l]), bd_out(s5_C_im[l])
    wl["s5_c"] = jnp.concatenate([c_re.reshape(nlb, LANES, gw), -c_im.reshape(nlb, LANES, gw)],
                                 axis=1).astype(BF16)
    wl["s5_lbar"] = jnp.stack([lbar_re.reshape(2, nlb, LANES), lbar_im.reshape(2, nlb, LANES)], axis=2)
    wl["s5_d"] = s5_D[l].reshape(1, gw)
    wl["s5_glu_w"] = s5_glu_w[l].astype(BF16)
    wl["s5_glu_b"] = s5_glu_b[l].reshape(1, gw)
    wl["hg_loglb"] = jnp.log(lower_bound).reshape(1, gw)
    wl["hg_log1m"] = jnp.log1p(-lower_bound).reshape(1, gw)
    wl["hg_1m"] = (1.0 - lower_bound).reshape(1, gw)
    wl["hg_onorm"] = tile_head(hgrn_onorm_g[l])
    wl["wq_hi"], wl["wq_lo"] = _hi_lo(peer_wq[l].T)
    wl["keys_hi"], wl["keys_lo"] = _hi_lo(peer_keys[l].reshape(PEER_HEADS * 2, PEER_NKEYS, PEER_DQ // 2))
    wl["u_t"] = peer_u[l].astype(BF16).T
    wl["v_tab"] = peer_v[l].astype(BF16)
    return wl


def _heads_to_lanes(s, transposed):
    perm = (0, 1, 4, 2, 3) if transposed else (0, 1, 3, 2, 4)
    return s.astype(F32).transpose(perm).reshape(s.shape[0], 2, HEAD_DIM, GROUP_W)


def _lanes_to_heads(s, transposed):
    s = s.reshape(s.shape[0], 2, HEAD_DIM, N_HEADS, HEAD_DIM)
    return s.transpose((0, 1, 3, 4, 2) if transposed else (0, 1, 3, 2, 4))


def _to_column_major(x):
    b, n, d = x.shape
    return x.reshape(b, n // GRID_W, GRID_W, d).transpose(0, 2, 1, 3).reshape(b, n, d)


def _to_row_major(x):
    b, n, d = x.shape
    return x.reshape(b, GRID_W, n // GRID_W, d).transpose(0, 2, 1, 3).reshape(b, n, d)


def _trunk_layer(x, mod, wl, norm1_g, norm2_g, states, shared_mod):
    st_gla, st_dn, st_s5, st_hg = states
    p_gla, p_dn, p_s5, p_hg = _in_projection(x, mod, norm1_g, [wl["w_gla"], wl["w_dn"], wl["w_s5"], wl["w_hg"]])
    y_a, n_gla = _decay_mixer("gla", p_gla, wl["gla_up_hi"], wl["gla_up_lo"], wl["gla_bias"], wl["gla_onorm"], st_gla)
    y_b, n_dn = _delta_mixer(p_dn, wl["dn_conv"], wl["dn_expand"], wl["dn_nega"], wl["dn_dtb"], wl["dn_onorm"], st_dn)
    y_c, n_s5 = _s5_mixer(p_s5, wl["s5_b"], wl["s5_c"], wl["s5_lbar"], wl["s5_d"],
                          wl["s5_glu_w"], wl["s5_glu_b"], st_s5)
    y_d, n_hg = _decay_mixer("hgrn", p_hg, wl["hg_loglb"], wl["hg_log1m"], wl["hg_1m"], wl["hg_onorm"], st_hg)
    x = _out_projection(x, mod, [y_a, y_b, y_c, y_d], wl["w_out"])
    h, a, c, gate = _peer_route(x, mod, norm2_g, wl["wq_hi"], wl["wq_lo"], wl["keys_hi"], wl["keys_lo"])
    b, t, _ = x.shape
    fold = max(1, min(PEER_TOKEN_BLOCK // t, b)) if shared_mod else 1
    fb = lambda z: z.reshape(b // fold, fold * t, z.shape[-1])
    x = _peer_experts(fb(x), mod[0:b // fold], fb(h), fb(a), fb(c), fb(gate), wl["u_t"], wl["v_tab"]).reshape(b, t, -1)
    return x, (n_gla, n_dn, n_s5, n_hg)


def kernel(x_prompt, x_sample, state_gla, state_delta, state_s5_re, state_s5_im, state_hgrn, c, c_ctx,
           mod_w, mod_b, norm1_g, norm2_g, w_in, w_out, gla_gk_up, gla_gk_bias, gla_onorm_g,
           dn_conv_w, dn_A_log, dn_dt_bias, dn_onorm_g, s5_lambda_re, s5_lambda_im, s5_log_dt,
           s5_B_re, s5_B_im, s5_C_re, s5_C_im, s5_D, s5_glu_w, s5_glu_b, hgrn_lb_logits,
           hgrn_onorm_g, peer_wq, peer_keys, peer_u, peer_v, final_norm_g):
    bp, bs = x_prompt.shape[0], x_sample.shape[0]
    nlb = S5_STATE // LANES
    lb_cum = jnp.cumsum(jax.nn.softmax(hgrn_lb_logits.astype(F32), axis=0), axis=0)
    lower_bounds = lb_cum - lb_cum[0:1]

    rows = 2 * SUBLANES
    cc = jnp.concatenate([c, c_ctx[None, :], jnp.zeros((rows - bs - 1, D_MODEL), F32)], axis=0)
    mod_all = _modulation(cc, mod_w, mod_b)

    zeros_hh = jnp.zeros((bp, 2, HEAD_DIM, GROUP_W), F32)
    zeros_s5 = jnp.zeros((bp, 2, 2, nlb, 1, LANES), F32)

    xp, xs = x_prompt, x_sample
    ctx_states = []
    for l in range(DEPTH):
        wl = _layer_weights(l, w_in, w_out, gla_gk_up, gla_gk_bias, gla_onorm_g, dn_conv_w, dn_A_log,
                            dn_dt_bias, dn_onorm_g, s5_lambda_re, s5_lambda_im, s5_log_dt, s5_B_re, s5_B_im,
                            s5_C_re, s5_C_im, s5_D, s5_glu_w, s5_glu_b, lower_bounds[l], hgrn_onorm_g,
                            peer_wq, peer_keys, peer_u, peer_v)
        n1, n2 = norm1_g[l].reshape(1, D_MODEL), norm2_g[l].reshape(1, D_MODEL)
        mod_ctx = jnp.broadcast_to(mod_all[l, bs].reshape(1, 1, 6 * D_MODEL), (bp, 1, 6 * D_MODEL))
        mod_lat = mod_all[l, 0:bs].reshape(bs, 1, 6 * D_MODEL)
        xp, st_ctx = _trunk_layer(xp, mod_ctx, wl, n1, n2, (zeros_hh, zeros_hh, zeros_s5, zeros_hh), True)
        ctx_states.append(st_ctx)
        cached_s5 = jnp.stack([state_s5_re[:, l], state_s5_im[:, l]], axis=2).reshape(bs, 2, 2, nlb, 1, LANES)
        cached = (_heads_to_lanes(state_gla[:, l], True), _heads_to_lanes(state_delta[:, l], False), cached_s5,
                  _heads_to_lanes(state_hgrn[:, l], True))
        if l % 2 == 1:
            xs = _to_column_major(xs)
        xs, _ = _trunk_layer(xs, mod_lat, wl, n1, n2, cached, False)
        if l % 2 == 1:
            xs = _to_row_major(xs)

    gain = final_norm_g.reshape(1, D_MODEL)
    y_prompt = _final_norm(xp, gain)
    y_sample = _final_norm(xs, gain)
    new_gla = jnp.stack([_lanes_to_heads(s[0], True) for s in ctx_states], axis=1)
    new_dn = jnp.stack([_lanes_to_heads(s[1], False) for s in ctx_states], axis=1)
    s5 = jnp.stack([s[2].reshape(bp, 2, 2, S5_G, S5_P) for s in ctx_states], axis=1)
    new_hg = jnp.stack([_lanes_to_heads(s[3], True) for s in ctx_states], axis=1)
    return (y_prompt, y_sample, new_gla, new_dn, s5[:, :, :, 0], s5[:, :, :, 1], new_hg)
```

```python
import functools

import numpy as np
import jax
import jax.numpy as jnp
from jax import lax
from jax.experimental import pallas as pl
from jax.experimental.pallas import tpu as pltpu

F32 = jnp.float32
BF16 = jnp.bfloat16

D_MODEL = 1024
DEPTH = 2
GRID_W = 64
GROUP_W = 256
HEAD_DIM = 64
N_HEADS = GROUP_W // HEAD_DIM
GLA_LOWRANK = 16
GLA_GATE_NORM = 16.0
DN_CHUNK = 64
S5_GROUP = 16
S5_G = GROUP_W // S5_GROUP
S5_P = 64
S5_STATE = S5_G * S5_P
PEER_HEADS = 8
PEER_NKEYS = 128
PEER_DQ = 256
PEER_TOPK = 16
PEER_SEL = PEER_HEADS * PEER_TOPK
EPS = 1e-6

LANES = 128
SUBLANES = 8
VMEM_LIMIT = 56 * 1024 * 1024

GLA_CHUNK = 64
S5_CHUNK = 256
TOKEN_BLOCK = 256
PEER_TOKEN_BLOCK = 512
PEER_EXPERT_BLOCK = 2048
GLA_W = 4 * GROUP_W + LANES
DN_W = 4 * GROUP_W + LANES
HG_W = 5 * GROUP_W


def _dot(a, b):
    return jnp.dot(a, b, preferred_element_type=F32)


def _dot_nt(a, b):
    return lax.dot_general(a, b, (((1,), (1,)), ((), ())), preferred_element_type=F32)


def _dot_tn(a, b):
    return lax.dot_general(a, b, (((0,), (0,)), ((), ())), preferred_element_type=F32)


def _split2(x):
    hi = x.astype(BF16)
    lo = (x - hi.astype(F32)).astype(BF16)
    return hi, lo


def _split3(x):
    hi = x.astype(BF16)
    r = x - hi.astype(F32)
    mid = r.astype(BF16)
    lo = (r - mid.astype(F32)).astype(BF16)
    return hi, mid, lo


def _sel_dot(m01x3, x):
    return _dot(m01x3, jnp.concatenate(_split3(x), axis=0))


def _dot_sel(x, m01):
    hi, mid, lo = _split3(x)
    return _dot(hi, m01) + _dot(mid, m01) + _dot(lo, m01)


def _dot_hl(a, b_hi, b_lo):
    a_hi, a_lo = _split2(a)
    return _dot(a_hi, b_hi) + _dot(a_hi, b_lo) + _dot(a_lo, b_hi)


def _head_sum(x, ones_bd):
    hi, lo = _split2(x)
    return _dot(hi, ones_bd) + _dot(lo, ones_bd)


HEADS_PER_TILE = LANES // HEAD_DIM
N_HEAD_TILES = GROUP_W // LANES


def _head_blockdiag(x16, ones_bd):
    mask = ones_bd[0:LANES, 0:LANES]
    return [jnp.tile(x16[:, t * LANES:(t + 1) * LANES], (HEADS_PER_TILE, 1)) * mask for t in range(N_HEAD_TILES)]


def _bd_dot(a16, bds):
    return jnp.concatenate([_dot(a16[:, t * LANES:(t + 1) * LANES], bd) for t, bd in enumerate(bds)], axis=1)


def _bd_dot_nt(a16, bds):
    return jnp.concatenate([_dot_nt(a16[:, t * LANES:(t + 1) * LANES], bd) for t, bd in enumerate(bds)], axis=1)


def _per_head_mm(a, x, ones_bd):
    a_hi, a_lo = _split2(a)
    x_hi, x_lo = _split2(x)
    bd_hi = _head_blockdiag(x_hi, ones_bd)
    return _bd_dot(a_hi, bd_hi) + _bd_dot(a_hi, _head_blockdiag(x_lo, ones_bd)) + _bd_dot(a_lo, bd_hi)


def _per_head_mm1(a, x, ones_bd):
    return _bd_dot(a.astype(BF16), _head_blockdiag(x.astype(BF16), ones_bd))


def _per_head_dot_tn(a16, b16):
    lane_head = lax.broadcasted_iota(jnp.int32, (HEAD_DIM, LANES), 1) // HEAD_DIM
    outs = []
    for t in range(N_HEAD_TILES):
        full = _dot_tn(a16[:, t * LANES:(t + 1) * LANES], b16[:, t * LANES:(t + 1) * LANES])
        out = None
        for h in range(HEADS_PER_TILE):
            blk = jnp.where(lane_head == h, full[h * HEAD_DIM:(h + 1) * HEAD_DIM, :], 0.0)
            out = blk if out is None else out + blk
        outs.append(out)
    return jnp.concatenate(outs, axis=1)


def _log_sigmoid(x):
    return jnp.minimum(x, 0.0) - jnp.log(1.0 + jnp.exp(-jnp.abs(x)))


def _softplus(x):
    return jnp.maximum(x, 0.0) + jnp.log(1.0 + jnp.exp(-jnp.abs(x)))


def _sigmoid(x):
    return 1.0 / (1.0 + jnp.exp(-x))


def _silu(x):
    return x * _sigmoid(x)


def _gelu_tanh(x):
    return 0.5 * x * (1.0 + jnp.tanh(0.7978845608028654 * (x + 0.044715 * x * x * x)))


def _rms_modulate(x, gain, shift, scale):
    ms = jnp.mean(x * x, axis=-1, keepdims=True)
    return x * lax.rsqrt(ms + EPS) * gain * (1.0 + scale) + shift


def _full_spec(shape):
    zeros = (0,) * len(shape)
    return pl.BlockSpec(shape, lambda *_: zeros)


def _params(sem):
    return pltpu.CompilerParams(dimension_semantics=sem, vmem_limit_bytes=VMEM_LIMIT)


def _scan_consts(c, reverse):
    levels = int(np.log2(c))
    idx = np.arange(c)
    i, t = idx[:, None], idx[None, :]
    sel, masks = [], []
    for lv in range(levels):
        s = 1 << lv
        right = (idx // s) % 2 == 1
        same_blk = (i // s) == (t // s)
        sel.append(np.where(right[:, None], same_blk & (t <= i), same_blk & (t > i)))
        same_parent = (i // (2 * s)) == (t // (2 * s))
        masks.append(same_parent & right[:, None] & ~right[None, :])
    sel.append(t <= i)
    sel.append(t > i)
    masks.append(i == t)
    if reverse:
        sel = [m[::-1, ::-1] for m in sel]
        masks = [m[::-1, ::-1] for m in masks]
    sel.append(np.ones((SUBLANES, c), bool))
    sel = np.tile(np.concatenate(sel, axis=0).astype(np.float32), (1, 3))
    masks = np.tile(np.stack(masks).astype(np.float32), (1, 1, N_HEADS))
    return jnp.asarray(sel, BF16), jnp.asarray(masks, F32)


def _dn_consts(c, reverse):
    levels = int(np.log2(c))
    idx = np.arange(c)
    i, t = idx[:, None], idx[None, :]
    incl, tail = t <= i, t > i
    masks = []
    for lv in range(levels):
        s = 1 << lv
        right = (idx // s) % 2 == 1
        masks.append(((i // (2 * s)) == (t // (2 * s))) & right[:, None] & ~right[None, :])
    masks += [incl, incl & (i != t), i == t]
    if reverse:
        incl, tail = incl[::-1, ::-1], tail[::-1, ::-1]
        masks = [m[::-1, ::-1] for m in masks]
    sel = np.tile(np.concatenate([incl, tail, np.ones((SUBLANES, c), bool)], axis=0).astype(np.float32), (1, 3))
    masks = np.tile(np.stack(masks).astype(np.float32), (1, 1, N_HEADS))
    return jnp.asarray(sel, BF16), jnp.asarray(masks, F32)


def _ones_blockdiag():
    h = np.arange(GROUP_W) // HEAD_DIM
    return jnp.asarray((h[:, None] == h[None, :]).astype(np.float32), BF16)


def _mod_kernel(c_ref, w_ref, b_ref, o_ref):
    a_hi, a_lo = _split2(_silu(c_ref[...]))
    w_hi, w_lo = _split2(w_ref[0])
    o_ref[0] = _dot(a_hi, w_hi) + _dot(a_hi, w_lo) + _dot(a_lo, w_hi) + b_ref[0]


def _modulation(cc, mod_w, mod_b):
    rows = cc.shape[0]
    nb = 6
    return pl.pallas_call(
        _mod_kernel,
        grid=(DEPTH, nb),
        in_specs=[pl.BlockSpec((rows, D_MODEL), lambda l, j: (0, 0)),
                  pl.BlockSpec((1, D_MODEL, D_MODEL), lambda l, j: (l, 0, j)),
                  pl.BlockSpec((1, 1, D_MODEL), lambda l, j: (l, 0, j))],
        out_specs=pl.BlockSpec((1, rows, D_MODEL), lambda l, j: (l, 0, j)),
        out_shape=jax.ShapeDtypeStruct((DEPTH, rows, 6 * D_MODEL), F32),
        compiler_params=_params(("arbitrary", "arbitrary")),
        name="modulation",
    )(cc, mod_w, mod_b.reshape(DEPTH, 1, 6 * D_MODEL))


def _inproj_kernel(x_ref, mod_ref, g_ref, wa_ref, wb_ref, wc_ref, wd_ref, oa_ref, ob_ref, oc_ref, od_ref):
    mod = mod_ref[0]
    h = _rms_modulate(x_ref[0], g_ref[...], mod[:, 0:D_MODEL], mod[:, D_MODEL:2 * D_MODEL]).astype(BF16)
    oa_ref[0] = _dot(h, wa_ref[...])
    ob_ref[0] = _dot(h, wb_ref[...])
    oc_ref[0] = _dot(h, wc_ref[...])
    od_ref[0] = _dot(h, wd_ref[...])


def _in_projection(x, mod, gain, weights):
    b, t, _ = x.shape
    tb = min(TOKEN_BLOCK, t)
    widths = [w.shape[1] for w in weights]
    return pl.pallas_call(
        _inproj_kernel,
        grid=(b, t // tb),
        in_specs=[pl.BlockSpec((1, tb, D_MODEL), lambda i, j: (i, j, 0)),
                  pl.BlockSpec((1, 1, 6 * D_MODEL), lambda i, j: (i, 0, 0)),
                  _full_spec((1, D_MODEL))] + [_full_spec(w.shape) for w in weights],
        out_specs=[pl.BlockSpec((1, tb, n), lambda i, j: (i, j, 0)) for n in widths],
        out_shape=[jax.ShapeDtypeStruct((b, t, n), F32) for n in widths],
        compiler_params=_params(("arbitrary", "arbitrary")),
        name="in_projection",
    )(x, mod, gain, *weights)


def _decay_scan_prepare(probs, ones_bd, c):
    levels = int(np.log2(c))
    exs = [jnp.exp(_sel_dot(pr["sel"][...], pr["g"])) for pr in probs]
    attns = [None] * len(probs)
    for lv in range(levels + 1):
        for i, (pr, ex) in enumerate(zip(probs, exs)):
            q, k = pr["q"], pr["k"]
            if lv < levels:
                e = ex[lv * c:(lv + 1) * c]
                qt, kt = (q * e).astype(BF16), (k * e).astype(BF16)
            else:
                qt, kt = q.astype(BF16), k.astype(BF16)
            p = _bd_dot_nt(qt, _head_blockdiag(kt, ones_bd)) * pr["mask"][lv]
            attns[i] = p if attns[i] is None else attns[i] + p
    res = []
    for pr, ex, attn in zip(probs, exs, attns):
        o = _bd_dot(attn.astype(BF16), _head_blockdiag(pr["v"].astype(BF16), ones_bd))
        qd = (pr["q"] * ex[levels * c:(levels + 1) * c]).astype(BF16)
        ktail = (pr["k"] * ex[(levels + 1) * c:(levels + 2) * c]).astype(BF16)
        res.append((o, qd, ktail, ex[(levels + 2) * c:(levels + 2) * c + SUBLANES]))
    return res


def _decay_scan_state_steps(items, ones_bd, state_ref):
    sts = [state_ref[it[0]] for it in items]
    outs = [it[1] + _bd_dot_nt(it[2], _head_blockdiag(st.astype(BF16), ones_bd)) for it, st in zip(items, sts)]
    for it, st in zip(items, sts):
        state_ref[it[0]] = st * it[5] + _per_head_dot_tn(it[4], it[3])
    return outs


def _head_norm_gate(o, gate, onorm, ones_bd):
    ms = _head_sum(o * o, ones_bd) * (1.0 / HEAD_DIM)
    return o * lax.rsqrt(ms + EPS) * onorm * _silu(gate)


def _decay_mixer_kernel(mode, t, c, p_ref, self_ref, selb_ref, maskf_ref, maskb_ref, ones_ref, onorm_ref,
                        pa_ref, pb_ref, pc_ref, h0_ref, y_ref, sout_ref,
                        qd_ref, kt_ref, v16_ref, dl_ref, of_ref, ob_ref, state_ref):
    n_chunks = t // c
    state_ref[...] = h0_ref[0]
    sels, masks, o_refs = (self_ref, selb_ref), (maskf_ref, maskb_ref), (of_ref, ob_ref)

    def inputs(rows, d):
        if mode == "gla":
            q = p_ref[0, rows, 0:GROUP_W] * (HEAD_DIM ** -0.5)
            k = p_ref[0, rows, GROUP_W:2 * GROUP_W]
            v = p_ref[0, rows, 2 * GROUP_W:3 * GROUP_W]
            low = p_ref[0, rows, 4 * GROUP_W:4 * GROUP_W + LANES]
            gk = _dot_hl(low, pa_ref[d], pb_ref[d]) + pc_ref[d]
            g = _log_sigmoid(gk) * (1.0 / GLA_GATE_NORM)
        else:
            q = _silu(p_ref[0, rows, 0:GROUP_W]) * (HEAD_DIM ** -0.5)
            v = p_ref[0, rows, GROUP_W:2 * GROUP_W]
            z = p_ref[0, rows, (3 + d) * GROUP_W:(4 + d) * GROUP_W]
            a = pa_ref[...]
            b = pb_ref[...] + _log_sigmoid(z)
            m = jnp.maximum(a, b)
            g = m + jnp.log(jnp.exp(a - m) + jnp.exp(b - m))
            k = pc_ref[...] * _sigmoid(-z)
        return q, k, v, g

    per_iter = 4 if n_chunks % 4 == 0 else 1

    def prepare(i, carry):
        chunks = [i * per_iter + j for j in range(per_iter)]
        probs = []
        for n in chunks:
            rows = pl.ds(pl.multiple_of(n * c, c), c)
            for d in range(2):
                q, k, v, g = inputs(rows, d)
                probs.append(dict(q=q, k=k, v=v, g=g, sel=sels[d], mask=masks[d]))
            v16_ref[rows, :] = probs[-1]["v"].astype(BF16)
        for idx, (o, qd, ktail, dlast) in enumerate(_decay_scan_prepare(probs, ones_ref[...], c)):
            n, d = chunks[idx // 2], idx % 2
            rows = pl.ds(pl.multiple_of(n * c, c), c)
            o_refs[d][rows, :] = o
            qd_ref[d, rows, :] = qd
            kt_ref[d, rows, :] = ktail
            dl_ref[d, pl.ds(pl.multiple_of(n * SUBLANES, SUBLANES), SUBLANES), :] = dlast
        return carry

    lax.fori_loop(0, n_chunks // per_iter, prepare, 0)

    def body(i, carry):
        items, rows_of = [], []
        for d, n in ((0, i), (1, n_chunks - 1 - i)):
            rows = pl.ds(pl.multiple_of(n * c, c), c)
            dlast = dl_ref[d, pl.ds(pl.multiple_of(n * SUBLANES, SUBLANES), SUBLANES), :][0:1, :]
            items.append((d, o_refs[d][rows, :], qd_ref[d, rows, :], kt_ref[d, rows, :], v16_ref[rows, :], dlast))
            rows_of.append(rows)
        o_f, o_b = _decay_scan_state_steps(items, ones_ref[...], state_ref)
        of_ref[rows_of[0], :] = o_f
        ob_ref[rows_of[1], :] = o_b
        return carry

    lax.fori_loop(0, n_chunks, body, 0)

    rbk = min(t, 256)
    gate_col = 3 * GROUP_W if mode == "gla" else 2 * GROUP_W

    def fin(i, carry):
        rows = pl.ds(pl.multiple_of(i * rbk, rbk), rbk)
        o = of_ref[rows, :] + ob_ref[rows, :]
        gate = p_ref[0, rows, gate_col:gate_col + GROUP_W]
        y_ref[0, rows, :] = _head_norm_gate(o, gate, onorm_ref[...], ones_ref[...]).astype(BF16)
        return carry

    lax.fori_loop(0, t // rbk, fin, 0)
    sout_ref[0] = state_ref[...]


def _decay_mixer(mode, p, pa, pb, pc, onorm, h0t):
    b, t, w = p.shape
    c = GLA_CHUNK
    sel_f, mask_f = _scan_consts(c, False)
    sel_b, mask_b = _scan_consts(c, True)
    ones_bd = _ones_blockdiag()
    consts = [sel_f, sel_b, mask_f, mask_b, ones_bd, onorm, pa, pb, pc]
    state_shape = (2, HEAD_DIM, GROUP_W)
    return pl.pallas_call(
        functools.partial(_decay_mixer_kernel, mode, t, c),
        grid=(b,),
        in_specs=[pl.BlockSpec((1, t, w), lambda i: (i, 0, 0))]
                 + [_full_spec(a.shape) for a in consts]
                 + [pl.BlockSpec((1,) + state_shape, lambda i: (i, 0, 0, 0))],
        out_specs=[pl.BlockSpec((1, t, GROUP_W), lambda i: (i, 0, 0)),
                   pl.BlockSpec((1,) + state_shape, lambda i: (i, 0, 0, 0))],
        out_shape=[jax.ShapeDtypeStruct((b, t, GROUP_W), BF16),
                   jax.ShapeDtypeStruct((b,) + state_shape, F32)],
        scratch_shapes=[pltpu.VMEM((2, t, GROUP_W), BF16), pltpu.VMEM((2, t, GROUP_W), BF16),
                        pltpu.VMEM((t, GROUP_W), BF16), pltpu.VMEM((2, (t // c) * SUBLANES, GROUP_W), F32),
                        pltpu.VMEM((t, GROUP_W), F32), pltpu.VMEM((t, GROUP_W), F32),
                        pltpu.VMEM(state_shape, F32)],
        compiler_params=_params(("arbitrary",)),
        name=mode + "_mixer",
    )(p, *consts, h0t)


def _delta_prepare(probs, ones_bd, expand_ref, nega_ref, dtb_ref, c):
    levels = int(np.log2(c))
    work = []
    for pr in probs:
        d, mask = pr["d"], pr["mask"]
        logits = _dot_sel(pr["ba"], expand_ref[d])
        beta = _sigmoid(logits[:, 0:GROUP_W])
        g = nega_ref[d] * _softplus(logits[:, GROUP_W:2 * GROUP_W] + dtb_ref[d])
        cs = _sel_dot(pr["sel"][...], g)
        b_incl = cs[0:c]
        b_t = jnp.concatenate([b_incl[:, h * HEAD_DIM:(h + 1) * HEAD_DIM].T for h in range(N_HEADS)], axis=1)
        lmat = jnp.exp(jnp.minimum(b_incl - b_t, 0.0)) * mask[levels]
        m = beta * pr["kk"] * lmat * mask[levels + 1]
        work.append(dict(beta=beta, eb=jnp.exp(b_incl), etail=jnp.exp(cs[c:2 * c]),
                         dlast=jnp.exp(cs[2 * c:2 * c + SUBLANES]), lmat=lmat, m=m,
                         t=mask[levels + 2] - m * mask[0]))
    for lv in range(1, levels):
        mm = _per_head_mm1 if lv == levels - 1 else _per_head_mm
        ps = [mm(wk["m"] * pr["mask"][lv], wk["t"], ones_bd) for pr, wk in zip(probs, work)]
        for wk, p in zip(work, ps):
            wk["t"] = wk["t"] - mm(wk["t"], p, ones_bd)
    us = [_per_head_mm1(wk["t"], pr["v"] * wk["beta"], ones_bd) for pr, wk in zip(probs, work)]
    ws = [_per_head_mm1(wk["t"], pr["k"] * wk["beta"] * wk["eb"], ones_bd) for pr, wk in zip(probs, work)]
    return [(u, w.astype(BF16), (pr["qk_raw"] * wk["lmat"]).astype(BF16), (pr["q"] * wk["eb"]).astype(BF16),
             (pr["k"] * wk["etail"]).astype(BF16), wk["dlast"]) for pr, wk, u, w in zip(probs, work, us, ws)]


def _delta_state_steps(items, ones_bd, state_ref):
    sts = [state_ref[it[0]] for it in items]
    bds = [_head_blockdiag(st.astype(BF16), ones_bd) for st in sts]
    v16s = [(it[1] - _bd_dot(it[2], bd)).astype(BF16) for it, bd in zip(items, bds)]
    outs = [_bd_dot(it[4], bd) + _bd_dot(it[3], _head_blockdiag(v16, ones_bd))
            for it, bd, v16 in zip(items, bds, v16s)]
    for it, st, v16 in zip(items, sts, v16s):
        state_ref[it[0]] = st * it[6] + _per_head_dot_tn(it[5], v16)
    return outs


def _delta_mixer_kernel(t, c, p_ref, self_ref, selb_ref, maskf_ref, maskb_ref, ones_ref, onorm_ref,
                        conv_ref, expand_ref, nega_ref, dtb_ref, h0_ref, y_ref, sout_ref,
                        u_ref, w_ref, qk_ref, qd_ref, kt_ref, dl_ref, of_ref, ob_ref, state_ref):
    n_chunks = t // c
    qkv_w = 3 * GROUP_W
    state_ref[...] = h0_ref[0]
    row_id = lax.broadcasted_iota(jnp.int32, (c, qkv_w), 0)
    w_prev, w_mid, w_next = conv_ref[0:1, :], conv_ref[1:2, :], conv_ref[2:3, :]
    sels, masks = (self_ref, selb_ref), (maskf_ref, maskb_ref)

    def conv_qkv(n):
        r0 = pl.multiple_of(n * c, c)
        x = p_ref[0, pl.ds(r0, c), 0:qkv_w]
        before = p_ref[0, pl.ds(pl.multiple_of(jnp.maximum(r0 - SUBLANES, 0), SUBLANES), SUBLANES), 0:qkv_w]
        after = p_ref[0, pl.ds(pl.multiple_of(jnp.minimum(r0 + c, t - SUBLANES), SUBLANES), SUBLANES), 0:qkv_w]
        before_row = jnp.where(n > 0, before[SUBLANES - 1:SUBLANES, :], 0.0)
        after_row = jnp.where(n < n_chunks - 1, after[0:1, :], 0.0)
        x_prev = jnp.where(row_id == 0, before_row, pltpu.roll(x, 1, axis=0))
        x_next = jnp.where(row_id == c - 1, after_row, pltpu.roll(x, c - 1, axis=0))
        y = _silu(x_prev * w_prev + x * w_mid + x_next * w_next)
        qk = y[:, 0:2 * GROUP_W]
        ones2 = ones_ref[...]
        ss = jnp.concatenate([_head_sum(qk[:, 0:GROUP_W] * qk[:, 0:GROUP_W], ones2),
                              _head_sum(qk[:, GROUP_W:] * qk[:, GROUP_W:], ones2)], axis=1)
        qk = qk * lax.rsqrt(ss + EPS)
        q, k, v = qk[:, 0:GROUP_W] * (HEAD_DIM ** -0.5), qk[:, GROUP_W:], y[:, 2 * GROUP_W:]
        k_hi, k_lo = _split2(k)
        bd_k = _head_blockdiag(k_hi, ones2)
        kk = _bd_dot_nt(k_hi, bd_k) + _bd_dot_nt(k_hi, _head_blockdiag(k_lo, ones2)) + _bd_dot_nt(k_lo, bd_k)
        qk_raw = _bd_dot_nt(q.astype(BF16), bd_k)
        ba = p_ref[0, pl.ds(r0, c), 4 * GROUP_W:4 * GROUP_W + LANES]
        return dict(q=q, k=k, v=v, kk=kk, qk_raw=qk_raw, ba=ba)

    per_iter = 4 if n_chunks % 4 == 0 else 1

    def prepare(i, carry):
        chunks = [i * per_iter + j for j in range(per_iter)]
        probs = []
        for n in chunks:
            shared = conv_qkv(n)
            probs += [dict(shared, sel=sels[d], mask=masks[d], d=d) for d in range(2)]
        res = _delta_prepare(probs, ones_ref[...], expand_ref, nega_ref, dtb_ref, c)
        for idx, (u, w16, qk16, qd, ktail, dlast) in enumerate(res):
            n, d = chunks[idx // 2], idx % 2
            rows = pl.ds(pl.multiple_of(n * c, c), c)
            u_ref[d, rows, :] = u
            w_ref[d, rows, :] = w16
            qk_ref[d, rows, :] = qk16
            qd_ref[d, rows, :] = qd
            kt_ref[d, rows, :] = ktail
            dl_ref[d, pl.ds(pl.multiple_of(n * SUBLANES, SUBLANES), SUBLANES), :] = dlast
        return carry

    lax.fori_loop(0, n_chunks // per_iter, prepare, 0)

    def body(i, carry):
        items, rows_of = [], []
        for d, n in ((0, i), (1, n_chunks - 1 - i)):
            rows = pl.ds(pl.multiple_of(n * c, c), c)
            dlast = dl_ref[d, pl.ds(pl.multiple_of(n * SUBLANES, SUBLANES), SUBLANES), :][0:1, :]
            items.append((d, u_ref[d, rows, :], w_ref[d, rows, :], qk_ref[d, rows, :], qd_ref[d, rows, :],
                          kt_ref[d, rows, :], dlast))
            rows_of.append(rows)
        o_f, o_b = _delta_state_steps(items, ones_ref[...], state_ref)
        of_ref[rows_of[0], :] = o_f
        ob_ref[rows_of[1], :] = o_b
        return carry

    lax.fori_loop(0, n_chunks, body, 0)

    rbk = min(t, 256)

    def fin(i, carry):
        rows = pl.ds(pl.multiple_of(i * rbk, rbk), rbk)
        o = of_ref[rows, :] + ob_ref[rows, :]
        gate = p_ref[0, rows, 3 * GROUP_W:4 * GROUP_W]
        y_ref[0, rows, :] = _head_norm_gate(o, gate, onorm_ref[...], ones_ref[...]).astype(BF16)
        return carry

    lax.fori_loop(0, t // rbk, fin, 0)
    sout_ref[0] = state_ref[...]


def _delta_mixer(p, conv_w, expand, nega, dtb, onorm, s0):
    b, t, w = p.shape
    c = DN_CHUNK
    sel_f, mask_f = _dn_consts(c, False)
    sel_b, mask_b = _dn_consts(c, True)
    consts = [sel_f, sel_b, mask_f, mask_b, _ones_blockdiag(), onorm, conv_w, expand, nega, dtb]
    state_shape = (2, HEAD_DIM, GROUP_W)
    return pl.pallas_call(
        functools.partial(_delta_mixer_kernel, t, c),
        grid=(b,),
        in_specs=[pl.BlockSpec((1, t, w), lambda i: (i, 0, 0))]
                 + [_full_spec(a.shape) for a in consts]
                 + [pl.BlockSpec((1,) + state_shape, lambda i: (i, 0, 0, 0))],
        out_specs=[pl.BlockSpec((1, t, GROUP_W), lambda i: (i, 0, 0)),
                   pl.BlockSpec((1,) + state_shape, lambda i: (i, 0, 0, 0))],
        out_shape=[jax.ShapeDtypeStruct((b, t, GROUP_W), BF16),
                   jax.ShapeDtypeStruct((b,) + state_shape, F32)],
        scratch_shapes=[pltpu.VMEM((2, t, GROUP_W), F32)]
                       + [pltpu.VMEM((2, t, GROUP_W), BF16)] * 4
                       + [pltpu.VMEM((2, (t // c) * SUBLANES, GROUP_W), F32),
                          pltpu.VMEM((t, GROUP_W), F32), pltpu.VMEM((t, GROUP_W), F32),
                          pltpu.VMEM(state_shape, F32)],
        compiler_params=_params(("arbitrary",)),
        name="delta_mixer",
    )(p, *consts, s0)


def _s5_kernel(t, tc, u_ref, b_ref, cm_ref, lbar_ref, dskip_ref, gluw_ref, glub_ref, h0_ref,
               y_ref, sout_ref, x_ref, yf_ref, yb_ref, carry_ref):
    n_chunks = t // tc
    n_lane_blocks = S5_STATE // LANES
    carry_ref[...] = h0_ref[0]
    sub8 = lax.broadcasted_iota(jnp.int32, (SUBLANES, LANES), 0)

    def scan_dirs(rows_of, y_refs):
        for d in range(2):
            u16 = u_ref[0, rows_of[d], :].astype(BF16)
            for j in range(n_lane_blocks):
                x_ref[d, j] = _dot(u16, b_ref[d, j])
            y_refs[d][rows_of[d], :] = jnp.zeros((tc, GROUP_W), F32)

        def scan_block(d, j):
            bu = x_ref[d, j]
            xr, xi = bu[:, 0:LANES], bu[:, LANES:2 * LANES]
            lb = lbar_ref[d, j]
            pw = [(lb[0:1, :], lb[1:2, :])]
            for _ in range(SUBLANES - 1):
                pr, pi = pw[-1]
                pw.append((pr * pw[0][0] - pi * pw[0][1], pr * pw[0][1] + pi * pw[0][0]))
            n_groups = tc // SUBLANES
            blocks_r = [xr[v * SUBLANES:(v + 1) * SUBLANES] for v in range(n_groups)]
            blocks_i = [xi[v * SUBLANES:(v + 1) * SUBLANES] for v in range(n_groups)]
            for s in range(int(np.log2(SUBLANES))):
                sh = 1 << s
                keep = sub8 < SUBLANES - sh if d else sub8 >= sh
                m_re, m_im = jnp.where(keep, pw[sh - 1][0], 0.0), jnp.where(keep, pw[sh - 1][1], 0.0)
                rot = SUBLANES - sh if d else sh
                for v in range(n_groups):
                    sr, si = pltpu.roll(blocks_r[v], rot, axis=0), pltpu.roll(blocks_i[v], rot, axis=0)
                    blocks_r[v] = blocks_r[v] + m_re * sr - m_im * si
                    blocks_i[v] = blocks_i[v] + m_re * si + m_im * sr
            p_re = jnp.zeros((SUBLANES, LANES), F32)
            p_im = jnp.zeros((SUBLANES, LANES), F32)
            for r in range(SUBLANES):
                pr, pi = pw[SUBLANES - 1 - r] if d else pw[r]
                p_re = jnp.where(sub8 == r, pr, p_re)
                p_im = jnp.where(sub8 == r, pi, p_im)
            cr, ci = carry_ref[d, 0, j], carry_ref[d, 1, j]
            last = 0 if d else SUBLANES - 1
            for step in range(n_groups):
                v = n_groups - 1 - step if d else step
                blocks_r[v] = blocks_r[v] + (p_re * cr - p_im * ci)
                blocks_i[v] = blocks_i[v] + (p_re * ci + p_im * cr)
                cr, ci = blocks_r[v][last:last + 1, :], blocks_i[v][last:last + 1, :]
            carry_ref[d, 0, j] = cr
            carry_ref[d, 1, j] = ci
            return jnp.concatenate([jnp.concatenate(blocks_r, axis=0), jnp.concatenate(blocks_i, axis=0)],
                                   axis=1).astype(BF16)

        def lane_block(j, carry):
            states = [scan_block(d, j) for d in range(2)]
            for d in range(2):
                y_refs[d][rows_of[d], :] += _dot(states[d], cm_ref[j])
            return carry

        lax.fori_loop(0, n_lane_blocks, lane_block, 0)

    def body(n, carry):
        scan_dirs((pl.ds(pl.multiple_of(n * tc, tc), tc), pl.ds(pl.multiple_of((n_chunks - 1 - n) * tc, tc), tc)),
                  (yf_ref, yb_ref))
        return carry

    lax.fori_loop(0, n_chunks, body, 0)

    def fin(i, carry):
        rows = pl.ds(pl.multiple_of(i * tc, tc), tc)
        u = u_ref[0, rows, :]
        y = _gelu_tanh(yf_ref[rows, :] + yb_ref[rows, :] + dskip_ref[...] * u)
        z = _dot(y.astype(BF16), gluw_ref[...]) + glub_ref[...]
        y_ref[0, rows, :] = (y * _sigmoid(z)).astype(BF16)
        return carry

    lax.fori_loop(0, n_chunks, fin, 0)
    sout_ref[0] = carry_ref[...]


def _s5_mixer(u, b_in, cmat, lbar, dskip, glu_w, glu_b, h0):
    b, t, _ = u.shape
    tc = min(S5_CHUNK, t)
    nlb = S5_STATE // LANES
    consts = [b_in, cmat, lbar, dskip, glu_w, glu_b]
    st_shape = (2, 2, nlb, 1, LANES)
    return pl.pallas_call(
        functools.partial(_s5_kernel, t, tc),
        grid=(b,),
        in_specs=[pl.BlockSpec((1, t, GROUP_W), lambda i: (i, 0, 0))]
                 + [_full_spec(a.shape) for a in consts]
                 + [pl.BlockSpec((1,) + st_shape, lambda i: (i, 0, 0, 0, 0, 0))],
        out_specs=[pl.BlockSpec((1, t, GROUP_W), lambda i: (i, 0, 0)),
                   pl.BlockSpec((1,) + st_shape, lambda i: (i, 0, 0, 0, 0, 0))],
        out_shape=[jax.ShapeDtypeStruct((b, t, GROUP_W), BF16),
                   jax.ShapeDtypeStruct((b,) + st_shape, F32)],
        scratch_shapes=[pltpu.VMEM((2, nlb, tc, 2 * LANES), F32), pltpu.VMEM((t, GROUP_W), F32),
                        pltpu.VMEM((t, GROUP_W), F32), pltpu.VMEM(st_shape, F32)],
        compiler_params=_params(("arbitrary",)),
        name="s5_mixer",
    )(u, *consts, h0)


def _outproj_kernel(x_ref, mod_ref, ya_ref, yb_ref, yc_ref, yd_ref, w_ref, o_ref):
    y = (_dot(ya_ref[0], w_ref[0]) + _dot(yb_ref[0], w_ref[1])
         + _dot(yc_ref[0], w_ref[2]) + _dot(yd_ref[0], w_ref[3]))
    o_ref[0] = x_ref[0] + mod_ref[0][:, 2 * D_MODEL:3 * D_MODEL] * y


def _out_projection(x, mod, ys, w_out):
    b, t, _ = x.shape
    tb = min(TOKEN_BLOCK, t)
    tok = lambda n: pl.BlockSpec((1, tb, n), lambda i, j: (i, j, 0))
    return pl.pallas_call(
        _outproj_kernel,
        grid=(b, t // tb),
        in_specs=[tok(D_MODEL), pl.BlockSpec((1, 1, 6 * D_MODEL), lambda i, j: (i, 0, 0))]
                 + [tok(GROUP_W)] * 4 + [_full_spec(w_out.shape)],
        out_specs=tok(D_MODEL),
        out_shape=jax.ShapeDtypeStruct(x.shape, F32),
        compiler_params=_params(("arbitrary", "arbitrary")),
        name="out_projection",
    )(x, mod, *ys, w_out)


def _topk_rows(scores, k, row_ids):
    n = scores[0].shape[1]
    slot = lax.broadcasted_iota(jnp.int32, (k, n), 0)
    scores = list(scores)
    vals = [jnp.zeros((k, n), F32) for _ in scores]
    ids = [jnp.zeros((k, n), F32) for _ in scores]
    for it in range(k):
        ms = [jnp.max(s, axis=0, keepdims=True) for s in scores]
        idxs = [jnp.min(jnp.where(s == m, row_ids, 1e9), axis=0, keepdims=True) for s, m in zip(scores, ms)]
        vals = [jnp.where(slot == it, m, v) for m, v in zip(ms, vals)]
        ids = [jnp.where(slot == it, idx, i) for idx, i in zip(idxs, ids)]
        scores = [jnp.where(row_ids == idx, -jnp.inf, s) for s, idx in zip(scores, idxs)]
    return list(zip(vals, ids))


def _staircase():
    pairs = [(i, j) for i in range(PEER_TOPK) for j in range(PEER_TOPK) if (i + 1) * (j + 1) <= PEER_TOPK]
    rows = -(-len(pairs) // SUBLANES) * SUBLANES
    sel = np.zeros((2, rows, PEER_TOPK), np.float32)
    aux = np.zeros((2, rows, LANES), np.float32)
    aux[0, len(pairs):] = -np.inf
    aux[1, len(pairs):] = 1e9
    for r, (i, j) in enumerate(pairs):
        sel[0, r, i] = 1.0
        sel[1, r, j] = 1.0
        aux[1, r] = i * PEER_TOPK + j
    return jnp.asarray(np.tile(sel, (1, 1, 3)), BF16), jnp.asarray(aux, F32)


def _take_rows(table, idx, k):
    out = jnp.zeros(idx.shape, F32)
    for r in range(k):
        out = jnp.where(idx == float(r), table[r:r + 1, :], out)
    return out


def _peer_route_kernel(tb, x_ref, mod_ref, g_ref, wqh_ref, wql_ref, kh_ref, kl_ref, stsel_ref, staux_ref,
                       h_ref, a_ref, c_ref, gate_ref, qt_ref, at_ref, ct_ref, gt_ref):
    mod = mod_ref[0]
    h = _rms_modulate(x_ref[0], g_ref[...], mod[:, 3 * D_MODEL:4 * D_MODEL], mod[:, 4 * D_MODEL:5 * D_MODEL])
    h_hi, h_lo = _split2(h)
    h_ref[0] = h_hi
    wq_hi = wqh_ref[...]
    qt_ref[...] = _dot_nt(wq_hi, h_hi) + _dot_nt(wq_hi, h_lo) + _dot_nt(wql_ref[...], h_hi)
    half = PEER_DQ // 2
    key_ids = lax.broadcasted_iota(jnp.int32, (PEER_NKEYS, LANES), 0).astype(F32)

    heads_per_iter = 4

    def head_body(it, carry):
        heads = [it * heads_per_iter + i for i in range(heads_per_iter)]
        tiles = [slice(lt * LANES, (lt + 1) * LANES) for lt in range(tb // LANES)]
        probs = [(head, lanes) for head in heads for lanes in tiles]
        scores = []
        for head, lanes in probs:
            for p in range(2):
                hp = head * 2 + p
                q_hi, q_lo = _split2(qt_ref[pl.ds(pl.multiple_of(hp * half, half), half), lanes])
                scores.append(_dot(kh_ref[hp], q_hi) + _dot(kh_ref[hp], q_lo) + _dot(kl_ref[hp], q_hi))
        top = _topk_rows(scores, PEER_TOPK, key_ids)
        cands = [_sel_dot(stsel_ref[0], top[2 * i][0]) + _sel_dot(stsel_ref[1], top[2 * i + 1][0]) + staux_ref[0]
                 for i in range(len(probs))]
        best = _topk_rows(cands, PEER_TOPK, staux_ref[1])
        for i, (head, lanes) in enumerate(probs):
            out_rows = pl.ds(pl.multiple_of(head * PEER_TOPK, PEER_TOPK), PEER_TOPK)
            sc, flat = best[i]
            r1 = jnp.floor(flat * (1.0 / PEER_TOPK))
            r2 = flat - r1 * PEER_TOPK
            e = jnp.exp(sc - sc[0:1, :])
            at_ref[out_rows, lanes] = _take_rows(top[2 * i][1], r1, PEER_TOPK)
            ct_ref[out_rows, lanes] = _take_rows(top[2 * i + 1][1], r2, PEER_TOPK)
            gt_ref[out_rows, lanes] = e / jnp.sum(e, axis=0, keepdims=True)
        return carry

    lax.fori_loop(0, PEER_HEADS // heads_per_iter, head_body, 0)
    a_ref[0] = at_ref[...].T.astype(jnp.int32)
    c_ref[0] = ct_ref[...].T.astype(jnp.int32)
    gate_ref[0] = gt_ref[...].T


def _peer_route(x, mod, gain, wq_hi, wq_lo, keys_hi, keys_lo):
    b, t, _ = x.shape
    tb = min(TOKEN_BLOCK, t)
    tok = lambda n: pl.BlockSpec((1, tb, n), lambda i, j: (i, j, 0))
    consts = [gain, wq_hi, wq_lo, keys_hi, keys_lo, *_staircase()]
    return pl.pallas_call(
        functools.partial(_peer_route_kernel, tb),
        grid=(b, t // tb),
        scratch_shapes=[pltpu.VMEM((PEER_HEADS * PEER_DQ, tb), F32)] + [pltpu.VMEM((PEER_SEL, tb), F32)] * 3,
        in_specs=[tok(D_MODEL), pl.BlockSpec((1, 1, 6 * D_MODEL), lambda i, j: (i, 0, 0))]
                 + [_full_spec(a.shape) for a in consts],
        out_specs=[tok(D_MODEL), tok(PEER_SEL), tok(PEER_SEL), tok(PEER_SEL)],
        out_shape=[jax.ShapeDtypeStruct((b, t, D_MODEL), BF16),
                   jax.ShapeDtypeStruct((b, t, PEER_SEL), jnp.int32),
                   jax.ShapeDtypeStruct((b, t, PEER_SEL), jnp.int32),
                   jax.ShapeDtypeStruct((b, t, PEER_SEL), F32)],
        compiler_params=_params(("arbitrary", "arbitrary")),
        name="peer_route",
    )(x, mod, *consts)


def _peer_expert_kernel(tb, n_slabs, x_ref, mod_ref, h_ref, a_ref, c_ref, gate_ref, ut_ref, v_ref,
                        o_ref, gmat_ref, acc_ref):
    j = pl.program_id(2)
    nk = PEER_NKEYS

    @pl.when(j == 0)
    def _():
        acc_ref[...] = jnp.zeros_like(acc_ref)
        key_id = lax.broadcasted_iota(jnp.int32, (nk, PEER_SEL), 0)

        pack = 2 * SUBLANES
        groups = 2

        def build(nb, carry):
            n0 = pl.multiple_of(nb * groups * pack, groups * pack)
            a_rows = a_ref[0, pl.ds(n0, groups * pack), :]
            c_rows = c_ref[0, pl.ds(n0, groups * pack), :]
            g_rows = gate_ref[0, pl.ds(n0, groups * pack), :]
            mats = []
            for r in range(groups * pack):
                row_sel = jnp.where(key_id == a_rows[r:r + 1], 1.0, 0.0).astype(BF16)
                col_sel = jnp.where(key_id == c_rows[r:r + 1], g_rows[r:r + 1], 0.0).astype(BF16)
                mats.append(_dot_nt(row_sel, col_sel).astype(BF16))
            for gi in range(groups):
                stacked = jnp.stack(mats[gi * pack:(gi + 1) * pack], axis=0)
                gmat_ref[:, pl.ds(n0 + gi * pack, pack), :] = pltpu.einshape("nac->anc", stacked)
            return carry

        lax.fori_loop(0, tb // (groups * pack), build, 0)

    h = h_ref[0]
    n_pairs = n_slabs // 2
    cols = [slice(i * 2 * nk, (i + 1) * 2 * nk) for i in range(n_pairs)]
    lookahead = 2
    s = [_dot(h, ut_ref[:, cols[i]]) for i in range(min(lookahead, n_pairs))]
    acc = None
    for i in range(n_pairs):
        if i + lookahead < n_pairs:
            s.append(_dot(h, ut_ref[:, cols[i + lookahead]]))
        a0 = j * n_slabs + 2 * i
        g = jnp.concatenate([gmat_ref[a0], gmat_ref[a0 + 1]], axis=1)
        w = g * _gelu_tanh(s[i]).astype(BF16)
        p = _dot(w, v_ref[cols[i], :])
        acc = p if acc is None else acc + p
    acc_ref[...] += acc

    @pl.when(j == pl.num_programs(2) - 1)
    def _():
        o_ref[0] = x_ref[0] + mod_ref[0][:, 5 * D_MODEL:6 * D_MODEL] * acc_ref[...]


def _peer_experts(x, mod, h, a, c, gate, u_t, v_tab):
    b, t, _ = x.shape
    tb = min(PEER_TOKEN_BLOCK, t)
    n_exp = u_t.shape[1]
    eb = PEER_EXPERT_BLOCK
    n_slabs = eb // PEER_NKEYS
    tok = lambda n: pl.BlockSpec((1, tb, n), lambda i, k, j: (i, k, 0))
    return pl.pallas_call(
        functools.partial(_peer_expert_kernel, tb, n_slabs),
        grid=(b, t // tb, n_exp // eb),
        in_specs=[tok(D_MODEL), pl.BlockSpec((1, 1, 6 * D_MODEL), lambda i, k, j: (i, 0, 0)),
                  tok(D_MODEL), tok(PEER_SEL), tok(PEER_SEL), tok(PEER_SEL),
                  pl.BlockSpec((D_MODEL, eb), lambda i, k, j: (0, j)),
                  pl.BlockSpec((eb, D_MODEL), lambda i, k, j: (j, 0))],
        out_specs=tok(D_MODEL),
        out_shape=jax.ShapeDtypeStruct(x.shape, F32),
        scratch_shapes=[pltpu.VMEM((PEER_NKEYS, tb, PEER_NKEYS), BF16), pltpu.VMEM((tb, D_MODEL), F32)],
        compiler_params=_params(("arbitrary", "arbitrary", "arbitrary")),
        name="peer_experts",
    )(x, mod, h, a, c, gate, u_t, v_tab)


def _final_norm_kernel(x_ref, g_ref, o_ref):
    x = x_ref[0]
    ms = jnp.mean(x * x, axis=-1, keepdims=True)
    o_ref[0] = x * lax.rsqrt(ms + EPS) * g_ref[...]


def _final_norm(x, gain):
    b, t, _ = x.shape
    tb = min(TOKEN_BLOCK, t)
    tok = pl.BlockSpec((1, tb, D_MODEL), lambda i, j: (i, j, 0))
    return pl.pallas_call(
        _final_norm_kernel,
        grid=(b, t // tb),
        in_specs=[tok, _full_spec((1, D_MODEL))],
        out_specs=tok,
        out_shape=jax.ShapeDtypeStruct(x.shape, F32),
        compiler_params=_params(("arbitrary", "arbitrary")),
        name="final_norm",
    )(x, gain)


def _hi_lo(w):
    hi = w.astype(BF16)
    return hi, (w - hi.astype(F32)).astype(BF16)


def _pad_cols(w, n):
    return jnp.pad(w, ((0, 0), (0, n - w.shape[1])))


def _layer_weights(l, w_in, w_out, gla_gk_up, gla_gk_bias, gla_onorm_g, dn_conv_w, dn_A_log, dn_dt_bias,
                   dn_onorm_g, s5_lambda_re, s5_lambda_im, s5_log_dt, s5_B_re, s5_B_im, s5_C_re, s5_C_im,
                   s5_D, s5_glu_w, s5_glu_b, lower_bound, hgrn_onorm_g, peer_wq, peer_keys, peer_u, peer_v):
    gw = GROUP_W
    sizes = (gw, gw, gw, gw, 2 * GLA_LOWRANK, gw, gw, gw, gw, 2 * N_HEADS, 2 * N_HEADS, gw, gw, gw, gw, 2 * gw)
    offs = np.concatenate([[0], np.cumsum(sizes)])
    col = lambda a, b: w_in[l][:, int(offs[a]):int(offs[b])]
    wl = {}
    wl["w_gla"] = _pad_cols(col(0, 5), GLA_W).astype(BF16)
    wl["w_dn"] = _pad_cols(col(5, 11), DN_W).astype(BF16)
    wl["w_s5"] = col(11, 12).astype(BF16)
    wl["w_hg"] = col(12, 16).astype(BF16)
    wl["w_out"] = w_out[l].reshape(4, gw, D_MODEL).astype(BF16)
    tile_head = lambda g: jnp.tile(g, N_HEADS).reshape(1, gw)
    up = jnp.zeros((2, LANES, gw), F32)
    for d in range(2):
        up = up.at[d, d * GLA_LOWRANK:(d + 1) * GLA_LOWRANK, :].set(gla_gk_up[l, d])
    wl["gla_up_hi"], wl["gla_up_lo"] = _hi_lo(up)
    wl["gla_bias"] = gla_gk_bias[l].reshape(2, 1, gw)
    wl["gla_onorm"] = tile_head(gla_onorm_g[l])
    expand = np.zeros((2, LANES, 2 * gw), np.float32)
    for d in range(2):
        for h in range(N_HEADS):
            expand[d, d * N_HEADS + h, h * HEAD_DIM:(h + 1) * HEAD_DIM] = 1.0
            expand[d, 2 * N_HEADS + d * N_HEADS + h, gw + h * HEAD_DIM:gw + (h + 1) * HEAD_DIM] = 1.0
    wl["dn_expand"] = jnp.asarray(expand, BF16)
    wl["dn_conv"] = jnp.pad(dn_conv_w[l], ((0, SUBLANES - 3), (0, 0)))
    wl["dn_nega"] = jnp.repeat(-jnp.exp(dn_A_log[l]), HEAD_DIM, axis=-1).reshape(2, 1, gw)
    wl["dn_dtb"] = jnp.repeat(dn_dt_bias[l], HEAD_DIM, axis=-1).reshape(2, 1, gw)
    wl["dn_onorm"] = tile_head(dn_onorm_g[l])
    nlb = S5_STATE // LANES
    lam_re, lam_im = s5_lambda_re[l].astype(F32), s5_lambda_im[l].astype(F32)
    dt = jnp.exp(s5_log_dt[l].astype(F32))[:, :, None]
    mag = jnp.exp(lam_re * dt)
    lbar_re, lbar_im = mag * jnp.cos(lam_im * dt), mag * jnp.sin(lam_im * dt)
    den = lam_re * lam_re + lam_im * lam_im
    num_re, num_im = lbar_re - 1.0, lbar_im
    coef_re = (num_re * lam_re + num_im * lam_im) / den
    coef_im = (num_im * lam_re - num_re * lam_im) / den
    b_re, b_im = s5_B_re[l].astype(F32), s5_B_im[l].astype(F32)
    bb_re = coef_re[..., None] * b_re - coef_im[..., None] * b_im
    bb_im = coef_re[..., None] * b_im + coef_im[..., None] * b_re
    eye_g = jnp.eye(S5_G, dtype=F32)
    bd_in = lambda m: jnp.einsum('dgpc,gk->dgckp', m, eye_g).reshape(2, gw, S5_STATE)
    blk = lambda m: m.reshape(2, gw, nlb, LANES).transpose(0, 2, 1, 3)
    b_cat = jnp.concatenate([blk(bd_in(bb_re)), blk(bd_in(bb_im))], axis=-1)
    wl["s5_b"] = b_cat.astype(BF16)
    bd_out = lambda m: jnp.einsum('gcp,gk->gpkc', m.astype(F32), eye_g).reshape(S5_STATE, gw)
    c_re, c_im = bd_out(s5_C_re[l]), bd_out(s5_C_im[l])
    wl["s5_c"] = jnp.concatenate([c_re.reshape(nlb, LANES, gw), -c_im.reshape(nlb, LANES, gw)],
                                 axis=1).astype(BF16)
    wl["s5_lbar"] = jnp.stack([lbar_re.reshape(2, nlb, LANES), lbar_im.reshape(2, nlb, LANES)], axis=2)
    wl["s5_d"] = s5_D[l].reshape(1, gw)
    wl["s5_glu_w"] = s5_glu_w[l].astype(BF16)
    wl["s5_glu_b"] = s5_glu_b[l].reshape(1, gw)
    wl["hg_loglb"] = jnp.log(lower_bound).reshape(1, gw)
    wl["hg_log1m"] = jnp.log1p(-lower_bound).reshape(1, gw)
    wl["hg_1m"] = (1.0 - lower_bound).reshape(1, gw)
    wl["hg_onorm"] = tile_head(hgrn_onorm_g[l])
    wl["wq_hi"], wl["wq_lo"] = _hi_lo(peer_wq[l].T)
    wl["keys_hi"], wl["keys_lo"] = _hi_lo(peer_keys[l].reshape(PEER_HEADS * 2, PEER_NKEYS, PEER_DQ // 2))
    wl["u_t"] = peer_u[l].astype(BF16).T
    wl["v_tab"] = peer_v[l].astype(BF16)
    return wl


def _heads_to_lanes(s, transposed):
    perm = (0, 1, 4, 2, 3) if transposed else (0, 1, 3, 2, 4)
    return s.astype(F32).transpose(perm).reshape(s.shape[0], 2, HEAD_DIM, GROUP_W)


def _lanes_to_heads(s, transposed):
    s = s.reshape(s.shape[0], 2, HEAD_DIM, N_HEADS, HEAD_DIM)
    return s.transpose((0, 1, 3, 4, 2) if transposed else (0, 1, 3, 2, 4))


def _to_column_major(x):
    b, n, d = x.shape
    return x.reshape(b, n // GRID_W, GRID_W, d).transpose(0, 2, 1, 3).reshape(b, n, d)


def _to_row_major(x):
    b, n, d = x.shape
    return x.reshape(b, GRID_W, n // GRID_W, d).transpose(0, 2, 1, 3).reshape(b, n, d)


def _trunk_layer(x, mod, wl, norm1_g, norm2_g, states, shared_mod):
    st_gla, st_dn, st_s5, st_hg = states
    p_gla, p_dn, p_s5, p_hg = _in_projection(x, mod, norm1_g, [wl["w_gla"], wl["w_dn"], wl["w_s5"], wl["w_hg"]])
    y_a, n_gla = _decay_mixer("gla", p_gla, wl["gla_up_hi"], wl["gla_up_lo"], wl["gla_bias"], wl["gla_onorm"], st_gla)
    y_b, n_dn = _delta_mixer(p_dn, wl["dn_conv"], wl["dn_expand"], wl["dn_nega"], wl["dn_dtb"], wl["dn_onorm"], st_dn)
    y_c, n_s5 = _s5_mixer(p_s5, wl["s5_b"], wl["s5_c"], wl["s5_lbar"], wl["s5_d"],
                          wl["s5_glu_w"], wl["s5_glu_b"], st_s5)
    y_d, n_hg = _decay_mixer("hgrn", p_hg, wl["hg_loglb"], wl["hg_log1m"], wl["hg_1m"], wl["hg_onorm"], st_hg)
    x = _out_projection(x, mod, [y_a, y_b, y_c, y_d], wl["w_out"])
    h, a, c, gate = _peer_route(x, mod, norm2_g, wl["wq_hi"], wl["wq_lo"], wl["keys_hi"], wl["keys_lo"])
    b, t, _ = x.shape
    fold = max(1, min(PEER_TOKEN_BLOCK // t, b)) if shared_mod else 1
    fb = lambda z: z.reshape(b // fold, fold * t, z.shape[-1])
    x = _peer_experts(fb(x), mod[0:b // fold], fb(h), fb(a), fb(c), fb(gate), wl["u_t"], wl["v_tab"]).reshape(b, t, -1)
    return x, (n_gla, n_dn, n_s5, n_hg)


def kernel(x_prompt, x_sample, state_gla, state_delta, state_s5_re, state_s5_im, state_hgrn, c, c_ctx,
           mod_w, mod_b, norm1_g, norm2_g, w_in, w_out, gla_gk_up, gla_gk_bias, gla_onorm_g,
           dn_conv_w, dn_A_log, dn_dt_bias, dn_onorm_g, s5_lambda_re, s5_lambda_im, s5_log_dt,
           s5_B_re, s5_B_im, s5_C_re, s5_C_im, s5_D, s5_glu_w, s5_glu_b, hgrn_lb_logits,
           hgrn_onorm_g, peer_wq, peer_keys, peer_u, peer_v, final_norm_g):
    bp, bs = x_prompt.shape[0], x_sample.shape[0]
    nlb = S5_STATE // LANES
    lb_cum = jnp.cumsum(jax.nn.softmax(hgrn_lb_logits.astype(F32), axis=0), axis=0)
    lower_bounds = lb_cum - lb_cum[0:1]

    rows = 2 * SUBLANES
    cc = jnp.concatenate([c, c_ctx[None, :], jnp.zeros((rows - bs - 1, D_MODEL), F32)], axis=0)
    mod_all = _modulation(cc, mod_w, mod_b)

    zeros_hh = jnp.zeros((bp, 2, HEAD_DIM, GROUP_W), F32)
    zeros_s5 = jnp.zeros((bp, 2, 2, nlb, 1, LANES), F32)

    xp, xs = x_prompt, x_sample
    ctx_states = []
    for l in range(DEPTH):
        wl = _layer_weights(l, w_in, w_out, gla_gk_up, gla_gk_bias, gla_onorm_g, dn_conv_w, dn_A_log,
                            dn_dt_bias, dn_onorm_g, s5_lambda_re, s5_lambda_im, s5_log_dt, s5_B_re, s5_B_im,
                            s5_C_re, s5_C_im, s5_D, s5_glu_w, s5_glu_b, lower_bounds[l], hgrn_onorm_g,
                            peer_wq, peer_keys, peer_u, peer_v)
        n1, n2 = norm1_g[l].reshape(1, D_MODEL), norm2_g[l].reshape(1, D_MODEL)
        mod_ctx = jnp.broadcast_to(mod_all[l, bs].reshape(1, 1, 6 * D_MODEL), (bp, 1, 6 * D_MODEL))
        mod_lat = mod_all[l, 0:bs].reshape(bs, 1, 6 * D_MODEL)
        xp, st_ctx = _trunk_layer(xp, mod_ctx, wl, n1, n2, (zeros_hh, zeros_hh, zeros_s5, zeros_hh), True)
        ctx_states.append(st_ctx)
        cached_s5 = jnp.stack([state_s5_re[:, l], state_s5_im[:, l]], axis=2).reshape(bs, 2, 2, nlb, 1, LANES)
        cached = (_heads_to_lanes(state_gla[:, l], True), _heads_to_lanes(state_delta[:, l], False), cached_s5,
                  _heads_to_lanes(state_hgrn[:, l], True))
        if l % 2 == 1:
            xs = _to_column_major(xs)
        xs, _ = _trunk_layer(xs, mod_lat, wl, n1, n2, cached, False)
        if l % 2 == 1:
            xs = _to_row_major(xs)

    gain = final_norm_g.reshape(1, D_MODEL)
    y_prompt = _final_norm(xp, gain)
    y_sample = _final_norm(xs, gain)
    new_gla = jnp.stack([_lanes_to_heads(s[0], True) for s in ctx_states], axis=1)
    new_dn = jnp.stack([_lanes_to_heads(s[1], False) for s in ctx_states], axis=1)
    s5 = jnp.stack([s[2].reshape(bp, 2, 2, S5_G, S5_P) for s in ctx_states], axis=1)
    new_hg = jnp.stack([_lanes_to_heads(s[3], True) for s in ctx_states], axis=1)
    return (y_prompt, y_sample, new_gla, new_dn, s5[:, :, :, 0], s5[:, :, :, 1], new_hg)
```

```python
import functools

import numpy as np
import jax
import jax.numpy as jnp
from jax import lax
from jax.experimental import pallas as pl
from jax.experimental.pallas import tpu as pltpu

F32 = jnp.float32
BF16 = jnp.bfloat16

D_MODEL = 1024
DEPTH = 2
GRID_W = 64
GROUP_W = 256
HEAD_DIM = 64
N_HEADS = GROUP_W // HEAD_DIM
GLA_LOWRANK = 16
GLA_GATE_NORM = 16.0
DN_CHUNK = 64
S5_GROUP = 16
S5_G = GROUP_W // S5_GROUP
S5_P = 64
S5_STATE = S5_G * S5_P
PEER_HEADS = 8
PEER_NKEYS = 128
PEER_DQ = 256
PEER_TOPK = 16
PEER_SEL = PEER_HEADS * PEER_TOPK
EPS = 1e-6

LANES = 128
SUBLANES = 8
VMEM_LIMIT = 56 * 1024 * 1024

GLA_CHUNK = 64
S5_CHUNK = 256
TOKEN_BLOCK = 256
PEER_TOKEN_BLOCK = 512
PEER_EXPERT_BLOCK = 2048
GLA_W = 4 * GROUP_W + LANES
DN_W = 4 * GROUP_W + LANES
HG_W = 5 * GROUP_W


def _dot(a, b):
    return jnp.dot(a, b, preferred_element_type=F32)


def _dot_nt(a, b):
    return lax.dot_general(a, b, (((1,), (1,)), ((), ())), preferred_element_type=F32)


def _dot_tn(a, b):
    return lax.dot_general(a, b, (((0,), (0,)), ((), ())), preferred_element_type=F32)


def _split2(x):
    hi = x.astype(BF16)
    lo = (x - hi.astype(F32)).astype(BF16)
    return hi, lo


def _split3(x):
    hi = x.astype(BF16)
    r = x - hi.astype(F32)
    mid = r.astype(BF16)
    lo = (r - mid.astype(F32)).astype(BF16)
    return hi, mid, lo


def _sel_dot(m01x3, x):
    return _dot(m01x3, jnp.concatenate(_split3(x), axis=0))


def _dot_sel(x, m01):
    hi, mid, lo = _split3(x)
    return _dot(hi, m01) + _dot(mid, m01) + _dot(lo, m01)


def _dot_hl(a, b_hi, b_lo):
    a_hi, a_lo = _split2(a)
    return _dot(a_hi, b_hi) + _dot(a_hi, b_lo) + _dot(a_lo, b_hi)


def _head_sum(x, ones_bd):
    hi, lo = _split2(x)
    return _dot(hi, ones_bd) + _dot(lo, ones_bd)


HEADS_PER_TILE = LANES // HEAD_DIM
N_HEAD_TILES = GROUP_W // LANES


def _head_blockdiag(x16, ones_bd):
    mask = ones_bd[0:LANES, 0:LANES]
    return [jnp.tile(x16[:, t * LANES:(t + 1) * LANES], (HEADS_PER_TILE, 1)) * mask for t in range(N_HEAD_TILES)]


def _bd_dot(a16, bds):
    return jnp.concatenate([_dot(a16[:, t * LANES:(t + 1) * LANES], bd) for t, bd in enumerate(bds)], axis=1)


def _bd_dot_nt(a16, bds):
    return jnp.concatenate([_dot_nt(a16[:, t * LANES:(t + 1) * LANES], bd) for t, bd in enumerate(bds)], axis=1)


def _per_head_mm(a, x, ones_bd):
    a_hi, a_lo = _split2(a)
    x_hi, x_lo = _split2(x)
    bd_hi = _head_blockdiag(x_hi, ones_bd)
    return _bd_dot(a_hi, bd_hi) + _bd_dot(a_hi, _head_blockdiag(x_lo, ones_bd)) + _bd_dot(a_lo, bd_hi)


def _per_head_mm1(a, x, ones_bd):
    return _bd_dot(a.astype(BF16), _head_blockdiag(x.astype(BF16), ones_bd))


def _per_head_dot_tn(a16, b16):
    lane_head = lax.broadcasted_iota(jnp.int32, (HEAD_DIM, LANES), 1) // HEAD_DIM
    outs = []
    for t in range(N_HEAD_TILES):
        full = _dot_tn(a16[:, t * LANES:(t + 1) * LANES], b16[:, t * LANES:(t + 1) * LANES])
        out = None
        for h in range(HEADS_PER_TILE):
            blk = jnp.where(lane_head == h, full[h * HEAD_DIM:(h + 1) * HEAD_DIM, :], 0.0)
            out = blk if out is None else out + blk
        outs.append(out)
    return jnp.concatenate(outs, axis=1)


def _log_sigmoid(x):
    return jnp.minimum(x, 0.0) - jnp.log(1.0 + jnp.exp(-jnp.abs(x)))


def _softplus(x):
    return jnp.maximum(x, 0.0) + jnp.log(1.0 + jnp.exp(-jnp.abs(x)))


def _sigmoid(x):
    return 1.0 / (1.0 + jnp.exp(-x))


def _silu(x):
    return x * _sigmoid(x)


def _gelu_tanh(x):
    return 0.5 * x * (1.0 + jnp.tanh(0.7978845608028654 * (x + 0.044715 * x * x * x)))


def _rms_modulate(x, gain, shift, scale):
    ms = jnp.mean(x * x, axis=-1, keepdims=True)
    return x * lax.rsqrt(ms + EPS) * gain * (1.0 + scale) + shift


def _full_spec(shape):
    zeros = (0,) * len(shape)
    return pl.BlockSpec(shape, lambda *_: zeros)


def _params(sem):
    return pltpu.CompilerParams(dimension_semantics=sem, vmem_limit_bytes=VMEM_LIMIT)


def _scan_consts(c, reverse):
    levels = int(np.log2(c))
    idx = np.arange(c)
    i, t = idx[:, None], idx[None, :]
    sel, masks = [], []
    for lv in range(levels):
        s = 1 << lv
        right = (idx // s) % 2 == 1
        same_blk = (i // s) == (t // s)
        sel.append(np.where(right[:, None], same_blk & (t <= i), same_blk & (t > i)))
        same_parent = (i // (2 * s)) == (t // (2 * s))
        masks.append(same_parent & right[:, None] & ~right[None, :])
    sel.append(t <= i)
    sel.append(t > i)
    masks.append(i == t)
    if reverse:
        sel = [m[::-1, ::-1] for m in sel]
        masks = [m[::-1, ::-1] for m in masks]
    sel.append(np.ones((SUBLANES, c), bool))
    sel = np.tile(np.concatenate(sel, axis=0).astype(np.float32), (1, 3))
    masks = np.tile(np.stack(masks).astype(np.float32), (1, 1, N_HEADS))
    return jnp.asarray(sel, BF16), jnp.asarray(masks, F32)


def _dn_consts(c, reverse):
    levels = int(np.log2(c))
    idx = np.arange(c)
    i, t = idx[:, None], idx[None, :]
    incl, tail = t <= i, t > i
    masks = []
    for lv in range(levels):
        s = 1 << lv
        right = (idx // s) % 2 == 1
        masks.append(((i // (2 * s)) == (t // (2 * s))) & right[:, None] & ~right[None, :])
    masks += [incl, incl & (i != t), i == t]
    if reverse:
        incl, tail = incl[::-1, ::-1], tail[::-1, ::-1]
        masks = [m[::-1, ::-1] for m in masks]
    sel = np.tile(np.concatenate([incl, tail, np.ones((SUBLANES, c), bool)], axis=0).astype(np.float32), (1, 3))
    masks = np.tile(np.stack(masks).astype(np.float32), (1, 1, N_HEADS))
    return jnp.asarray(sel, BF16), jnp.asarray(masks, F32)


def _ones_blockdiag():
    h = np.arange(GROUP_W) // HEAD_DIM
    return jnp.asarray((h[:, None] == h[None, :]).astype(np.float32), BF16)


def _mod_kernel(c_ref, w_ref, b_ref, o_ref):
    a_hi, a_lo = _split2(_silu(c_ref[...]))
    w_hi, w_lo = _split2(w_ref[0])
    o_ref[0] = _dot(a_hi, w_hi) + _dot(a_hi, w_lo) + _dot(a_lo, w_hi) + b_ref[0]


def _modulation(cc, mod_w, mod_b):
    rows = cc.shape[0]
    nb = 6
    return pl.pallas_call(
        _mod_kernel,
        grid=(DEPTH, nb),
        in_specs=[pl.BlockSpec((rows, D_MODEL), lambda l, j: (0, 0)),
                  pl.BlockSpec((1, D_MODEL, D_MODEL), lambda l, j: (l, 0, j)),
                  pl.BlockSpec((1, 1, D_MODEL), lambda l, j: (l, 0, j))],
        out_specs=pl.BlockSpec((1, rows, D_MODEL), lambda l, j: (l, 0, j)),
        out_shape=jax.ShapeDtypeStruct((DEPTH, rows, 6 * D_MODEL), F32),
        compiler_params=_params(("arbitrary", "arbitrary")),
        name="modulation",
    )(cc, mod_w, mod_b.reshape(DEPTH, 1, 6 * D_MODEL))


def _inproj_kernel(x_ref, mod_ref, g_ref, wa_ref, wb_ref, wc_ref, wd_ref, oa_ref, ob_ref, oc_ref, od_ref):
    mod = mod_ref[0]
    h = _rms_modulate(x_ref[0], g_ref[...], mod[:, 0:D_MODEL], mod[:, D_MODEL:2 * D_MODEL]).astype(BF16)
    oa_ref[0] = _dot(h, wa_ref[...])
    ob_ref[0] = _dot(h, wb_ref[...])
    oc_ref[0] = _dot(h, wc_ref[...])
    od_ref[0] = _dot(h, wd_ref[...])


def _in_projection(x, mod, gain, weights):
    b, t, _ = x.shape
    tb = min(TOKEN_BLOCK, t)
    widths = [w.shape[1] for w in weights]
    return pl.pallas_call(
        _inproj_kernel,
        grid=(b, t // tb),
        in_specs=[pl.BlockSpec((1, tb, D_MODEL), lambda i, j: (i, j, 0)),
                  pl.BlockSpec((1, 1, 6 * D_MODEL), lambda i, j: (i, 0, 0)),
                  _full_spec((1, D_MODEL))] + [_full_spec(w.shape) for w in weights],
        out_specs=[pl.BlockSpec((1, tb, n), lambda i, j: (i, j, 0)) for n in widths],
        out_shape=[jax.ShapeDtypeStruct((b, t, n), F32) for n in widths],
        compiler_params=_params(("arbitrary", "arbitrary")),
        name="in_projection",
    )(x, mod, gain, *weights)


def _decay_scan_prepare(probs, ones_bd, c):
    levels = int(np.log2(c))
    exs = [jnp.exp(_sel_dot(pr["sel"][...], pr["g"])) for pr in probs]
    attns = [None] * len(probs)
    for lv in range(levels + 1):
        for i, (pr, ex) in enumerate(zip(probs, exs)):
            q, k = pr["q"], pr["k"]
            if lv < levels:
                e = ex[lv * c:(lv + 1) * c]
                qt, kt = (q * e).astype(BF16), (k * e).astype(BF16)
            else:
                qt, kt = q.astype(BF16), k.astype(BF16)
            p = _bd_dot_nt(qt, _head_blockdiag(kt, ones_bd)) * pr["mask"][lv]
            attns[i] = p if attns[i] is None else attns[i] + p
    res = []
    for pr, ex, attn in zip(probs, exs, attns):
        o = _bd_dot(attn.astype(BF16), _head_blockdiag(pr["v"].astype(BF16), ones_bd))
        qd = (pr["q"] * ex[levels * c:(levels + 1) * c]).astype(BF16)
        ktail = (pr["k"] * ex[(levels + 1) * c:(levels + 2) * c]).astype(BF16)
        res.append((o, qd, ktail, ex[(levels + 2) * c:(levels + 2) * c + SUBLANES]))
    return res


def _decay_scan_state_steps(items, ones_bd, state_ref):
    sts = [state_ref[it[0]] for it in items]
    outs = [it[1] + _bd_dot_nt(it[2], _head_blockdiag(st.astype(BF16), ones_bd)) for it, st in zip(items, sts)]
    for it, st in zip(items, sts):
        state_ref[it[0]] = st * it[5] + _per_head_dot_tn(it[4], it[3])
    return outs


def _head_norm_gate(o, gate, onorm, ones_bd):
    ms = _head_sum(o * o, ones_bd) * (1.0 / HEAD_DIM)
    return o * lax.rsqrt(ms + EPS) * onorm * _silu(gate)


def _decay_mixer_kernel(mode, t, c, p_ref, self_ref, selb_ref, maskf_ref, maskb_ref, ones_ref, onorm_ref,
                        pa_ref, pb_ref, pc_ref, h0_ref, y_ref, sout_ref,
                        qd_ref, kt_ref, v16_ref, dl_ref, of_ref, ob_ref, state_ref):
    n_chunks = t // c
    state_ref[...] = h0_ref[0]
    sels, masks, o_refs = (self_ref, selb_ref), (maskf_ref, maskb_ref), (of_ref, ob_ref)

    def inputs(rows, d):
        if mode == "gla":
            q = p_ref[0, rows, 0:GROUP_W] * (HEAD_DIM ** -0.5)
            k = p_ref[0, rows, GROUP_W:2 * GROUP_W]
            v = p_ref[0, rows, 2 * GROUP_W:3 * GROUP_W]
            low = p_ref[0, rows, 4 * GROUP_W:4 * GROUP_W + LANES]
            gk = _dot_hl(low, pa_ref[d], pb_ref[d]) + pc_ref[d]
            g = _log_sigmoid(gk) * (1.0 / GLA_GATE_NORM)
        else:
            q = _silu(p_ref[0, rows, 0:GROUP_W]) * (HEAD_DIM ** -0.5)
            v = p_ref[0, rows, GROUP_W:2 * GROUP_W]
            z = p_ref[0, rows, (3 + d) * GROUP_W:(4 + d) * GROUP_W]
            a = pa_ref[...]
            b = pb_ref[...] + _log_sigmoid(z)
            m = jnp.maximum(a, b)
            g = m + jnp.log(jnp.exp(a - m) + jnp.exp(b - m))
            k = pc_ref[...] * _sigmoid(-z)
        return q, k, v, g

    per_iter = 4 if n_chunks % 4 == 0 else 1

    def prepare(i, carry):
        chunks = [i * per_iter + j for j in range(per_iter)]
        probs = []
        for n in chunks:
            rows = pl.ds(pl.multiple_of(n * c, c), c)
            for d in range(2):
                q, k, v, g = inputs(rows, d)
                probs.append(dict(q=q, k=k, v=v, g=g, sel=sels[d], mask=masks[d]))
            v16_ref[rows, :] = probs[-1]["v"].astype(BF16)
        for idx, (o, qd, ktail, dlast) in enumerate(_decay_scan_prepare(probs, ones_ref[...], c)):
            n, d = chunks[idx // 2], idx % 2
            rows = pl.ds(pl.multiple_of(n * c, c), c)
            o_refs[d][rows, :] = o
            qd_ref[d, rows, :] = qd
            kt_ref[d, rows, :] = ktail
            dl_ref[d, pl.ds(pl.multiple_of(n * SUBLANES, SUBLANES), SUBLANES), :] = dlast
        return carry

    lax.fori_loop(0, n_chunks // per_iter, prepare, 0)

    def body(i, carry):
        items, rows_of = [], []
        for d, n in ((0, i), (1, n_chunks - 1 - i)):
            rows = pl.ds(pl.multiple_of(n * c, c), c)
            dlast = dl_ref[d, pl.ds(pl.multiple_of(n * SUBLANES, SUBLANES), SUBLANES), :][0:1, :]
            items.append((d, o_refs[d][rows, :], qd_ref[d, rows, :], kt_ref[d, rows, :], v16_ref[rows, :], dlast))
            rows_of.append(rows)
        o_f, o_b = _decay_scan_state_steps(items, ones_ref[...], state_ref)
        of_ref[rows_of[0], :] = o_f
        ob_ref[rows_of[1], :] = o_b
        return carry

    lax.fori_loop(0, n_chunks, body, 0)

    rbk = min(t, 256)
    gate_col = 3 * GROUP_W if mode == "gla" else 2 * GROUP_W

    def fin(i, carry):
        rows = pl.ds(pl.multiple_of(i * rbk, rbk), rbk)
        o = of_ref[rows, :] + ob_ref[rows, :]
        gate = p_ref[0, rows, gate_col:gate_col + GROUP_W]
        y_ref[0, rows, :] = _head_norm_gate(o, gate, onorm_ref[...], ones_ref[...]).astype(BF16)
        return carry

    lax.fori_loop(0, t // rbk, fin, 0)
    sout_ref[0] = state_ref[...]


def _decay_mixer(mode, p, pa, pb, pc, onorm, h0t):
    b, t, w = p.shape
    c = GLA_CHUNK
    sel_f, mask_f = _scan_consts(c, False)
    sel_b, mask_b = _scan_consts(c, True)
    ones_bd = _ones_blockdiag()
    consts = [sel_f, sel_b, mask_f, mask_b, ones_bd, onorm, pa, pb, pc]
    state_shape = (2, HEAD_DIM, GROUP_W)
    return pl.pallas_call(
        functools.partial(_decay_mixer_kernel, mode, t, c),
        grid=(b,),
        in_specs=[pl.BlockSpec((1, t, w), lambda i: (i, 0, 0))]
                 + [_full_spec(a.shape) for a in consts]
                 + [pl.BlockSpec((1,) + state_shape, lambda i: (i, 0, 0, 0))],
        out_specs=[pl.BlockSpec((1, t, GROUP_W), lambda i: (i, 0, 0)),
                   pl.BlockSpec((1,) + state_shape, lambda i: (i, 0, 0, 0))],
        out_shape=[jax.ShapeDtypeStruct((b, t, GROUP_W), BF16),
                   jax.ShapeDtypeStruct((b,) + state_shape, F32)],
        scratch_shapes=[pltpu.VMEM((2, t, GROUP_W), BF16), pltpu.VMEM((2, t, GROUP_W), BF16),
                        pltpu.VMEM((t, GROUP_W), BF16), pltpu.VMEM((2, (t // c) * SUBLANES, GROUP_W), F32),
                        pltpu.VMEM((t, GROUP_W), F32), pltpu.VMEM((t, GROUP_W), F32),
                        pltpu.VMEM(state_shape, F32)],
        compiler_params=_params(("arbitrary",)),
        name=mode + "_mixer",
    )(p, *consts, h0t)


def _delta_prepare(probs, ones_bd, expand_ref, nega_ref, dtb_ref, c):
    levels = int(np.log2(c))
    work = []
    for pr in probs:
        d, mask = pr["d"], pr["mask"]
        logits = _dot_sel(pr["ba"], expand_ref[d])
        beta = _sigmoid(logits[:, 0:GROUP_W])
        g = nega_ref[d] * _softplus(logits[:, GROUP_W:2 * GROUP_W] + dtb_ref[d])
        cs = _sel_dot(pr["sel"][...], g)
        b_incl = cs[0:c]
        b_t = jnp.concatenate([b_incl[:, h * HEAD_DIM:(h + 1) * HEAD_DIM].T for h in range(N_HEADS)], axis=1)
        lmat = jnp.exp(jnp.minimum(b_incl - b_t, 0.0)) * mask[levels]
        m = beta * pr["kk"] * lmat * mask[levels + 1]
        work.append(dict(beta=beta, eb=jnp.exp(b_incl), etail=jnp.exp(cs[c:2 * c]),
                         dlast=jnp.exp(cs[2 * c:2 * c + SUBLANES]), lmat=lmat, m=m,
                         t=mask[levels + 2] - m * mask[0]))
    for lv in range(1, levels):
        mm = _per_head_mm1 if lv == levels - 1 else _per_head_mm
        ps = [mm(wk["m"] * pr["mask"][lv], wk["t"], ones_bd) for pr, wk in zip(probs, work)]
        for wk, p in zip(work, ps):
            wk["t"] = wk["t"] - mm(wk["t"], p, ones_bd)
    us = [_per_head_mm1(wk["t"], pr["v"] * wk["beta"], ones_bd) for pr, wk in zip(probs, work)]
    ws = [_per_head_mm1(wk["t"], pr["k"] * wk["beta"] * wk["eb"], ones_bd) for pr, wk in zip(probs, work)]
    return [(u, w.astype(BF16), (pr["qk_raw"] * wk["lmat"]).astype(BF16), (pr["q"] * wk["eb"]).astype(BF16),
             (pr["k"] * wk["etail"]).astype(BF16), wk["dlast"]) for pr, wk, u, w in zip(probs, work, us, ws)]


def _delta_state_steps(items, ones_bd, state_ref):
    sts = [state_ref[it[0]] for it in items]
    bds = [_head_blockdiag(st.astype(BF16), ones_bd) for st in sts]
    v16s = [(it[1] - _bd_dot(it[2], bd)).astype(BF16) for it, bd in zip(items, bds)]
    outs = [_bd_dot(it[4], bd) + _bd_dot(it[3], _head_blockdiag(v16, ones_bd))
            for it, bd, v16 in zip(items, bds, v16s)]
    for it, st, v16 in zip(items, sts, v16s):
        state_ref[it[0]] = st * it[6] + _per_head_dot_tn(it[5], v16)
    return outs


def _delta_mixer_kernel(t, c, p_ref, self_ref, selb_ref, maskf_ref, maskb_ref, ones_ref, onorm_ref,
                        conv_ref, expand_ref, nega_ref, dtb_ref, h0_ref, y_ref, sout_ref,
                        u_ref, w_ref, qk_ref, qd_ref, kt_ref, dl_ref, of_ref, ob_ref, state_ref):
    n_chunks = t // c
    qkv_w = 3 * GROUP_W
    state_ref[...] = h0_ref[0]
    row_id = lax.broadcasted_iota(jnp.int32, (c, qkv_w), 0)
    w_prev, w_mid, w_next = conv_ref[0:1, :], conv_ref[1:2, :], conv_ref[2:3, :]
    sels, masks = (self_ref, selb_ref), (maskf_ref, maskb_ref)

    def conv_qkv(n):
        r0 = pl.multiple_of(n * c, c)
        x = p_ref[0, pl.ds(r0, c), 0:qkv_w]
        before = p_ref[0, pl.ds(pl.multiple_of(jnp.maximum(r0 - SUBLANES, 0), SUBLANES), SUBLANES), 0:qkv_w]
        after = p_ref[0, pl.ds(pl.multiple_of(jnp.minimum(r0 + c, t - SUBLANES), SUBLANES), SUBLANES), 0:qkv_w]
        before_row = jnp.where(n > 0, before[SUBLANES - 1:SUBLANES, :], 0.0)
        after_row = jnp.where(n < n_chunks - 1, after[0:1, :], 0.0)
        x_prev = jnp.where(row_id == 0, before_row, pltpu.roll(x, 1, axis=0))
        x_next = jnp.where(row_id == c - 1, after_row, pltpu.roll(x, c - 1, axis=0))
        y = _silu(x_prev * w_prev + x * w_mid + x_next * w_next)
        qk = y[:, 0:2 * GROUP_W]
        ones2 = ones_ref[...]
        ss = jnp.concatenate([_head_sum(qk[:, 0:GROUP_W] * qk[:, 0:GROUP_W], ones2),
                              _head_sum(qk[:, GROUP_W:] * qk[:, GROUP_W:], ones2)], axis=1)
        qk = qk * lax.rsqrt(ss + EPS)
        q, k, v = qk[:, 0:GROUP_W] * (HEAD_DIM ** -0.5), qk[:, GROUP_W:], y[:, 2 * GROUP_W:]
        k_hi, k_lo = _split2(k)
        bd_k = _head_blockdiag(k_hi, ones2)
        kk = _bd_dot_nt(k_hi, bd_k) + _bd_dot_nt(k_hi, _head_blockdiag(k_lo, ones2)) + _bd_dot_nt(k_lo, bd_k)
        qk_raw = _bd_dot_nt(q.astype(BF16), bd_k)
        ba = p_ref[0, pl.ds(r0, c), 4 * GROUP_W:4 * GROUP_W + LANES]
        return dict(q=q, k=k, v=v, kk=kk, qk_raw=qk_raw, ba=ba)

    per_iter = 4 if n_chunks % 4 == 0 else 1

    def prepare(i, carry):
        chunks = [i * per_iter + j for j in range(per_iter)]
        probs = []
        for n in chunks:
            shared = conv_qkv(n)
            probs += [dict(shared, sel=sels[d], mask=masks[d], d=d) for d in range(2)]
        res = _delta_prepare(probs, ones_ref[...], expand_ref, nega_ref, dtb_ref, c)
        for idx, (u, w16, qk16, qd, ktail, dlast) in enumerate(res):
            n, d = chunks[idx // 2], idx % 2
            rows = pl.ds(pl.multiple_of(n * c, c), c)
            u_ref[d, rows, :] = u
            w_ref[d, rows, :] = w16
            qk_ref[d, rows, :] = qk16
            qd_ref[d, rows, :] = qd
            kt_ref[d, rows, :] = ktail
            dl_ref[d, pl.ds(pl.multiple_of(n * SUBLANES, SUBLANES), SUBLANES), :] = dlast
        return carry

    lax.fori_loop(0, n_chunks // per_iter, prepare, 0)

    def body(i, carry):
        items, rows_of = [], []
        for d, n in ((0, i), (1, n_chunks - 1 - i)):
            rows = pl.ds(pl.multiple_of(n * c, c), c)
            dlast = dl_ref[d, pl.ds(pl.multiple_of(n * SUBLANES, SUBLANES), SUBLANES), :][0:1, :]
            items.append((d, u_ref[d, rows, :], w_ref[d, rows, :], qk_ref[d, rows, :], qd_ref[d, rows, :],
                          kt_ref[d, rows, :], dlast))
            rows_of.append(rows)
        o_f, o_b = _delta_state_steps(items, ones_ref[...], state_ref)
        of_ref[rows_of[0], :] = o_f
        ob_ref[rows_of[1], :] = o_b
        return carry

    lax.fori_loop(0, n_chunks, body, 0)

    rbk = min(t, 256)

    def fin(i, carry):
        rows = pl.ds(pl.multiple_of(i * rbk, rbk), rbk)
        o = of_ref[rows, :] + ob_ref[rows, :]
        gate = p_ref[0, rows, 3 * GROUP_W:4 * GROUP_W]
        y_ref[0, rows, :] = _head_norm_gate(o, gate, onorm_ref[...], ones_ref[...]).astype(BF16)
        return carry

    lax.fori_loop(0, t // rbk, fin, 0)
    sout_ref[0] = state_ref[...]


def _delta_mixer(p, conv_w, expand, nega, dtb, onorm, s0):
    b, t, w = p.shape
    c = DN_CHUNK
    sel_f, mask_f = _dn_consts(c, False)
    sel_b, mask_b = _dn_consts(c, True)
    consts = [sel_f, sel_b, mask_f, mask_b, _ones_blockdiag(), onorm, conv_w, expand, nega, dtb]
    state_shape = (2, HEAD_DIM, GROUP_W)
    return pl.pallas_call(
        functools.partial(_delta_mixer_kernel, t, c),
        grid=(b,),
        in_specs=[pl.BlockSpec((1, t, w), lambda i: (i, 0, 0))]
                 + [_full_spec(a.shape) for a in consts]
                 + [pl.BlockSpec((1,) + state_shape, lambda i: (i, 0, 0, 0))],
        out_specs=[pl.BlockSpec((1, t, GROUP_W), lambda i: (i, 0, 0)),
                   pl.BlockSpec((1,) + state_shape, lambda i: (i, 0, 0, 0))],
        out_shape=[jax.ShapeDtypeStruct((b, t, GROUP_W), BF16),
                   jax.ShapeDtypeStruct((b,) + state_shape, F32)],
        scratch_shapes=[pltpu.VMEM((2, t, GROUP_W), F32)]
                       + [pltpu.VMEM((2, t, GROUP_W), BF16)] * 4
                       + [pltpu.VMEM((2, (t // c) * SUBLANES, GROUP_W), F32),
                          pltpu.VMEM((t, GROUP_W), F32), pltpu.VMEM((t, GROUP_W), F32),
                          pltpu.VMEM(state_shape, F32)],
        compiler_params=_params(("arbitrary",)),
        name="delta_mixer",
    )(p, *consts, s0)


def _s5_kernel(t, tc, u_ref, b_ref, cm_ref, lbar_ref, dskip_ref, gluw_ref, glub_ref, h0_ref,
               y_ref, sout_ref, x_ref, yf_ref, yb_ref, carry_ref):
    n_chunks = t // tc
    n_lane_blocks = S5_STATE // LANES
    carry_ref[...] = h0_ref[0]
    sub8 = lax.broadcasted_iota(jnp.int32, (SUBLANES, LANES), 0)

    def scan_dirs(rows_of, y_refs):
        for d in range(2):
            u16 = u_ref[0, rows_of[d], :].astype(BF16)
            for j in range(n_lane_blocks):
                x_ref[d, j] = _dot(u16, b_ref[d, j])
            y_refs[d][rows_of[d], :] = jnp.zeros((tc, GROUP_W), F32)

        def scan_block(d, j):
            bu = x_ref[d, j]
            xr, xi = bu[:, 0:LANES], bu[:, LANES:2 * LANES]
            lb = lbar_ref[d, j]
            pw = [(lb[0:1, :], lb[1:2, :])]
            for _ in range(SUBLANES - 1):
                pr, pi = pw[-1]
                pw.append((pr * pw[0][0] - pi * pw[0][1], pr * pw[0][1] + pi * pw[0][0]))
            n_groups = tc // SUBLANES
            blocks_r = [xr[v * SUBLANES:(v + 1) * SUBLANES] for v in range(n_groups)]
            blocks_i = [xi[v * SUBLANES:(v + 1) * SUBLANES] for v in range(n_groups)]
            for s in range(int(np.log2(SUBLANES))):
                sh = 1 << s
                keep = sub8 < SUBLANES - sh if d else sub8 >= sh
                m_re, m_im = jnp.where(keep, pw[sh - 1][0], 0.0), jnp.where(keep, pw[sh - 1][1], 0.0)
                rot = SUBLANES - sh if d else sh
                for v in range(n_groups):
                    sr, si = pltpu.roll(blocks_r[v], rot, axis=0), pltpu.roll(blocks_i[v], rot, axis=0)
                    blocks_r[v] = blocks_r[v] + m_re * sr - m_im * si
                    blocks_i[v] = blocks_i[v] + m_re * si + m_im * sr
            p_re = jnp.zeros((SUBLANES, LANES), F32)
            p_im = jnp.zeros((SUBLANES, LANES), F32)
            for r in range(SUBLANES):
                pr, pi = pw[SUBLANES - 1 - r] if d else pw[r]
                p_re = jnp.where(sub8 == r, pr, p_re)
                p_im = jnp.where(sub8 == r, pi, p_im)
            cr, ci = carry_ref[d, 0, j], carry_ref[d, 1, j]
            last = 0 if d else SUBLANES - 1
            for step in range(n_groups):
                v = n_groups - 1 - step if d else step
                blocks_r[v] = blocks_r[v] + (p_re * cr - p_im * ci)
                blocks_i[v] = blocks_i[v] + (p_re * ci + p_im * cr)
                cr, ci = blocks_r[v][last:last + 1, :], blocks_i[v][last:last + 1, :]
            carry_ref[d, 0, j] = cr
            carry_ref[d, 1, j] = ci
            return jnp.concatenate([jnp.concatenate(blocks_r, axis=0), jnp.concatenate(blocks_i, axis=0)],
                                   axis=1).astype(BF16)

        def lane_block(j, carry):
            states = [scan_block(d, j) for d in range(2)]
            for d in range(2):
                y_refs[d][rows_of[d], :] += _dot(states[d], cm_ref[j])
            return carry

        lax.fori_loop(0, n_lane_blocks, lane_block, 0)

    def body(n, carry):
        scan_dirs((pl.ds(pl.multiple_of(n * tc, tc), tc), pl.ds(pl.multiple_of((n_chunks - 1 - n) * tc, tc), tc)),
                  (yf_ref, yb_ref))
        return carry

    lax.fori_loop(0, n_chunks, body, 0)

    def fin(i, carry):
        rows = pl.ds(pl.multiple_of(i * tc, tc), tc)
        u = u_ref[0, rows, :]
        y = _gelu_tanh(yf_ref[rows, :] + yb_ref[rows, :] + dskip_ref[...] * u)
        z = _dot(y.astype(BF16), gluw_ref[...]) + glub_ref[...]
        y_ref[0, rows, :] = (y * _sigmoid(z)).astype(BF16)
        return carry

    lax.fori_loop(0, n_chunks, fin, 0)
    sout_ref[0] = carry_ref[...]


def _s5_mixer(u, b_in, cmat, lbar, dskip, glu_w, glu_b, h0):
    b, t, _ = u.shape
    tc = min(S5_CHUNK, t)
    nlb = S5_STATE // LANES
    consts = [b_in, cmat, lbar, dskip, glu_w, glu_b]
    st_shape = (2, 2, nlb, 1, LANES)
    return pl.pallas_call(
        functools.partial(_s5_kernel, t, tc),
        grid=(b,),
        in_specs=[pl.BlockSpec((1, t, GROUP_W), lambda i: (i, 0, 0))]
                 + [_full_spec(a.shape) for a in consts]
                 + [pl.BlockSpec((1,) + st_shape, lambda i: (i, 0, 0, 0, 0, 0))],
        out_specs=[pl.BlockSpec((1, t, GROUP_W), lambda i: (i, 0, 0)),
                   pl.BlockSpec((1,) + st_shape, lambda i: (i, 0, 0, 0, 0, 0))],
        out_shape=[jax.ShapeDtypeStruct((b, t, GROUP_W), BF16),
                   jax.ShapeDtypeStruct((b,) + st_shape, F32)],
        scratch_shapes=[pltpu.VMEM((2, nlb, tc, 2 * LANES), F32), pltpu.VMEM((t, GROUP_W), F32),
                        pltpu.VMEM((t, GROUP_W), F32), pltpu.VMEM(st_shape, F32)],
        compiler_params=_params(("arbitrary",)),
        name="s5_mixer",
    )(u, *consts, h0)


def _outproj_kernel(x_ref, mod_ref, ya_ref, yb_ref, yc_ref, yd_ref, w_ref, o_ref):
    y = (_dot(ya_ref[0], w_ref[0]) + _dot(yb_ref[0], w_ref[1])
         + _dot(yc_ref[0], w_ref[2]) + _dot(yd_ref[0], w_ref[3]))
    o_ref[0] = x_ref[0] + mod_ref[0][:, 2 * D_MODEL:3 * D_MODEL] * y


def _out_projection(x, mod, ys, w_out):
    b, t, _ = x.shape
    tb = min(TOKEN_BLOCK, t)
    tok = lambda n: pl.BlockSpec((1, tb, n), lambda i, j: (i, j, 0))
    return pl.pallas_call(
        _outproj_kernel,
        grid=(b, t // tb),
        in_specs=[tok(D_MODEL), pl.BlockSpec((1, 1, 6 * D_MODEL), lambda i, j: (i, 0, 0))]
                 + [tok(GROUP_W)] * 4 + [_full_spec(w_out.shape)],
        out_specs=tok(D_MODEL),
        out_shape=jax.ShapeDtypeStruct(x.shape, F32),
        compiler_params=_params(("arbitrary", "arbitrary")),
        name="out_projection",
    )(x, mod, *ys, w_out)


def _topk_rows(scores, k, row_ids):
    n = scores[0].shape[1]
    slot = lax.broadcasted_iota(jnp.int32, (k, n), 0)
    scores = list(scores)
    vals = [jnp.zeros((k, n), F32) for _ in scores]
    ids = [jnp.zeros((k, n), F32) for _ in scores]
    for it in range(k):
        ms = [jnp.max(s, axis=0, keepdims=True) for s in scores]
        idxs = [jnp.min(jnp.where(s == m, row_ids, 1e9), axis=0, keepdims=True) for s, m in zip(scores, ms)]
        vals = [jnp.where(slot == it, m, v) for m, v in zip(ms, vals)]
        ids = [jnp.where(slot == it, idx, i) for idx, i in zip(idxs, ids)]
        scores = [jnp.where(row_ids == idx, -jnp.inf, s) for s, idx in zip(scores, idxs)]
    return list(zip(vals, ids))


def _sorting_network(n):
    pairs = []

    def merge(lo, hi, r):
        step = r * 2
        if step < hi - lo:
            merge(lo, hi, step)
            merge(lo + r, hi, step)
            pairs.extend((i, i + r) for i in range(lo + r, hi - r, step))
        else:
            pairs.append((lo, lo + r))

    def sort(lo, hi):
        if hi > lo:
            mid = lo + (hi - lo) // 2
            sort(lo, mid)
            sort(mid + 1, hi)
            merge(lo, hi, 1)

    sort(0, n - 1)
    return pairs


def _lex_first(va, ia, vb, ib):
    tie = va == vb
    return jnp.where(tie, -ia, va) > jnp.where(tie, -ib, vb)


def _topk_columns(scores, k, row_ids):
    n_wires = scores[0].shape[0] // SUBLANES
    n = scores[0].shape[1]
    assert n_wires == k
    net = _sorting_network(n_wires)
    slot = lax.broadcasted_iota(jnp.int32, (k, n), 0)
    sub = lambda x, w: x[w * SUBLANES:(w + 1) * SUBLANES]
    vs = [[sub(s, w) for w in range(n_wires)] for s in scores]
    ids = [[sub(row_ids, w) for w in range(n_wires)] for _ in scores]
    for i, j in net:
        for v, d in zip(vs, ids):
            first = _lex_first(v[i], d[i], v[j], d[j])
            v[i], v[j] = jnp.where(first, v[i], v[j]), jnp.where(first, v[j], v[i])
            d[i], d[j] = jnp.where(first, d[i], d[j]), jnp.where(first, d[j], d[i])
    vals = [jnp.zeros((k, n), F32) for _ in scores]
    wins = [jnp.zeros((k, n), F32) for _ in scores]
    for it in range(k):
        for p, (v, d) in enumerate(zip(vs, ids)):
            hv, hd = v[0], d[0]
            for sh in (4, 2, 1):
                pv, pd = pltpu.roll(hv, sh, axis=0), pltpu.roll(hd, sh, axis=0)
                first = _lex_first(hv, hd, pv, pd)
                hv, hd = jnp.where(first, hv, pv), jnp.where(first, hd, pd)
            vals[p] = jnp.where(slot == it, hv[0:1, :], vals[p])
            wins[p] = jnp.where(slot == it, hd[0:1, :], wins[p])
            popped = d[0] == hd
            for w in range(k - 1 - it):
                v[w] = jnp.where(popped, v[w + 1], v[w])
                d[w] = jnp.where(popped, d[w + 1], d[w])
    return list(zip(vals, wins))


def _staircase():
    pairs = [(i, j) for i in range(PEER_TOPK) for j in range(PEER_TOPK) if (i + 1) * (j + 1) <= PEER_TOPK]
    rows = -(-len(pairs) // SUBLANES) * SUBLANES
    sel = np.zeros((2, rows, PEER_TOPK), np.float32)
    aux = np.zeros((2, rows, LANES), np.float32)
    aux[0, len(pairs):] = -np.inf
    aux[1, len(pairs):] = 1e9
    for r, (i, j) in enumerate(pairs):
        sel[0, r, i] = 1.0
        sel[1, r, j] = 1.0
        aux[1, r] = i * PEER_TOPK + j
    return jnp.asarray(np.tile(sel, (1, 1, 3)), BF16), jnp.asarray(aux, F32)


def _take_rows(table, idx, k):
    out = jnp.zeros(idx.shape, F32)
    for r in range(k):
        out = jnp.where(idx == float(r), table[r:r + 1, :], out)
    return out


def _peer_route_kernel(tb, x_ref, mod_ref, g_ref, wqh_ref, wql_ref, kh_ref, kl_ref, stsel_ref, staux_ref,
                       h_ref, a_ref, c_ref, gate_ref, qt_ref, at_ref, ct_ref, gt_ref):
    mod = mod_ref[0]
    h = _rms_modulate(x_ref[0], g_ref[...], mod[:, 3 * D_MODEL:4 * D_MODEL], mod[:, 4 * D_MODEL:5 * D_MODEL])
    h_hi, h_lo = _split2(h)
    h_ref[0] = h_hi
    wq_hi = wqh_ref[...]
    qt_ref[...] = _dot_nt(wq_hi, h_hi) + _dot_nt(wq_hi, h_lo) + _dot_nt(wql_ref[...], h_hi)
    half = PEER_DQ // 2
    key_ids = lax.broadcasted_iota(jnp.int32, (PEER_NKEYS, LANES), 0).astype(F32)

    heads_per_iter = 4

    def head_body(it, carry):
        heads = [it * heads_per_iter + i for i in range(heads_per_iter)]
        tiles = [slice(lt * LANES, (lt + 1) * LANES) for lt in range(tb // LANES)]
        probs = [(head, lanes) for head in heads for lanes in tiles]
        scores = []
        for head, lanes in probs:
            for p in range(2):
                hp = head * 2 + p
                q_hi, q_lo = _split2(qt_ref[pl.ds(pl.multiple_of(hp * half, half), half), lanes])
                scores.append(_dot(kh_ref[hp], q_hi) + _dot(kh_ref[hp], q_lo) + _dot(kl_ref[hp], q_hi))
        top = _topk_columns(scores, PEER_TOPK, key_ids)
        cands = [_sel_dot(stsel_ref[0], top[2 * i][0]) + _sel_dot(stsel_ref[1], top[2 * i + 1][0]) + staux_ref[0]
                 for i in range(len(probs))]
        best = _topk_rows(cands, PEER_TOPK, staux_ref[1])
        for i, (head, lanes) in enumerate(probs):
            out_rows = pl.ds(pl.multiple_of(head * PEER_TOPK, PEER_TOPK), PEER_TOPK)
            sc, flat = best[i]
            r1 = jnp.floor(flat * (1.0 / PEER_TOPK))
            r2 = flat - r1 * PEER_TOPK
            e = jnp.exp(sc - sc[0:1, :])
            at_ref[out_rows, lanes] = _take_rows(top[2 * i][1], r1, PEER_TOPK)
            ct_ref[out_rows, lanes] = _take_rows(top[2 * i + 1][1], r2, PEER_TOPK)
            gt_ref[out_rows, lanes] = e / jnp.sum(e, axis=0, keepdims=True)
        return carry

    lax.fori_loop(0, PEER_HEADS // heads_per_iter, head_body, 0)
    a_ref[0] = at_ref[...].T.astype(jnp.int32)
    c_ref[0] = ct_ref[...].T.astype(jnp.int32)
    gate_ref[0] = gt_ref[...].T


def _peer_route(x, mod, gain, wq_hi, wq_lo, keys_hi, keys_lo):
    b, t, _ = x.shape
    tb = min(TOKEN_BLOCK, t)
    tok = lambda n: pl.BlockSpec((1, tb, n), lambda i, j: (i, j, 0))
    consts = [gain, wq_hi, wq_lo, keys_hi, keys_lo, *_staircase()]
    return pl.pallas_call(
        functools.partial(_peer_route_kernel, tb),
        grid=(b, t // tb),
        scratch_shapes=[pltpu.VMEM((PEER_HEADS * PEER_DQ, tb), F32)] + [pltpu.VMEM((PEER_SEL, tb), F32)] * 3,
        in_specs=[tok(D_MODEL), pl.BlockSpec((1, 1, 6 * D_MODEL), lambda i, j: (i, 0, 0))]
                 + [_full_spec(a.shape) for a in consts],
        out_specs=[tok(D_MODEL), tok(PEER_SEL), tok(PEER_SEL), tok(PEER_SEL)],
        out_shape=[jax.ShapeDtypeStruct((b, t, D_MODEL), BF16),
                   jax.ShapeDtypeStruct((b, t, PEER_SEL), jnp.int32),
                   jax.ShapeDtypeStruct((b, t, PEER_SEL), jnp.int32),
                   jax.ShapeDtypeStruct((b, t, PEER_SEL), F32)],
        compiler_params=_params(("arbitrary", "arbitrary")),
        name="peer_route",
    )(x, mod, *consts)


def _peer_expert_kernel(tb, n_slabs, x_ref, mod_ref, h_ref, a_ref, c_ref, gate_ref, ut_ref, v_ref,
                        o_ref, gmat_ref, acc_ref):
    j = pl.program_id(2)
    nk = PEER_NKEYS

    @pl.when(j == 0)
    def _():
        acc_ref[...] = jnp.zeros_like(acc_ref)
        key_id = lax.broadcasted_iota(jnp.int32, (nk, PEER_SEL), 0)

        pack = 2 * SUBLANES
        groups = 2

        def build(nb, carry):
            n0 = pl.multiple_of(nb * groups * pack, groups * pack)
            a_rows = a_ref[0, pl.ds(n0, groups * pack), :]
            c_rows = c_ref[0, pl.ds(n0, groups * pack), :]
            g_rows = gate_ref[0, pl.ds(n0, groups * pack), :]
            mats = []
            for r in range(groups * pack):
                row_sel = jnp.where(key_id == a_rows[r:r + 1], 1.0, 0.0).astype(BF16)
                col_sel = jnp.where(key_id == c_rows[r:r + 1], g_rows[r:r + 1], 0.0).astype(BF16)
                mats.append(_dot_nt(row_sel, col_sel).astype(BF16))
            for gi in range(groups):
                stacked = jnp.stack(mats[gi * pack:(gi + 1) * pack], axis=0)
                gmat_ref[:, pl.ds(n0 + gi * pack, pack), :] = pltpu.einshape("nac->anc", stacked)
            return carry

        lax.fori_loop(0, tb // (groups * pack), build, 0)

    h = h_ref[0]
    n_pairs = n_slabs // 2
    cols = [slice(i * 2 * nk, (i + 1) * 2 * nk) for i in range(n_pairs)]
    lookahead = 2
    s = [_dot(h, ut_ref[:, cols[i]]) for i in range(min(lookahead, n_pairs))]
    acc = None
    for i in range(n_pairs):
        if i + lookahead < n_pairs:
            s.append(_dot(h, ut_ref[:, cols[i + lookahead]]))
        a0 = j * n_slabs + 2 * i
        g = jnp.concatenate([gmat_ref[a0], gmat_ref[a0 + 1]], axis=1)
        w = g * _gelu_tanh(s[i]).astype(BF16)
        p = _dot(w, v_ref[cols[i], :])
        acc = p if acc is None else acc + p
    acc_ref[...] += acc

    @pl.when(j == pl.num_programs(2) - 1)
    def _():
        o_ref[0] = x_ref[0] + mod_ref[0][:, 5 * D_MODEL:6 * D_MODEL] * acc_ref[...]


def _peer_experts(x, mod, h, a, c, gate, u_t, v_tab):
    b, t, _ = x.shape
    tb = min(PEER_TOKEN_BLOCK, t)
    n_exp = u_t.shape[1]
    eb = PEER_EXPERT_BLOCK
    n_slabs = eb // PEER_NKEYS
    tok = lambda n: pl.BlockSpec((1, tb, n), lambda i, k, j: (i, k, 0))
    return pl.pallas_call(
        functools.partial(_peer_expert_kernel, tb, n_slabs),
        grid=(b, t // tb, n_exp // eb),
        in_specs=[tok(D_MODEL), pl.BlockSpec((1, 1, 6 * D_MODEL), lambda i, k, j: (i, 0, 0)),
                  tok(D_MODEL), tok(PEER_SEL), tok(PEER_SEL), tok(PEER_SEL),
                  pl.BlockSpec((D_MODEL, eb), lambda i, k, j: (0, j)),
                  pl.BlockSpec((eb, D_MODEL), lambda i, k, j: (j, 0))],
        out_specs=tok(D_MODEL),
        out_shape=jax.ShapeDtypeStruct(x.shape, F32),
        scratch_shapes=[pltpu.VMEM((PEER_NKEYS, tb, PEER_NKEYS), BF16), pltpu.VMEM((tb, D_MODEL), F32)],
        compiler_params=_params(("arbitrary", "arbitrary", "arbitrary")),
        name="peer_experts",
    )(x, mod, h, a, c, gate, u_t, v_tab)


def _final_norm_kernel(x_ref, g_ref, o_ref):
    x = x_ref[0]
    ms = jnp.mean(x * x, axis=-1, keepdims=True)
    o_ref[0] = x * lax.rsqrt(ms + EPS) * g_ref[...]


def _final_norm(x, gain):
    b, t, _ = x.shape
    tb = min(TOKEN_BLOCK, t)
    tok = pl.BlockSpec((1, tb, D_MODEL), lambda i, j: (i, j, 0))
    return pl.pallas_call(
        _final_norm_kernel,
        grid=(b, t // tb),
        in_specs=[tok, _full_spec((1, D_MODEL))],
        out_specs=tok,
        out_shape=jax.ShapeDtypeStruct(x.shape, F32),
        compiler_params=_params(("arbitrary", "arbitrary")),
        name="final_norm",
    )(x, gain)


def _hi_lo(w):
    hi = w.astype(BF16)
    return hi, (w - hi.astype(F32)).astype(BF16)


def _pad_cols(w, n):
    return jnp.pad(w, ((0, 0), (0, n - w.shape[1])))


def _layer_weights(l, w_in, w_out, gla_gk_up, gla_gk_bias, gla_onorm_g, dn_conv_w, dn_A_log, dn_dt_bias,
                   dn_onorm_g, s5_lambda_re, s5_lambda_im, s5_log_dt, s5_B_re, s5_B_im, s5_C_re, s5_C_im,
                   s5_D, s5_glu_w, s5_glu_b, lower_bound, hgrn_onorm_g, peer_wq, peer_keys, peer_u, peer_v):
    gw = GROUP_W
    sizes = (gw, gw, gw, gw, 2 * GLA_LOWRANK, gw, gw, gw, gw, 2 * N_HEADS, 2 * N_HEADS, gw, gw, gw, gw, 2 * gw)
    offs = np.concatenate([[0], np.cumsum(sizes)])
    col = lambda a, b: w_in[l][:, int(offs[a]):int(offs[b])]
    wl = {}
    wl["w_gla"] = _pad_cols(col(0, 5), GLA_W).astype(BF16)
    wl["w_dn"] = _pad_cols(col(5, 11), DN_W).astype(BF16)
    wl["w_s5"] = col(11, 12).astype(BF16)
    wl["w_hg"] = col(12, 16).astype(BF16)
    wl["w_out"] = w_out[l].reshape(4, gw, D_MODEL).astype(BF16)
    tile_head = lambda g: jnp.tile(g, N_HEADS).reshape(1, gw)
    up = jnp.zeros((2, LANES, gw), F32)
    for d in range(2):
        up = up.at[d, d * GLA_LOWRANK:(d + 1) * GLA_LOWRANK, :].set(gla_gk_up[l, d])
    wl["gla_up_hi"], wl["gla_up_lo"] = _hi_lo(up)
    wl["gla_bias"] = gla_gk_bias[l].reshape(2, 1, gw)
    wl["gla_onorm"] = tile_head(gla_onorm_g[l])
    expand = np.zeros((2, LANES, 2 * gw), np.float32)
    for d in range(2):
        for h in range(N_HEADS):
            expand[d, d * N_HEADS + h, h * HEAD_DIM:(h + 1) * HEAD_DIM] = 1.0
            expand[d, 2 * N_HEADS + d * N_HEADS + h, gw + h * HEAD_DIM:gw + (h + 1) * HEAD_DIM] = 1.0
    wl["dn_expand"] = jnp.asarray(expand, BF16)
    wl["dn_conv"] = jnp.pad(dn_conv_w[l], ((0, SUBLANES - 3), (0, 0)))
    wl["dn_nega"] = jnp.repeat(-jnp.exp(dn_A_log[l]), HEAD_DIM, axis=-1).reshape(2, 1, gw)
    wl["dn_dtb"] = jnp.repeat(dn_dt_bias[l], HEAD_DIM, axis=-1).reshape(2, 1, gw)
    wl["dn_onorm"] = tile_head(dn_onorm_g[l])
    nlb = S5_STATE // LANES
    lam_re, lam_im = s5_lambda_re[l].astype(F32), s5_lambda_im[l].astype(F32)
    dt = jnp.exp(s5_log_dt[l].astype(F32))[:, :, None]
    mag = jnp.exp(lam_re * dt)
    lbar_re, lbar_im = mag * jnp.cos(lam_im * dt), mag * jnp.sin(lam_im * dt)
    den = lam_re * lam_re + lam_im * lam_im
    num_re, num_im = lbar_re - 1.0, lbar_im
    coef_re = (num_re * lam_re + num_im * lam_im) / den
    coef_im = (num_im * lam_re - num_re * lam_im) / den
    b_re, b_im = s5_B_re[l].astype(F32), s5_B_im[l].astype(F32)
    bb_re = coef_re[..., None] * b_re - coef_im[..., None] * b_im
    bb_im = coef_re[..., None] * b_im + coef_im[..., None] * b_re
    eye_g = jnp.eye(S5_G, dtype=F32)
    bd_in = lambda m: jnp.einsum('dgpc,gk->dgckp', m, eye_g).reshape(2, gw, S5_STATE)
    blk = lambda m: m.reshape(2, gw, nlb, LANES).transpose(0, 2, 1, 3)
    b_cat = jnp.concatenate([blk(bd_in(bb_re)), blk(bd_in(bb_im))], axis=-1)
    wl["s5_b"] = b_cat.astype(BF16)
    bd_out = lambda m: jnp.einsum('gcp,gk->gpkc', m.astype(F32), eye_g).reshape(S5_STATE, gw)
    c_re, c_im = bd_out(s5_C_re[l]), bd_out(s5_C_im[l])
    wl["s5_c"] = jnp.concatenate([c_re.reshape(nlb, LANES, gw), -c_im.reshape(nlb, LANES, gw)],
                                 axis=1).astype(BF16)
    wl["s5_lbar"] = jnp.stack([lbar_re.reshape(2, nlb, LANES), lbar_im.reshape(2, nlb, LANES)], axis=2)
    wl["s5_d"] = s5_D[l].reshape(1, gw)
    wl["s5_glu_w"] = s5_glu_w[l].astype(BF16)
    wl["s5_glu_b"] = s5_glu_b[l].reshape(1, gw)
    wl["hg_loglb"] = jnp.log(lower_bound).reshape(1, gw)
    wl["hg_log1m"] = jnp.log1p(-lower_bound).reshape(1, gw)
    wl["hg_1m"] = (1.0 - lower_bound).reshape(1, gw)
    wl["hg_onorm"] = tile_head(hgrn_onorm_g[l])
    wl["wq_hi"], wl["wq_lo"] = _hi_lo(peer_wq[l].T)
    wl["keys_hi"], wl["keys_lo"] = _hi_lo(peer_keys[l].reshape(PEER_HEADS * 2, PEER_NKEYS, PEER_DQ // 2))
    wl["u_t"] = peer_u[l].astype(BF16).T
    wl["v_tab"] = peer_v[l].astype(BF16)
    return wl


def _heads_to_lanes(s, transposed):
    perm = (0, 1, 4, 2, 3) if transposed else (0, 1, 3, 2, 4)
    return s.astype(F32).transpose(perm).reshape(s.shape[0], 2, HEAD_DIM, GROUP_W)


def _lanes_to_heads(s, transposed):
    s = s.reshape(s.shape[0], 2, HEAD_DIM, N_HEADS, HEAD_DIM)
    return s.transpose((0, 1, 3, 4, 2) if transposed else (0, 1, 3, 2, 4))


def _to_column_major(x):
    b, n, d = x.shape
    return x.reshape(b, n // GRID_W, GRID_W, d).transpose(0, 2, 1, 3).reshape(b, n, d)


def _to_row_major(x):
    b, n, d = x.shape
    return x.reshape(b, GRID_W, n // GRID_W, d).transpose(0, 2, 1, 3).reshape(b, n, d)


def _trunk_layer(x, mod, wl, norm1_g, norm2_g, states, shared_mod):
    st_gla, st_dn, st_s5, st_hg = states
    p_gla, p_dn, p_s5, p_hg = _in_projection(x, mod, norm1_g, [wl["w_gla"], wl["w_dn"], wl["w_s5"], wl["w_hg"]])
    y_a, n_gla = _decay_mixer("gla", p_gla, wl["gla_up_hi"], wl["gla_up_lo"], wl["gla_bias"], wl["gla_onorm"], st_gla)
    y_b, n_dn = _delta_mixer(p_dn, wl["dn_conv"], wl["dn_expand"], wl["dn_nega"], wl["dn_dtb"], wl["dn_onorm"], st_dn)
    y_c, n_s5 = _s5_mixer(p_s5, wl["s5_b"], wl["s5_c"], wl["s5_lbar"], wl["s5_d"],
                          wl["s5_glu_w"], wl["s5_glu_b"], st_s5)
    y_d, n_hg = _decay_mixer("hgrn", p_hg, wl["hg_loglb"], wl["hg_log1m"], wl["hg_1m"], wl["hg_onorm"], st_hg)
    x = _out_projection(x, mod, [y_a, y_b, y_c, y_d], wl["w_out"])
    h, a, c, gate = _peer_route(x, mod, norm2_g, wl["wq_hi"], wl["wq_lo"], wl["keys_hi"], wl["keys_lo"])
    b, t, _ = x.shape
    fold = max(1, min(PEER_TOKEN_BLOCK // t, b)) if shared_mod else 1
    fb = lambda z: z.reshape(b // fold, fold * t, z.shape[-1])
    x = _peer_experts(fb(x), mod[0:b // fold], fb(h), fb(a), fb(c), fb(gate), wl["u_t"], wl["v_tab"]).reshape(b, t, -1)
    return x, (n_gla, n_dn, n_s5, n_hg)


def kernel(x_prompt, x_sample, state_gla, state_delta, state_s5_re, state_s5_im, state_hgrn, c, c_ctx,
           mod_w, mod_b, norm1_g, norm2_g, w_in, w_out, gla_gk_up, gla_gk_bias, gla_onorm_g,
           dn_conv_w, dn_A_log, dn_dt_bias, dn_onorm_g, s5_lambda_re, s5_lambda_im, s5_log_dt,
           s5_B_re, s5_B_im, s5_C_re, s5_C_im, s5_D, s5_glu_w, s5_glu_b, hgrn_lb_logits,
           hgrn_onorm_g, peer_wq, peer_keys, peer_u, peer_v, final_norm_g):
    bp, bs = x_prompt.shape[0], x_sample.shape[0]
    nlb = S5_STATE // LANES
    lb_cum = jnp.cumsum(jax.nn.softmax(hgrn_lb_logits.astype(F32), axis=0), axis=0)
    lower_bounds = lb_cum - lb_cum[0:1]

    rows = 2 * SUBLANES
    cc = jnp.concatenate([c, c_ctx[None, :], jnp.zeros((rows - bs - 1, D_MODEL), F32)], axis=0)
    mod_all = _modulation(cc, mod_w, mod_b)

    zeros_hh = jnp.zeros((bp, 2, HEAD_DIM, GROUP_W), F32)
    zeros_s5 = jnp.zeros((bp, 2, 2, nlb, 1, LANES), F32)

    xp, xs = x_prompt, x_sample
    ctx_states = []
    for l in range(DEPTH):
        wl = _layer_weights(l, w_in, w_out, gla_gk_up, gla_gk_bias, gla_onorm_g, dn_conv_w, dn_A_log,
                            dn_dt_bias, dn_onorm_g, s5_lambda_re, s5_lambda_im, s5_log_dt, s5_B_re, s5_B_im,
                            s5_C_re, s5_C_im, s5_D, s5_glu_w, s5_glu_b, lower_bounds[l], hgrn_onorm_g,
                            peer_wq, peer_keys, peer_u, peer_v)
        n1, n2 = norm1_g[l].reshape(1, D_MODEL), norm2_g[l].reshape(1, D_MODEL)
        mod_ctx = jnp.broadcast_to(mod_all[l, bs].reshape(1, 1, 6 * D_MODEL), (bp, 1, 6 * D_MODEL))
        mod_lat = mod_all[l, 0:bs].reshape(bs, 1, 6 * D_MODEL)
        xp, st_ctx = _trunk_layer(xp, mod_ctx, wl, n1, n2, (zeros_hh, zeros_hh, zeros_s5, zeros_hh), True)
        ctx_states.append(st_ctx)
        cached_s5 = jnp.stack([state_s5_re[:, l], state_s5_im[:, l]], axis=2).reshape(bs, 2, 2, nlb, 1, LANES)
        cached = (_heads_to_lanes(state_gla[:, l], True), _heads_to_lanes(state_delta[:, l], False), cached_s5,
                  _heads_to_lanes(state_hgrn[:, l], True))
        if l % 2 == 1:
            xs = _to_column_major(xs)
        xs, _ = _trunk_layer(xs, mod_lat, wl, n1, n2, cached, False)
        if l % 2 == 1:
            xs = _to_row_major(xs)

    gain = final_norm_g.reshape(1, D_MODEL)
    y_prompt = _final_norm(xp, gain)
    y_sample = _final_norm(xs, gain)
    new_gla = jnp.stack([_lanes_to_heads(s[0], True) for s in ctx_states], axis=1)
    new_dn = jnp.stack([_lanes_to_heads(s[1], False) for s in ctx_states], axis=1)
    s5 = jnp.stack([s[2].reshape(bp, 2, 2, S5_G, S5_P) for s in ctx_states], axis=1)
    new_hg = jnp.stack([_lanes_to_heads(s[3], True) for s in ctx_states], axis=1)
    return (y_prompt, y_sample, new_gla, new_dn, s5[:, :, :, 0], s5[:, :, :, 1], new_hg)
```

```python
import functools

import numpy as np
import jax
import jax.numpy as jnp
from jax import lax
from jax.experimental import pallas as pl
from jax.experimental.pallas import tpu as pltpu

F32 = jnp.float32
BF16 = jnp.bfloat16

D_MODEL = 1024
DEPTH = 2
GRID_W = 64
GROUP_W = 256
HEAD_DIM = 64
N_HEADS = GROUP_W // HEAD_DIM
GLA_LOWRANK = 16
GLA_GATE_NORM = 16.0
DN_CHUNK = 64
S5_GROUP = 16
S5_G = GROUP_W // S5_GROUP
S5_P = 64
S5_STATE = S5_G * S5_P
PEER_HEADS = 8
PEER_NKEYS = 128
PEER_DQ = 256
PEER_TOPK = 16
PEER_SEL = PEER_HEADS * PEER_TOPK
EPS = 1e-6

LANES = 128
SUBLANES = 8
VMEM_LIMIT = 56 * 1024 * 1024

GLA_CHUNK = 64
S5_CHUNK = 256
TOKEN_BLOCK = 256
PEER_TOKEN_BLOCK = 512
PEER_EXPERT_BLOCK = 2048
GLA_W = 4 * GROUP_W + LANES
DN_W = 4 * GROUP_W + LANES
HG_W = 5 * GROUP_W


def _dot(a, b):
    return jnp.dot(a, b, preferred_element_type=F32)


def _dot_nt(a, b):
    return lax.dot_general(a, b, (((1,), (1,)), ((), ())), preferred_element_type=F32)


def _dot_tn(a, b):
    return lax.dot_general(a, b, (((0,), (0,)), ((), ())), preferred_element_type=F32)


def _split2(x):
    hi = x.astype(BF16)
    lo = (x - hi.astype(F32)).astype(BF16)
    return hi, lo


def _split3(x):
    hi = x.astype(BF16)
    r = x - hi.astype(F32)
    mid = r.astype(BF16)
    lo = (r - mid.astype(F32)).astype(BF16)
    return hi, mid, lo


def _sel_dot(m01x3, x):
    return _dot(m01x3, jnp.concatenate(_split3(x), axis=0))


def _dot_sel(x, m01):
    hi, mid, lo = _split3(x)
    return _dot(hi, m01) + _dot(mid, m01) + _dot(lo, m01)


def _dot_hl(a, b_hi, b_lo):
    a_hi, a_lo = _split2(a)
    return _dot(a_hi, b_hi) + _dot(a_hi, b_lo) + _dot(a_lo, b_hi)


def _head_sum(x, ones_bd):
    hi, lo = _split2(x)
    return _dot(hi, ones_bd) + _dot(lo, ones_bd)


HEADS_PER_TILE = LANES // HEAD_DIM
N_HEAD_TILES = GROUP_W // LANES


def _head_blockdiag(x16, ones_bd):
    mask = ones_bd[0:LANES, 0:LANES]
    return [jnp.tile(x16[:, t * LANES:(t + 1) * LANES], (HEADS_PER_TILE, 1)) * mask for t in range(N_HEAD_TILES)]


def _bd_dot(a16, bds):
    return jnp.concatenate([_dot(a16[:, t * LANES:(t + 1) * LANES], bd) for t, bd in enumerate(bds)], axis=1)


def _bd_dot_nt(a16, bds):
    return jnp.concatenate([_dot_nt(a16[:, t * LANES:(t + 1) * LANES], bd) for t, bd in enumerate(bds)], axis=1)


def _per_head_mm(a, x, ones_bd):
    a_hi, a_lo = _split2(a)
    x_hi, x_lo = _split2(x)
    rows = a.shape[0]
    both = _bd_dot(jnp.concatenate([a_hi, a_lo], axis=0), _head_blockdiag(x_hi, ones_bd))
    return both[0:rows] + both[rows:2 * rows] + _bd_dot(a_hi, _head_blockdiag(x_lo, ones_bd))


def _per_head_mm1(a, x, ones_bd):
    return _bd_dot(a.astype(BF16), _head_blockdiag(x.astype(BF16), ones_bd))


def _per_head_dot_tn(a16, b16):
    lane_head = lax.broadcasted_iota(jnp.int32, (HEAD_DIM, LANES), 1) // HEAD_DIM
    outs = []
    for t in range(N_HEAD_TILES):
        full = _dot_tn(a16[:, t * LANES:(t + 1) * LANES], b16[:, t * LANES:(t + 1) * LANES])
        out = None
        for h in range(HEADS_PER_TILE):
            blk = jnp.where(lane_head == h, full[h * HEAD_DIM:(h + 1) * HEAD_DIM, :], 0.0)
            out = blk if out is None else out + blk
        outs.append(out)
    return jnp.concatenate(outs, axis=1)


def _log_sigmoid(x):
    return jnp.minimum(x, 0.0) - jnp.log(1.0 + jnp.exp(-jnp.abs(x)))


def _softplus(x):
    return jnp.maximum(x, 0.0) + jnp.log(1.0 + jnp.exp(-jnp.abs(x)))


def _sigmoid(x):
    return 1.0 / (1.0 + jnp.exp(-x))


def _silu(x):
    return x * _sigmoid(x)


def _gelu_tanh(x):
    return 0.5 * x * (1.0 + jnp.tanh(0.7978845608028654 * (x + 0.044715 * x * x * x)))


def _rms_modulate(x, gain, shift, scale):
    ms = jnp.mean(x * x, axis=-1, keepdims=True)
    return x * lax.rsqrt(ms + EPS) * gain * (1.0 + scale) + shift


def _full_spec(shape):
    zeros = (0,) * len(shape)
    return pl.BlockSpec(shape, lambda *_: zeros)


def _params(sem):
    return pltpu.CompilerParams(dimension_semantics=sem, vmem_limit_bytes=VMEM_LIMIT)


def _scan_consts(c, reverse):
    levels = int(np.log2(c))
    idx = np.arange(c)
    i, t = idx[:, None], idx[None, :]
    sel, masks = [], []
    for lv in range(levels):
        s = 1 << lv
        right = (idx // s) % 2 == 1
        same_blk = (i // s) == (t // s)
        sel.append(np.where(right[:, None], same_blk & (t <= i), same_blk & (t > i)))
        same_parent = (i // (2 * s)) == (t // (2 * s))
        masks.append(same_parent & right[:, None] & ~right[None, :])
    sel.append(t <= i)
    sel.append(t > i)
    masks.append(i == t)
    if reverse:
        sel = [m[::-1, ::-1] for m in sel]
        masks = [m[::-1, ::-1] for m in masks]
    sel.append(np.ones((SUBLANES, c), bool))
    sel = np.tile(np.concatenate(sel, axis=0).astype(np.float32), (1, 3))
    masks = np.tile(np.stack(masks).astype(np.float32), (1, 1, N_HEADS))
    return jnp.asarray(sel, BF16), jnp.asarray(masks, F32)


def _dn_consts(c, reverse):
    levels = int(np.log2(c))
    idx = np.arange(c)
    i, t = idx[:, None], idx[None, :]
    incl, tail = t <= i, t > i
    masks = []
    for lv in range(levels):
        s = 1 << lv
        right = (idx // s) % 2 == 1
        masks.append(((i // (2 * s)) == (t // (2 * s))) & right[:, None] & ~right[None, :])
    masks += [incl, incl & (i != t), i == t]
    if reverse:
        incl, tail = incl[::-1, ::-1], tail[::-1, ::-1]
        masks = [m[::-1, ::-1] for m in masks]
    sel = np.tile(np.concatenate([incl, tail, np.ones((SUBLANES, c), bool)], axis=0).astype(np.float32), (1, 3))
    masks = np.tile(np.stack(masks).astype(np.float32), (1, 1, N_HEADS))
    return jnp.asarray(sel, BF16), jnp.asarray(masks, F32)


def _ones_blockdiag():
    h = np.arange(GROUP_W) // HEAD_DIM
    return jnp.asarray((h[:, None] == h[None, :]).astype(np.float32), BF16)


def _mod_kernel(c_ref, w_ref, b_ref, o_ref):
    a_hi, a_lo = _split2(_silu(c_ref[...]))
    w_hi, w_lo = _split2(w_ref[0])
    o_ref[0] = _dot(a_hi, w_hi) + _dot(a_hi, w_lo) + _dot(a_lo, w_hi) + b_ref[0]


def _modulation(cc, mod_w, mod_b):
    rows = cc.shape[0]
    nb = 6
    return pl.pallas_call(
        _mod_kernel,
        grid=(DEPTH, nb),
        in_specs=[pl.BlockSpec((rows, D_MODEL), lambda l, j: (0, 0)),
                  pl.BlockSpec((1, D_MODEL, D_MODEL), lambda l, j: (l, 0, j)),
                  pl.BlockSpec((1, 1, D_MODEL), lambda l, j: (l, 0, j))],
        out_specs=pl.BlockSpec((1, rows, D_MODEL), lambda l, j: (l, 0, j)),
        out_shape=jax.ShapeDtypeStruct((DEPTH, rows, 6 * D_MODEL), F32),
        compiler_params=_params(("arbitrary", "arbitrary")),
        name="modulation",
    )(cc, mod_w, mod_b.reshape(DEPTH, 1, 6 * D_MODEL))


def _inproj_kernel(x_ref, mod_ref, g_ref, wa_ref, wb_ref, wc_ref, wd_ref, oa_ref, ob_ref, oc_ref, od_ref):
    mod = mod_ref[0]
    h = _rms_modulate(x_ref[0], g_ref[...], mod[:, 0:D_MODEL], mod[:, D_MODEL:2 * D_MODEL]).astype(BF16)
    oa_ref[0] = _dot(h, wa_ref[...])
    ob_ref[0] = _dot(h, wb_ref[...])
    oc_ref[0] = _dot(h, wc_ref[...])
    od_ref[0] = _dot(h, wd_ref[...])


def _in_projection(x, mod, gain, weights):
    b, t, _ = x.shape
    tb = min(TOKEN_BLOCK, t)
    widths = [w.shape[1] for w in weights]
    return pl.pallas_call(
        _inproj_kernel,
        grid=(b, t // tb),
        in_specs=[pl.BlockSpec((1, tb, D_MODEL), lambda i, j: (i, j, 0)),
                  pl.BlockSpec((1, 1, 6 * D_MODEL), lambda i, j: (i, 0, 0)),
                  _full_spec((1, D_MODEL))] + [_full_spec(w.shape) for w in weights],
        out_specs=[pl.BlockSpec((1, tb, n), lambda i, j: (i, j, 0)) for n in widths],
        out_shape=[jax.ShapeDtypeStruct((b, t, n), F32) for n in widths],
        compiler_params=_params(("arbitrary", "arbitrary")),
        name="in_projection",
    )(x, mod, gain, *weights)


def _decay_scan_prepare(probs, ones_bd, c):
    levels = int(np.log2(c))
    exs = [jnp.exp(_sel_dot(pr["sel"][...], pr["g"])) for pr in probs]
    attns = [None] * len(probs)
    for lv in range(levels + 1):
        for i, (pr, ex) in enumerate(zip(probs, exs)):
            q, k = pr["q"], pr["k"]
            if lv < levels:
                e = ex[lv * c:(lv + 1) * c]
                qt, kt = (q * e).astype(BF16), (k * e).astype(BF16)
            else:
                qt, kt = q.astype(BF16), k.astype(BF16)
            p = _bd_dot_nt(qt, _head_blockdiag(kt, ones_bd)) * pr["mask"][lv]
            attns[i] = p if attns[i] is None else attns[i] + p
    res = []
    for pr, ex, attn in zip(probs, exs, attns):
        o = _bd_dot(attn.astype(BF16), _head_blockdiag(pr["v"].astype(BF16), ones_bd))
        qd = (pr["q"] * ex[levels * c:(levels + 1) * c]).astype(BF16)
        ktail = (pr["k"] * ex[(levels + 1) * c:(levels + 2) * c]).astype(BF16)
        res.append((o, qd, ktail, ex[(levels + 2) * c:(levels + 2) * c + SUBLANES]))
    return res


def _decay_scan_state_steps(items, ones_bd, state_ref):
    sts = [state_ref[it[0]] for it in items]
    outs = [it[1] + _bd_dot_nt(it[2], _head_blockdiag(st.astype(BF16), ones_bd)) for it, st in zip(items, sts)]
    for it, st in zip(items, sts):
        state_ref[it[0]] = st * it[5] + _per_head_dot_tn(it[4], it[3])
    return outs


def _head_norm_gate(o, gate, onorm, ones_bd):
    ms = _head_sum(o * o, ones_bd) * (1.0 / HEAD_DIM)
    return o * lax.rsqrt(ms + EPS) * onorm * _silu(gate)


def _decay_mixer_kernel(mode, t, c, p_ref, self_ref, selb_ref, maskf_ref, maskb_ref, ones_ref, onorm_ref,
                        pa_ref, pb_ref, pc_ref, h0_ref, y_ref, sout_ref,
                        qd_ref, kt_ref, v16_ref, dl_ref, of_ref, ob_ref, state_ref):
    n_chunks = t // c
    state_ref[...] = h0_ref[0]
    sels, masks, o_refs = (self_ref, selb_ref), (maskf_ref, maskb_ref), (of_ref, ob_ref)

    def inputs(rows, d):
        if mode == "gla":
            q = p_ref[0, rows, 0:GROUP_W] * (HEAD_DIM ** -0.5)
            k = p_ref[0, rows, GROUP_W:2 * GROUP_W]
            v = p_ref[0, rows, 2 * GROUP_W:3 * GROUP_W]
            low = p_ref[0, rows, 4 * GROUP_W:4 * GROUP_W + LANES]
            gk = _dot_hl(low, pa_ref[d], pb_ref[d]) + pc_ref[d]
            g = _log_sigmoid(gk) * (1.0 / GLA_GATE_NORM)
        else:
            q = _silu(p_ref[0, rows, 0:GROUP_W]) * (HEAD_DIM ** -0.5)
            v = p_ref[0, rows, GROUP_W:2 * GROUP_W]
            z = p_ref[0, rows, (3 + d) * GROUP_W:(4 + d) * GROUP_W]
            a = pa_ref[...]
            b = pb_ref[...] + _log_sigmoid(z)
            m = jnp.maximum(a, b)
            g = m + jnp.log(jnp.exp(a - m) + jnp.exp(b - m))
            k = pc_ref[...] * _sigmoid(-z)
        return q, k, v, g

    per_iter = 4 if n_chunks % 4 == 0 else 1

    def prepare(i, carry):
        chunks = [i * per_iter + j for j in range(per_iter)]
        probs = []
        for n in chunks:
            rows = pl.ds(pl.multiple_of(n * c, c), c)
            for d in range(2):
                q, k, v, g = inputs(rows, d)
                probs.append(dict(q=q, k=k, v=v, g=g, sel=sels[d], mask=masks[d]))
            v16_ref[rows, :] = probs[-1]["v"].astype(BF16)
        for idx, (o, qd, ktail, dlast) in enumerate(_decay_scan_prepare(probs, ones_ref[...], c)):
            n, d = chunks[idx // 2], idx % 2
            rows = pl.ds(pl.multiple_of(n * c, c), c)
            o_refs[d][rows, :] = o
            qd_ref[d, rows, :] = qd
            kt_ref[d, rows, :] = ktail
            dl_ref[d, pl.ds(pl.multiple_of(n * SUBLANES, SUBLANES), SUBLANES), :] = dlast
        return carry

    lax.fori_loop(0, n_chunks // per_iter, prepare, 0)

    def body(i, carry):
        items, rows_of = [], []
        for d, n in ((0, i), (1, n_chunks - 1 - i)):
            rows = pl.ds(pl.multiple_of(n * c, c), c)
            dlast = dl_ref[d, pl.ds(pl.multiple_of(n * SUBLANES, SUBLANES), SUBLANES), :][0:1, :]
            items.append((d, o_refs[d][rows, :], qd_ref[d, rows, :], kt_ref[d, rows, :], v16_ref[rows, :], dlast))
            rows_of.append(rows)
        o_f, o_b = _decay_scan_state_steps(items, ones_ref[...], state_ref)
        of_ref[rows_of[0], :] = o_f
        ob_ref[rows_of[1], :] = o_b
        return carry

    lax.fori_loop(0, n_chunks, body, 0)

    rbk = min(t, 256)
    gate_col = 3 * GROUP_W if mode == "gla" else 2 * GROUP_W

    def fin(i, carry):
        rows = pl.ds(pl.multiple_of(i * rbk, rbk), rbk)
        o = of_ref[rows, :] + ob_ref[rows, :]
        gate = p_ref[0, rows, gate_col:gate_col + GROUP_W]
        y_ref[0, rows, :] = _head_norm_gate(o, gate, onorm_ref[...], ones_ref[...]).astype(BF16)
        return carry

    lax.fori_loop(0, t // rbk, fin, 0)
    sout_ref[0] = state_ref[...]


def _decay_mixer(mode, p, pa, pb, pc, onorm, h0t):
    b, t, w = p.shape
    c = GLA_CHUNK
    sel_f, mask_f = _scan_consts(c, False)
    sel_b, mask_b = _scan_consts(c, True)
    ones_bd = _ones_blockdiag()
    consts = [sel_f, sel_b, mask_f, mask_b, ones_bd, onorm, pa, pb, pc]
    state_shape = (2, HEAD_DIM, GROUP_W)
    return pl.pallas_call(
        functools.partial(_decay_mixer_kernel, mode, t, c),
        grid=(b,),
        in_specs=[pl.BlockSpec((1, t, w), lambda i: (i, 0, 0))]
                 + [_full_spec(a.shape) for a in consts]
                 + [pl.BlockSpec((1,) + state_shape, lambda i: (i, 0, 0, 0))],
        out_specs=[pl.BlockSpec((1, t, GROUP_W), lambda i: (i, 0, 0)),
                   pl.BlockSpec((1,) + state_shape, lambda i: (i, 0, 0, 0))],
        out_shape=[jax.ShapeDtypeStruct((b, t, GROUP_W), BF16),
                   jax.ShapeDtypeStruct((b,) + state_shape, F32)],
        scratch_shapes=[pltpu.VMEM((2, t, GROUP_W), BF16), pltpu.VMEM((2, t, GROUP_W), BF16),
                        pltpu.VMEM((t, GROUP_W), BF16), pltpu.VMEM((2, (t // c) * SUBLANES, GROUP_W), F32),
                        pltpu.VMEM((t, GROUP_W), F32), pltpu.VMEM((t, GROUP_W), F32),
                        pltpu.VMEM(state_shape, F32)],
        compiler_params=_params(("arbitrary",)),
        name=mode + "_mixer",
    )(p, *consts, h0t)


def _delta_prepare(probs, ones_bd, expand_ref, nega_ref, dtb_ref, c):
    levels = int(np.log2(c))
    work = []
    for pr in probs:
        d, mask = pr["d"], pr["mask"]
        logits = _dot_sel(pr["ba"], expand_ref[d])
        beta = _sigmoid(logits[:, 0:GROUP_W])
        g = nega_ref[d] * _softplus(logits[:, GROUP_W:2 * GROUP_W] + dtb_ref[d])
        cs = _sel_dot(pr["sel"][...], g)
        b_incl = cs[0:c]
        b_t = jnp.concatenate([b_incl[:, h * HEAD_DIM:(h + 1) * HEAD_DIM].T for h in range(N_HEADS)], axis=1)
        lmat = jnp.exp(jnp.minimum(b_incl - b_t, 0.0)) * mask[levels]
        m = beta * pr["kk"] * lmat * mask[levels + 1]
        work.append(dict(beta=beta, eb=jnp.exp(b_incl), etail=jnp.exp(cs[c:2 * c]),
                         dlast=jnp.exp(cs[2 * c:2 * c + SUBLANES]), lmat=lmat, m=m,
                         t=mask[levels + 2] - m * mask[0]))
    for lv in range(1, levels):
        mm = _per_head_mm1 if lv == levels - 1 else _per_head_mm
        ps = [mm(wk["m"] * pr["mask"][lv], wk["t"], ones_bd) for pr, wk in zip(probs, work)]
        for wk, p in zip(work, ps):
            wk["t"] = wk["t"] - mm(wk["t"], p, ones_bd)
    us = [_per_head_mm1(wk["t"], pr["v"] * wk["beta"], ones_bd) for pr, wk in zip(probs, work)]
    ws = [_per_head_mm1(wk["t"], pr["k"] * wk["beta"] * wk["eb"], ones_bd) for pr, wk in zip(probs, work)]
    return [(u, w.astype(BF16), (pr["qk_raw"] * wk["lmat"]).astype(BF16), (pr["q"] * wk["eb"]).astype(BF16),
             (pr["k"] * wk["etail"]).astype(BF16), wk["dlast"]) for pr, wk, u, w in zip(probs, work, us, ws)]


def _delta_state_steps(items, ones_bd, state_ref):
    sts = [state_ref[it[0]] for it in items]
    bds = [_head_blockdiag(st.astype(BF16), ones_bd) for st in sts]
    rows = items[0][1].shape[0]
    ws_qs = [_bd_dot(jnp.concatenate([it[2], it[4]], axis=0), bd) for it, bd in zip(items, bds)]
    v16s = [(it[1] - wq[0:rows]).astype(BF16) for it, wq in zip(items, ws_qs)]
    outs = [wq[rows:2 * rows] + _bd_dot(it[3], _head_blockdiag(v16, ones_bd))
            for it, wq, v16 in zip(items, ws_qs, v16s)]
    for it, st, v16 in zip(items, sts, v16s):
        state_ref[it[0]] = st * it[6] + _per_head_dot_tn(it[5], v16)
    return outs


def _delta_mixer_kernel(t, c, p_ref, self_ref, selb_ref, maskf_ref, maskb_ref, ones_ref, onorm_ref,
                        conv_ref, expand_ref, nega_ref, dtb_ref, h0_ref, y_ref, sout_ref,
                        u_ref, w_ref, qk_ref, qd_ref, kt_ref, dl_ref, of_ref, ob_ref, state_ref):
    n_chunks = t // c
    qkv_w = 3 * GROUP_W
    state_ref[...] = h0_ref[0]
    row_id = lax.broadcasted_iota(jnp.int32, (c, qkv_w), 0)
    w_prev, w_mid, w_next = conv_ref[0:1, :], conv_ref[1:2, :], conv_ref[2:3, :]
    sels, masks = (self_ref, selb_ref), (maskf_ref, maskb_ref)

    def conv_qkv(n):
        r0 = pl.multiple_of(n * c, c)
        x = p_ref[0, pl.ds(r0, c), 0:qkv_w]
        before = p_ref[0, pl.ds(pl.multiple_of(jnp.maximum(r0 - SUBLANES, 0), SUBLANES), SUBLANES), 0:qkv_w]
        after = p_ref[0, pl.ds(pl.multiple_of(jnp.minimum(r0 + c, t - SUBLANES), SUBLANES), SUBLANES), 0:qkv_w]
        before_row = jnp.where(n > 0, before[SUBLANES - 1:SUBLANES, :], 0.0)
        after_row = jnp.where(n < n_chunks - 1, after[0:1, :], 0.0)
        x_prev = jnp.where(row_id == 0, before_row, pltpu.roll(x, 1, axis=0))
        x_next = jnp.where(row_id == c - 1, after_row, pltpu.roll(x, c - 1, axis=0))
        y = _silu(x_prev * w_prev + x * w_mid + x_next * w_next)
        qk = y[:, 0:2 * GROUP_W]
        ones2 = ones_ref[...]
        ss = jnp.concatenate([_head_sum(qk[:, 0:GROUP_W] * qk[:, 0:GROUP_W], ones2),
                              _head_sum(qk[:, GROUP_W:] * qk[:, GROUP_W:], ones2)], axis=1)
        qk = qk * lax.rsqrt(ss + EPS)
        q, k, v = qk[:, 0:GROUP_W] * (HEAD_DIM ** -0.5), qk[:, GROUP_W:], y[:, 2 * GROUP_W:]
        k_hi, k_lo = _split2(k)
        bd_k = _head_blockdiag(k_hi, ones2)
        shared = _bd_dot_nt(jnp.concatenate([k_hi, k_lo, q.astype(BF16)], axis=0), bd_k)
        kk = shared[0:c] + shared[c:2 * c] + _bd_dot_nt(k_hi, _head_blockdiag(k_lo, ones2))
        qk_raw = shared[2 * c:3 * c]
        ba = p_ref[0, pl.ds(r0, c), 4 * GROUP_W:4 * GROUP_W + LANES]
        return dict(q=q, k=k, v=v, kk=kk, qk_raw=qk_raw, ba=ba)

    per_iter = 4 if n_chunks % 4 == 0 else 1

    def prepare(i, carry):
        chunks = [i * per_iter + j for j in range(per_iter)]
        probs = []
        for n in chunks:
            shared = conv_qkv(n)
            probs += [dict(shared, sel=sels[d], mask=masks[d], d=d) for d in range(2)]
        res = _delta_prepare(probs, ones_ref[...], expand_ref, nega_ref, dtb_ref, c)
        for idx, (u, w16, qk16, qd, ktail, dlast) in enumerate(res):
            n, d = chunks[idx // 2], idx % 2
            rows = pl.ds(pl.multiple_of(n * c, c), c)
            u_ref[d, rows, :] = u
            w_ref[d, rows, :] = w16
            qk_ref[d, rows, :] = qk16
            qd_ref[d, rows, :] = qd
            kt_ref[d, rows, :] = ktail
            dl_ref[d, pl.ds(pl.multiple_of(n * SUBLANES, SUBLANES), SUBLANES), :] = dlast
        return carry

    lax.fori_loop(0, n_chunks // per_iter, prepare, 0)

    def body(i, carry):
        items, rows_of = [], []
        for d, n in ((0, i), (1, n_chunks - 1 - i)):
            rows = pl.ds(pl.multiple_of(n * c, c), c)
            dlast = dl_ref[d, pl.ds(pl.multiple_of(n * SUBLANES, SUBLANES), SUBLANES), :][0:1, :]
            items.append((d, u_ref[d, rows, :], w_ref[d, rows, :], qk_ref[d, rows, :], qd_ref[d, rows, :],
                          kt_ref[d, rows, :], dlast))
            rows_of.append(rows)
        o_f, o_b = _delta_state_steps(items, ones_ref[...], state_ref)
        of_ref[rows_of[0], :] = o_f
        ob_ref[rows_of[1], :] = o_b
        return carry

    lax.fori_loop(0, n_chunks, body, 0)

    rbk = min(t, 256)

    def fin(i, carry):
        rows = pl.ds(pl.multiple_of(i * rbk, rbk), rbk)
        o = of_ref[rows, :] + ob_ref[rows, :]
        gate = p_ref[0, rows, 3 * GROUP_W:4 * GROUP_W]
        y_ref[0, rows, :] = _head_norm_gate(o, gate, onorm_ref[...], ones_ref[...]).astype(BF16)
        return carry

    lax.fori_loop(0, t // rbk, fin, 0)
    sout_ref[0] = state_ref[...]


def _delta_mixer(p, conv_w, expand, nega, dtb, onorm, s0):
    b, t, w = p.shape
    c = DN_CHUNK
    sel_f, mask_f = _dn_consts(c, False)
    sel_b, mask_b = _dn_consts(c, True)
    consts = [sel_f, sel_b, mask_f, mask_b, _ones_blockdiag(), onorm, conv_w, expand, nega, dtb]
    state_shape = (2, HEAD_DIM, GROUP_W)
    return pl.pallas_call(
        functools.partial(_delta_mixer_kernel, t, c),
        grid=(b,),
        in_specs=[pl.BlockSpec((1, t, w), lambda i: (i, 0, 0))]
                 + [_full_spec(a.shape) for a in consts]
                 + [pl.BlockSpec((1,) + state_shape, lambda i: (i, 0, 0, 0))],
        out_specs=[pl.BlockSpec((1, t, GROUP_W), lambda i: (i, 0, 0)),
                   pl.BlockSpec((1,) + state_shape, lambda i: (i, 0, 0, 0))],
        out_shape=[jax.ShapeDtypeStruct((b, t, GROUP_W), BF16),
                   jax.ShapeDtypeStruct((b,) + state_shape, F32)],
        scratch_shapes=[pltpu.VMEM((2, t, GROUP_W), F32)]
                       + [pltpu.VMEM((2, t, GROUP_W), BF16)] * 4
                       + [pltpu.VMEM((2, (t // c) * SUBLANES, GROUP_W), F32),
                          pltpu.VMEM((t, GROUP_W), F32), pltpu.VMEM((t, GROUP_W), F32),
                          pltpu.VMEM(state_shape, F32)],
        compiler_params=_params(("arbitrary",)),
        name="delta_mixer",
    )(p, *consts, s0)


def _s5_kernel(t, tc, u_ref, b_ref, cm_ref, lbar_ref, dskip_ref, gluw_ref, glub_ref, h0_ref,
               y_ref, sout_ref, x_ref, yf_ref, yb_ref, carry_ref):
    n_chunks = t // tc
    n_lane_blocks = S5_STATE // LANES
    carry_ref[...] = h0_ref[0]
    sub8 = lax.broadcasted_iota(jnp.int32, (SUBLANES, LANES), 0)

    def scan_dirs(rows_of, y_refs):
        for d in range(2):
            u16 = u_ref[0, rows_of[d], :].astype(BF16)
            for j in range(n_lane_blocks):
                x_ref[d, j] = _dot(u16, b_ref[d, j])
            y_refs[d][rows_of[d], :] = jnp.zeros((tc, GROUP_W), F32)

        def scan_block(d, j):
            bu = x_ref[d, j]
            xr, xi = bu[:, 0:LANES], bu[:, LANES:2 * LANES]
            lb = lbar_ref[d, j]
            pw = [(lb[0:1, :], lb[1:2, :])]
            for _ in range(SUBLANES - 1):
                pr, pi = pw[-1]
                pw.append((pr * pw[0][0] - pi * pw[0][1], pr * pw[0][1] + pi * pw[0][0]))
            n_groups = tc // SUBLANES
            blocks_r = [xr[v * SUBLANES:(v + 1) * SUBLANES] for v in range(n_groups)]
            blocks_i = [xi[v * SUBLANES:(v + 1) * SUBLANES] for v in range(n_groups)]
            for s in range(int(np.log2(SUBLANES))):
                sh = 1 << s
                keep = sub8 < SUBLANES - sh if d else sub8 >= sh
                m_re, m_im = jnp.where(keep, pw[sh - 1][0], 0.0), jnp.where(keep, pw[sh - 1][1], 0.0)
                rot = SUBLANES - sh if d else sh
                for v in range(n_groups):
                    sr, si = pltpu.roll(blocks_r[v], rot, axis=0), pltpu.roll(blocks_i[v], rot, axis=0)
                    blocks_r[v] = blocks_r[v] + m_re * sr - m_im * si
                    blocks_i[v] = blocks_i[v] + m_re * si + m_im * sr
            p_re = jnp.zeros((SUBLANES, LANES), F32)
            p_im = jnp.zeros((SUBLANES, LANES), F32)
            for r in range(SUBLANES):
                pr, pi = pw[SUBLANES - 1 - r] if d else pw[r]
                p_re = jnp.where(sub8 == r, pr, p_re)
                p_im = jnp.where(sub8 == r, pi, p_im)
            cr, ci = carry_ref[d, 0, j], carry_ref[d, 1, j]
            last = 0 if d else SUBLANES - 1
            for step in range(n_groups):
                v = n_groups - 1 - step if d else step
                blocks_r[v] = blocks_r[v] + (p_re * cr - p_im * ci)
                blocks_i[v] = blocks_i[v] + (p_re * ci + p_im * cr)
                cr, ci = blocks_r[v][last:last + 1, :], blocks_i[v][last:last + 1, :]
            carry_ref[d, 0, j] = cr
            carry_ref[d, 1, j] = ci
            return jnp.concatenate([jnp.concatenate(blocks_r, axis=0), jnp.concatenate(blocks_i, axis=0)],
                                   axis=1).astype(BF16)

        def lane_block(j, carry):
            states = [scan_block(d, j) for d in range(2)]
            for d in range(2):
                y_refs[d][rows_of[d], :] += _dot(states[d], cm_ref[j])
            return carry

        lax.fori_loop(0, n_lane_blocks, lane_block, 0)

    def body(n, carry):
        scan_dirs((pl.ds(pl.multiple_of(n * tc, tc), tc), pl.ds(pl.multiple_of((n_chunks - 1 - n) * tc, tc), tc)),
                  (yf_ref, yb_ref))
        return carry

    lax.fori_loop(0, n_chunks, body, 0)

    def fin(i, carry):
        rows = pl.ds(pl.multiple_of(i * tc, tc), tc)
        u = u_ref[0, rows, :]
        y = _gelu_tanh(yf_ref[rows, :] + yb_ref[rows, :] + dskip_ref[...] * u)
        z = _dot(y.astype(BF16), gluw_ref[...]) + glub_ref[...]
        y_ref[0, rows, :] = (y * _sigmoid(z)).astype(BF16)
        return carry

    lax.fori_loop(0, n_chunks, fin, 0)
    sout_ref[0] = carry_ref[...]


def _s5_mixer(u, b_in, cmat, lbar, dskip, glu_w, glu_b, h0):
    b, t, _ = u.shape
    tc = min(S5_CHUNK, t)
    nlb = S5_STATE // LANES
    consts = [b_in, cmat, lbar, dskip, glu_w, glu_b]
    st_shape = (2, 2, nlb, 1, LANES)
    return pl.pallas_call(
        functools.partial(_s5_kernel, t, tc),
        grid=(b,),
        in_specs=[pl.BlockSpec((1, t, GROUP_W), lambda i: (i, 0, 0))]
                 + [_full_spec(a.shape) for a in consts]
                 + [pl.BlockSpec((1,) + st_shape, lambda i: (i, 0, 0, 0, 0, 0))],
        out_specs=[pl.BlockSpec((1, t, GROUP_W), lambda i: (i, 0, 0)),
                   pl.BlockSpec((1,) + st_shape, lambda i: (i, 0, 0, 0, 0, 0))],
        out_shape=[jax.ShapeDtypeStruct((b, t, GROUP_W), BF16),
                   jax.ShapeDtypeStruct((b,) + st_shape, F32)],
        scratch_shapes=[pltpu.VMEM((2, nlb, tc, 2 * LANES), F32), pltpu.VMEM((t, GROUP_W), F32),
                        pltpu.VMEM((t, GROUP_W), F32), pltpu.VMEM(st_shape, F32)],
        compiler_params=_params(("arbitrary",)),
        name="s5_mixer",
    )(u, *consts, h0)


def _outproj_kernel(x_ref, mod_ref, ya_ref, yb_ref, yc_ref, yd_ref, w_ref, o_ref):
    y = (_dot(ya_ref[0], w_ref[0]) + _dot(yb_ref[0], w_ref[1])
         + _dot(yc_ref[0], w_ref[2]) + _dot(yd_ref[0], w_ref[3]))
    o_ref[0] = x_ref[0] + mod_ref[0][:, 2 * D_MODEL:3 * D_MODEL] * y


def _out_projection(x, mod, ys, w_out):
    b, t, _ = x.shape
    tb = min(TOKEN_BLOCK, t)
    tok = lambda n: pl.BlockSpec((1, tb, n), lambda i, j: (i, j, 0))
    return pl.pallas_call(
        _outproj_kernel,
        grid=(b, t // tb),
        in_specs=[tok(D_MODEL), pl.BlockSpec((1, 1, 6 * D_MODEL), lambda i, j: (i, 0, 0))]
                 + [tok(GROUP_W)] * 4 + [_full_spec(w_out.shape)],
        out_specs=tok(D_MODEL),
        out_shape=jax.ShapeDtypeStruct(x.shape, F32),
        compiler_params=_params(("arbitrary", "arbitrary")),
        name="out_projection",
    )(x, mod, *ys, w_out)


def _topk_rows(scores, k, row_ids):
    n = scores[0].shape[1]
    slot = lax.broadcasted_iota(jnp.int32, (k, n), 0)
    scores = list(scores)
    vals = [jnp.zeros((k, n), F32) for _ in scores]
    ids = [jnp.zeros((k, n), F32) for _ in scores]
    for it in range(k):
        ms = [jnp.max(s, axis=0, keepdims=True) for s in scores]
        idxs = [jnp.min(jnp.where(s == m, row_ids, 1e9), axis=0, keepdims=True) for s, m in zip(scores, ms)]
        vals = [jnp.where(slot == it, m, v) for m, v in zip(ms, vals)]
        ids = [jnp.where(slot == it, idx, i) for idx, i in zip(idxs, ids)]
        scores = [jnp.where(row_ids == idx, -jnp.inf, s) for s, idx in zip(scores, idxs)]
    return list(zip(vals, ids))


def _sorting_network(n):
    pairs = []

    def merge(lo, hi, r):
        step = r * 2
        if step < hi - lo:
            merge(lo, hi, step)
            merge(lo + r, hi, step)
            pairs.extend((i, i + r) for i in range(lo + r, hi - r, step))
        else:
            pairs.append((lo, lo + r))

    def sort(lo, hi):
        if hi > lo:
            mid = lo + (hi - lo) // 2
            sort(lo, mid)
            sort(mid + 1, hi)
            merge(lo, hi, 1)

    sort(0, n - 1)
    return pairs


def _lex_first(va, ia, vb, ib):
    tie = va == vb
    return jnp.where(tie, -ia, va) > jnp.where(tie, -ib, vb)


def _topk_columns(scores, k, row_ids):
    n_wires = scores[0].shape[0] // SUBLANES
    n = scores[0].shape[1]
    assert n_wires == k
    net = _sorting_network(n_wires)
    slot = lax.broadcasted_iota(jnp.int32, (k, n), 0)
    sub = lambda x, w: x[w * SUBLANES:(w + 1) * SUBLANES]
    vs = [[sub(s, w) for w in range(n_wires)] for s in scores]
    ids = [[sub(row_ids, w) for w in range(n_wires)] for _ in scores]
    for i, j in net:
        for v, d in zip(vs, ids):
            first = _lex_first(v[i], d[i], v[j], d[j])
            v[i], v[j] = jnp.where(first, v[i], v[j]), jnp.where(first, v[j], v[i])
            d[i], d[j] = jnp.where(first, d[i], d[j]), jnp.where(first, d[j], d[i])
    vals = [jnp.zeros((k, n), F32) for _ in scores]
    wins = [jnp.zeros((k, n), F32) for _ in scores]
    for it in range(k):
        for p, (v, d) in enumerate(zip(vs, ids)):
            hv, hd = v[0], d[0]
            for sh in (4, 2, 1):
                pv, pd = pltpu.roll(hv, sh, axis=0), pltpu.roll(hd, sh, axis=0)
                first = _lex_first(hv, hd, pv, pd)
                hv, hd = jnp.where(first, hv, pv), jnp.where(first, hd, pd)
            vals[p] = jnp.where(slot == it, hv[0:1, :], vals[p])
            wins[p] = jnp.where(slot == it, hd[0:1, :], wins[p])
            popped = d[0] == hd
            for w in range(k - 1 - it):
                v[w] = jnp.where(popped, v[w + 1], v[w])
                d[w] = jnp.where(popped, d[w + 1], d[w])
    return list(zip(vals, wins))


def _staircase():
    pairs = [(i, j) for i in range(PEER_TOPK) for j in range(PEER_TOPK) if (i + 1) * (j + 1) <= PEER_TOPK]
    rows = -(-len(pairs) // SUBLANES) * SUBLANES
    sel = np.zeros((2, rows, PEER_TOPK), np.float32)
    aux = np.zeros((2, rows, LANES), np.float32)
    aux[0, len(pairs):] = -np.inf
    aux[1, len(pairs):] = 1e9
    for r, (i, j) in enumerate(pairs):
        sel[0, r, i] = 1.0
        sel[1, r, j] = 1.0
        aux[1, r] = i * PEER_TOPK + j
    return jnp.asarray(np.tile(sel, (1, 1, 3)), BF16), jnp.asarray(aux, F32)


def _take_rows(table, idx, k):
    out = jnp.zeros(idx.shape, F32)
    for r in range(k):
        out = jnp.where(idx == float(r), table[r:r + 1, :], out)
    return out


def _peer_route_kernel(tb, x_ref, mod_ref, g_ref, wqh_ref, wql_ref, kh_ref, kl_ref, stsel_ref, staux_ref,
                       h_ref, a_ref, c_ref, gate_ref, qt_ref, at_ref, ct_ref, gt_ref):
    mod = mod_ref[0]
    h = _rms_modulate(x_ref[0], g_ref[...], mod[:, 3 * D_MODEL:4 * D_MODEL], mod[:, 4 * D_MODEL:5 * D_MODEL])
    h_hi, h_lo = _split2(h)
    h_ref[0] = h_hi
    wq_hi = wqh_ref[...]
    qt_ref[...] = _dot_nt(wq_hi, h_hi) + _dot_nt(wq_hi, h_lo) + _dot_nt(wql_ref[...], h_hi)
    half = PEER_DQ // 2
    key_ids = lax.broadcasted_iota(jnp.int32, (PEER_NKEYS, LANES), 0).astype(F32)

    heads_per_iter = 4

    def head_body(it, carry):
        heads = [it * heads_per_iter + i for i in range(heads_per_iter)]
        tiles = [slice(lt * LANES, (lt + 1) * LANES) for lt in range(tb // LANES)]
        probs = [(head, lanes) for head in heads for lanes in tiles]
        scores = []
        for head, lanes in probs:
            for p in range(2):
                hp = head * 2 + p
                q_hi, q_lo = _split2(qt_ref[pl.ds(pl.multiple_of(hp * half, half), half), lanes])
                scores.append(_dot(kh_ref[hp], q_hi) + _dot(kh_ref[hp], q_lo) + _dot(kl_ref[hp], q_hi))
        top = _topk_columns(scores, PEER_TOPK, key_ids)
        cands = [_sel_dot(stsel_ref[0], top[2 * i][0]) + _sel_dot(stsel_ref[1], top[2 * i + 1][0]) + staux_ref[0]
                 for i in range(len(probs))]
        best = _topk_rows(cands, PEER_TOPK, staux_ref[1])
        for i, (head, lanes) in enumerate(probs):
            out_rows = pl.ds(pl.multiple_of(head * PEER_TOPK, PEER_TOPK), PEER_TOPK)
            sc, flat = best[i]
            r1 = jnp.floor(flat * (1.0 / PEER_TOPK))
            r2 = flat - r1 * PEER_TOPK
            e = jnp.exp(sc - sc[0:1, :])
            at_ref[out_rows, lanes] = _take_rows(top[2 * i][1], r1, PEER_TOPK)
            ct_ref[out_rows, lanes] = _take_rows(top[2 * i + 1][1], r2, PEER_TOPK)
            gt_ref[out_rows, lanes] = e / jnp.sum(e, axis=0, keepdims=True)
        return carry

    lax.fori_loop(0, PEER_HEADS // heads_per_iter, head_body, 0)
    a_ref[0] = at_ref[...].T.astype(jnp.int32)
    c_ref[0] = ct_ref[...].T.astype(jnp.int32)
    gate_ref[0] = gt_ref[...].T


def _peer_route(x, mod, gain, wq_hi, wq_lo, keys_hi, keys_lo):
    b, t, _ = x.shape
    tb = min(TOKEN_BLOCK, t)
    tok = lambda n: pl.BlockSpec((1, tb, n), lambda i, j: (i, j, 0))
    consts = [gain, wq_hi, wq_lo, keys_hi, keys_lo, *_staircase()]
    return pl.pallas_call(
        functools.partial(_peer_route_kernel, tb),
        grid=(b, t // tb),
        scratch_shapes=[pltpu.VMEM((PEER_HEADS * PEER_DQ, tb), F32)] + [pltpu.VMEM((PEER_SEL, tb), F32)] * 3,
        in_specs=[tok(D_MODEL), pl.BlockSpec((1, 1, 6 * D_MODEL), lambda i, j: (i, 0, 0))]
                 + [_full_spec(a.shape) for a in consts],
        out_specs=[tok(D_MODEL), tok(PEER_SEL), tok(PEER_SEL), tok(PEER_SEL)],
        out_shape=[jax.ShapeDtypeStruct((b, t, D_MODEL), BF16),
                   jax.ShapeDtypeStruct((b, t, PEER_SEL), jnp.int32),
                   jax.ShapeDtypeStruct((b, t, PEER_SEL), jnp.int32),
                   jax.ShapeDtypeStruct((b, t, PEER_SEL), F32)],
        compiler_params=_params(("arbitrary", "arbitrary")),
        name="peer_route",
    )(x, mod, *consts)


def _peer_expert_kernel(tb, n_slabs, x_ref, mod_ref, h_ref, a_ref, c_ref, gate_ref, ut_ref, v_ref,
                        o_ref, gmat_ref, acc_ref):
    j = pl.program_id(2)
    nk = PEER_NKEYS

    @pl.when(j == 0)
    def _():
        acc_ref[...] = jnp.zeros_like(acc_ref)
        key_id = lax.broadcasted_iota(jnp.int32, (nk, PEER_SEL), 0)

        pack = 2 * SUBLANES
        groups = 4

        def build(nb, carry):
            n0 = pl.multiple_of(nb * groups * pack, groups * pack)
            a_rows = a_ref[0, pl.ds(n0, groups * pack), :]
            c_rows = c_ref[0, pl.ds(n0, groups * pack), :]
            g_rows = gate_ref[0, pl.ds(n0, groups * pack), :]
            mats = []
            for r in range(groups * pack):
                row_sel = jnp.where(key_id == a_rows[r:r + 1], 1.0, 0.0).astype(BF16)
                col_sel = jnp.where(key_id == c_rows[r:r + 1], g_rows[r:r + 1], 0.0).astype(BF16)
                mats.append(_dot_nt(row_sel, col_sel).astype(BF16))
            for gi in range(groups):
                stacked = jnp.stack(mats[gi * pack:(gi + 1) * pack], axis=0)
                gmat_ref[:, pl.ds(n0 + gi * pack, pack), :] = pltpu.einshape("nac->anc", stacked)
            return carry

        lax.fori_loop(0, tb // (groups * pack), build, 0)

    h = h_ref[0]
    n_pairs = n_slabs // 2
    cols = [slice(i * 2 * nk, (i + 1) * 2 * nk) for i in range(n_pairs)]
    lookahead = 2
    s = [_dot(h, ut_ref[:, cols[i]]) for i in range(min(lookahead, n_pairs))]
    acc = None
    for i in range(n_pairs):
        if i + lookahead < n_pairs:
            s.append(_dot(h, ut_ref[:, cols[i + lookahead]]))
        a0 = j * n_slabs + 2 * i
        g = jnp.concatenate([gmat_ref[a0], gmat_ref[a0 + 1]], axis=1)
        w = g * _gelu_tanh(s[i]).astype(BF16)
        p = _dot(w, v_ref[cols[i], :])
        acc = p if acc is None else acc + p
    acc_ref[...] += acc

    @pl.when(j == pl.num_programs(2) - 1)
    def _():
        o_ref[0] = x_ref[0] + mod_ref[0][:, 5 * D_MODEL:6 * D_MODEL] * acc_ref[...]


def _peer_experts(x, mod, h, a, c, gate, u_t, v_tab):
    b, t, _ = x.shape
    tb = min(PEER_TOKEN_BLOCK, t)
    n_exp = u_t.shape[1]
    eb = PEER_EXPERT_BLOCK
    n_slabs = eb // PEER_NKEYS
    tok = lambda n: pl.BlockSpec((1, tb, n), lambda i, k, j: (i, k, 0))
    return pl.pallas_call(
        functools.partial(_peer_expert_kernel, tb, n_slabs),
        grid=(b, t // tb, n_exp // eb),
        in_specs=[tok(D_MODEL), pl.BlockSpec((1, 1, 6 * D_MODEL), lambda i, k, j: (i, 0, 0)),
                  tok(D_MODEL), tok(PEER_SEL), tok(PEER_SEL), tok(PEER_SEL),
                  pl.BlockSpec((D_MODEL, eb), lambda i, k, j: (0, j)),
                  pl.BlockSpec((eb, D_MODEL), lambda i, k, j: (j, 0))],
        out_specs=tok(D_MODEL),
        out_shape=jax.ShapeDtypeStruct(x.shape, F32),
        scratch_shapes=[pltpu.VMEM((PEER_NKEYS, tb, PEER_NKEYS), BF16), pltpu.VMEM((tb, D_MODEL), F32)],
        compiler_params=_params(("arbitrary", "arbitrary", "arbitrary")),
        name="peer_experts",
    )(x, mod, h, a, c, gate, u_t, v_tab)


def _final_norm_kernel(x_ref, g_ref, o_ref):
    x = x_ref[0]
    ms = jnp.mean(x * x, axis=-1, keepdims=True)
    o_ref[0] = x * lax.rsqrt(ms + EPS) * g_ref[...]


def _final_norm(x, gain):
    b, t, _ = x.shape
    tb = min(TOKEN_BLOCK, t)
    tok = pl.BlockSpec((1, tb, D_MODEL), lambda i, j: (i, j, 0))
    return pl.pallas_call(
        _final_norm_kernel,
        grid=(b, t // tb),
        in_specs=[tok, _full_spec((1, D_MODEL))],
        out_specs=tok,
        out_shape=jax.ShapeDtypeStruct(x.shape, F32),
        compiler_params=_params(("arbitrary", "arbitrary")),
        name="final_norm",
    )(x, gain)


def _hi_lo(w):
    hi = w.astype(BF16)
    return hi, (w - hi.astype(F32)).astype(BF16)


def _pad_cols(w, n):
    return jnp.pad(w, ((0, 0), (0, n - w.shape[1])))


def _layer_weights(l, w_in, w_out, gla_gk_up, gla_gk_bias, gla_onorm_g, dn_conv_w, dn_A_log, dn_dt_bias,
                   dn_onorm_g, s5_lambda_re, s5_lambda_im, s5_log_dt, s5_B_re, s5_B_im, s5_C_re, s5_C_im,
                   s5_D, s5_glu_w, s5_glu_b, lower_bound, hgrn_onorm_g, peer_wq, peer_keys, peer_u, peer_v):
    gw = GROUP_W
    sizes = (gw, gw, gw, gw, 2 * GLA_LOWRANK, gw, gw, gw, gw, 2 * N_HEADS, 2 * N_HEADS, gw, gw, gw, gw, 2 * gw)
    offs = np.concatenate([[0], np.cumsum(sizes)])
    col = lambda a, b: w_in[l][:, int(offs[a]):int(offs[b])]
    wl = {}
    wl["w_gla"] = _pad_cols(col(0, 5), GLA_W).astype(BF16)
    wl["w_dn"] = _pad_cols(col(5, 11), DN_W).astype(BF16)
    wl["w_s5"] = col(11, 12).astype(BF16)
    wl["w_hg"] = col(12, 16).astype(BF16)
    wl["w_out"] = w_out[l].reshape(4, gw, D_MODEL).astype(BF16)
    tile_head = lambda g: jnp.tile(g, N_HEADS).reshape(1, gw)
    up = jnp.zeros((2, LANES, gw), F32)
    for d in range(2):
        up = up.at[d, d * GLA_LOWRANK:(d + 1) * GLA_LOWRANK, :].set(gla_gk_up[l, d])
    wl["gla_up_hi"], wl["gla_up_lo"] = _hi_lo(up)
    wl["gla_bias"] = gla_gk_bias[l].reshape(2, 1, gw)
    wl["gla_onorm"] = tile_head(gla_onorm_g[l])
    expand = np.zeros((2, LANES, 2 * gw), np.float32)
    for d in range(2):
        for h in range(N_HEADS):
            expand[d, d * N_HEADS + h, h * HEAD_DIM:(h + 1) * HEAD_DIM] = 1.0
            expand[d, 2 * N_HEADS + d * N_HEADS + h, gw + h * HEAD_DIM:gw + (h + 1) * HEAD_DIM] = 1.0
    wl["dn_expand"] = jnp.asarray(expand, BF16)
    wl["dn_conv"] = jnp.pad(dn_conv_w[l], ((0, SUBLANES - 3), (0, 0)))
    wl["dn_nega"] = jnp.repeat(-jnp.exp(dn_A_log[l]), HEAD_DIM, axis=-1).reshape(2, 1, gw)
    wl["dn_dtb"] = jnp.repeat(dn_dt_bias[l], HEAD_DIM, axis=-1).reshape(2, 1, gw)
    wl["dn_onorm"] = tile_head(dn_onorm_g[l])
    nlb = S5_STATE // LANES
    lam_re, lam_im = s5_lambda_re[l].astype(F32), s5_lambda_im[l].astype(F32)
    dt = jnp.exp(s5_log_dt[l].astype(F32))[:, :, None]
    mag = jnp.exp(lam_re * dt)
    lbar_re, lbar_im = mag * jnp.cos(lam_im * dt), mag * jnp.sin(lam_im * dt)
    den = lam_re * lam_re + lam_im * lam_im
    num_re, num_im = lbar_re - 1.0, lbar_im
    coef_re = (num_re * lam_re + num_im * lam_im) / den
    coef_im = (num_im * lam_re - num_re * lam_im) / den
    b_re, b_im = s5_B_re[l].astype(F32), s5_B_im[l].astype(F32)
    bb_re = coef_re[..., None] * b_re - coef_im[..., None] * b_im
    bb_im = coef_re[..., None] * b_im + coef_im[..., None] * b_re
    eye_g = jnp.eye(S5_G, dtype=F32)
    bd_in = lambda m: jnp.einsum('dgpc,gk->dgckp', m, eye_g).reshape(2, gw, S5_STATE)
    blk = lambda m: m.reshape(2, gw, nlb, LANES).transpose(0, 2, 1, 3)
    b_cat = jnp.concatenate([blk(bd_in(bb_re)), blk(bd_in(bb_im))], axis=-1)
    wl["s5_b"] = b_cat.astype(BF16)
    bd_out = lambda m: jnp.einsum('gcp,gk->gpkc', m.astype(F32), eye_g).reshape(S5_STATE, gw)
    c_re, c_im = bd_out(s5_C_re[l]), bd_out(s5_C_im[l])
    wl["s5_c"] = jnp.concatenate([c_re.reshape(nlb, LANES, gw), -c_im.reshape(nlb, LANES, gw)],
                                 axis=1).astype(BF16)
    wl["s5_lbar"] = jnp.stack([lbar_re.reshape(2, nlb, LANES), lbar_im.reshape(2, nlb, LANES)], axis=2)
    wl["s5_d"] = s5_D[l].reshape(1, gw)
    wl["s5_glu_w"] = s5_glu_w[l].astype(BF16)
    wl["s5_glu_b"] = s5_glu_b[l].reshape(1, gw)
    wl["hg_loglb"] = jnp.log(lower_bound).reshape(1, gw)
    wl["hg_log1m"] = jnp.log1p(-lower_bound).reshape(1, gw)
    wl["hg_1m"] = (1.0 - lower_bound).reshape(1, gw)
    wl["hg_onorm"] = tile_head(hgrn_onorm_g[l])
    wl["wq_hi"], wl["wq_lo"] = _hi_lo(peer_wq[l].T)
    wl["keys_hi"], wl["keys_lo"] = _hi_lo(peer_keys[l].reshape(PEER_HEADS * 2, PEER_NKEYS, PEER_DQ // 2))
    wl["u_t"] = peer_u[l].astype(BF16).T
    wl["v_tab"] = peer_v[l].astype(BF16)
    return wl


def _heads_to_lanes(s, transposed):
    perm = (0, 1, 4, 2, 3) if transposed else (0, 1, 3, 2, 4)
    return s.astype(F32).transpose(perm).reshape(s.shape[0], 2, HEAD_DIM, GROUP_W)


def _lanes_to_heads(s, transposed):
    s = s.reshape(s.shape[0], 2, HEAD_DIM, N_HEADS, HEAD_DIM)
    return s.transpose((0, 1, 3, 4, 2) if transposed else (0, 1, 3, 2, 4))


def _to_column_major(x):
    b, n, d = x.shape
    return x.reshape(b, n // GRID_W, GRID_W, d).transpose(0, 2, 1, 3).reshape(b, n, d)


def _to_row_major(x):
    b, n, d = x.shape
    return x.reshape(b, GRID_W, n // GRID_W, d).transpose(0, 2, 1, 3).reshape(b, n, d)


def _trunk_layer(x, mod, wl, norm1_g, norm2_g, states, shared_mod):
    st_gla, st_dn, st_s5, st_hg = states
    p_gla, p_dn, p_s5, p_hg = _in_projection(x, mod, norm1_g, [wl["w_gla"], wl["w_dn"], wl["w_s5"], wl["w_hg"]])
    y_a, n_gla = _decay_mixer("gla", p_gla, wl["gla_up_hi"], wl["gla_up_lo"], wl["gla_bias"], wl["gla_onorm"], st_gla)
    y_b, n_dn = _delta_mixer(p_dn, wl["dn_conv"], wl["dn_expand"], wl["dn_nega"], wl["dn_dtb"], wl["dn_onorm"], st_dn)
    y_c, n_s5 = _s5_mixer(p_s5, wl["s5_b"], wl["s5_c"], wl["s5_lbar"], wl["s5_d"],
                          wl["s5_glu_w"], wl["s5_glu_b"], st_s5)
    y_d, n_hg = _decay_mixer("hgrn", p_hg, wl["hg_loglb"], wl["hg_log1m"], wl["hg_1m"], wl["hg_onorm"], st_hg)
    x = _out_projection(x, mod, [y_a, y_b, y_c, y_d], wl["w_out"])
    h, a, c, gate = _peer_route(x, mod, norm2_g, wl["wq_hi"], wl["wq_lo"], wl["keys_hi"], wl["keys_lo"])
    b, t, _ = x.shape
    fold = max(1, min(PEER_TOKEN_BLOCK // t, b)) if shared_mod else 1
    fb = lambda z: z.reshape(b // fold, fold * t, z.shape[-1])
    x = _peer_experts(fb(x), mod[0:b // fold], fb(h), fb(a), fb(c), fb(gate), wl["u_t"], wl["v_tab"]).reshape(b, t, -1)
    return x, (n_gla, n_dn, n_s5, n_hg)


def kernel(x_prompt, x_sample, state_gla, state_delta, state_s5_re, state_s5_im, state_hgrn, c, c_ctx,
           mod_w, mod_b, norm1_g, norm2_g, w_in, w_out, gla_gk_up, gla_gk_bias, gla_onorm_g,
           dn_conv_w, dn_A_log, dn_dt_bias, dn_onorm_g, s5_lambda_re, s5_lambda_im, s5_log_dt,
           s5_B_re, s5_B_im, s5_C_re, s5_C_im, s5_D, s5_glu_w, s5_glu_b, hgrn_lb_logits,
           hgrn_onorm_g, peer_wq, peer_keys, peer_u, peer_v, final_norm_g):
    bp, bs = x_prompt.shape[0], x_sample.shape[0]
    nlb = S5_STATE // LANES
    lb_cum = jnp.cumsum(jax.nn.softmax(hgrn_lb_logits.astype(F32), axis=0), axis=0)
    lower_bounds = lb_cum - lb_cum[0:1]

    rows = 2 * SUBLANES
    cc = jnp.concatenate([c, c_ctx[None, :], jnp.zeros((rows - bs - 1, D_MODEL), F32)], axis=0)
    mod_all = _modulation(cc, mod_w, mod_b)

    zeros_hh = jnp.zeros((bp, 2, HEAD_DIM, GROUP_W), F32)
    zeros_s5 = jnp.zeros((bp, 2, 2, nlb, 1, LANES), F32)

    xp, xs = x_prompt, x_sample
    ctx_states = []
    for l in range(DEPTH):
        wl = _layer_weights(l, w_in, w_out, gla_gk_up, gla_gk_bias, gla_onorm_g, dn_conv_w, dn_A_log,
                            dn_dt_bias, dn_onorm_g, s5_lambda_re, s5_lambda_im, s5_log_dt, s5_B_re, s5_B_im,
                            s5_C_re, s5_C_im, s5_D, s5_glu_w, s5_glu_b, lower_bounds[l], hgrn_onorm_g,
                            peer_wq, peer_keys, peer_u, peer_v)
        n1, n2 = norm1_g[l].reshape(1, D_MODEL), norm2_g[l].reshape(1, D_MODEL)
        mod_ctx = jnp.broadcast_to(mod_all[l, bs].reshape(1, 1, 6 * D_MODEL), (bp, 1, 6 * D_MODEL))
        mod_lat = mod_all[l, 0:bs].reshape(bs, 1, 6 * D_MODEL)
        xp, st_ctx = _trunk_layer(xp, mod_ctx, wl, n1, n2, (zeros_hh, zeros_hh, zeros_s5, zeros_hh), True)
        ctx_states.append(st_ctx)
        cached_s5 = jnp.stack([state_s5_re[:, l], state_s5_im[:, l]], axis=2).reshape(bs, 2, 2, nlb, 1, LANES)
        cached = (_heads_to_lanes(state_gla[:, l], True), _heads_to_lanes(state_delta[:, l], False), cached_s5,
                  _heads_to_lanes(state_hgrn[:, l], True))
        if l % 2 == 1:
            xs = _to_column_major(xs)
        xs, _ = _trunk_layer(xs, mod_lat, wl, n1, n2, cached, False)
        if l % 2 == 1:
            xs = _to_row_major(xs)

    gain = final_norm_g.reshape(1, D_MODEL)
    y_prompt = _final_norm(xp, gain)
    y_sample = _final_norm(xs, gain)
    new_gla = jnp.stack([_lanes_to_heads(s[0], True) for s in ctx_states], axis=1)
    new_dn = jnp.stack([_lanes_to_heads(s[1], False) for s in ctx_states], axis=1)
    s5 = jnp.stack([s[2].reshape(bp, 2, 2, S5_G, S5_P) for s in ctx_states], axis=1)
    new_hg = jnp.stack([_lanes_to_heads(s[3], True) for s in ctx_states], axis=1)
    return (y_prompt, y_sample, new_gla, new_dn, s5[:, :, :, 0], s5[:, :, :, 1], new_hg)
```

```python
import functools

import numpy as np
import jax
import jax.numpy as jnp
from jax import lax
from jax.experimental import pallas as pl
from jax.experimental.pallas import tpu as pltpu

F32 = jnp.float32
BF16 = jnp.bfloat16

D_MODEL = 1024
DEPTH = 2
GRID_W = 64
GROUP_W = 256
HEAD_DIM = 64
N_HEADS = GROUP_W // HEAD_DIM
GLA_LOWRANK = 16
GLA_GATE_NORM = 16.0
DN_CHUNK = 64
S5_GROUP = 16
S5_G = GROUP_W // S5_GROUP
S5_P = 64
S5_STATE = S5_G * S5_P
PEER_HEADS = 8
PEER_NKEYS = 128
PEER_DQ = 256
PEER_TOPK = 16
PEER_SEL = PEER_HEADS * PEER_TOPK
EPS = 1e-6

LANES = 128
SUBLANES = 8
VMEM_LIMIT = 56 * 1024 * 1024

GLA_CHUNK = 64
S5_CHUNK = 1024
TOKEN_BLOCK = 256
PEER_TOKEN_BLOCK = 512
PEER_EXPERT_BLOCK = 2048
GLA_W = 4 * GROUP_W + LANES
DN_W = 4 * GROUP_W + LANES


def _dot(a, b):
    return jnp.dot(a, b, preferred_element_type=F32)


def _dot_nt(a, b):
    return lax.dot_general(a, b, (((1,), (1,)), ((), ())), preferred_element_type=F32)


def _dot_tn(a, b):
    return lax.dot_general(a, b, (((0,), (0,)), ((), ())), preferred_element_type=F32)


def _split2(x):
    hi = x.astype(BF16)
    lo = (x - hi.astype(F32)).astype(BF16)
    return hi, lo


def _split3(x):
    hi = x.astype(BF16)
    r = x - hi.astype(F32)
    mid = r.astype(BF16)
    lo = (r - mid.astype(F32)).astype(BF16)
    return hi, mid, lo


def _sel_dot(m01x3, x):
    return _dot(m01x3, jnp.concatenate(_split3(x), axis=0))


def _dot_sel(x, m01):
    hi, mid, lo = _split3(x)
    return _dot(hi, m01) + _dot(mid, m01) + _dot(lo, m01)


def _dot_hl(a, b_hi, b_lo):
    a_hi, a_lo = _split2(a)
    return _dot(a_hi, b_hi) + _dot(a_hi, b_lo) + _dot(a_lo, b_hi)


def _head_sum(x, ones_bd):
    hi, lo = _split2(x)
    return _dot(hi, ones_bd) + _dot(lo, ones_bd)


HEADS_PER_TILE = LANES // HEAD_DIM
N_HEAD_TILES = GROUP_W // LANES


def _head_blockdiag(x16, ones_bd):
    mask = ones_bd[0:LANES, 0:LANES]
    return [jnp.tile(x16[:, t * LANES:(t + 1) * LANES], (HEADS_PER_TILE, 1)) * mask for t in range(N_HEAD_TILES)]


def _bd_dot(a16, bds):
    return jnp.concatenate([_dot(a16[:, t * LANES:(t + 1) * LANES], bd) for t, bd in enumerate(bds)], axis=1)


def _bd_dot_nt(a16, bds):
    return jnp.concatenate([_dot_nt(a16[:, t * LANES:(t + 1) * LANES], bd) for t, bd in enumerate(bds)], axis=1)


def _per_head_mm(a, x, ones_bd):
    a_hi, a_lo = _split2(a)
    x_hi, x_lo = _split2(x)
    rows = a.shape[0]
    both = _bd_dot(jnp.concatenate([a_hi, a_lo], axis=0), _head_blockdiag(x_hi, ones_bd))
    return both[0:rows] + both[rows:2 * rows] + _bd_dot(a_hi, _head_blockdiag(x_lo, ones_bd))


def _per_head_mm1(a, x, ones_bd):
    return _bd_dot(a.astype(BF16), _head_blockdiag(x.astype(BF16), ones_bd))


def _per_head_dot_tn(a16, b16):
    lane_head = lax.broadcasted_iota(jnp.int32, (HEAD_DIM, LANES), 1) // HEAD_DIM
    outs = []
    for t in range(N_HEAD_TILES):
        full = _dot_tn(a16[:, t * LANES:(t + 1) * LANES], b16[:, t * LANES:(t + 1) * LANES])
        out = None
        for h in range(HEADS_PER_TILE):
            blk = jnp.where(lane_head == h, full[h * HEAD_DIM:(h + 1) * HEAD_DIM, :], 0.0)
            out = blk if out is None else out + blk
        outs.append(out)
    return jnp.concatenate(outs, axis=1)


def _log_sigmoid(x):
    return jnp.minimum(x, 0.0) - jnp.log(1.0 + jnp.exp(-jnp.abs(x)))


def _softplus(x):
    return jnp.maximum(x, 0.0) + jnp.log(1.0 + jnp.exp(-jnp.abs(x)))


def _sigmoid(x):
    return 1.0 / (1.0 + jnp.exp(-x))


def _silu(x):
    return x * _sigmoid(x)


def _gelu_tanh(x):
    return 0.5 * x * (1.0 + jnp.tanh(0.7978845608028654 * (x + 0.044715 * x * x * x)))


def _rms_modulate(x, gain, shift, scale):
    ms = jnp.mean(x * x, axis=-1, keepdims=True)
    return x * lax.rsqrt(ms + EPS) * gain * (1.0 + scale) + shift


def _full_spec(shape):
    zeros = (0,) * len(shape)
    return pl.BlockSpec(shape, lambda *_: zeros)


def _params(sem):
    return pltpu.CompilerParams(dimension_semantics=sem, vmem_limit_bytes=VMEM_LIMIT)


def _scan_consts(c, reverse):
    levels = int(np.log2(c))
    idx = np.arange(c)
    i, t = idx[:, None], idx[None, :]
    sel, masks = [], []
    for lv in range(levels):
        s = 1 << lv
        right = (idx // s) % 2 == 1
        same_blk = (i // s) == (t // s)
        sel.append(np.where(right[:, None], same_blk & (t <= i), same_blk & (t > i)))
        same_parent = (i // (2 * s)) == (t // (2 * s))
        masks.append(same_parent & right[:, None] & ~right[None, :])
    sel.append(t <= i)
    sel.append(t > i)
    masks.append(i == t)
    if reverse:
        sel = [m[::-1, ::-1] for m in sel]
        masks = [m[::-1, ::-1] for m in masks]
    sel.append(np.ones((SUBLANES, c), bool))
    sel = np.tile(np.concatenate(sel, axis=0).astype(np.float32), (1, 3))
    masks = np.tile(np.stack(masks).astype(np.float32), (1, 1, N_HEADS))
    return jnp.asarray(sel, BF16), jnp.asarray(masks, F32)


def _dn_consts(c, reverse):
    levels = int(np.log2(c))
    idx = np.arange(c)
    i, t = idx[:, None], idx[None, :]
    incl, tail = t <= i, t > i
    masks = []
    for lv in range(levels):
        s = 1 << lv
        right = (idx // s) % 2 == 1
        masks.append(((i // (2 * s)) == (t // (2 * s))) & right[:, None] & ~right[None, :])
    masks += [incl, incl & (i != t), i == t]
    if reverse:
        incl, tail = incl[::-1, ::-1], tail[::-1, ::-1]
        masks = [m[::-1, ::-1] for m in masks]
    sel = np.tile(np.concatenate([incl, tail, np.ones((SUBLANES, c), bool)], axis=0).astype(np.float32), (1, 3))
    masks = np.tile(np.stack(masks).astype(np.float32), (1, 1, N_HEADS))
    return jnp.asarray(sel, BF16), jnp.asarray(masks, F32)


def _ones_blockdiag():
    h = np.arange(GROUP_W) // HEAD_DIM
    return jnp.asarray((h[:, None] == h[None, :]).astype(np.float32), BF16)


def _mod_kernel(c_ref, w_ref, b_ref, o_ref):
    a_hi, a_lo = _split2(_silu(c_ref[...]))
    w_hi, w_lo = _split2(w_ref[0])
    o_ref[0] = _dot(a_hi, w_hi) + _dot(a_hi, w_lo) + _dot(a_lo, w_hi) + b_ref[0]


def _modulation(cc, mod_w, mod_b):
    rows = cc.shape[0]
    nb = 6
    return pl.pallas_call(
        _mod_kernel,
        grid=(DEPTH, nb),
        in_specs=[pl.BlockSpec((rows, D_MODEL), lambda l, j: (0, 0)),
                  pl.BlockSpec((1, D_MODEL, D_MODEL), lambda l, j: (l, 0, j)),
                  pl.BlockSpec((1, 1, D_MODEL), lambda l, j: (l, 0, j))],
        out_specs=pl.BlockSpec((1, rows, D_MODEL), lambda l, j: (l, 0, j)),
        out_shape=jax.ShapeDtypeStruct((DEPTH, rows, 6 * D_MODEL), F32),
        compiler_params=_params(("arbitrary", "arbitrary")),
        name="modulation",
    )(cc, mod_w, mod_b.reshape(DEPTH, 1, 6 * D_MODEL))


def _inproj_kernel(x_ref, mod_ref, g_ref, wa_ref, wb_ref, wc_ref, wd_ref, oa_ref, ob_ref, oc_ref, od_ref):
    mod = mod_ref[0]
    h = _rms_modulate(x_ref[0], g_ref[...], mod[:, 0:D_MODEL], mod[:, D_MODEL:2 * D_MODEL]).astype(BF16)
    oa_ref[0] = _dot(h, wa_ref[...])
    ob_ref[0] = _dot(h, wb_ref[...])
    oc_ref[0] = _dot(h, wc_ref[...])
    od_ref[0] = _dot(h, wd_ref[...])


def _in_projection(x, mod, gain, weights):
    b, t, _ = x.shape
    tb = min(TOKEN_BLOCK, t)
    widths = [w.shape[1] for w in weights]
    return pl.pallas_call(
        _inproj_kernel,
        grid=(b, t // tb),
        in_specs=[pl.BlockSpec((1, tb, D_MODEL), lambda i, j: (i, j, 0)),
                  pl.BlockSpec((1, 1, 6 * D_MODEL), lambda i, j: (i, 0, 0)),
                  _full_spec((1, D_MODEL))] + [_full_spec(w.shape) for w in weights],
        out_specs=[pl.BlockSpec((1, tb, n), lambda i, j: (i, j, 0)) for n in widths],
        out_shape=[jax.ShapeDtypeStruct((b, t, n), F32) for n in widths],
        compiler_params=_params(("arbitrary", "arbitrary")),
        name="in_projection",
    )(x, mod, gain, *weights)


def _decay_scan_prepare(probs, ones_bd, c):
    levels = int(np.log2(c))
    exs = [jnp.exp(_sel_dot(pr["sel"][...], pr["g"])) for pr in probs]
    attns = [None] * len(probs)
    for lv in range(levels + 1):
        for i, (pr, ex) in enumerate(zip(probs, exs)):
            q, k = pr["q"], pr["k"]
            if lv < levels:
                e = ex[lv * c:(lv + 1) * c]
                qt, kt = (q * e).astype(BF16), (k * e).astype(BF16)
            else:
                qt, kt = q.astype(BF16), k.astype(BF16)
            p = _bd_dot_nt(qt, _head_blockdiag(kt, ones_bd)) * pr["mask"][lv]
            attns[i] = p if attns[i] is None else attns[i] + p
    res = []
    for pr, ex, attn in zip(probs, exs, attns):
        o = _bd_dot(attn.astype(BF16), _head_blockdiag(pr["v"].astype(BF16), ones_bd))
        qd = (pr["q"] * ex[levels * c:(levels + 1) * c]).astype(BF16)
        ktail = (pr["k"] * ex[(levels + 1) * c:(levels + 2) * c]).astype(BF16)
        res.append((o, qd, ktail, ex[(levels + 2) * c:(levels + 2) * c + SUBLANES]))
    return res


def _decay_scan_state_steps(items, ones_bd, state_ref):
    sts = [state_ref[it[0]] for it in items]
    outs = [it[1] + _bd_dot_nt(it[2], _head_blockdiag(st.astype(BF16), ones_bd)) for it, st in zip(items, sts)]
    for it, st in zip(items, sts):
        state_ref[it[0]] = st * it[5] + _per_head_dot_tn(it[4], it[3])
    return outs


def _head_norm_gate(o, gate, onorm, ones_bd):
    ms = _head_sum(o * o, ones_bd) * (1.0 / HEAD_DIM)
    return o * lax.rsqrt(ms + EPS) * onorm * _silu(gate)


def _decay_mixer_kernel(mode, t, c, p_ref, self_ref, selb_ref, maskf_ref, maskb_ref, ones_ref, onorm_ref,
                        pa_ref, pb_ref, pc_ref, h0_ref, y_ref, sout_ref,
                        qd_ref, kt_ref, v16_ref, dl_ref, of_ref, ob_ref, state_ref):
    n_chunks = t // c
    state_ref[...] = h0_ref[0]
    sels, masks, o_refs = (self_ref, selb_ref), (maskf_ref, maskb_ref), (of_ref, ob_ref)

    def inputs(rows, d):
        if mode == "gla":
            q = p_ref[0, rows, 0:GROUP_W] * (HEAD_DIM ** -0.5)
            k = p_ref[0, rows, GROUP_W:2 * GROUP_W]
            v = p_ref[0, rows, 2 * GROUP_W:3 * GROUP_W]
            low = p_ref[0, rows, 4 * GROUP_W:4 * GROUP_W + LANES]
            gk = _dot_hl(low, pa_ref[d], pb_ref[d]) + pc_ref[d]
            g = _log_sigmoid(gk) * (1.0 / GLA_GATE_NORM)
        else:
            q = _silu(p_ref[0, rows, 0:GROUP_W]) * (HEAD_DIM ** -0.5)
            v = p_ref[0, rows, GROUP_W:2 * GROUP_W]
            z = p_ref[0, rows, (3 + d) * GROUP_W:(4 + d) * GROUP_W]
            a = pa_ref[...]
            b = pb_ref[...] + _log_sigmoid(z)
            m = jnp.maximum(a, b)
            g = m + jnp.log(jnp.exp(a - m) + jnp.exp(b - m))
            k = pc_ref[...] * _sigmoid(-z)
        return q, k, v, g

    per_iter = 4 if n_chunks % 4 == 0 else 1

    def prepare(i, carry):
        chunks = [i * per_iter + j for j in range(per_iter)]
        probs = []
        for n in chunks:
            rows = pl.ds(pl.multiple_of(n * c, c), c)
            for d in range(2):
                q, k, v, g = inputs(rows, d)
                probs.append(dict(q=q, k=k, v=v, g=g, sel=sels[d], mask=masks[d]))
            v16_ref[rows, :] = probs[-1]["v"].astype(BF16)
        for idx, (o, qd, ktail, dlast) in enumerate(_decay_scan_prepare(probs, ones_ref[...], c)):
            n, d = chunks[idx // 2], idx % 2
            rows = pl.ds(pl.multiple_of(n * c, c), c)
            o_refs[d][rows, :] = o
            qd_ref[d, rows, :] = qd
            kt_ref[d, rows, :] = ktail
            dl_ref[d, pl.ds(pl.multiple_of(n * SUBLANES, SUBLANES), SUBLANES), :] = dlast
        return carry

    lax.fori_loop(0, n_chunks // per_iter, prepare, 0)

    def body(i, carry):
        items, rows_of = [], []
        for d, n in ((0, i), (1, n_chunks - 1 - i)):
            rows = pl.ds(pl.multiple_of(n * c, c), c)
            dlast = dl_ref[d, pl.ds(pl.multiple_of(n * SUBLANES, SUBLANES), SUBLANES), :][0:1, :]
            items.append((d, o_refs[d][rows, :], qd_ref[d, rows, :], kt_ref[d, rows, :], v16_ref[rows, :], dlast))
            rows_of.append(rows)
        o_f, o_b = _decay_scan_state_steps(items, ones_ref[...], state_ref)
        of_ref[rows_of[0], :] = o_f
        ob_ref[rows_of[1], :] = o_b
        return carry

    lax.fori_loop(0, n_chunks, body, 0)

    rbk = min(t, 256)
    gate_col = 3 * GROUP_W if mode == "gla" else 2 * GROUP_W

    def fin(i, carry):
        rows = pl.ds(pl.multiple_of(i * rbk, rbk), rbk)
        o = of_ref[rows, :] + ob_ref[rows, :]
        gate = p_ref[0, rows, gate_col:gate_col + GROUP_W]
        y_ref[0, rows, :] = _head_norm_gate(o, gate, onorm_ref[...], ones_ref[...]).astype(BF16)
        return carry

    lax.fori_loop(0, t // rbk, fin, 0)
    sout_ref[0] = state_ref[...]


def _decay_mixer(mode, p, pa, pb, pc, onorm, h0t):
    b, t, w = p.shape
    c = GLA_CHUNK
    sel_f, mask_f = _scan_consts(c, False)
    sel_b, mask_b = _scan_consts(c, True)
    ones_bd = _ones_blockdiag()
    consts = [sel_f, sel_b, mask_f, mask_b, ones_bd, onorm, pa, pb, pc]
    state_shape = (2, HEAD_DIM, GROUP_W)
    return pl.pallas_call(
        functools.partial(_decay_mixer_kernel, mode, t, c),
        grid=(b,),
        in_specs=[pl.BlockSpec((1, t, w), lambda i: (i, 0, 0))]
                 + [_full_spec(a.shape) for a in consts]
                 + [pl.BlockSpec((1,) + state_shape, lambda i: (i, 0, 0, 0))],
        out_specs=[pl.BlockSpec((1, t, GROUP_W), lambda i: (i, 0, 0)),
                   pl.BlockSpec((1,) + state_shape, lambda i: (i, 0, 0, 0))],
        out_shape=[jax.ShapeDtypeStruct((b, t, GROUP_W), BF16),
                   jax.ShapeDtypeStruct((b,) + state_shape, F32)],
        scratch_shapes=[pltpu.VMEM((2, t, GROUP_W), BF16), pltpu.VMEM((2, t, GROUP_W), BF16),
                        pltpu.VMEM((t, GROUP_W), BF16), pltpu.VMEM((2, (t // c) * SUBLANES, GROUP_W), F32),
                        pltpu.VMEM((t, GROUP_W), F32), pltpu.VMEM((t, GROUP_W), F32),
                        pltpu.VMEM(state_shape, F32)],
        compiler_params=_params(("arbitrary",)),
        name=mode + "_mixer",
    )(p, *consts, h0t)


def _delta_prepare(probs, ones_bd, expand_ref, nega_ref, dtb_ref, c):
    levels = int(np.log2(c))
    work = []
    for pr in probs:
        d, mask = pr["d"], pr["mask"]
        logits = _dot_sel(pr["ba"], expand_ref[d])
        beta = _sigmoid(logits[:, 0:GROUP_W])
        g = nega_ref[d] * _softplus(logits[:, GROUP_W:2 * GROUP_W] + dtb_ref[d])
        cs = _sel_dot(pr["sel"][...], g)
        b_incl = cs[0:c]
        b_t = jnp.concatenate([b_incl[:, h * HEAD_DIM:(h + 1) * HEAD_DIM].T for h in range(N_HEADS)], axis=1)
        lmat = jnp.exp(jnp.minimum(b_incl - b_t, 0.0)) * mask[levels]
        m = beta * pr["kk"] * lmat * mask[levels + 1]
        work.append(dict(beta=beta, eb=jnp.exp(b_incl), etail=jnp.exp(cs[c:2 * c]),
                         dlast=jnp.exp(cs[2 * c:2 * c + SUBLANES]), lmat=lmat, m=m,
                         t=mask[levels + 2] - m * mask[0]))
    for lv in range(1, levels):
        mm = _per_head_mm1 if lv == levels - 1 else _per_head_mm
        ps = [mm(wk["m"] * pr["mask"][lv], wk["t"], ones_bd) for pr, wk in zip(probs, work)]
        for wk, p in zip(work, ps):
            wk["t"] = wk["t"] - mm(wk["t"], p, ones_bd)
    us = [_per_head_mm1(wk["t"], pr["v"] * wk["beta"], ones_bd) for pr, wk in zip(probs, work)]
    ws = [_per_head_mm1(wk["t"], pr["k"] * wk["beta"] * wk["eb"], ones_bd) for pr, wk in zip(probs, work)]
    return [(u, w.astype(BF16), (pr["qk_raw"] * wk["lmat"]).astype(BF16), (pr["q"] * wk["eb"]).astype(BF16),
             (pr["k"] * wk["etail"]).astype(BF16), wk["dlast"]) for pr, wk, u, w in zip(probs, work, us, ws)]


def _delta_state_steps(items, ones_bd, state_ref):
    sts = [state_ref[it[0]] for it in items]
    bds = [_head_blockdiag(st.astype(BF16), ones_bd) for st in sts]
    rows = items[0][1].shape[0]
    ws_qs = [_bd_dot(jnp.concatenate([it[2], it[4]], axis=0), bd) for it, bd in zip(items, bds)]
    v16s = [(it[1] - wq[0:rows]).astype(BF16) for it, wq in zip(items, ws_qs)]
    outs = [wq[rows:2 * rows] + _bd_dot(it[3], _head_blockdiag(v16, ones_bd))
            for it, wq, v16 in zip(items, ws_qs, v16s)]
    for it, st, v16 in zip(items, sts, v16s):
        state_ref[it[0]] = st * it[6] + _per_head_dot_tn(it[5], v16)
    return outs


def _delta_mixer_kernel(t, c, p_ref, self_ref, selb_ref, maskf_ref, maskb_ref, ones_ref, onorm_ref,
                        conv_ref, expand_ref, nega_ref, dtb_ref, h0_ref, y_ref, sout_ref,
                        u_ref, w_ref, qk_ref, qd_ref, kt_ref, dl_ref, of_ref, ob_ref, state_ref):
    n_chunks = t // c
    qkv_w = 3 * GROUP_W
    state_ref[...] = h0_ref[0]
    row_id = lax.broadcasted_iota(jnp.int32, (c, qkv_w), 0)
    w_prev, w_mid, w_next = conv_ref[0:1, :], conv_ref[1:2, :], conv_ref[2:3, :]
    sels, masks = (self_ref, selb_ref), (maskf_ref, maskb_ref)

    def conv_qkv(n):
        r0 = pl.multiple_of(n * c, c)
        x = p_ref[0, pl.ds(r0, c), 0:qkv_w]
        before = p_ref[0, pl.ds(pl.multiple_of(jnp.maximum(r0 - SUBLANES, 0), SUBLANES), SUBLANES), 0:qkv_w]
        after = p_ref[0, pl.ds(pl.multiple_of(jnp.minimum(r0 + c, t - SUBLANES), SUBLANES), SUBLANES), 0:qkv_w]
        before_row = jnp.where(n > 0, before[SUBLANES - 1:SUBLANES, :], 0.0)
        after_row = jnp.where(n < n_chunks - 1, after[0:1, :], 0.0)
        x_prev = jnp.where(row_id == 0, before_row, pltpu.roll(x, 1, axis=0))
        x_next = jnp.where(row_id == c - 1, after_row, pltpu.roll(x, c - 1, axis=0))
        y = _silu(x_prev * w_prev + x * w_mid + x_next * w_next)
        qk = y[:, 0:2 * GROUP_W]
        ones2 = ones_ref[...]
        ss = jnp.concatenate([_head_sum(qk[:, 0:GROUP_W] * qk[:, 0:GROUP_W], ones2),
                              _head_sum(qk[:, GROUP_W:] * qk[:, GROUP_W:], ones2)], axis=1)
        qk = qk * lax.rsqrt(ss + EPS)
        q, k, v = qk[:, 0:GROUP_W] * (HEAD_DIM ** -0.5), qk[:, GROUP_W:], y[:, 2 * GROUP_W:]
        k_hi, k_lo = _split2(k)
        bd_k = _head_blockdiag(k_hi, ones2)
        shared = _bd_dot_nt(jnp.concatenate([k_hi, k_lo, q.astype(BF16)], axis=0), bd_k)
        kk = shared[0:c] + shared[c:2 * c] + _bd_dot_nt(k_hi, _head_blockdiag(k_lo, ones2))
        qk_raw = shared[2 * c:3 * c]
        ba = p_ref[0, pl.ds(r0, c), 4 * GROUP_W:4 * GROUP_W + LANES]
        return dict(q=q, k=k, v=v, kk=kk, qk_raw=qk_raw, ba=ba)

    per_iter = 4 if n_chunks % 4 == 0 else 1

    def prepare(i, carry):
        chunks = [i * per_iter + j for j in range(per_iter)]
        probs = []
        for n in chunks:
            shared = conv_qkv(n)
            probs += [dict(shared, sel=sels[d], mask=masks[d], d=d) for d in range(2)]
        res = _delta_prepare(probs, ones_ref[...], expand_ref, nega_ref, dtb_ref, c)
        for idx, (u, w16, qk16, qd, ktail, dlast) in enumerate(res):
            n, d = chunks[idx // 2], idx % 2
            rows = pl.ds(pl.multiple_of(n * c, c), c)
            u_ref[d, rows, :] = u
            w_ref[d, rows, :] = w16
            qk_ref[d, rows, :] = qk16
            qd_ref[d, rows, :] = qd
            kt_ref[d, rows, :] = ktail
            dl_ref[d, pl.ds(pl.multiple_of(n * SUBLANES, SUBLANES), SUBLANES), :] = dlast
        return carry

    lax.fori_loop(0, n_chunks // per_iter, prepare, 0)

    def body(i, carry):
        items, rows_of = [], []
        for d, n in ((0, i), (1, n_chunks - 1 - i)):
            rows = pl.ds(pl.multiple_of(n * c, c), c)
            dlast = dl_ref[d, pl.ds(pl.multiple_of(n * SUBLANES, SUBLANES), SUBLANES), :][0:1, :]
            items.append((d, u_ref[d, rows, :], w_ref[d, rows, :], qk_ref[d, rows, :], qd_ref[d, rows, :],
                          kt_ref[d, rows, :], dlast))
            rows_of.append(rows)
        o_f, o_b = _delta_state_steps(items, ones_ref[...], state_ref)
        of_ref[rows_of[0], :] = o_f
        ob_ref[rows_of[1], :] = o_b
        return carry

    lax.fori_loop(0, n_chunks, body, 0)

    rbk = min(t, 256)

    def fin(i, carry):
        rows = pl.ds(pl.multiple_of(i * rbk, rbk), rbk)
        o = of_ref[rows, :] + ob_ref[rows, :]
        gate = p_ref[0, rows, 3 * GROUP_W:4 * GROUP_W]
        y_ref[0, rows, :] = _head_norm_gate(o, gate, onorm_ref[...], ones_ref[...]).astype(BF16)
        return carry

    lax.fori_loop(0, t // rbk, fin, 0)
    sout_ref[0] = state_ref[...]


def _delta_mixer(p, conv_w, expand, nega, dtb, onorm, s0):
    b, t, w = p.shape
    c = DN_CHUNK
    sel_f, mask_f = _dn_consts(c, False)
    sel_b, mask_b = _dn_consts(c, True)
    consts = [sel_f, sel_b, mask_f, mask_b, _ones_blockdiag(), onorm, conv_w, expand, nega, dtb]
    state_shape = (2, HEAD_DIM, GROUP_W)
    return pl.pallas_call(
        functools.partial(_delta_mixer_kernel, t, c),
        grid=(b,),
        in_specs=[pl.BlockSpec((1, t, w), lambda i: (i, 0, 0))]
                 + [_full_spec(a.shape) for a in consts]
                 + [pl.BlockSpec((1,) + state_shape, lambda i: (i, 0, 0, 0))],
        out_specs=[pl.BlockSpec((1, t, GROUP_W), lambda i: (i, 0, 0)),
                   pl.BlockSpec((1,) + state_shape, lambda i: (i, 0, 0, 0))],
        out_shape=[jax.ShapeDtypeStruct((b, t, GROUP_W), BF16),
                   jax.ShapeDtypeStruct((b,) + state_shape, F32)],
        scratch_shapes=[pltpu.VMEM((2, t, GROUP_W), F32)]
                       + [pltpu.VMEM((2, t, GROUP_W), BF16)] * 4
                       + [pltpu.VMEM((2, (t // c) * SUBLANES, GROUP_W), F32),
                          pltpu.VMEM((t, GROUP_W), F32), pltpu.VMEM((t, GROUP_W), F32),
                          pltpu.VMEM(state_shape, F32)],
        compiler_params=_params(("arbitrary",)),
        name="delta_mixer",
    )(p, *consts, s0)


def _s5_kernel(t, tc, u_ref, b_ref, cm_ref, lbar_ref, dskip_ref, gluw_ref, glub_ref, h0_ref,
               y_ref, sout_ref, x_ref, yf_ref, yb_ref, carry_ref):
    n_chunks = t // tc
    n_lane_blocks = S5_STATE // LANES
    carry_ref[...] = h0_ref[0]
    sub8 = lax.broadcasted_iota(jnp.int32, (SUBLANES, LANES), 0)

    def scan_dirs(rows_of, y_refs):
        for d in range(2):
            u16 = u_ref[0, rows_of[d], :].astype(BF16)
            for j in range(n_lane_blocks):
                x_ref[d, j] = _dot(u16, b_ref[d, j])
            y_refs[d][rows_of[d], :] = jnp.zeros((tc, GROUP_W), F32)

        def scan_block(d, j):
            bu = x_ref[d, j]
            xr, xi = bu[:, 0:LANES], bu[:, LANES:2 * LANES]
            lb = lbar_ref[d, j]
            pw = [(lb[0:1, :], lb[1:2, :])]
            for _ in range(SUBLANES - 1):
                pr, pi = pw[-1]
                pw.append((pr * pw[0][0] - pi * pw[0][1], pr * pw[0][1] + pi * pw[0][0]))
            n_groups = tc // SUBLANES
            blocks_r = [xr[v * SUBLANES:(v + 1) * SUBLANES] for v in range(n_groups)]
            blocks_i = [xi[v * SUBLANES:(v + 1) * SUBLANES] for v in range(n_groups)]
            for s in range(int(np.log2(SUBLANES))):
                sh = 1 << s
                keep = sub8 < SUBLANES - sh if d else sub8 >= sh
                m_re, m_im = jnp.where(keep, pw[sh - 1][0], 0.0), jnp.where(keep, pw[sh - 1][1], 0.0)
                rot = SUBLANES - sh if d else sh
                for v in range(n_groups):
                    sr, si = pltpu.roll(blocks_r[v], rot, axis=0), pltpu.roll(blocks_i[v], rot, axis=0)
                    blocks_r[v] = blocks_r[v] + m_re * sr - m_im * si
                    blocks_i[v] = blocks_i[v] + m_re * si + m_im * sr
            p_re = jnp.zeros((SUBLANES, LANES), F32)
            p_im = jnp.zeros((SUBLANES, LANES), F32)
            for r in range(SUBLANES):
                pr, pi = pw[SUBLANES - 1 - r] if d else pw[r]
                p_re = jnp.where(sub8 == r, pr, p_re)
                p_im = jnp.where(sub8 == r, pi, p_im)
            cr, ci = carry_ref[d, 0, j], carry_ref[d, 1, j]
            last = 0 if d else SUBLANES - 1
            for step in range(n_groups):
                v = n_groups - 1 - step if d else step
                blocks_r[v] = blocks_r[v] + (p_re * cr - p_im * ci)
                blocks_i[v] = blocks_i[v] + (p_re * ci + p_im * cr)
                cr, ci = blocks_r[v][last:last + 1, :], blocks_i[v][last:last + 1, :]
            carry_ref[d, 0, j] = cr
            carry_ref[d, 1, j] = ci
            return jnp.concatenate([jnp.concatenate(blocks_r, axis=0), jnp.concatenate(blocks_i, axis=0)],
                                   axis=1).astype(BF16)

        def lane_block(j, carry):
            states = [scan_block(d, j) for d in range(2)]
            for d in range(2):
                y_refs[d][rows_of[d], :] += _dot(states[d], cm_ref[j])
            return carry

        lax.fori_loop(0, n_lane_blocks, lane_block, 0)

    def body(n, carry):
        scan_dirs((pl.ds(pl.multiple_of(n * tc, tc), tc), pl.ds(pl.multiple_of((n_chunks - 1 - n) * tc, tc), tc)),
                  (yf_ref, yb_ref))
        return carry

    lax.fori_loop(0, n_chunks, body, 0)

    def fin(i, carry):
        rows = pl.ds(pl.multiple_of(i * tc, tc), tc)
        u = u_ref[0, rows, :]
        y = _gelu_tanh(yf_ref[rows, :] + yb_ref[rows, :] + dskip_ref[...] * u)
        z = _dot(y.astype(BF16), gluw_ref[...]) + glub_ref[...]
        y_ref[0, rows, :] = (y * _sigmoid(z)).astype(BF16)
        return carry

    lax.fori_loop(0, n_chunks, fin, 0)
    sout_ref[0] = carry_ref[...]


def _s5_mixer(u, b_in, cmat, lbar, dskip, glu_w, glu_b, h0):
    b, t, _ = u.shape
    tc = min(S5_CHUNK, t)
    nlb = S5_STATE // LANES
    consts = [b_in, cmat, lbar, dskip, glu_w, glu_b]
    st_shape = (2, 2, nlb, 1, LANES)
    return pl.pallas_call(
        functools.partial(_s5_kernel, t, tc),
        grid=(b,),
        in_specs=[pl.BlockSpec((1, t, GROUP_W), lambda i: (i, 0, 0))]
                 + [_full_spec(a.shape) for a in consts]
                 + [pl.BlockSpec((1,) + st_shape, lambda i: (i, 0, 0, 0, 0, 0))],
        out_specs=[pl.BlockSpec((1, t, GROUP_W), lambda i: (i, 0, 0)),
                   pl.BlockSpec((1,) + st_shape, lambda i: (i, 0, 0, 0, 0, 0))],
        out_shape=[jax.ShapeDtypeStruct((b, t, GROUP_W), BF16),
                   jax.ShapeDtypeStruct((b,) + st_shape, F32)],
        scratch_shapes=[pltpu.VMEM((2, nlb, tc, 2 * LANES), F32), pltpu.VMEM((t, GROUP_W), F32),
                        pltpu.VMEM((t, GROUP_W), F32), pltpu.VMEM(st_shape, F32)],
        compiler_params=_params(("arbitrary",)),
        name="s5_mixer",
    )(u, *consts, h0)


def _outproj_kernel(x_ref, mod_ref, ya_ref, yb_ref, yc_ref, yd_ref, w_ref, o_ref):
    y = (_dot(ya_ref[0], w_ref[0]) + _dot(yb_ref[0], w_ref[1])
         + _dot(yc_ref[0], w_ref[2]) + _dot(yd_ref[0], w_ref[3]))
    o_ref[0] = x_ref[0] + mod_ref[0][:, 2 * D_MODEL:3 * D_MODEL] * y


def _out_projection(x, mod, ys, w_out):
    b, t, _ = x.shape
    tb = min(TOKEN_BLOCK, t)
    tok = lambda n: pl.BlockSpec((1, tb, n), lambda i, j: (i, j, 0))
    return pl.pallas_call(
        _outproj_kernel,
        grid=(b, t // tb),
        in_specs=[tok(D_MODEL), pl.BlockSpec((1, 1, 6 * D_MODEL), lambda i, j: (i, 0, 0))]
                 + [tok(GROUP_W)] * 4 + [_full_spec(w_out.shape)],
        out_specs=tok(D_MODEL),
        out_shape=jax.ShapeDtypeStruct(x.shape, F32),
        compiler_params=_params(("arbitrary", "arbitrary")),
        name="out_projection",
    )(x, mod, *ys, w_out)


def _topk_rows(scores, k, row_ids):
    n = scores[0].shape[1]
    slot = lax.broadcasted_iota(jnp.int32, (k, n), 0)
    scores = list(scores)
    vals = [jnp.zeros((k, n), F32) for _ in scores]
    ids = [jnp.zeros((k, n), F32) for _ in scores]
    for it in range(k):
        ms = [jnp.max(s, axis=0, keepdims=True) for s in scores]
        idxs = [jnp.min(jnp.where(s == m, row_ids, 1e9), axis=0, keepdims=True) for s, m in zip(scores, ms)]
        vals = [jnp.where(slot == it, m, v) for m, v in zip(ms, vals)]
        ids = [jnp.where(slot == it, idx, i) for idx, i in zip(idxs, ids)]
        scores = [jnp.where(row_ids == idx, -jnp.inf, s) for s, idx in zip(scores, idxs)]
    return list(zip(vals, ids))


def _sorting_network(n):
    pairs = []

    def merge(lo, hi, r):
        step = r * 2
        if step < hi - lo:
            merge(lo, hi, step)
            merge(lo + r, hi, step)
            pairs.extend((i, i + r) for i in range(lo + r, hi - r, step))
        else:
            pairs.append((lo, lo + r))

    def sort(lo, hi):
        if hi > lo:
            mid = lo + (hi - lo) // 2
            sort(lo, mid)
            sort(mid + 1, hi)
            merge(lo, hi, 1)

    sort(0, n - 1)
    return pairs


def _lex_first(va, ia, vb, ib):
    tie = va == vb
    return jnp.where(tie, -ia, va) > jnp.where(tie, -ib, vb)


def _topk_columns(scores, k, row_ids):
    n_wires = scores[0].shape[0] // SUBLANES
    n = scores[0].shape[1]
    assert n_wires == k
    net = _sorting_network(n_wires)
    slot = lax.broadcasted_iota(jnp.int32, (k, n), 0)
    sub = lambda x, w: x[w * SUBLANES:(w + 1) * SUBLANES]
    vs = [[sub(s, w) for w in range(n_wires)] for s in scores]
    ids = [[sub(row_ids, w) for w in range(n_wires)] for _ in scores]
    for i, j in net:
        for v, d in zip(vs, ids):
            first = _lex_first(v[i], d[i], v[j], d[j])
            v[i], v[j] = jnp.where(first, v[i], v[j]), jnp.where(first, v[j], v[i])
            d[i], d[j] = jnp.where(first, d[i], d[j]), jnp.where(first, d[j], d[i])
    vals = [jnp.zeros((k, n), F32) for _ in scores]
    wins = [jnp.zeros((k, n), F32) for _ in scores]
    for it in range(k):
        for p, (v, d) in enumerate(zip(vs, ids)):
            hv, hd = v[0], d[0]
            for sh in (4, 2, 1):
                pv, pd = pltpu.roll(hv, sh, axis=0), pltpu.roll(hd, sh, axis=0)
                first = _lex_first(hv, hd, pv, pd)
                hv, hd = jnp.where(first, hv, pv), jnp.where(first, hd, pd)
            vals[p] = jnp.where(slot == it, hv[0:1, :], vals[p])
            wins[p] = jnp.where(slot == it, hd[0:1, :], wins[p])
            popped = d[0] == hd
            for w in range(k - 1 - it):
                v[w] = jnp.where(popped, v[w + 1], v[w])
                d[w] = jnp.where(popped, d[w + 1], d[w])
    return list(zip(vals, wins))


def _staircase():
    pairs = [(i, j) for i in range(PEER_TOPK) for j in range(PEER_TOPK) if (i + 1) * (j + 1) <= PEER_TOPK]
    rows = -(-len(pairs) // SUBLANES) * SUBLANES
    sel = np.zeros((2, rows, PEER_TOPK), np.float32)
    aux = np.zeros((2, rows, LANES), np.float32)
    aux[0, len(pairs):] = -np.inf
    aux[1, len(pairs):] = 1e9
    for r, (i, j) in enumerate(pairs):
        sel[0, r, i] = 1.0
        sel[1, r, j] = 1.0
        aux[1, r] = i * PEER_TOPK + j
    return jnp.asarray(np.tile(sel, (1, 1, 3)), BF16), jnp.asarray(aux, F32)


def _take_rows(table, idx, k):
    out = jnp.zeros(idx.shape, F32)
    for r in range(k):
        out = jnp.where(idx == float(r), table[r:r + 1, :], out)
    return out


def _peer_route_kernel(tb, x_ref, mod_ref, g_ref, wqh_ref, wql_ref, kh_ref, kl_ref, stsel_ref, staux_ref,
                       h_ref, a_ref, c_ref, gate_ref, qt_ref, at_ref, ct_ref, gt_ref):
    mod = mod_ref[0]
    h = _rms_modulate(x_ref[0], g_ref[...], mod[:, 3 * D_MODEL:4 * D_MODEL], mod[:, 4 * D_MODEL:5 * D_MODEL])
    h_hi, h_lo = _split2(h)
    h_ref[0] = h_hi
    wq_hi = wqh_ref[...]
    qt_ref[...] = _dot_nt(wq_hi, h_hi) + _dot_nt(wq_hi, h_lo) + _dot_nt(wql_ref[...], h_hi)
    half = PEER_DQ // 2
    key_ids = lax.broadcasted_iota(jnp.int32, (PEER_NKEYS, LANES), 0).astype(F32)

    heads_per_iter = 4

    def head_body(it, carry):
        heads = [it * heads_per_iter + i for i in range(heads_per_iter)]
        tiles = [slice(lt * LANES, (lt + 1) * LANES) for lt in range(tb // LANES)]
        probs = [(head, lanes) for head in heads for lanes in tiles]
        scores = []
        for head, lanes in probs:
            for p in range(2):
                hp = head * 2 + p
                q_hi, q_lo = _split2(qt_ref[pl.ds(pl.multiple_of(hp * half, half), half), lanes])
                scores.append(_dot(kh_ref[hp], q_hi) + _dot(kh_ref[hp], q_lo) + _dot(kl_ref[hp], q_hi))
        top = _topk_columns(scores, PEER_TOPK, key_ids)
        cands = [_sel_dot(stsel_ref[0], top[2 * i][0]) + _sel_dot(stsel_ref[1], top[2 * i + 1][0]) + staux_ref[0]
                 for i in range(len(probs))]
        best = _topk_rows(cands, PEER_TOPK, staux_ref[1])
        for i, (head, lanes) in enumerate(probs):
            out_rows = pl.ds(pl.multiple_of(head * PEER_TOPK, PEER_TOPK), PEER_TOPK)
            sc, flat = best[i]
            r1 = jnp.floor(flat * (1.0 / PEER_TOPK))
            r2 = flat - r1 * PEER_TOPK
            e = jnp.exp(sc - sc[0:1, :])
            at_ref[out_rows, lanes] = _take_rows(top[2 * i][1], r1, PEER_TOPK)
            ct_ref[out_rows, lanes] = _take_rows(top[2 * i + 1][1], r2, PEER_TOPK)
            gt_ref[out_rows, lanes] = e / jnp.sum(e, axis=0, keepdims=True)
        return carry

    lax.fori_loop(0, PEER_HEADS // heads_per_iter, head_body, 0)
    a_ref[0] = at_ref[...].T.astype(jnp.int32)
    c_ref[0] = ct_ref[...].T.astype(jnp.int32)
    gate_ref[0] = gt_ref[...].T


def _peer_route(x, mod, gain, wq_hi, wq_lo, keys_hi, keys_lo):
    b, t, _ = x.shape
    tb = min(TOKEN_BLOCK, t)
    tok = lambda n: pl.BlockSpec((1, tb, n), lambda i, j: (i, j, 0))
    consts = [gain, wq_hi, wq_lo, keys_hi, keys_lo, *_staircase()]
    return pl.pallas_call(
        functools.partial(_peer_route_kernel, tb),
        grid=(b, t // tb),
        scratch_shapes=[pltpu.VMEM((PEER_HEADS * PEER_DQ, tb), F32)] + [pltpu.VMEM((PEER_SEL, tb), F32)] * 3,
        in_specs=[tok(D_MODEL), pl.BlockSpec((1, 1, 6 * D_MODEL), lambda i, j: (i, 0, 0))]
                 + [_full_spec(a.shape) for a in consts],
        out_specs=[tok(D_MODEL), tok(PEER_SEL), tok(PEER_SEL), tok(PEER_SEL)],
        out_shape=[jax.ShapeDtypeStruct((b, t, D_MODEL), BF16),
                   jax.ShapeDtypeStruct((b, t, PEER_SEL), jnp.int32),
                   jax.ShapeDtypeStruct((b, t, PEER_SEL), jnp.int32),
                   jax.ShapeDtypeStruct((b, t, PEER_SEL), F32)],
        compiler_params=_params(("arbitrary", "arbitrary")),
        name="peer_route",
    )(x, mod, *consts)


def _peer_expert_kernel(tb, n_slabs, x_ref, mod_ref, h_ref, a_ref, c_ref, gate_ref, ut_ref, v_ref,
                        o_ref, gmat_ref, acc_ref):
    j = pl.program_id(2)
    nk = PEER_NKEYS

    @pl.when(j == 0)
    def _():
        acc_ref[...] = jnp.zeros_like(acc_ref)
        key_id = lax.broadcasted_iota(jnp.int32, (nk, PEER_SEL), 0)

        pack = 2 * SUBLANES
        groups = 4

        def build(nb, carry):
            n0 = pl.multiple_of(nb * groups * pack, groups * pack)
            a_rows = a_ref[0, pl.ds(n0, groups * pack), :]
            c_rows = c_ref[0, pl.ds(n0, groups * pack), :]
            g_rows = gate_ref[0, pl.ds(n0, groups * pack), :]
            mats = []
            for r in range(groups * pack):
                row_sel = jnp.where(key_id == a_rows[r:r + 1], 1.0, 0.0).astype(BF16)
                col_sel = jnp.where(key_id == c_rows[r:r + 1], g_rows[r:r + 1], 0.0).astype(BF16)
                mats.append(_dot_nt(row_sel, col_sel).astype(BF16))
            for gi in range(groups):
                stacked = jnp.stack(mats[gi * pack:(gi + 1) * pack], axis=0)
                gmat_ref[:, pl.ds(n0 + gi * pack, pack), :] = pltpu.einshape("nac->anc", stacked)
            return carry

        lax.fori_loop(0, tb // (groups * pack), build, 0)

    h = h_ref[0]
    n_pairs = n_slabs // 2
    cols = [slice(i * 2 * nk, (i + 1) * 2 * nk) for i in range(n_pairs)]
    lookahead = 2
    s = [_dot(h, ut_ref[:, cols[i]]) for i in range(min(lookahead, n_pairs))]
    acc = None
    for i in range(n_pairs):
        if i + lookahead < n_pairs:
            s.append(_dot(h, ut_ref[:, cols[i + lookahead]]))
        a0 = j * n_slabs + 2 * i
        g = jnp.concatenate([gmat_ref[a0], gmat_ref[a0 + 1]], axis=1)
        w = g * _gelu_tanh(s[i]).astype(BF16)
        p = _dot(w, v_ref[cols[i], :])
        acc = p if acc is None else acc + p
    acc_ref[...] += acc

    @pl.when(j == pl.num_programs(2) - 1)
    def _():
        o_ref[0] = x_ref[0] + mod_ref[0][:, 5 * D_MODEL:6 * D_MODEL] * acc_ref[...]


def _peer_experts(x, mod, h, a, c, gate, u_t, v_tab):
    b, t, _ = x.shape
    tb = min(PEER_TOKEN_BLOCK, t)
    n_exp = u_t.shape[1]
    eb = PEER_EXPERT_BLOCK
    n_slabs = eb // PEER_NKEYS
    tok = lambda n: pl.BlockSpec((1, tb, n), lambda i, k, j: (i, k, 0))
    return pl.pallas_call(
        functools.partial(_peer_expert_kernel, tb, n_slabs),
        grid=(b, t // tb, n_exp // eb),
        in_specs=[tok(D_MODEL), pl.BlockSpec((1, 1, 6 * D_MODEL), lambda i, k, j: (i, 0, 0)),
                  tok(D_MODEL), tok(PEER_SEL), tok(PEER_SEL), tok(PEER_SEL),
                  pl.BlockSpec((D_MODEL, eb), lambda i, k, j: (0, j)),
                  pl.BlockSpec((eb, D_MODEL), lambda i, k, j: (j, 0))],
        out_specs=tok(D_MODEL),
        out_shape=jax.ShapeDtypeStruct(x.shape, F32),
        scratch_shapes=[pltpu.VMEM((PEER_NKEYS, tb, PEER_NKEYS), BF16), pltpu.VMEM((tb, D_MODEL), F32)],
        compiler_params=_params(("arbitrary", "arbitrary", "arbitrary")),
        name="peer_experts",
    )(x, mod, h, a, c, gate, u_t, v_tab)


def _final_norm_kernel(x_ref, g_ref, o_ref):
    x = x_ref[0]
    ms = jnp.mean(x * x, axis=-1, keepdims=True)
    o_ref[0] = x * lax.rsqrt(ms + EPS) * g_ref[...]


def _final_norm(x, gain):
    b, t, _ = x.shape
    tb = min(TOKEN_BLOCK, t)
    tok = pl.BlockSpec((1, tb, D_MODEL), lambda i, j: (i, j, 0))
    return pl.pallas_call(
        _final_norm_kernel,
        grid=(b, t // tb),
        in_specs=[tok, _full_spec((1, D_MODEL))],
        out_specs=tok,
        out_shape=jax.ShapeDtypeStruct(x.shape, F32),
        compiler_params=_params(("arbitrary", "arbitrary")),
        name="final_norm",
    )(x, gain)


def _hi_lo(w):
    hi = w.astype(BF16)
    return hi, (w - hi.astype(F32)).astype(BF16)


def _pad_cols(w, n):
    return jnp.pad(w, ((0, 0), (0, n - w.shape[1])))


def _layer_weights(l, w_in, w_out, gla_gk_up, gla_gk_bias, gla_onorm_g, dn_conv_w, dn_A_log, dn_dt_bias,
                   dn_onorm_g, s5_lambda_re, s5_lambda_im, s5_log_dt, s5_B_re, s5_B_im, s5_C_re, s5_C_im,
                   s5_D, s5_glu_w, s5_glu_b, lower_bound, hgrn_onorm_g, peer_wq, peer_keys, peer_u, peer_v):
    gw = GROUP_W
    sizes = (gw, gw, gw, gw, 2 * GLA_LOWRANK, gw, gw, gw, gw, 2 * N_HEADS, 2 * N_HEADS, gw, gw, gw, gw, 2 * gw)
    offs = np.concatenate([[0], np.cumsum(sizes)])
    col = lambda a, b: w_in[l][:, int(offs[a]):int(offs[b])]
    wl = {}
    wl["w_gla"] = _pad_cols(col(0, 5), GLA_W).astype(BF16)
    wl["w_dn"] = _pad_cols(col(5, 11), DN_W).astype(BF16)
    wl["w_s5"] = col(11, 12).astype(BF16)
    wl["w_hg"] = col(12, 16).astype(BF16)
    wl["w_out"] = w_out[l].reshape(4, gw, D_MODEL).astype(BF16)
    tile_head = lambda g: jnp.tile(g, N_HEADS).reshape(1, gw)
    up = jnp.zeros((2, LANES, gw), F32)
    for d in range(2):
        up = up.at[d, d * GLA_LOWRANK:(d + 1) * GLA_LOWRANK, :].set(gla_gk_up[l, d])
    wl["gla_up_hi"], wl["gla_up_lo"] = _hi_lo(up)
    wl["gla_bias"] = gla_gk_bias[l].reshape(2, 1, gw)
    wl["gla_onorm"] = tile_head(gla_onorm_g[l])
    expand = np.zeros((2, LANES, 2 * gw), np.float32)
    for d in range(2):
        for h in range(N_HEADS):
            expand[d, d * N_HEADS + h, h * HEAD_DIM:(h + 1) * HEAD_DIM] = 1.0
            expand[d, 2 * N_HEADS + d * N_HEADS + h, gw + h * HEAD_DIM:gw + (h + 1) * HEAD_DIM] = 1.0
    wl["dn_expand"] = jnp.asarray(expand, BF16)
    wl["dn_conv"] = jnp.pad(dn_conv_w[l], ((0, SUBLANES - 3), (0, 0)))
    wl["dn_nega"] = jnp.repeat(-jnp.exp(dn_A_log[l]), HEAD_DIM, axis=-1).reshape(2, 1, gw)
    wl["dn_dtb"] = jnp.repeat(dn_dt_bias[l], HEAD_DIM, axis=-1).reshape(2, 1, gw)
    wl["dn_onorm"] = tile_head(dn_onorm_g[l])
    nlb = S5_STATE // LANES
    lam_re, lam_im = s5_lambda_re[l].astype(F32), s5_lambda_im[l].astype(F32)
    dt = jnp.exp(s5_log_dt[l].astype(F32))[:, :, None]
    mag = jnp.exp(lam_re * dt)
    lbar_re, lbar_im = mag * jnp.cos(lam_im * dt), mag * jnp.sin(lam_im * dt)
    den = lam_re * lam_re + lam_im * lam_im
    num_re, num_im = lbar_re - 1.0, lbar_im
    coef_re = (num_re * lam_re + num_im * lam_im) / den
    coef_im = (num_im * lam_re - num_re * lam_im) / den
    b_re, b_im = s5_B_re[l].astype(F32), s5_B_im[l].astype(F32)
    bb_re = coef_re[..., None] * b_re - coef_im[..., None] * b_im
    bb_im = coef_re[..., None] * b_im + coef_im[..., None] * b_re
    eye_g = jnp.eye(S5_G, dtype=F32)
    bd_in = lambda m: jnp.einsum('dgpc,gk->dgckp', m, eye_g).reshape(2, gw, S5_STATE)
    blk = lambda m: m.reshape(2, gw, nlb, LANES).transpose(0, 2, 1, 3)
    b_cat = jnp.concatenate([blk(bd_in(bb_re)), blk(bd_in(bb_im))], axis=-1)
    wl["s5_b"] = b_cat.astype(BF16)
    bd_out = lambda m: jnp.einsum('gcp,gk->gpkc', m.astype(F32), eye_g).reshape(S5_STATE, gw)
    c_re, c_im = bd_out(s5_C_re[l]), bd_out(s5_C_im[l])
    wl["s5_c"] = jnp.concatenate([c_re.reshape(nlb, LANES, gw), -c_im.reshape(nlb, LANES, gw)],
                                 axis=1).astype(BF16)
    wl["s5_lbar"] = jnp.stack([lbar_re.reshape(2, nlb, LANES), lbar_im.reshape(2, nlb, LANES)], axis=2)
    wl["s5_d"] = s5_D[l].reshape(1, gw)
    wl["s5_glu_w"] = s5_glu_w[l].astype(BF16)
    wl["s5_glu_b"] = s5_glu_b[l].reshape(1, gw)
    wl["hg_loglb"] = jnp.log(lower_bound).reshape(1, gw)
    wl["hg_log1m"] = jnp.log1p(-lower_bound).reshape(1, gw)
    wl["hg_1m"] = (1.0 - lower_bound).reshape(1, gw)
    wl["hg_onorm"] = tile_head(hgrn_onorm_g[l])
    wl["wq_hi"], wl["wq_lo"] = _hi_lo(peer_wq[l].T)
    wl["keys_hi"], wl["keys_lo"] = _hi_lo(peer_keys[l].reshape(PEER_HEADS * 2, PEER_NKEYS, PEER_DQ // 2))
    wl["u_t"] = peer_u[l].astype(BF16).T
    wl["v_tab"] = peer_v[l].astype(BF16)
    return wl


def _heads_to_lanes(s, transposed):
    perm = (0, 1, 4, 2, 3) if transposed else (0, 1, 3, 2, 4)
    return s.astype(F32).transpose(perm).reshape(s.shape[0], 2, HEAD_DIM, GROUP_W)


def _lanes_to_heads(s, transposed):
    s = s.reshape(s.shape[0], 2, HEAD_DIM, N_HEADS, HEAD_DIM)
    return s.transpose((0, 1, 3, 4, 2) if transposed else (0, 1, 3, 2, 4))


def _to_column_major(x):
    b, n, d = x.shape
    return x.reshape(b, n // GRID_W, GRID_W, d).transpose(0, 2, 1, 3).reshape(b, n, d)


def _to_row_major(x):
    b, n, d = x.shape
    return x.reshape(b, GRID_W, n // GRID_W, d).transpose(0, 2, 1, 3).reshape(b, n, d)


def _trunk_layer(x, mod, wl, norm1_g, norm2_g, states, shared_mod):
    st_gla, st_dn, st_s5, st_hg = states
    p_gla, p_dn, p_s5, p_hg = _in_projection(x, mod, norm1_g, [wl["w_gla"], wl["w_dn"], wl["w_s5"], wl["w_hg"]])
    y_a, n_gla = _decay_mixer("gla", p_gla, wl["gla_up_hi"], wl["gla_up_lo"], wl["gla_bias"], wl["gla_onorm"], st_gla)
    y_b, n_dn = _delta_mixer(p_dn, wl["dn_conv"], wl["dn_expand"], wl["dn_nega"], wl["dn_dtb"], wl["dn_onorm"], st_dn)
    y_c, n_s5 = _s5_mixer(p_s5, wl["s5_b"], wl["s5_c"], wl["s5_lbar"], wl["s5_d"],
                          wl["s5_glu_w"], wl["s5_glu_b"], st_s5)
    y_d, n_hg = _decay_mixer("hgrn", p_hg, wl["hg_loglb"], wl["hg_log1m"], wl["hg_1m"], wl["hg_onorm"], st_hg)
    x = _out_projection(x, mod, [y_a, y_b, y_c, y_d], wl["w_out"])
    h, a, c, gate = _peer_route(x, mod, norm2_g, wl["wq_hi"], wl["wq_lo"], wl["keys_hi"], wl["keys_lo"])
    b, t, _ = x.shape
    fold = max(1, min(PEER_TOKEN_BLOCK // t, b)) if shared_mod else 1
    fb = lambda z: z.reshape(b // fold, fold * t, z.shape[-1])
    x = _peer_experts(fb(x), mod[0:b // fold], fb(h), fb(a), fb(c), fb(gate), wl["u_t"], wl["v_tab"]).reshape(b, t, -1)
    return x, (n_gla, n_dn, n_s5, n_hg)


def kernel(x_prompt, x_sample, state_gla, state_delta, state_s5_re, state_s5_im, state_hgrn, c, c_ctx,
           mod_w, mod_b, norm1_g, norm2_g, w_in, w_out, gla_gk_up, gla_gk_bias, gla_onorm_g,
           dn_conv_w, dn_A_log, dn_dt_bias, dn_onorm_g, s5_lambda_re, s5_lambda_im, s5_log_dt,
           s5_B_re, s5_B_im, s5_C_re, s5_C_im, s5_D, s5_glu_w, s5_glu_b, hgrn_lb_logits,
           hgrn_onorm_g, peer_wq, peer_keys, peer_u, peer_v, final_norm_g):
    bp, bs = x_prompt.shape[0], x_sample.shape[0]
    nlb = S5_STATE // LANES
    lb_cum = jnp.cumsum(jax.nn.softmax(hgrn_lb_logits.astype(F32), axis=0), axis=0)
    lower_bounds = lb_cum - lb_cum[0:1]

    rows = 2 * SUBLANES
    cc = jnp.concatenate([c, c_ctx[None, :], jnp.zeros((rows - bs - 1, D_MODEL), F32)], axis=0)
    mod_all = _modulation(cc, mod_w, mod_b)

    zeros_hh = jnp.zeros((bp, 2, HEAD_DIM, GROUP_W), F32)
    zeros_s5 = jnp.zeros((bp, 2, 2, nlb, 1, LANES), F32)

    xp, xs = x_prompt, x_sample
    ctx_states = []
    for l in range(DEPTH):
        wl = _layer_weights(l, w_in, w_out, gla_gk_up, gla_gk_bias, gla_onorm_g, dn_conv_w, dn_A_log,
                            dn_dt_bias, dn_onorm_g, s5_lambda_re, s5_lambda_im, s5_log_dt, s5_B_re, s5_B_im,
                            s5_C_re, s5_C_im, s5_D, s5_glu_w, s5_glu_b, lower_bounds[l], hgrn_onorm_g,
                            peer_wq, peer_keys, peer_u, peer_v)
        n1, n2 = norm1_g[l].reshape(1, D_MODEL), norm2_g[l].reshape(1, D_MODEL)
        mod_ctx = jnp.broadcast_to(mod_all[l, bs].reshape(1, 1, 6 * D_MODEL), (bp, 1, 6 * D_MODEL))
        mod_lat = mod_all[l, 0:bs].reshape(bs, 1, 6 * D_MODEL)
        xp, st_ctx = _trunk_layer(xp, mod_ctx, wl, n1, n2, (zeros_hh, zeros_hh, zeros_s5, zeros_hh), True)
        ctx_states.append(st_ctx)
        cached_s5 = jnp.stack([state_s5_re[:, l], state_s5_im[:, l]], axis=2).reshape(bs, 2, 2, nlb, 1, LANES)
        cached = (_heads_to_lanes(state_gla[:, l], True), _heads_to_lanes(state_delta[:, l], False), cached_s5,
                  _heads_to_lanes(state_hgrn[:, l], True))
        if l % 2 == 1:
            xs = _to_column_major(xs)
        xs, _ = _trunk_layer(xs, mod_lat, wl, n1, n2, cached, False)
        if l % 2 == 1:
            xs = _to_row_major(xs)

    gain = final_norm_g.reshape(1, D_MODEL)
    y_prompt = _final_norm(xp, gain)
    y_sample = _final_norm(xs, gain)
    new_gla = jnp.stack([_lanes_to_heads(s[0], True) for s in ctx_states], axis=1)
    new_dn = jnp.stack([_lanes_to_heads(s[1], False) for s in ctx_states], axis=1)
    s5 = jnp.stack([s[2].reshape(bp, 2, 2, S5_G, S5_P) for s in ctx_states], axis=1)
    new_hg = jnp.stack([_lanes_to_heads(s[3], True) for s in ctx_states], axis=1)
    return (y_prompt, y_sample, new_gla, new_dn, s5[:, :, :, 0], s5[:, :, :, 1], new_hg)
```

```python
import functools

import numpy as np
import jax
import jax.numpy as jnp
from jax import lax
from jax.experimental import pallas as pl
from jax.experimental.pallas import tpu as pltpu

F32 = jnp.float32
BF16 = jnp.bfloat16

D_MODEL = 1024
DEPTH = 2
GRID_W = 64
GROUP_W = 256
HEAD_DIM = 64
N_HEADS = GROUP_W // HEAD_DIM
GLA_LOWRANK = 16
GLA_GATE_NORM = 16.0
DN_CHUNK = 64
S5_GROUP = 16
S5_G = GROUP_W // S5_GROUP
S5_P = 64
S5_STATE = S5_G * S5_P
PEER_HEADS = 8
PEER_NKEYS = 128
PEER_DQ = 256
PEER_TOPK = 16
PEER_SEL = PEER_HEADS * PEER_TOPK
EPS = 1e-6

LANES = 128
SUBLANES = 8
VMEM_LIMIT = 56 * 1024 * 1024

GLA_CHUNK = 64
S5_CHUNK = 1024
TOKEN_BLOCK = 256
PEER_TOKEN_BLOCK = 512
PEER_EXPERT_BLOCK = 2048
GLA_W = 4 * GROUP_W + LANES
DN_W = 4 * GROUP_W + LANES


def _dot(a, b):
    return jnp.dot(a, b, preferred_element_type=F32)


def _dot_nt(a, b):
    return lax.dot_general(a, b, (((1,), (1,)), ((), ())), preferred_element_type=F32)


def _dot_tn(a, b):
    return lax.dot_general(a, b, (((0,), (0,)), ((), ())), preferred_element_type=F32)


def _split2(x):
    hi = x.astype(BF16)
    lo = (x - hi.astype(F32)).astype(BF16)
    return hi, lo


def _split3(x):
    hi = x.astype(BF16)
    r = x - hi.astype(F32)
    mid = r.astype(BF16)
    lo = (r - mid.astype(F32)).astype(BF16)
    return hi, mid, lo


def _sel_dot(m01x3, x):
    return _dot(m01x3, jnp.concatenate(_split3(x), axis=0))


def _dot_sel(x, m01):
    hi, mid, lo = _split3(x)
    return _dot(hi, m01) + _dot(mid, m01) + _dot(lo, m01)


def _dot_hl(a, b_hi, b_lo):
    a_hi, a_lo = _split2(a)
    return _dot(a_hi, b_hi) + _dot(a_hi, b_lo) + _dot(a_lo, b_hi)


def _head_sum(x, ones_bd):
    hi, lo = _split2(x)
    return _dot(hi, ones_bd) + _dot(lo, ones_bd)


HEADS_PER_TILE = LANES // HEAD_DIM
N_HEAD_TILES = GROUP_W // LANES


def _head_blockdiag(x16, ones_bd):
    mask = ones_bd[0:LANES, 0:LANES]
    return [jnp.tile(x16[:, t * LANES:(t + 1) * LANES], (HEADS_PER_TILE, 1)) * mask for t in range(N_HEAD_TILES)]


def _bd_dot(a16, bds):
    return jnp.concatenate([_dot(a16[:, t * LANES:(t + 1) * LANES], bd) for t, bd in enumerate(bds)], axis=1)


def _bd_dot_nt(a16, bds):
    return jnp.concatenate([_dot_nt(a16[:, t * LANES:(t + 1) * LANES], bd) for t, bd in enumerate(bds)], axis=1)


def _per_head_mm(a, x, ones_bd):
    a_hi, a_lo = _split2(a)
    x_hi, x_lo = _split2(x)
    rows = a.shape[0]
    both = _bd_dot(jnp.concatenate([a_hi, a_lo], axis=0), _head_blockdiag(x_hi, ones_bd))
    return both[0:rows] + both[rows:2 * rows] + _bd_dot(a_hi, _head_blockdiag(x_lo, ones_bd))


def _per_head_mm1(a, x, ones_bd):
    return _bd_dot(a.astype(BF16), _head_blockdiag(x.astype(BF16), ones_bd))


def _per_head_dot_tn(a16, b16):
    lane_head = lax.broadcasted_iota(jnp.int32, (HEAD_DIM, LANES), 1) // HEAD_DIM
    outs = []
    for t in range(N_HEAD_TILES):
        full = _dot_tn(a16[:, t * LANES:(t + 1) * LANES], b16[:, t * LANES:(t + 1) * LANES])
        out = None
        for h in range(HEADS_PER_TILE):
            blk = jnp.where(lane_head == h, full[h * HEAD_DIM:(h + 1) * HEAD_DIM, :], 0.0)
            out = blk if out is None else out + blk
        outs.append(out)
    return jnp.concatenate(outs, axis=1)


def _log_sigmoid(x):
    return jnp.minimum(x, 0.0) - jnp.log(1.0 + jnp.exp(-jnp.abs(x)))


def _softplus(x):
    return jnp.maximum(x, 0.0) + jnp.log(1.0 + jnp.exp(-jnp.abs(x)))


def _sigmoid(x):
    return 1.0 / (1.0 + jnp.exp(-x))


def _silu(x):
    return x * _sigmoid(x)


def _gelu_tanh(x):
    return 0.5 * x * (1.0 + jnp.tanh(0.7978845608028654 * (x + 0.044715 * x * x * x)))


def _rms_modulate(x, gain, shift, scale):
    ms = jnp.mean(x * x, axis=-1, keepdims=True)
    return x * lax.rsqrt(ms + EPS) * gain * (1.0 + scale) + shift


def _full_spec(shape):
    zeros = (0,) * len(shape)
    return pl.BlockSpec(shape, lambda *_: zeros)


def _params(sem):
    return pltpu.CompilerParams(dimension_semantics=sem, vmem_limit_bytes=VMEM_LIMIT)


def _scan_consts(c, reverse):
    levels = int(np.log2(c))
    idx = np.arange(c)
    i, t = idx[:, None], idx[None, :]
    sel, masks = [], []
    for lv in range(levels):
        s = 1 << lv
        right = (idx // s) % 2 == 1
        same_blk = (i // s) == (t // s)
        sel.append(np.where(right[:, None], same_blk & (t <= i), same_blk & (t > i)))
        same_parent = (i // (2 * s)) == (t // (2 * s))
        masks.append(same_parent & right[:, None] & ~right[None, :])
    sel.append(t <= i)
    sel.append(t > i)
    masks.append(i == t)
    if reverse:
        sel = [m[::-1, ::-1] for m in sel]
        masks = [m[::-1, ::-1] for m in masks]
    sel.append(np.ones((SUBLANES, c), bool))
    sel = np.tile(np.concatenate(sel, axis=0).astype(np.float32), (1, 3))
    masks = np.tile(np.stack(masks).astype(np.float32), (1, 1, N_HEADS))
    return jnp.asarray(sel, BF16), jnp.asarray(masks, F32)


def _dn_consts(c, reverse):
    levels = int(np.log2(c))
    idx = np.arange(c)
    i, t = idx[:, None], idx[None, :]
    incl, tail = t <= i, t > i
    masks = []
    for lv in range(levels):
        s = 1 << lv
        right = (idx // s) % 2 == 1
        masks.append(((i // (2 * s)) == (t // (2 * s))) & right[:, None] & ~right[None, :])
    masks += [incl, incl & (i != t), i == t]
    if reverse:
        incl, tail = incl[::-1, ::-1], tail[::-1, ::-1]
        masks = [m[::-1, ::-1] for m in masks]
    sel = np.tile(np.concatenate([incl, tail, np.ones((SUBLANES, c), bool)], axis=0).astype(np.float32), (1, 3))
    masks = np.tile(np.stack(masks).astype(np.float32), (1, 1, N_HEADS))
    return jnp.asarray(sel, BF16), jnp.asarray(masks, F32)


def _ones_blockdiag():
    h = np.arange(GROUP_W) // HEAD_DIM
    return jnp.asarray((h[:, None] == h[None, :]).astype(np.float32), BF16)


def _mod_kernel(c_ref, w_ref, b_ref, o_ref):
    a_hi, a_lo = _split2(_silu(c_ref[...]))
    w_hi, w_lo = _split2(w_ref[0])
    o_ref[0] = _dot(a_hi, w_hi) + _dot(a_hi, w_lo) + _dot(a_lo, w_hi) + b_ref[0]


def _modulation(cc, mod_w, mod_b):
    rows = cc.shape[0]
    nb = 6
    return pl.pallas_call(
        _mod_kernel,
        grid=(DEPTH, nb),
        in_specs=[pl.BlockSpec((rows, D_MODEL), lambda l, j: (0, 0)),
                  pl.BlockSpec((1, D_MODEL, D_MODEL), lambda l, j: (l, 0, j)),
                  pl.BlockSpec((1, 1, D_MODEL), lambda l, j: (l, 0, j))],
        out_specs=pl.BlockSpec((1, rows, D_MODEL), lambda l, j: (l, 0, j)),
        out_shape=jax.ShapeDtypeStruct((DEPTH, rows, 6 * D_MODEL), F32),
        compiler_params=_params(("arbitrary", "arbitrary")),
        name="modulation",
    )(cc, mod_w, mod_b.reshape(DEPTH, 1, 6 * D_MODEL))


def _inproj_kernel(x_ref, mod_ref, g_ref, wa_ref, wb_ref, wc_ref, wd_ref, oa_ref, ob_ref, oc_ref, od_ref):
    mod = mod_ref[0]
    h = _rms_modulate(x_ref[0], g_ref[...], mod[:, 0:D_MODEL], mod[:, D_MODEL:2 * D_MODEL]).astype(BF16)
    oa_ref[0] = _dot(h, wa_ref[...])
    ob_ref[0] = _dot(h, wb_ref[...])
    oc_ref[0] = _dot(h, wc_ref[...])
    od_ref[0] = _dot(h, wd_ref[...])


def _in_projection(x, mod, gain, weights):
    b, t, _ = x.shape
    tb = min(TOKEN_BLOCK, t)
    widths = [w.shape[1] for w in weights]
    return pl.pallas_call(
        _inproj_kernel,
        grid=(b, t // tb),
        in_specs=[pl.BlockSpec((1, tb, D_MODEL), lambda i, j: (i, j, 0)),
                  pl.BlockSpec((1, 1, 6 * D_MODEL), lambda i, j: (i, 0, 0)),
                  _full_spec((1, D_MODEL))] + [_full_spec(w.shape) for w in weights],
        out_specs=[pl.BlockSpec((1, tb, n), lambda i, j: (i, j, 0)) for n in widths],
        out_shape=[jax.ShapeDtypeStruct((b, t, n), F32) for n in widths],
        compiler_params=_params(("arbitrary", "arbitrary")),
        name="in_projection",
    )(x, mod, gain, *weights)


def _decay_scan_prepare(probs, ones_bd, c):
    levels = int(np.log2(c))
    exs = [jnp.exp(_sel_dot(pr["sel"][...], pr["g"])) for pr in probs]
    attns = [None] * len(probs)
    for lv in range(levels + 1):
        for i, (pr, ex) in enumerate(zip(probs, exs)):
            q, k = pr["q"], pr["k"]
            if lv < levels:
                e = ex[lv * c:(lv + 1) * c]
                qt, kt = (q * e).astype(BF16), (k * e).astype(BF16)
            else:
                qt, kt = q.astype(BF16), k.astype(BF16)
            p = _bd_dot_nt(qt, _head_blockdiag(kt, ones_bd)) * pr["mask"][lv]
            attns[i] = p if attns[i] is None else attns[i] + p
    res = []
    for pr, ex, attn in zip(probs, exs, attns):
        o = _bd_dot(attn.astype(BF16), _head_blockdiag(pr["v"].astype(BF16), ones_bd))
        qd = (pr["q"] * ex[levels * c:(levels + 1) * c]).astype(BF16)
        ktail = (pr["k"] * ex[(levels + 1) * c:(levels + 2) * c]).astype(BF16)
        res.append((o, qd, ktail, ex[(levels + 2) * c:(levels + 2) * c + SUBLANES]))
    return res


def _decay_scan_state_steps(items, ones_bd, state_ref):
    sts = [state_ref[it[0]] for it in items]
    outs = [it[1] + _bd_dot_nt(it[2], _head_blockdiag(st.astype(BF16), ones_bd)) for it, st in zip(items, sts)]
    for it, st in zip(items, sts):
        state_ref[it[0]] = st * it[5] + _per_head_dot_tn(it[4], it[3])
    return outs


def _head_norm_gate(o, gate, onorm, ones_bd):
    ms = _head_sum(o * o, ones_bd) * (1.0 / HEAD_DIM)
    return o * lax.rsqrt(ms + EPS) * onorm * _silu(gate)


def _decay_mixer_kernel(mode, t, c, p_ref, self_ref, selb_ref, maskf_ref, maskb_ref, ones_ref, onorm_ref,
                        pa_ref, pb_ref, pc_ref, h0_ref, y_ref, sout_ref,
                        qd_ref, kt_ref, v16_ref, dl_ref, of_ref, ob_ref, state_ref):
    n_chunks = t // c
    state_ref[...] = h0_ref[0]
    sels, masks, o_refs = (self_ref, selb_ref), (maskf_ref, maskb_ref), (of_ref, ob_ref)

    def inputs(rows, d):
        if mode == "gla":
            q = p_ref[0, rows, 0:GROUP_W] * (HEAD_DIM ** -0.5)
            k = p_ref[0, rows, GROUP_W:2 * GROUP_W]
            v = p_ref[0, rows, 2 * GROUP_W:3 * GROUP_W]
            low = p_ref[0, rows, 4 * GROUP_W:4 * GROUP_W + LANES]
            gk = _dot_hl(low, pa_ref[d], pb_ref[d]) + pc_ref[d]
            g = _log_sigmoid(gk) * (1.0 / GLA_GATE_NORM)
        else:
            q = _silu(p_ref[0, rows, 0:GROUP_W]) * (HEAD_DIM ** -0.5)
            v = p_ref[0, rows, GROUP_W:2 * GROUP_W]
            z = p_ref[0, rows, (3 + d) * GROUP_W:(4 + d) * GROUP_W]
            a = pa_ref[...]
            b = pb_ref[...] + _log_sigmoid(z)
            m = jnp.maximum(a, b)
            g = m + jnp.log(jnp.exp(a - m) + jnp.exp(b - m))
            k = pc_ref[...] * _sigmoid(-z)
        return q, k, v, g

    per_iter = 4 if n_chunks % 4 == 0 else 1

    def prepare(i, carry):
        chunks = [i * per_iter + j for j in range(per_iter)]
        probs = []
        for n in chunks:
            rows = pl.ds(pl.multiple_of(n * c, c), c)
            for d in range(2):
                q, k, v, g = inputs(rows, d)
                probs.append(dict(q=q, k=k, v=v, g=g, sel=sels[d], mask=masks[d]))
            v16_ref[rows, :] = probs[-1]["v"].astype(BF16)
        for idx, (o, qd, ktail, dlast) in enumerate(_decay_scan_prepare(probs, ones_ref[...], c)):
            n, d = chunks[idx // 2], idx % 2
            rows = pl.ds(pl.multiple_of(n * c, c), c)
            o_refs[d][rows, :] = o
            qd_ref[d, rows, :] = qd
            kt_ref[d, rows, :] = ktail
            dl_ref[d, pl.ds(pl.multiple_of(n * SUBLANES, SUBLANES), SUBLANES), :] = dlast
        return carry

    lax.fori_loop(0, n_chunks // per_iter, prepare, 0)

    def body(i, carry):
        items, rows_of = [], []
        for d, n in ((0, i), (1, n_chunks - 1 - i)):
            rows = pl.ds(pl.multiple_of(n * c, c), c)
            dlast = dl_ref[d, pl.ds(pl.multiple_of(n * SUBLANES, SUBLANES), SUBLANES), :][0:1, :]
            items.append((d, o_refs[d][rows, :], qd_ref[d, rows, :], kt_ref[d, rows, :], v16_ref[rows, :], dlast))
            rows_of.append(rows)
        o_f, o_b = _decay_scan_state_steps(items, ones_ref[...], state_ref)
        of_ref[rows_of[0], :] = o_f
        ob_ref[rows_of[1], :] = o_b
        return carry

    lax.fori_loop(0, n_chunks, body, 0)

    rbk = min(t, 256)
    gate_col = 3 * GROUP_W if mode == "gla" else 2 * GROUP_W

    def fin(i, carry):
        rows = pl.ds(pl.multiple_of(i * rbk, rbk), rbk)
        o = of_ref[rows, :] + ob_ref[rows, :]
        gate = p_ref[0, rows, gate_col:gate_col + GROUP_W]
        y_ref[0, rows, :] = _head_norm_gate(o, gate, onorm_ref[...], ones_ref[...]).astype(BF16)
        return carry

    lax.fori_loop(0, t // rbk, fin, 0)
    sout_ref[0] = state_ref[...]


def _decay_mixer(mode, p, pa, pb, pc, onorm, h0t):
    b, t, w = p.shape
    c = GLA_CHUNK
    sel_f, mask_f = _scan_consts(c, False)
    sel_b, mask_b = _scan_consts(c, True)
    ones_bd = _ones_blockdiag()
    consts = [sel_f, sel_b, mask_f, mask_b, ones_bd, onorm, pa, pb, pc]
    state_shape = (2, HEAD_DIM, GROUP_W)
    return pl.pallas_call(
        functools.partial(_decay_mixer_kernel, mode, t, c),
        grid=(b,),
        in_specs=[pl.BlockSpec((1, t, w), lambda i: (i, 0, 0))]
                 + [_full_spec(a.shape) for a in consts]
                 + [pl.BlockSpec((1,) + state_shape, lambda i: (i, 0, 0, 0))],
        out_specs=[pl.BlockSpec((1, t, GROUP_W), lambda i: (i, 0, 0)),
                   pl.BlockSpec((1,) + state_shape, lambda i: (i, 0, 0, 0))],
        out_shape=[jax.ShapeDtypeStruct((b, t, GROUP_W), BF16),
                   jax.ShapeDtypeStruct((b,) + state_shape, F32)],
        scratch_shapes=[pltpu.VMEM((2, t, GROUP_W), BF16), pltpu.VMEM((2, t, GROUP_W), BF16),
                        pltpu.VMEM((t, GROUP_W), BF16), pltpu.VMEM((2, (t // c) * SUBLANES, GROUP_W), F32),
                        pltpu.VMEM((t, GROUP_W), F32), pltpu.VMEM((t, GROUP_W), F32),
                        pltpu.VMEM(state_shape, F32)],
        compiler_params=_params(("arbitrary",)),
        name=mode + "_mixer",
    )(p, *consts, h0t)


def _delta_prepare(probs, ones_bd, expand_ref, nega_ref, dtb_ref, c):
    levels = int(np.log2(c))
    work = []
    for pr in probs:
        d, mask = pr["d"], pr["mask"]
        logits = _dot_sel(pr["ba"], expand_ref[d])
        beta = _sigmoid(logits[:, 0:GROUP_W])
        g = nega_ref[d] * _softplus(logits[:, GROUP_W:2 * GROUP_W] + dtb_ref[d])
        cs = _sel_dot(pr["sel"][...], g)
        b_incl = cs[0:c]
        b_t = jnp.concatenate([b_incl[:, h * HEAD_DIM:(h + 1) * HEAD_DIM].T for h in range(N_HEADS)], axis=1)
        lmat = jnp.exp(jnp.minimum(b_incl - b_t, 0.0)) * mask[levels]
        m = beta * pr["kk"] * lmat * mask[levels + 1]
        work.append(dict(beta=beta, eb=jnp.exp(b_incl), etail=jnp.exp(cs[c:2 * c]),
                         dlast=jnp.exp(cs[2 * c:2 * c + SUBLANES]), lmat=lmat, m=m,
                         t=mask[levels + 2] - m * mask[0]))
    for lv in range(1, levels):
        mm = _per_head_mm1 if lv == levels - 1 else _per_head_mm
        ps = [mm(wk["m"] * pr["mask"][lv], wk["t"], ones_bd) for pr, wk in zip(probs, work)]
        for wk, p in zip(work, ps):
            wk["t"] = wk["t"] - mm(wk["t"], p, ones_bd)
    us = [_per_head_mm1(wk["t"], pr["v"] * wk["beta"], ones_bd) for pr, wk in zip(probs, work)]
    ws = [_per_head_mm1(wk["t"], pr["k"] * wk["beta"] * wk["eb"], ones_bd) for pr, wk in zip(probs, work)]
    return [(u, w.astype(BF16), (pr["qk_raw"] * wk["lmat"]).astype(BF16), (pr["q"] * wk["eb"]).astype(BF16),
             (pr["k"] * wk["etail"]).astype(BF16), wk["dlast"]) for pr, wk, u, w in zip(probs, work, us, ws)]


def _delta_state_steps(items, ones_bd, state_ref):
    sts = [state_ref[it[0]] for it in items]
    bds = [_head_blockdiag(st.astype(BF16), ones_bd) for st in sts]
    rows = items[0][1].shape[0]
    ws_qs = [_bd_dot(jnp.concatenate([it[2], it[4]], axis=0), bd) for it, bd in zip(items, bds)]
    v16s = [(it[1] - wq[0:rows]).astype(BF16) for it, wq in zip(items, ws_qs)]
    outs = [wq[rows:2 * rows] + _bd_dot(it[3], _head_blockdiag(v16, ones_bd))
            for it, wq, v16 in zip(items, ws_qs, v16s)]
    for it, st, v16 in zip(items, sts, v16s):
        state_ref[it[0]] = st * it[6] + _per_head_dot_tn(it[5], v16)
    return outs


def _delta_mixer_kernel(t, c, p_ref, self_ref, selb_ref, maskf_ref, maskb_ref, ones_ref, onorm_ref,
                        conv_ref, expand_ref, nega_ref, dtb_ref, h0_ref, y_ref, sout_ref,
                        u_ref, w_ref, qk_ref, qd_ref, kt_ref, dl_ref, of_ref, ob_ref, state_ref):
    n_chunks = t // c
    qkv_w = 3 * GROUP_W
    state_ref[...] = h0_ref[0]
    row_id = lax.broadcasted_iota(jnp.int32, (c, qkv_w), 0)
    w_prev, w_mid, w_next = conv_ref[0:1, :], conv_ref[1:2, :], conv_ref[2:3, :]
    sels, masks = (self_ref, selb_ref), (maskf_ref, maskb_ref)

    def conv_qkv(n):
        r0 = pl.multiple_of(n * c, c)
        x = p_ref[0, pl.ds(r0, c), 0:qkv_w]
        before = p_ref[0, pl.ds(pl.multiple_of(jnp.maximum(r0 - SUBLANES, 0), SUBLANES), SUBLANES), 0:qkv_w]
        after = p_ref[0, pl.ds(pl.multiple_of(jnp.minimum(r0 + c, t - SUBLANES), SUBLANES), SUBLANES), 0:qkv_w]
        before_row = jnp.where(n > 0, before[SUBLANES - 1:SUBLANES, :], 0.0)
        after_row = jnp.where(n < n_chunks - 1, after[0:1, :], 0.0)
        x_prev = jnp.where(row_id == 0, before_row, pltpu.roll(x, 1, axis=0))
        x_next = jnp.where(row_id == c - 1, after_row, pltpu.roll(x, c - 1, axis=0))
        y = _silu(x_prev * w_prev + x * w_mid + x_next * w_next)
        qk = y[:, 0:2 * GROUP_W]
        ones2 = ones_ref[...]
        ss = jnp.concatenate([_head_sum(qk[:, 0:GROUP_W] * qk[:, 0:GROUP_W], ones2),
                              _head_sum(qk[:, GROUP_W:] * qk[:, GROUP_W:], ones2)], axis=1)
        qk = qk * lax.rsqrt(ss + EPS)
        q, k, v = qk[:, 0:GROUP_W] * (HEAD_DIM ** -0.5), qk[:, GROUP_W:], y[:, 2 * GROUP_W:]
        k_hi, k_lo = _split2(k)
        bd_k = _head_blockdiag(k_hi, ones2)
        shared = _bd_dot_nt(jnp.concatenate([k_hi, k_lo, q.astype(BF16)], axis=0), bd_k)
        kk = shared[0:c] + shared[c:2 * c] + _bd_dot_nt(k_hi, _head_blockdiag(k_lo, ones2))
        qk_raw = shared[2 * c:3 * c]
        ba = p_ref[0, pl.ds(r0, c), 4 * GROUP_W:4 * GROUP_W + LANES]
        return dict(q=q, k=k, v=v, kk=kk, qk_raw=qk_raw, ba=ba)

    per_iter = 4 if n_chunks % 4 == 0 else 1

    def prepare(i, carry):
        chunks = [i * per_iter + j for j in range(per_iter)]
        probs = []
        for n in chunks:
            shared = conv_qkv(n)
            probs += [dict(shared, sel=sels[d], mask=masks[d], d=d) for d in range(2)]
        res = _delta_prepare(probs, ones_ref[...], expand_ref, nega_ref, dtb_ref, c)
        for idx, (u, w16, qk16, qd, ktail, dlast) in enumerate(res):
            n, d = chunks[idx // 2], idx % 2
            rows = pl.ds(pl.multiple_of(n * c, c), c)
            u_ref[d, rows, :] = u
            w_ref[d, rows, :] = w16
            qk_ref[d, rows, :] = qk16
            qd_ref[d, rows, :] = qd
            kt_ref[d, rows, :] = ktail
            dl_ref[d, pl.ds(pl.multiple_of(n * SUBLANES, SUBLANES), SUBLANES), :] = dlast
        return carry

    lax.fori_loop(0, n_chunks // per_iter, prepare, 0)

    def body(i, carry):
        items, rows_of = [], []
        for d, n in ((0, i), (1, n_chunks - 1 - i)):
            rows = pl.ds(pl.multiple_of(n * c, c), c)
            dlast = dl_ref[d, pl.ds(pl.multiple_of(n * SUBLANES, SUBLANES), SUBLANES), :][0:1, :]
            items.append((d, u_ref[d, rows, :], w_ref[d, rows, :], qk_ref[d, rows, :], qd_ref[d, rows, :],
                          kt_ref[d, rows, :], dlast))
            rows_of.append(rows)
        o_f, o_b = _delta_state_steps(items, ones_ref[...], state_ref)
        of_ref[rows_of[0], :] = o_f
        ob_ref[rows_of[1], :] = o_b
        return carry

    lax.fori_loop(0, n_chunks, body, 0)

    rbk = min(t, 256)

    def fin(i, carry):
        rows = pl.ds(pl.multiple_of(i * rbk, rbk), rbk)
        o = of_ref[rows, :] + ob_ref[rows, :]
        gate = p_ref[0, rows, 3 * GROUP_W:4 * GROUP_W]
        y_ref[0, rows, :] = _head_norm_gate(o, gate, onorm_ref[...], ones_ref[...]).astype(BF16)
        return carry

    lax.fori_loop(0, t // rbk, fin, 0)
    sout_ref[0] = state_ref[...]


def _delta_mixer(p, conv_w, expand, nega, dtb, onorm, s0):
    b, t, w = p.shape
    c = DN_CHUNK
    sel_f, mask_f = _dn_consts(c, False)
    sel_b, mask_b = _dn_consts(c, True)
    consts = [sel_f, sel_b, mask_f, mask_b, _ones_blockdiag(), onorm, conv_w, expand, nega, dtb]
    state_shape = (2, HEAD_DIM, GROUP_W)
    return pl.pallas_call(
        functools.partial(_delta_mixer_kernel, t, c),
        grid=(b,),
        in_specs=[pl.BlockSpec((1, t, w), lambda i: (i, 0, 0))]
                 + [_full_spec(a.shape) for a in consts]
                 + [pl.BlockSpec((1,) + state_shape, lambda i: (i, 0, 0, 0))],
        out_specs=[pl.BlockSpec((1, t, GROUP_W), lambda i: (i, 0, 0)),
                   pl.BlockSpec((1,) + state_shape, lambda i: (i, 0, 0, 0))],
        out_shape=[jax.ShapeDtypeStruct((b, t, GROUP_W), BF16),
                   jax.ShapeDtypeStruct((b,) + state_shape, F32)],
        scratch_shapes=[pltpu.VMEM((2, t, GROUP_W), F32)]
                       + [pltpu.VMEM((2, t, GROUP_W), BF16)] * 4
                       + [pltpu.VMEM((2, (t // c) * SUBLANES, GROUP_W), F32),
                          pltpu.VMEM((t, GROUP_W), F32), pltpu.VMEM((t, GROUP_W), F32),
                          pltpu.VMEM(state_shape, F32)],
        compiler_params=_params(("arbitrary",)),
        name="delta_mixer",
    )(p, *consts, s0)


def _s5_kernel(t, tc, u_ref, b_ref, cm_ref, lbar_ref, dskip_ref, gluw_ref, glub_ref, h0_ref,
               y_ref, sout_ref, x_ref, yf_ref, yb_ref, carry_ref):
    n_chunks = t // tc
    n_lane_blocks = S5_STATE // LANES
    carry_ref[...] = h0_ref[0]
    sub8 = lax.broadcasted_iota(jnp.int32, (SUBLANES, LANES), 0)

    def scan_dirs(rows_of, y_refs):
        for d in range(2):
            u16 = u_ref[0, rows_of[d], :].astype(BF16)
            for j in range(n_lane_blocks):
                x_ref[d, j] = _dot(u16, b_ref[d, j])
            y_refs[d][rows_of[d], :] = jnp.zeros((tc, GROUP_W), F32)

        def scan_block(d, j):
            bu = x_ref[d, j]
            xr, xi = bu[:, 0:LANES], bu[:, LANES:2 * LANES]
            lb = lbar_ref[d, j]
            pw = [(lb[0:1, :], lb[1:2, :])]
            for _ in range(SUBLANES - 1):
                pr, pi = pw[-1]
                pw.append((pr * pw[0][0] - pi * pw[0][1], pr * pw[0][1] + pi * pw[0][0]))
            n_groups = tc // SUBLANES
            blocks_r = [xr[v * SUBLANES:(v + 1) * SUBLANES] for v in range(n_groups)]
            blocks_i = [xi[v * SUBLANES:(v + 1) * SUBLANES] for v in range(n_groups)]
            for s in range(int(np.log2(SUBLANES))):
                sh = 1 << s
                keep = sub8 < SUBLANES - sh if d else sub8 >= sh
                m_re, m_im = jnp.where(keep, pw[sh - 1][0], 0.0), jnp.where(keep, pw[sh - 1][1], 0.0)
                rot = SUBLANES - sh if d else sh
                for v in range(n_groups):
                    sr, si = pltpu.roll(blocks_r[v], rot, axis=0), pltpu.roll(blocks_i[v], rot, axis=0)
                    blocks_r[v] = blocks_r[v] + m_re * sr - m_im * si
                    blocks_i[v] = blocks_i[v] + m_re * si + m_im * sr
            p_re = jnp.zeros((SUBLANES, LANES), F32)
            p_im = jnp.zeros((SUBLANES, LANES), F32)
            for r in range(SUBLANES):
                pr, pi = pw[SUBLANES - 1 - r] if d else pw[r]
                p_re = jnp.where(sub8 == r, pr, p_re)
                p_im = jnp.where(sub8 == r, pi, p_im)
            cr, ci = carry_ref[d, 0, j], carry_ref[d, 1, j]
            last = 0 if d else SUBLANES - 1
            for step in range(n_groups):
                v = n_groups - 1 - step if d else step
                blocks_r[v] = blocks_r[v] + (p_re * cr - p_im * ci)
                blocks_i[v] = blocks_i[v] + (p_re * ci + p_im * cr)
                cr, ci = blocks_r[v][last:last + 1, :], blocks_i[v][last:last + 1, :]
            carry_ref[d, 0, j] = cr
            carry_ref[d, 1, j] = ci
            return jnp.concatenate([jnp.concatenate(blocks_r, axis=0), jnp.concatenate(blocks_i, axis=0)],
                                   axis=1).astype(BF16)

        def lane_block(j, carry):
            states = [scan_block(d, j) for d in range(2)]
            for d in range(2):
                y_refs[d][rows_of[d], :] += _dot(states[d], cm_ref[j])
            return carry

        lax.fori_loop(0, n_lane_blocks, lane_block, 0)

    def body(n, carry):
        scan_dirs((pl.ds(pl.multiple_of(n * tc, tc), tc), pl.ds(pl.multiple_of((n_chunks - 1 - n) * tc, tc), tc)),
                  (yf_ref, yb_ref))
        return carry

    lax.fori_loop(0, n_chunks, body, 0)

    def fin(i, carry):
        rows = pl.ds(pl.multiple_of(i * tc, tc), tc)
        u = u_ref[0, rows, :]
        y = _gelu_tanh(yf_ref[rows, :] + yb_ref[rows, :] + dskip_ref[...] * u)
        z = _dot(y.astype(BF16), gluw_ref[...]) + glub_ref[...]
        y_ref[0, rows, :] = (y * _sigmoid(z)).astype(BF16)
        return carry

    lax.fori_loop(0, n_chunks, fin, 0)
    sout_ref[0] = carry_ref[...]


def _s5_mixer(u, b_in, cmat, lbar, dskip, glu_w, glu_b, h0):
    b, t, _ = u.shape
    tc = min(S5_CHUNK, t)
    nlb = S5_STATE // LANES
    consts = [b_in, cmat, lbar, dskip, glu_w, glu_b]
    st_shape = (2, 2, nlb, 1, LANES)
    return pl.pallas_call(
        functools.partial(_s5_kernel, t, tc),
        grid=(b,),
        in_specs=[pl.BlockSpec((1, t, GROUP_W), lambda i: (i, 0, 0))]
                 + [_full_spec(a.shape) for a in consts]
                 + [pl.BlockSpec((1,) + st_shape, lambda i: (i, 0, 0, 0, 0, 0))],
        out_specs=[pl.BlockSpec((1, t, GROUP_W), lambda i: (i, 0, 0)),
                   pl.BlockSpec((1,) + st_shape, lambda i: (i, 0, 0, 0, 0, 0))],
        out_shape=[jax.ShapeDtypeStruct((b, t, GROUP_W), BF16),
                   jax.ShapeDtypeStruct((b,) + st_shape, F32)],
        scratch_shapes=[pltpu.VMEM((2, nlb, tc, 2 * LANES), F32), pltpu.VMEM((t, GROUP_W), F32),
                        pltpu.VMEM((t, GROUP_W), F32), pltpu.VMEM(st_shape, F32)],
        compiler_params=_params(("arbitrary",)),
        name="s5_mixer",
    )(u, *consts, h0)


def _outproj_kernel(x_ref, mod_ref, ya_ref, yb_ref, yc_ref, yd_ref, w_ref, o_ref):
    y = (_dot(ya_ref[0], w_ref[0]) + _dot(yb_ref[0], w_ref[1])
         + _dot(yc_ref[0], w_ref[2]) + _dot(yd_ref[0], w_ref[3]))
    o_ref[0] = x_ref[0] + mod_ref[0][:, 2 * D_MODEL:3 * D_MODEL] * y


def _out_projection(x, mod, ys, w_out):
    b, t, _ = x.shape
    tb = min(TOKEN_BLOCK, t)
    tok = lambda n: pl.BlockSpec((1, tb, n), lambda i, j: (i, j, 0))
    return pl.pallas_call(
        _outproj_kernel,
        grid=(b, t // tb),
        in_specs=[tok(D_MODEL), pl.BlockSpec((1, 1, 6 * D_MODEL), lambda i, j: (i, 0, 0))]
                 + [tok(GROUP_W)] * 4 + [_full_spec(w_out.shape)],
        out_specs=tok(D_MODEL),
        out_shape=jax.ShapeDtypeStruct(x.shape, F32),
        compiler_params=_params(("arbitrary", "arbitrary")),
        name="out_projection",
    )(x, mod, *ys, w_out)


def _topk_rows(scores, k, row_ids):
    n = scores[0].shape[1]
    slot = lax.broadcasted_iota(jnp.int32, (k, n), 0)
    scores = list(scores)
    vals = [jnp.zeros((k, n), F32) for _ in scores]
    ids = [jnp.zeros((k, n), F32) for _ in scores]
    for it in range(k):
        ms = [jnp.max(s, axis=0, keepdims=True) for s in scores]
        idxs = [jnp.min(jnp.where(s == m, row_ids, 1e9), axis=0, keepdims=True) for s, m in zip(scores, ms)]
        vals = [jnp.where(slot == it, m, v) for m, v in zip(ms, vals)]
        ids = [jnp.where(slot == it, idx, i) for idx, i in zip(idxs, ids)]
        scores = [jnp.where(row_ids == idx, -jnp.inf, s) for s, idx in zip(scores, idxs)]
    return list(zip(vals, ids))


def _sorting_network(n):
    pairs = []

    def merge(lo, hi, r):
        step = r * 2
        if step < hi - lo:
            merge(lo, hi, step)
            merge(lo + r, hi, step)
            pairs.extend((i, i + r) for i in range(lo + r, hi - r, step))
        else:
            pairs.append((lo, lo + r))

    def sort(lo, hi):
        if hi > lo:
            mid = lo + (hi - lo) // 2
            sort(lo, mid)
            sort(mid + 1, hi)
            merge(lo, hi, 1)

    sort(0, n - 1)
    return pairs


def _lex_first(va, ia, vb, ib):
    tie = va == vb
    return jnp.where(tie, -ia, va) > jnp.where(tie, -ib, vb)


def _topk_columns(scores, k, row_ids):
    n_wires = scores[0].shape[0] // SUBLANES
    n = scores[0].shape[1]
    assert n_wires == k
    net = _sorting_network(n_wires)
    slot = lax.broadcasted_iota(jnp.int32, (k, n), 0)
    sub = lambda x, w: x[w * SUBLANES:(w + 1) * SUBLANES]
    vs = [[sub(s, w) for w in range(n_wires)] for s in scores]
    ids = [[sub(row_ids, w) for w in range(n_wires)] for _ in scores]
    for i, j in net:
        for v, d in zip(vs, ids):
            first = _lex_first(v[i], d[i], v[j], d[j])
            v[i], v[j] = jnp.where(first, v[i], v[j]), jnp.where(first, v[j], v[i])
            d[i], d[j] = jnp.where(first, d[i], d[j]), jnp.where(first, d[j], d[i])
    vals = [jnp.zeros((k, n), F32) for _ in scores]
    wins = [jnp.zeros((k, n), F32) for _ in scores]
    for it in range(k):
        for p, (v, d) in enumerate(zip(vs, ids)):
            hv, hd = v[0], d[0]
            for sh in (4, 2, 1):
                pv, pd = pltpu.roll(hv, sh, axis=0), pltpu.roll(hd, sh, axis=0)
                first = _lex_first(hv, hd, pv, pd)
                hv, hd = jnp.where(first, hv, pv), jnp.where(first, hd, pd)
            vals[p] = jnp.where(slot == it, hv[0:1, :], vals[p])
            wins[p] = jnp.where(slot == it, hd[0:1, :], wins[p])
            popped = d[0] == hd
            for w in range(k - 1 - it):
                v[w] = jnp.where(popped, v[w + 1], v[w])
                d[w] = jnp.where(popped, d[w + 1], d[w])
    return list(zip(vals, wins))


def _staircase():
    pairs = [(i, j) for i in range(PEER_TOPK) for j in range(PEER_TOPK) if (i + 1) * (j + 1) <= PEER_TOPK]
    rows = -(-len(pairs) // SUBLANES) * SUBLANES
    sel = np.zeros((2, rows, PEER_TOPK), np.float32)
    aux = np.zeros((2, rows, LANES), np.float32)
    aux[0, len(pairs):] = -np.inf
    aux[1, len(pairs):] = 1e9
    for r, (i, j) in enumerate(pairs):
        sel[0, r, i] = 1.0
        sel[1, r, j] = 1.0
        aux[1, r] = i * PEER_TOPK + j
    return jnp.asarray(np.tile(sel, (1, 1, 3)), BF16), jnp.asarray(aux, F32)


def _take_rows(table, idx, k):
    out = jnp.zeros(idx.shape, F32)
    for r in range(k):
        out = jnp.where(idx == float(r), table[r:r + 1, :], out)
    return out


def _peer_route_kernel(tb, blocks_per_seq, x_ref, mod_ref, xn_ref, modn_ref, g_ref, wqh_ref, wql_ref, kh_ref, kl_ref,
                       stsel_ref, staux_ref, h_ref, a_ref, c_ref, gate_ref,
                       qt_ref, hh_ref, hl_ref, at_ref, ct_ref, gt_ref):
    lin = pl.program_id(0) * blocks_per_seq + pl.program_id(1)
    slot = lin % 2
    nq = PEER_HEADS * PEER_DQ

    def normed(xr, mr):
        mod = mr[0]
        return _split2(_rms_modulate(xr[0], g_ref[...], mod[:, 3 * D_MODEL:4 * D_MODEL],
                                     mod[:, 4 * D_MODEL:5 * D_MODEL]))

    @pl.when(lin == 0)
    def _():
        h_hi, h_lo = normed(x_ref, mod_ref)
        hh_ref[0] = h_hi
        wq_hi = wqh_ref[...]
        qt_ref[0] = _dot_nt(wq_hi, h_hi) + _dot_nt(wq_hi, h_lo) + _dot_nt(wql_ref[...], h_hi)

    h_ref[0] = hh_ref[slot]
    hn_hi, hn_lo = normed(xn_ref, modn_ref)
    hh_ref[1 - slot] = hn_hi
    hl_ref[...] = hn_lo
    half = PEER_DQ // 2
    key_ids = lax.broadcasted_iota(jnp.int32, (PEER_NKEYS, LANES), 0).astype(F32)

    heads_per_iter = 4

    q_rows = nq // (PEER_HEADS // heads_per_iter)

    def head_body(it, carry):
        heads = [it * heads_per_iter + i for i in range(heads_per_iter)]
        tiles = [slice(lt * LANES, (lt + 1) * LANES) for lt in range(tb // LANES)]
        probs = [(head, lanes) for head in heads for lanes in tiles]
        scores = []
        for head, lanes in probs:
            for p in range(2):
                hp = head * 2 + p
                q_hi, q_lo = _split2(qt_ref[slot, pl.ds(pl.multiple_of(hp * half, half), half), lanes])
                scores.append(_dot(kh_ref[hp], q_hi) + _dot(kh_ref[hp], q_lo) + _dot(kl_ref[hp], q_hi))
        rows = pl.ds(pl.multiple_of(it * q_rows, q_rows), q_rows)
        nh_hi, w_hi = hh_ref[1 - slot], wqh_ref[rows, :]
        qt_ref[1 - slot, rows, :] = (_dot_nt(w_hi, nh_hi) + _dot_nt(w_hi, hl_ref[...])
                                     + _dot_nt(wql_ref[rows, :], nh_hi))
        top = _topk_columns(scores, PEER_TOPK, key_ids)
        cands = [_sel_dot(stsel_ref[0], top[2 * i][0]) + _sel_dot(stsel_ref[1], top[2 * i + 1][0]) + staux_ref[0]
                 for i in range(len(probs))]
        best = _topk_rows(cands, PEER_TOPK, staux_ref[1])
        for i, (head, lanes) in enumerate(probs):
            out_rows = pl.ds(pl.multiple_of(head * PEER_TOPK, PEER_TOPK), PEER_TOPK)
            sc, flat = best[i]
            r1 = jnp.floor(flat * (1.0 / PEER_TOPK))
            r2 = flat - r1 * PEER_TOPK
            e = jnp.exp(sc - sc[0:1, :])
            at_ref[out_rows, lanes] = _take_rows(top[2 * i][1], r1, PEER_TOPK)
            ct_ref[out_rows, lanes] = _take_rows(top[2 * i + 1][1], r2, PEER_TOPK)
            gt_ref[out_rows, lanes] = e / jnp.sum(e, axis=0, keepdims=True)
        return carry

    lax.fori_loop(0, PEER_HEADS // heads_per_iter, head_body, 0)
    a_ref[0] = at_ref[...].T.astype(jnp.int32)
    c_ref[0] = ct_ref[...].T.astype(jnp.int32)
    gate_ref[0] = gt_ref[...].T


def _peer_route(x, mod, gain, wq_hi, wq_lo, keys_hi, keys_lo):
    b, t, _ = x.shape
    tb = min(TOKEN_BLOCK, t)
    tok = lambda n: pl.BlockSpec((1, tb, n), lambda i, j: (i, j, 0))
    blocks_per_seq = t // tb
    last = b * blocks_per_seq - 1

    def next_block(i, j):
        lin = jnp.minimum(i * blocks_per_seq + j + 1, last)
        return (lin // blocks_per_seq, lin % blocks_per_seq, 0)

    consts = [gain, wq_hi, wq_lo, keys_hi, keys_lo, *_staircase()]
    return pl.pallas_call(
        functools.partial(_peer_route_kernel, tb, blocks_per_seq),
        grid=(b, t // tb),
        scratch_shapes=[pltpu.VMEM((2, PEER_HEADS * PEER_DQ, tb), F32), pltpu.VMEM((2, tb, D_MODEL), BF16),
                        pltpu.VMEM((tb, D_MODEL), BF16)] + [pltpu.VMEM((PEER_SEL, tb), F32)] * 3,
        in_specs=[tok(D_MODEL), pl.BlockSpec((1, 1, 6 * D_MODEL), lambda i, j: (i, 0, 0)),
                  pl.BlockSpec((1, tb, D_MODEL), next_block),
                  pl.BlockSpec((1, 1, 6 * D_MODEL), lambda i, j: (next_block(i, j)[0], 0, 0))]
                 + [_full_spec(a.shape) for a in consts],
        out_specs=[tok(D_MODEL), tok(PEER_SEL), tok(PEER_SEL), tok(PEER_SEL)],
        out_shape=[jax.ShapeDtypeStruct((b, t, D_MODEL), BF16),
                   jax.ShapeDtypeStruct((b, t, PEER_SEL), jnp.int32),
                   jax.ShapeDtypeStruct((b, t, PEER_SEL), jnp.int32),
                   jax.ShapeDtypeStruct((b, t, PEER_SEL), F32)],
        compiler_params=_params(("arbitrary", "arbitrary")),
        name="peer_route",
    )(x, mod, x, mod, *consts)


def _peer_expert_kernel(tb, n_slabs, x_ref, mod_ref, h_ref, a_ref, c_ref, gate_ref, ut_ref, v_ref,
                        o_ref, gmat_ref, acc_ref):
    j = pl.program_id(2)
    nk = PEER_NKEYS

    @pl.when(j == 0)
    def _():
        acc_ref[...] = jnp.zeros_like(acc_ref)
        key_id = lax.broadcasted_iota(jnp.int32, (nk, PEER_SEL), 0)

        pack = 2 * SUBLANES
        groups = 4

        def build(nb, carry):
            n0 = pl.multiple_of(nb * groups * pack, groups * pack)
            a_rows = a_ref[0, pl.ds(n0, groups * pack), :]
            c_rows = c_ref[0, pl.ds(n0, groups * pack), :]
            g_rows = gate_ref[0, pl.ds(n0, groups * pack), :]
            mats = []
            for r in range(groups * pack):
                row_sel = jnp.where(key_id == a_rows[r:r + 1], 1.0, 0.0).astype(BF16)
                col_sel = jnp.where(key_id == c_rows[r:r + 1], g_rows[r:r + 1], 0.0).astype(BF16)
                mats.append(_dot_nt(row_sel, col_sel).astype(BF16))
            for gi in range(groups):
                stacked = jnp.stack(mats[gi * pack:(gi + 1) * pack], axis=0)
                gmat_ref[:, pl.ds(n0 + gi * pack, pack), :] = pltpu.einshape("nac->anc", stacked)
            return carry

        lax.fori_loop(0, tb // (groups * pack), build, 0)

    h = h_ref[0]
    n_pairs = n_slabs // 2
    cols = [slice(i * 2 * nk, (i + 1) * 2 * nk) for i in range(n_pairs)]
    lookahead = 2
    s = [_dot(h, ut_ref[:, cols[i]]) for i in range(min(lookahead, n_pairs))]
    acc = None
    for i in range(n_pairs):
        if i + lookahead < n_pairs:
            s.append(_dot(h, ut_ref[:, cols[i + lookahead]]))
        a0 = j * n_slabs + 2 * i
        g = jnp.concatenate([gmat_ref[a0], gmat_ref[a0 + 1]], axis=1)
        w = g * _gelu_tanh(s[i]).astype(BF16)
        p = _dot(w, v_ref[cols[i], :])
        acc = p if acc is None else acc + p
    acc_ref[...] += acc

    @pl.when(j == pl.num_programs(2) - 1)
    def _():
        o_ref[0] = x_ref[0] + mod_ref[0][:, 5 * D_MODEL:6 * D_MODEL] * acc_ref[...]


def _peer_experts(x, mod, h, a, c, gate, u_t, v_tab):
    b, t, _ = x.shape
    tb = min(PEER_TOKEN_BLOCK, t)
    n_exp = u_t.shape[1]
    eb = PEER_EXPERT_BLOCK
    n_slabs = eb // PEER_NKEYS
    tok = lambda n: pl.BlockSpec((1, tb, n), lambda i, k, j: (i, k, 0))
    return pl.pallas_call(
        functools.partial(_peer_expert_kernel, tb, n_slabs),
        grid=(b, t // tb, n_exp // eb),
        in_specs=[tok(D_MODEL), pl.BlockSpec((1, 1, 6 * D_MODEL), lambda i, k, j: (i, 0, 0)),
                  tok(D_MODEL), tok(PEER_SEL), tok(PEER_SEL), tok(PEER_SEL),
                  pl.BlockSpec((D_MODEL, eb), lambda i, k, j: (0, j)),
                  pl.BlockSpec((eb, D_MODEL), lambda i, k, j: (j, 0))],
        out_specs=tok(D_MODEL),
        out_shape=jax.ShapeDtypeStruct(x.shape, F32),
        scratch_shapes=[pltpu.VMEM((PEER_NKEYS, tb, PEER_NKEYS), BF16), pltpu.VMEM((tb, D_MODEL), F32)],
        compiler_params=_params(("arbitrary", "arbitrary", "arbitrary")),
        name="peer_experts",
    )(x, mod, h, a, c, gate, u_t, v_tab)


def _final_norm_kernel(x_ref, g_ref, o_ref):
    x = x_ref[0]
    ms = jnp.mean(x * x, axis=-1, keepdims=True)
    o_ref[0] = x * lax.rsqrt(ms + EPS) * g_ref[...]


def _final_norm(x, gain):
    b, t, _ = x.shape
    tb = min(TOKEN_BLOCK, t)
    tok = pl.BlockSpec((1, tb, D_MODEL), lambda i, j: (i, j, 0))
    return pl.pallas_call(
        _final_norm_kernel,
        grid=(b, t // tb),
        in_specs=[tok, _full_spec((1, D_MODEL))],
        out_specs=tok,
        out_shape=jax.ShapeDtypeStruct(x.shape, F32),
        compiler_params=_params(("arbitrary", "arbitrary")),
        name="final_norm",
    )(x, gain)


def _hi_lo(w):
    hi = w.astype(BF16)
    return hi, (w - hi.astype(F32)).astype(BF16)


def _pad_cols(w, n):
    return jnp.pad(w, ((0, 0), (0, n - w.shape[1])))


def _layer_weights(l, w_in, w_out, gla_gk_up, gla_gk_bias, gla_onorm_g, dn_conv_w, dn_A_log, dn_dt_bias,
                   dn_onorm_g, s5_lambda_re, s5_lambda_im, s5_log_dt, s5_B_re, s5_B_im, s5_C_re, s5_C_im,
                   s5_D, s5_glu_w, s5_glu_b, lower_bound, hgrn_onorm_g, peer_wq, peer_keys, peer_u, peer_v):
    gw = GROUP_W
    sizes = (gw, gw, gw, gw, 2 * GLA_LOWRANK, gw, gw, gw, gw, 2 * N_HEADS, 2 * N_HEADS, gw, gw, gw, gw, 2 * gw)
    offs = np.concatenate([[0], np.cumsum(sizes)])
    col = lambda a, b: w_in[l][:, int(offs[a]):int(offs[b])]
    wl = {}
    wl["w_gla"] = _pad_cols(col(0, 5), GLA_W).astype(BF16)
    wl["w_dn"] = _pad_cols(col(5, 11), DN_W).astype(BF16)
    wl["w_s5"] = col(11, 12).astype(BF16)
    wl["w_hg"] = col(12, 16).astype(BF16)
    wl["w_out"] = w_out[l].reshape(4, gw, D_MODEL).astype(BF16)
    tile_head = lambda g: jnp.tile(g, N_HEADS).reshape(1, gw)
    up = jnp.zeros((2, LANES, gw), F32)
    for d in range(2):
        up = up.at[d, d * GLA_LOWRANK:(d + 1) * GLA_LOWRANK, :].set(gla_gk_up[l, d])
    wl["gla_up_hi"], wl["gla_up_lo"] = _hi_lo(up)
    wl["gla_bias"] = gla_gk_bias[l].reshape(2, 1, gw)
    wl["gla_onorm"] = tile_head(gla_onorm_g[l])
    expand = np.zeros((2, LANES, 2 * gw), np.float32)
    for d in range(2):
        for h in range(N_HEADS):
            expand[d, d * N_HEADS + h, h * HEAD_DIM:(h + 1) * HEAD_DIM] = 1.0
            expand[d, 2 * N_HEADS + d * N_HEADS + h, gw + h * HEAD_DIM:gw + (h + 1) * HEAD_DIM] = 1.0
    wl["dn_expand"] = jnp.asarray(expand, BF16)
    wl["dn_conv"] = jnp.pad(dn_conv_w[l], ((0, SUBLANES - 3), (0, 0)))
    wl["dn_nega"] = jnp.repeat(-jnp.exp(dn_A_log[l]), HEAD_DIM, axis=-1).reshape(2, 1, gw)
    wl["dn_dtb"] = jnp.repeat(dn_dt_bias[l], HEAD_DIM, axis=-1).reshape(2, 1, gw)
    wl["dn_onorm"] = tile_head(dn_onorm_g[l])
    nlb = S5_STATE // LANES
    lam_re, lam_im = s5_lambda_re[l].astype(F32), s5_lambda_im[l].astype(F32)
    dt = jnp.exp(s5_log_dt[l].astype(F32))[:, :, None]
    mag = jnp.exp(lam_re * dt)
    lbar_re, lbar_im = mag * jnp.cos(lam_im * dt), mag * jnp.sin(lam_im * dt)
    den = lam_re * lam_re + lam_im * lam_im
    num_re, num_im = lbar_re - 1.0, lbar_im
    coef_re = (num_re * lam_re + num_im * lam_im) / den
    coef_im = (num_im * lam_re - num_re * lam_im) / den
    b_re, b_im = s5_B_re[l].astype(F32), s5_B_im[l].astype(F32)
    bb_re = coef_re[..., None] * b_re - coef_im[..., None] * b_im
    bb_im = coef_re[..., None] * b_im + coef_im[..., None] * b_re
    eye_g = jnp.eye(S5_G, dtype=F32)
    bd_in = lambda m: jnp.einsum('dgpc,gk->dgckp', m, eye_g).reshape(2, gw, S5_STATE)
    blk = lambda m: m.reshape(2, gw, nlb, LANES).transpose(0, 2, 1, 3)
    b_cat = jnp.concatenate([blk(bd_in(bb_re)), blk(bd_in(bb_im))], axis=-1)
    wl["s5_b"] = b_cat.astype(BF16)
    bd_out = lambda m: jnp.einsum('gcp,gk->gpkc', m.astype(F32), eye_g).reshape(S5_STATE, gw)
    c_re, c_im = bd_out(s5_C_re[l]), bd_out(s5_C_im[l])
    wl["s5_c"] = jnp.concatenate([c_re.reshape(nlb, LANES, gw), -c_im.reshape(nlb, LANES, gw)],
                                 axis=1).astype(BF16)
    wl["s5_lbar"] = jnp.stack([lbar_re.reshape(2, nlb, LANES), lbar_im.reshape(2, nlb, LANES)], axis=2)
    wl["s5_d"] = s5_D[l].reshape(1, gw)
    wl["s5_glu_w"] = s5_glu_w[l].astype(BF16)
    wl["s5_glu_b"] = s5_glu_b[l].reshape(1, gw)
    wl["hg_loglb"] = jnp.log(lower_bound).reshape(1, gw)
    wl["hg_log1m"] = jnp.log1p(-lower_bound).reshape(1, gw)
    wl["hg_1m"] = (1.0 - lower_bound).reshape(1, gw)
    wl["hg_onorm"] = tile_head(hgrn_onorm_g[l])
    wl["wq_hi"], wl["wq_lo"] = _hi_lo(peer_wq[l].T)
    wl["keys_hi"], wl["keys_lo"] = _hi_lo(peer_keys[l].reshape(PEER_HEADS * 2, PEER_NKEYS, PEER_DQ // 2))
    wl["u_t"] = peer_u[l].astype(BF16).T
    wl["v_tab"] = peer_v[l].astype(BF16)
    return wl


def _heads_to_lanes(s, transposed):
    perm = (0, 1, 4, 2, 3) if transposed else (0, 1, 3, 2, 4)
    return s.astype(F32).transpose(perm).reshape(s.shape[0], 2, HEAD_DIM, GROUP_W)


def _lanes_to_heads(s, transposed):
    s = s.reshape(s.shape[0], 2, HEAD_DIM, N_HEADS, HEAD_DIM)
    return s.transpose((0, 1, 3, 4, 2) if transposed else (0, 1, 3, 2, 4))


def _to_column_major(x):
    b, n, d = x.shape
    return x.reshape(b, n // GRID_W, GRID_W, d).transpose(0, 2, 1, 3).reshape(b, n, d)


def _to_row_major(x):
    b, n, d = x.shape
    return x.reshape(b, GRID_W, n // GRID_W, d).transpose(0, 2, 1, 3).reshape(b, n, d)


def _trunk_layer(x, mod, wl, norm1_g, norm2_g, states, shared_mod):
    st_gla, st_dn, st_s5, st_hg = states
    p_gla, p_dn, p_s5, p_hg = _in_projection(x, mod, norm1_g, [wl["w_gla"], wl["w_dn"], wl["w_s5"], wl["w_hg"]])
    y_a, n_gla = _decay_mixer("gla", p_gla, wl["gla_up_hi"], wl["gla_up_lo"], wl["gla_bias"], wl["gla_onorm"], st_gla)
    y_b, n_dn = _delta_mixer(p_dn, wl["dn_conv"], wl["dn_expand"], wl["dn_nega"], wl["dn_dtb"], wl["dn_onorm"], st_dn)
    y_c, n_s5 = _s5_mixer(p_s5, wl["s5_b"], wl["s5_c"], wl["s5_lbar"], wl["s5_d"],
                          wl["s5_glu_w"], wl["s5_glu_b"], st_s5)
    y_d, n_hg = _decay_mixer("hgrn", p_hg, wl["hg_loglb"], wl["hg_log1m"], wl["hg_1m"], wl["hg_onorm"], st_hg)
    x = _out_projection(x, mod, [y_a, y_b, y_c, y_d], wl["w_out"])
    h, a, c, gate = _peer_route(x, mod, norm2_g, wl["wq_hi"], wl["wq_lo"], wl["keys_hi"], wl["keys_lo"])
    b, t, _ = x.shape
    fold = max(1, min(PEER_TOKEN_BLOCK // t, b)) if shared_mod else 1
    fb = lambda z: z.reshape(b // fold, fold * t, z.shape[-1])
    x = _peer_experts(fb(x), mod[0:b // fold], fb(h), fb(a), fb(c), fb(gate), wl["u_t"], wl["v_tab"]).reshape(b, t, -1)
    return x, (n_gla, n_dn, n_s5, n_hg)


def kernel(x_prompt, x_sample, state_gla, state_delta, state_s5_re, state_s5_im, state_hgrn, c, c_ctx,
           mod_w, mod_b, norm1_g, norm2_g, w_in, w_out, gla_gk_up, gla_gk_bias, gla_onorm_g,
           dn_conv_w, dn_A_log, dn_dt_bias, dn_onorm_g, s5_lambda_re, s5_lambda_im, s5_log_dt,
           s5_B_re, s5_B_im, s5_C_re, s5_C_im, s5_D, s5_glu_w, s5_glu_b, hgrn_lb_logits,
           hgrn_onorm_g, peer_wq, peer_keys, peer_u, peer_v, final_norm_g):
    bp, bs = x_prompt.shape[0], x_sample.shape[0]
    nlb = S5_STATE // LANES
    lb_cum = jnp.cumsum(jax.nn.softmax(hgrn_lb_logits.astype(F32), axis=0), axis=0)
    lower_bounds = lb_cum - lb_cum[0:1]

    rows = 2 * SUBLANES
    cc = jnp.concatenate([c, c_ctx[None, :], jnp.zeros((rows - bs - 1, D_MODEL), F32)], axis=0)
    mod_all = _modulation(cc, mod_w, mod_b)

    zeros_hh = jnp.zeros((bp, 2, HEAD_DIM, GROUP_W), F32)
    zeros_s5 = jnp.zeros((bp, 2, 2, nlb, 1, LANES), F32)

    xp, xs = x_prompt, x_sample
    ctx_states = []
    for l in range(DEPTH):
        wl = _layer_weights(l, w_in, w_out, gla_gk_up, gla_gk_bias, gla_onorm_g, dn_conv_w, dn_A_log,
                            dn_dt_bias, dn_onorm_g, s5_lambda_re, s5_lambda_im, s5_log_dt, s5_B_re, s5_B_im,
                            s5_C_re, s5_C_im, s5_D, s5_glu_w, s5_glu_b, lower_bounds[l], hgrn_onorm_g,
                            peer_wq, peer_keys, peer_u, peer_v)
        n1, n2 = norm1_g[l].reshape(1, D_MODEL), norm2_g[l].reshape(1, D_MODEL)
        mod_ctx = jnp.broadcast_to(mod_all[l, bs].reshape(1, 1, 6 * D_MODEL), (bp, 1, 6 * D_MODEL))
        mod_lat = mod_all[l, 0:bs].reshape(bs, 1, 6 * D_MODEL)
        xp, st_ctx = _trunk_layer(xp, mod_ctx, wl, n1, n2, (zeros_hh, zeros_hh, zeros_s5, zeros_hh), True)
        ctx_states.append(st_ctx)
        cached_s5 = jnp.stack([state_s5_re[:, l], state_s5_im[:, l]], axis=2).reshape(bs, 2, 2, nlb, 1, LANES)
        cached = (_heads_to_lanes(state_gla[:, l], True), _heads_to_lanes(state_delta[:, l], False), cached_s5,
                  _heads_to_lanes(state_hgrn[:, l], True))
        if l % 2 == 1:
            xs = _to_column_major(xs)
        xs, _ = _trunk_layer(xs, mod_lat, wl, n1, n2, cached, False)
        if l % 2 == 1:
            xs = _to_row_major(xs)

    gain = final_norm_g.reshape(1, D_MODEL)
    y_prompt = _final_norm(xp, gain)
    y_sample = _final_norm(xs, gain)
    new_gla = jnp.stack([_lanes_to_heads(s[0], True) for s in ctx_states], axis=1)
    new_dn = jnp.stack([_lanes_to_heads(s[1], False) for s in ctx_states], axis=1)
    s5 = jnp.stack([s[2].reshape(bp, 2, 2, S5_G, S5_P) for s in ctx_states], axis=1)
    new_hg = jnp.stack([_lanes_to_heads(s[3], True) for s in ctx_states], axis=1)
    return (y_prompt, y_sample, new_gla, new_dn, s5[:, :, :, 0], s5[:, :, :, 1], new_hg)
```
